```python
import math
import jax, jax.numpy as jnp
from jax import lax
import numpy as np

D_MODEL = 1024
BATCH = 16
SEQ = 2048
DEPTH = 4

N_A_LAYERS = DEPTH // 2
N_B_LAYERS = DEPTH - N_A_LAYERS
SSM_GROUP = 16
SSM_GROUPS = D_MODEL // SSM_GROUP
SSM_STATE = 64
DT_MIN = 1e-3
DT_MAX = 1e-1
HEAD_DIM = 64
N_HEADS = D_MODEL // HEAD_DIM
DILATED_BRANCHES = ((128, 1), (512, 4), (2048, 16))
N_BRANCHES = len(DILATED_BRANCHES)
BRANCH_WIDTH = N_HEADS * HEAD_DIM
Q_WIDTH = N_BRANCHES * BRANCH_WIDTH
D_FF = 4 * D_MODEL
BLOCK = 128
EPS = 1e-6
NEG = -1e30

kernel_name = "yoco_s5_dilated_attn_hybrid"


def rms_norm(x, g):
    xf = x.astype(jnp.float32)
    y = xf * lax.rsqrt(jnp.mean(xf * xf, axis=-1, keepdims=True) + EPS)
    return (y * g.astype(jnp.float32)).astype(x.dtype)


def ada_chunks(c, w, b, n):
    m = jax.nn.silu(c) @ w + b
    return jnp.split(m[:, None, :], n, axis=-1)


def s5_mixer(u, lam_re, lam_im, log_dt, b_re, b_im, c_re, c_im, d_skip, w_glu):
    bsz, seq, dm = u.shape
    f32 = jnp.float32
    lam = lax.complex(lam_re.astype(f32), lam_im.astype(f32))
    dt = jnp.exp(log_dt.astype(f32))[:, None]
    lam_bar = jnp.exp(lam * dt)
    b_mat = lax.complex(b_re.astype(f32), b_im.astype(f32))
    b_bar = ((lam_bar - 1.0) / lam)[..., None] * b_mat
    c_mat = lax.complex(c_re.astype(f32), c_im.astype(f32))
    uf = u.astype(f32)
    ug = uf.reshape(bsz, seq, SSM_GROUPS, SSM_GROUP).astype(jnp.complex64)
    bu = jnp.einsum('bsgc,gpc->bsgp', ug, b_bar)
    a = jnp.broadcast_to(lam_bar, (1, seq) + lam_bar.shape)

    def combine(left, right):
        a_l, b_l = left
        a_r, b_r = right
        return a_r * a_l, a_r * b_l + b_r

    _, state = lax.associative_scan(combine, (a, bu), axis=1)
    y = jnp.einsum('bsgp,gcp->bsgc', state, c_mat).real.reshape(bsz, seq, dm)
    y = y + d_skip.astype(f32) * uf
    z = jax.nn.gelu(y).astype(u.dtype)
    val, gate = jnp.split(z @ w_glu, 2, axis=-1)
    return val * jax.nn.sigmoid(gate)


def to_dilated_blocks(t, dil):
    bsz, seq = t.shape[:2]
    rest = t.shape[2:]
    sub = seq // dil
    nb = -(-sub // BLOCK)
    t = jnp.swapaxes(t.reshape((bsz, sub, dil) + rest), 1, 2)
    t = jnp.pad(t, [(0, 0), (0, 0), (0, nb * BLOCK - sub)] + [(0, 0)] * len(rest))
    return t.reshape((bsz, dil, nb, BLOCK) + rest)


def from_dilated_blocks(t, seq):
    bsz, dil, nb, blk = t.shape[:4]
    rest = t.shape[4:]
    sub = seq // dil
    t = t.reshape((bsz, dil, nb * blk) + rest)[:, :, :sub]
    return jnp.swapaxes(t, 1, 2).reshape((bsz, seq) + rest)


def band_keys(t):
    prev = jnp.concatenate([jnp.zeros_like(t[:, :, :1]), t[:, :, :-1]], axis=2)
    return jnp.concatenate([prev, t], axis=3)


def band_mask(nb, span):
    qi = jnp.arange(BLOCK)[:, None]
    kj = jnp.arange(2 * BLOCK)[None, :] - BLOCK
    dist = qi - kj
    rel = (dist >= 0) & (dist <= span)
    abs_k = jnp.arange(nb)[:, None, None] * BLOCK + kj[None]
    return rel[None] & (abs_k >= 0)


def dilated_branch(q, k_band, v_band, span, dil):
    f32 = jnp.float32
    seq = q.shape[1]
    qb = to_dilated_blocks(q, dil).astype(f32)
    nb = qb.shape[2]
    s = jnp.einsum('brnqhe,brnkhe->brnhqk', qb, k_band.astype(f32)) * (HEAD_DIM ** -0.5)
    s = jnp.where(band_mask(nb, span)[None, None, :, None], s, NEG)
    m = jnp.max(s, axis=-1, keepdims=True)
    p = jnp.exp(s - m)
    den = jnp.sum(p, axis=-1)
    o = jnp.einsum('brnhqk,brnkhe->brnqhe', p, v_band.astype(f32))
    o = o / jnp.swapaxes(den, 3, 4)[..., None]
    lse = jnp.swapaxes(m[..., 0] + jnp.log(den), 3, 4)
    return from_dilated_blocks(o, seq), from_dilated_blocks(lse, seq)


def shared_kv(h, c, kv_g, kv_ada_w, kv_ada_b, w_kv):
    bsz, seq, _ = h.shape
    shift, scale = ada_chunks(c, kv_ada_w, kv_ada_b, 2)
    u = rms_norm(h, kv_g) * (1.0 + scale) + shift
    kv = (u @ w_kv).reshape(bsz, seq, 2, N_BRANCHES, N_HEADS, HEAD_DIM)
    k_bands, v_bands = [], []
    for i, (win, dil) in enumerate(DILATED_BRANCHES):
        k_bands.append(band_keys(to_dilated_blocks(kv[:, :, 0, i], dil)))
        v_bands.append(band_keys(to_dilated_blocks(kv[:, :, 1, i], dil)))
    return k_bands, v_bands


def dilated_mixer(u, w_q, k_bands, v_bands, w_o):
    bsz, seq, _ = u.shape
    q = (u @ w_q).reshape(bsz, seq, N_BRANCHES, N_HEADS, HEAD_DIM)
    outs, lses = [], []
    for i, (win, dil) in enumerate(DILATED_BRANCHES):
        o, l = dilated_branch(q[:, :, i], k_bands[i], v_bands[i], win // dil, dil)
        outs.append(o)
        lses.append(l)
    weights = jax.nn.softmax(jnp.stack(lses, axis=-1), axis=-1)
    o = jnp.einsum('gbshe,bshg->bshe', jnp.stack(outs), weights)
    return o.reshape(bsz, seq, BRANCH_WIDTH).astype(u.dtype) @ w_o


def setup_inputs(seed: int = 0) -> dict:
    key = jax.random.key(seed)
    ks = jax.random.split(key, 24)
    f32 = jnp.float32

    def nrm(k, shape, std):
        return jax.random.normal(k, shape, f32) * std

    n_idx = jnp.arange(SSM_STATE, dtype=f32)
    gp = (N_A_LAYERS, SSM_GROUPS, SSM_STATE)
    return {
        "x": nrm(ks[0], (BATCH, SEQ, D_MODEL), 1.0),
        "c": nrm(ks[1], (BATCH, D_MODEL), 1.0),
        "ln_g": 1.0 + nrm(ks[2], (DEPTH, 2, D_MODEL), 0.02),
        "ada_w": nrm(ks[3], (DEPTH, 2, D_MODEL, 3 * D_MODEL), 0.5 * D_MODEL ** -0.5),
        "ada_b": nrm(ks[4], (DEPTH, 2, 3 * D_MODEL), 0.02),
        "ssm_lam_re": -0.5 + nrm(ks[5], gp, 0.01),
        "ssm_lam_im": math.pi * n_idx + nrm(ks[6], gp, 0.01),
        "ssm_log_dt": jax.random.uniform(ks[7], (N_A_LAYERS, SSM_GROUPS), f32, math.log(DT_MIN), math.log(DT_MAX)),
        "ssm_b_re": nrm(ks[8], gp + (SSM_GROUP,), (2 * SSM_GROUP) ** -0.5),
        "ssm_b_im": nrm(ks[9], gp + (SSM_GROUP,), (2 * SSM_GROUP) ** -0.5),
        "ssm_c_re": nrm(ks[10], (N_A_LAYERS, SSM_GROUPS, SSM_GROUP, SSM_STATE), 0.5),
        "ssm_c_im": nrm(ks[11], (N_A_LAYERS, SSM_GROUPS, SSM_GROUP, SSM_STATE), 0.5),
        "ssm_d": nrm(ks[12], (N_A_LAYERS, D_MODEL), 1.0),
        "ssm_w_glu": nrm(ks[13], (N_A_LAYERS, D_MODEL, 2 * D_MODEL), D_MODEL ** -0.5),
        "kv_g": 1.0 + nrm(ks[14], (D_MODEL,), 0.02),
        "kv_ada_w": nrm(ks[15], (D_MODEL, 2 * D_MODEL), 0.5 * D_MODEL ** -0.5),
        "kv_ada_b": nrm(ks[16], (2 * D_MODEL,), 0.02),
        "w_kv": nrm(ks[17], (D_MODEL, 2 * Q_WIDTH), D_MODEL ** -0.5),
        "attn_w_q": nrm(ks[18], (N_B_LAYERS, D_MODEL, Q_WIDTH), D_MODEL ** -0.5),
        "attn_w_o": nrm(ks[19], (N_B_LAYERS, BRANCH_WIDTH, D_MODEL), BRANCH_WIDTH ** -0.5),
        "mlp_w1": nrm(ks[20], (DEPTH, D_MODEL, D_FF), D_MODEL ** -0.5),
        "mlp_w2": nrm(ks[21], (DEPTH, D_FF, D_MODEL), D_FF ** -0.5),
        "final_g": 1.0 + nrm(ks[22], (D_MODEL,), 0.02),
    }


def reference(x, c, ln_g, ada_w, ada_b, ssm_lam_re, ssm_lam_im, ssm_log_dt, ssm_b_re, ssm_b_im,
              ssm_c_re, ssm_c_im, ssm_d, ssm_w_glu, kv_g, kv_ada_w, kv_ada_b, w_kv,
              attn_w_q, attn_w_o, mlp_w1, mlp_w2, final_g):
    h = x
    k_bands, v_bands = None, None
    for layer in range(DEPTH):
        if layer == N_A_LAYERS:
            k_bands, v_bands = shared_kv(h, c, kv_g, kv_ada_w, kv_ada_b, w_kv)
        shift, scale, gate = ada_chunks(c, ada_w[layer, 0], ada_b[layer, 0], 3)
        u = rms_norm(h, ln_g[layer, 0]) * (1.0 + scale) + shift
        if layer < N_A_LAYERS:
            y = s5_mixer(u, ssm_lam_re[layer], ssm_lam_im[layer], ssm_log_dt[layer], ssm_b_re[layer],
                         ssm_b_im[layer], ssm_c_re[layer], ssm_c_im[layer], ssm_d[layer], ssm_w_glu[layer])
        else:
            j = layer - N_A_LAYERS
            y = dilated_mixer(u, attn_w_q[j], k_bands, v_bands, attn_w_o[j])
        h = h + gate * y
        shift, scale, gate = ada_chunks(c, ada_w[layer, 1], ada_b[layer, 1], 3)
        u = rms_norm(h, ln_g[layer, 1]) * (1.0 + scale) + shift
        h = h + gate * (jnp.square(jax.nn.relu(u @ mlp_w1[layer])) @ mlp_w2[layer])
    return rms_norm(h, final_g)
```

```python
import functools

import jax
import jax.numpy as jnp
from jax import lax
from jax.experimental import pallas as pl
from jax.experimental.pallas import tpu as pltpu

F32 = jnp.float32
BF16 = jnp.bfloat16

D_MODEL = 1024
SSM_GROUP = 16
SSM_GROUPS = D_MODEL // SSM_GROUP
SSM_STATE = 64
HEAD_DIM = 64
N_HEADS = D_MODEL // HEAD_DIM
DILATIONS = (1, 4, 16)
N_BRANCHES = len(DILATIONS)
ATT_BLOCK = 128
D_FF = 4 * D_MODEL
EPS = 1e-6
NEG = -1e30

LANES = 128
CHUNK = 16
GROUP_COLS = CHUNK * SSM_GROUP
PAIR_COLS = 2 * GROUP_COLS
N_PAIRS = SSM_GROUPS // 2
STATE_COLS = 2 * SSM_STATE * SSM_GROUPS
VMEM_LIMIT = 56 * 1024 * 1024

HIGHEST = lax.Precision.HIGHEST


def _cparams(sem):
    return pltpu.CompilerParams(dimension_semantics=sem, vmem_limit_bytes=VMEM_LIMIT)


def _resident(shape, index_map):
    return pl.BlockSpec(shape, index_map, pipeline_mode=pl.Buffered(1))


def _normmod(x, g, shift, scale):
    ms = jnp.mean(x * x, axis=-1, keepdims=True)
    return (x * lax.rsqrt(ms + EPS) * g) * (1.0 + scale) + shift


def _ada_kernel(c_ref, w_ref, b_ref, o_ref):
    c = c_ref[...]
    sc = c * jax.nn.sigmoid(c)
    o_ref[...] = jnp.dot(sc, w_ref[...], preferred_element_type=F32, precision=HIGHEST) + b_ref[...]


def _ada_mods(c, w, b):
    n, d, width = w.shape
    bsz = c.shape[0]
    tn = 1024
    return pl.pallas_call(
        _ada_kernel,
        grid=(n, width // tn),
        in_specs=[
            pl.BlockSpec((bsz, d), lambda i, j: (0, 0)),
            pl.BlockSpec((None, d, tn), lambda i, j: (i, 0, j)),
            pl.BlockSpec((None, 1, tn), lambda i, j: (i, 0, j)),
        ],
        out_specs=pl.BlockSpec((None, bsz, tn), lambda i, j: (i, 0, j)),
        out_shape=jax.ShapeDtypeStruct((n, bsz, width), F32),
        compiler_params=_cparams(("parallel", "parallel")),
        name="ada_mods",
    )(c, w, b.reshape(n, 1, width))


def _proj_kernel(h_ref, mod_ref, g_ref, w_ref, o_ref):
    d = D_MODEL
    u = _normmod(h_ref[...], g_ref[...], mod_ref[:, :d], mod_ref[:, d:2 * d]).astype(BF16)
    for n in range(o_ref.shape[1] // d):
        o_ref[:, n * d:(n + 1) * d] = jnp.dot(
            u, w_ref[:, n * d:(n + 1) * d], preferred_element_type=F32).astype(o_ref.dtype)


def _norm_proj(h, mod, g, w, tm=512):
    bsz, seq, d = h.shape
    width = w.shape[1]
    per_b = seq // tm
    return pl.pallas_call(
        _proj_kernel,
        grid=(bsz * per_b,),
        in_specs=[
            pl.BlockSpec((None, tm, d), lambda i: (i // per_b, i % per_b, 0)),
            pl.BlockSpec((None, 1, mod.shape[-1]), lambda i: (i // per_b, 0, 0)),
            pl.BlockSpec((1, d), lambda i: (0, 0)),
            _resident((d, width), lambda i: (0, 0)),
        ],
        out_specs=pl.BlockSpec((None, tm, width), lambda i: (i // per_b, i % per_b, 0)),
        out_shape=jax.ShapeDtypeStruct((bsz, seq, width), BF16),
        compiler_params=_cparams(("parallel",)),
        name="norm_proj",
    )(h, mod, g.reshape(1, d), w)


def _attn_kernel(*refs, carry):
    if carry:
        q_ref, kc_ref, kp_ref, vc_ref, vp_ref, op_ref, lp_ref, o_ref, l_ref = refs
    else:
        q_ref, kc_ref, kp_ref, vc_ref, vp_ref, o_ref, l_ref = refs
    blk = ATT_BLOCK
    has_prev = pl.program_id(2) > 0
    lane = lax.broadcasted_iota(jnp.int32, (blk, LANES), 1)
    qi = lax.broadcasted_iota(jnp.int32, (blk, blk), 0)
    kj = lax.broadcasted_iota(jnp.int32, (blk, blk), 1)
    mask_c = kj <= qi
    mask_p = jnp.logical_and(kj >= qi, has_prev)
    nt = (((1,), (1,)), ((), ()))
    lse_tile = jnp.zeros((blk, LANES), F32)
    for hp in range(N_HEADS // 2):
        sl = slice(hp * LANES, (hp + 1) * LANES)
        q = q_ref[:, sl]
        kc, kp, vc, vp = kc_ref[:, sl], kp_ref[:, sl], vc_ref[:, sl], vp_ref[:, sl]
        o_heads = []
        for hh in range(2):
            head = 2 * hp + hh
            hmask = (lane >= HEAD_DIM) if hh else (lane < HEAD_DIM)
            qh = jnp.where(hmask, q, jnp.zeros_like(q)) * jnp.asarray(HEAD_DIM ** -0.5, BF16)
            sc = jnp.where(mask_c, lax.dot_general(qh, kc, nt, preferred_element_type=F32), NEG)
            sp = jnp.where(mask_p, lax.dot_general(qh, kp, nt, preferred_element_type=F32), NEG)
            m = jnp.maximum(jnp.max(sc, axis=-1, keepdims=True), jnp.max(sp, axis=-1, keepdims=True))
            if carry:
                lprev = lp_ref[:, head:head + 1]
                m = jnp.maximum(m, lprev)
            pc = jnp.exp(sc - m)
            pp = jnp.exp(sp - m)
            den = jnp.sum(pc, axis=-1, keepdims=True) + jnp.sum(pp, axis=-1, keepdims=True)
            acc = (jnp.dot(pc.astype(BF16), vc, preferred_element_type=F32)
                   + jnp.dot(pp.astype(BF16), vp, preferred_element_type=F32))
            if carry:
                wprev = jnp.exp(lprev - m)
                den = den + wprev
                acc = acc + wprev * op_ref[:, sl]
            o_heads.append(acc / den)
            lse_tile = jnp.where(lane == head, m + jnp.log(den), lse_tile)
        o_ref[:, sl] = jnp.where(lane < HEAD_DIM, o_heads[0], o_heads[1]).astype(o_ref.dtype)
    l_ref[...] = lse_tile


def _attn_pass(q, kv, branch, carry):
    bsz, seq, _ = q.shape
    d = D_MODEL
    dil = DILATIONS[branch]
    sub = seq // dil
    nb = sub // ATT_BLOCK
    blk = ATT_BLOCK
    qv = q.reshape(bsz, sub, dil * N_BRANCHES * d)
    kvv = kv.reshape(bsz, sub, dil * 2 * N_BRANCHES * d)
    nq, nkv = N_BRANCHES, 2 * N_BRANCHES

    def prev(n):
        return jnp.maximum(n - 1, 0)

    in_specs = [
        pl.BlockSpec((None, blk, d), lambda b, r, n: (b, n, r * nq + branch)),
        pl.BlockSpec((None, blk, d), lambda b, r, n: (b, n, r * nkv + branch)),
        pl.BlockSpec((None, blk, d), lambda b, r, n: (b, prev(n), r * nkv + branch)),
        pl.BlockSpec((None, blk, d), lambda b, r, n: (b, n, r * nkv + N_BRANCHES + branch)),
        pl.BlockSpec((None, blk, d), lambda b, r, n: (b, prev(n), r * nkv + N_BRANCHES + branch)),
    ]
    args = [qv, kvv, kvv, kvv, kvv]
    o_spec = pl.BlockSpec((None, blk, d), lambda b, r, n: (b, n, r))
    l_spec = pl.BlockSpec((None, blk, LANES), lambda b, r, n: (b, n, r))
    if carry is not None:
        in_specs += [o_spec, l_spec]
        args += [carry[0].reshape(bsz, sub, dil * d), carry[1].reshape(bsz, sub, dil * LANES)]
    o, lse = pl.pallas_call(
        functools.partial(_attn_kernel, carry=carry is not None),
        grid=(bsz, dil, nb),
        in_specs=in_specs,
        out_specs=[o_spec, l_spec],
        out_shape=[jax.ShapeDtypeStruct((bsz, sub, dil * d), F32),
                   jax.ShapeDtypeStruct((bsz, sub, dil * LANES), F32)],
        compiler_params=_cparams(("parallel", "parallel", "arbitrary")),
        name=f"attn_branch{branch}",
    )(*args)
    return o.reshape(bsz, seq, d), lse.reshape(bsz, seq, LANES)


def _regroup8(tiles):
    t = list(tiles)
    lane = lax.broadcasted_iota(jnp.int32, t[0].shape, 1)
    piece = lane // SSM_GROUP
    for dist in (4, 2, 1):
        bit = (piece & dist) != 0
        shift = dist * SSM_GROUP
        for a in range(8):
            if a & dist:
                continue
            b = a | dist
            ta, tb = t[a], t[b]
            t[a] = jnp.where(bit, pltpu.roll(tb, shift, 1), ta)
            t[b] = jnp.where(bit, tb, pltpu.roll(ta, LANES - shift, 1))
    return t


def _s5_pre_kernel(h_ref, mod_ref, g_ref, z_ref, u_scr):
    d = D_MODEL
    g, shift, scale = g_ref[...], mod_ref[:, :d], mod_ref[:, d:2 * d]
    for t in range(CHUNK):
        u_scr[:, t * d:(t + 1) * d] = _normmod(h_ref[:, t * d:(t + 1) * d], g, shift, scale)
    for j in range(d // LANES):
        for half in range(CHUNK // 8):
            tiles = [u_scr[:, (8 * half + tl) * d + j * LANES:(8 * half + tl) * d + (j + 1) * LANES]
                     for tl in range(8)]
            outs = _regroup8(tiles)
            for gl in range(8):
                col = (8 * j + gl) * GROUP_COLS + half * LANES
                z_ref[:, col:col + LANES] = outs[gl].astype(z_ref.dtype)


def _s5_pre(hc, mod, g, rows=32):
    nrow, width = hc.shape
    per_b = nrow // mod.shape[0] // rows
    return pl.pallas_call(
        _s5_pre_kernel,
        grid=(nrow // rows,),
        in_specs=[
            pl.BlockSpec((rows, width), lambda i: (i, 0)),
            pl.BlockSpec((None, 1, mod.shape[-1]), lambda i: (i // per_b, 0, 0)),
            pl.BlockSpec((1, D_MODEL), lambda i: (0, 0)),
        ],
        out_specs=pl.BlockSpec((rows, width), lambda i: (i, 0)),
        out_shape=jax.ShapeDtypeStruct((nrow, width), BF16),
        scratch_shapes=[pltpu.VMEM((rows, width), F32)],
        compiler_params=_cparams(("parallel",)),
        name="s5_pre",
    )(hc, mod, g.reshape(1, D_MODEL))


def _s5_in_kernel(z_ref, w_ref, s_ref):
    gc = GROUP_COLS
    s_ref[...] = (jnp.dot(z_ref[:, :gc], w_ref[0], preferred_element_type=F32)
                  + jnp.dot(z_ref[:, gc:], w_ref[1], preferred_element_type=F32))


def _s5_state_in(z, w_in):
    nrow = z.shape[0]
    return pl.pallas_call(
        _s5_in_kernel,
        grid=(N_PAIRS,),
        in_specs=[
            pl.BlockSpec((nrow, PAIR_COLS), lambda k: (0, k)),
            pl.BlockSpec((2, GROUP_COLS, GROUP_COLS), lambda k: (k, 0, 0)),
        ],
        out_specs=pl.BlockSpec((nrow, GROUP_COLS), lambda k: (0, k)),
        out_shape=jax.ShapeDtypeStruct((nrow, STATE_COLS), F32),
        compiler_params=_cparams(("parallel",)),
        name="s5_state_in",
    )(z, w_in)


def _s5_scan_kernel(s_ref, dec_ref, x_ref, st_ref):
    @pl.when(pl.program_id(0) == 0)
    def _():
        st_ref[...] = jnp.zeros_like(st_ref)

    x_ref[...] = st_ref[...].astype(x_ref.dtype)
    for k in range(N_PAIRS):
        re = slice(k * GROUP_COLS, k * GROUP_COLS + LANES)
        im = slice(k * GROUP_COLS + LANES, (k + 1) * GROUP_COLS)
        xr, xi = st_ref[:, re], st_ref[:, im]
        ar, ai = dec_ref[:, re], dec_ref[:, im]
        st_ref[:, re] = ar * xr - ai * xi + s_ref[:, re]
        st_ref[:, im] = ar * xi + ai * xr + s_ref[:, im]


def _s5_scan(s, dec, bsz):
    nrow = s.shape[0]
    nchunk = nrow // bsz
    sv = s.reshape(bsz, nchunk * STATE_COLS)
    x = pl.pallas_call(
        _s5_scan_kernel,
        grid=(nchunk,),
        in_specs=[
            pl.BlockSpec((bsz, STATE_COLS), lambda c: (0, c)),
            pl.BlockSpec((1, STATE_COLS), lambda c: (0, 0)),
        ],
        out_specs=pl.BlockSpec((bsz, STATE_COLS), lambda c: (0, c)),
        out_shape=jax.ShapeDtypeStruct((bsz, nchunk * STATE_COLS), BF16),
        scratch_shapes=[pltpu.VMEM((bsz, STATE_COLS), F32)],
        compiler_params=_cparams(("arbitrary",)),
        name="s5_scan",
    )(sv, dec)
    return x.reshape(nrow, STATE_COLS)


def _s5_out_kernel(z_ref, x_ref, wt_ref, wx_ref, d_ref, o_ref):
    gc = GROUP_COLS
    x = x_ref[...]
    for gi in range(2):
        z = z_ref[:, gi * gc:(gi + 1) * gc]
        y = (jnp.dot(z, wt_ref[gi], preferred_element_type=F32)
             + jnp.dot(x, wx_ref[gi], preferred_element_type=F32))
        y = y + d_ref[:, gi * gc:(gi + 1) * gc] * z.astype(F32)
        o_ref[:, gi * gc:(gi + 1) * gc] = jax.nn.gelu(y).astype(o_ref.dtype)


def _s5_out(z, x, w_toep, w_x, d_perm):
    nrow, width = z.shape
    return pl.pallas_call(
        _s5_out_kernel,
        grid=(N_PAIRS,),
        in_specs=[
            pl.BlockSpec((nrow, PAIR_COLS), lambda k: (0, k)),
            pl.BlockSpec((nrow, GROUP_COLS), lambda k: (0, k)),
            pl.BlockSpec((2, GROUP_COLS, GROUP_COLS), lambda k: (k, 0, 0)),
            pl.BlockSpec((2, GROUP_COLS, GROUP_COLS), lambda k: (k, 0, 0)),
            pl.BlockSpec((1, PAIR_COLS), lambda k: (0, k)),
        ],
        out_specs=pl.BlockSpec((nrow, PAIR_COLS), lambda k: (0, k)),
        out_shape=jax.ShapeDtypeStruct((nrow, width), BF16),
        compiler_params=_cparams(("parallel",)),
        name="s5_out",
    )(z, x, w_toep, w_x, d_perm)


def _s5_weights(lam_re, lam_im, log_dt, b_re, b_im, c_re, c_im, d_skip):
    g, p, c16 = SSM_GROUPS, SSM_STATE, SSM_GROUP
    lam = lax.complex(lam_re.astype(F32), lam_im.astype(F32))
    dt = jnp.exp(log_dt.astype(F32))[:, None]
    steps = jnp.arange(CHUNK + 1, dtype=F32)
    apow = jnp.exp((lam * dt)[None] * steps[:, None, None])
    a = apow[1]
    bbar = ((a - 1.0) / lam)[..., None] * lax.complex(b_re.astype(F32), b_im.astype(F32))
    cmat = lax.complex(c_re.astype(F32), c_im.astype(F32))
    odd = (jnp.arange(g) % 2)[:, None, None]

    win = apow[CHUNK - 1::-1][:CHUNK][..., None] * bbar[None]
    win = jnp.transpose(win, (1, 0, 3, 2)).reshape(g, GROUP_COLS, p)
    zeros = jnp.zeros_like(win.real)
    w_in = jnp.concatenate([
        jnp.where(odd == 0, win.real, zeros), jnp.where(odd == 1, win.real, zeros),
        jnp.where(odd == 0, win.imag, zeros), jnp.where(odd == 1, win.imag, zeros)], axis=-1)

    kern = jnp.einsum('gop,kgp,gpi->kgoi', cmat, apow[:CHUNK], bbar, precision=HIGHEST).real
    lag = jnp.arange(CHUNK)[None, :] - jnp.arange(CHUNK)[:, None]
    toep = jnp.where((lag >= 0)[:, :, None, None, None], kern[jnp.clip(lag, 0, CHUNK - 1)], 0.0)
    w_toep = jnp.transpose(toep, (2, 0, 4, 1, 3)).reshape(g, GROUP_COLS, GROUP_COLS)

    cw = cmat[None] * apow[1:, :, None, :]
    cw = jnp.transpose(cw, (1, 3, 0, 2)).reshape(g, p, GROUP_COLS)
    zx = jnp.zeros_like(cw.real)
    w_x = jnp.concatenate([
        jnp.where(odd == 0, cw.real, zx), jnp.where(odd == 1, cw.real, zx),
        jnp.where(odd == 0, -cw.imag, zx), jnp.where(odd == 1, -cw.imag, zx)], axis=1)

    a16 = apow[CHUNK].reshape(N_PAIRS, 2 * p)
    dec = jnp.concatenate([a16.real, a16.imag], axis=-1).reshape(1, STATE_COLS)
    d_perm = jnp.broadcast_to(d_skip.astype(F32).reshape(g, 1, c16), (g, CHUNK, c16)).reshape(1, g * GROUP_COLS)
    return w_in.astype(BF16), w_toep.astype(BF16), w_x.astype(BF16), dec, d_perm


def _mlp_residual(h, ymix, moda_ref, modm_ref, g_ref, w1_ref, w2_ref):
    d = D_MODEL
    h1 = h + moda_ref[:, 2 * d:3 * d] * ymix
    u = _normmod(h1, g_ref[...], modm_ref[:, :d], modm_ref[:, d:2 * d]).astype(BF16)
    acc = jnp.zeros_like(h1)
    for k in range(D_FF // d):
        a = jnp.dot(u, w1_ref[:, k * d:(k + 1) * d], preferred_element_type=F32)
        a = jnp.square(jnp.maximum(a, 0.0)).astype(BF16)
        acc = acc + jnp.dot(a, w2_ref[k * d:(k + 1) * d, :], preferred_element_type=F32)
    return h1 + modm_ref[:, 2 * d:3 * d] * acc


def _final_norm(h, fg_ref):
    ms = jnp.mean(h * h, axis=-1, keepdims=True)
    return h * lax.rsqrt(ms + EPS) * fg_ref[...]


def _post_attn_kernel(*refs, final):
    if final:
        h_ref, y_ref, moda_ref, modm_ref, g_ref, wp_ref, w1_ref, w2_ref, fg_ref, o_ref = refs
    else:
        h_ref, y_ref, moda_ref, modm_ref, g_ref, wp_ref, w1_ref, w2_ref, o_ref = refs
    ymix = jnp.dot(y_ref[...].astype(BF16), wp_ref[...], preferred_element_type=F32)
    h2 = _mlp_residual(h_ref[...], ymix, moda_ref, modm_ref, g_ref, w1_ref, w2_ref)
    o_ref[...] = _final_norm(h2, fg_ref) if final else h2


def _post_s5_kernel(h_ref, z_ref, moda_ref, modm_ref, g_ref, wp_ref, w1_ref, w2_ref, o_ref, h_scr, z_scr):
    d = D_MODEL
    rows = h_ref.shape[0]
    for t in range(CHUNK):
        h_scr[t * rows:(t + 1) * rows, :] = h_ref[:, t * d:(t + 1) * d]
    for j in range(d // LANES):
        for half in range(CHUNK // 8):
            tiles = []
            for gl in range(8):
                col = (8 * j + gl) * GROUP_COLS + half * LANES
                tiles.append(z_ref[:, col:col + LANES].astype(F32))
            outs = _regroup8(tiles)
            for tl in range(8):
                t = 8 * half + tl
                z_scr[t * rows:(t + 1) * rows, j * LANES:(j + 1) * LANES] = outs[tl].astype(BF16)
    zz = jnp.dot(z_scr[...], wp_ref[...], preferred_element_type=F32)
    ymix = zz[:, :d] * jax.nn.sigmoid(zz[:, d:])
    h2 = _mlp_residual(h_scr[...], ymix, moda_ref, modm_ref, g_ref, w1_ref, w2_ref)
    for t in range(CHUNK):
        o_ref[:, t * d:(t + 1) * d] = h2[t * rows:(t + 1) * rows, :]


def _post_common_specs(moda, modm, wp, per_b):
    d = D_MODEL
    return [
        pl.BlockSpec((None, 1, moda.shape[-1]), lambda i: (i // per_b, 0, 0)),
        pl.BlockSpec((None, 1, modm.shape[-1]), lambda i: (i // per_b, 0, 0)),
        pl.BlockSpec((1, d), lambda i: (0, 0)),
        _resident(wp.shape, lambda i: (0, 0)),
        _resident((d, D_FF), lambda i: (0, 0)),
        _resident((D_FF, d), lambda i: (0, 0)),
    ]


def _post_attn(h, o, moda, modm, g2, w_o, w1, w2, final_g, tm=512):
    bsz, seq, d = h.shape
    per_b = seq // tm
    row_spec = pl.BlockSpec((None, tm, d), lambda i: (i // per_b, i % per_b, 0))
    final = final_g is not None
    in_specs = [row_spec, row_spec] + _post_common_specs(moda, modm, w_o, per_b)
    args = [h, o, moda, modm, g2.reshape(1, d), w_o, w1, w2]
    if final:
        in_specs.append(pl.BlockSpec((1, d), lambda i: (0, 0)))
        args.append(final_g.reshape(1, d))
    return pl.pallas_call(
        functools.partial(_post_attn_kernel, final=final),
        grid=(bsz * per_b,),
        in_specs=in_specs,
        out_specs=row_spec,
        out_shape=jax.ShapeDtypeStruct((bsz, seq, d), F32),
        compiler_params=_cparams(("parallel",)),
        name="post_attn_mlp",
    )(*args)


def _post_s5(hc, zc, moda, modm, g2, w_glu, w1, w2, rows=32):
    nrow, width = hc.shape
    per_b = nrow // moda.shape[0] // rows
    row_spec = pl.BlockSpec((rows, width), lambda i: (i, 0))
    return pl.pallas_call(
        _post_s5_kernel,
        grid=(nrow // rows,),
        in_specs=[row_spec, row_spec] + _post_common_specs(moda, modm, w_glu, per_b),
        out_specs=row_spec,
        out_shape=jax.ShapeDtypeStruct((nrow, width), F32),
        scratch_shapes=[pltpu.VMEM((rows * CHUNK, D_MODEL), F32), pltpu.VMEM((rows * CHUNK, D_MODEL), BF16)],
        compiler_params=_cparams(("parallel",)),
        name="post_s5_mlp",
    )(hc, zc, moda, modm, g2.reshape(1, D_MODEL), w_glu, w1, w2)


def kernel(x, c, ln_g, ada_w, ada_b, ssm_lam_re, ssm_lam_im, ssm_log_dt, ssm_b_re, ssm_b_im, ssm_c_re, ssm_c_im, ssm_d, ssm_w_glu, kv_g, kv_ada_w, kv_ada_b, w_kv, attn_w_q, attn_w_o, mlp_w1, mlp_w2, final_g):
    bsz, seq, d = x.shape
    depth = ln_g.shape[0]
    n_s5 = ssm_lam_re.shape[0]
    assert d == D_MODEL and seq % (DILATIONS[-1] * ATT_BLOCK) == 0

    mods = _ada_mods(c, ada_w.reshape(depth * 2, d, 3 * d), ada_b.reshape(depth * 2, 3 * d))
    mods = mods.reshape(depth, 2, bsz, 1, 3 * d)
    kv_mod = _ada_mods(c, kv_ada_w[None], kv_ada_b[None]).reshape(bsz, 1, 2 * d)

    w1 = mlp_w1.astype(BF16)
    w2 = mlp_w2.astype(BF16)

    h = x
    nrow = bsz * seq // CHUNK
    for layer in range(n_s5):
        w_in, w_toep, w_x, dec, d_perm = _s5_weights(
            ssm_lam_re[layer], ssm_lam_im[layer], ssm_log_dt[layer], ssm_b_re[layer], ssm_b_im[layer],
            ssm_c_re[layer], ssm_c_im[layer], ssm_d[layer])
        hc = h.reshape(nrow, CHUNK * d)
        z = _s5_pre(hc, mods[layer, 0], ln_g[layer, 0])
        s = _s5_state_in(z, w_in)
        xs = _s5_scan(s, dec, bsz)
        zo = _s5_out(z, xs, w_toep, w_x, d_perm)
        hc = _post_s5(hc, zo, mods[layer, 0], mods[layer, 1], ln_g[layer, 1],
                      ssm_w_glu[layer].astype(BF16), w1[layer], w2[layer])
        h = hc.reshape(bsz, seq, d)

    kv = _norm_proj(h, kv_mod, kv_g, w_kv.astype(BF16))
    for layer in range(n_s5, depth):
        j = layer - n_s5
        q = _norm_proj(h, mods[layer, 0], ln_g[layer, 0], attn_w_q[j].astype(BF16))
        carry = None
        for branch in range(N_BRANCHES):
            carry = _attn_pass(q, kv, branch, carry)
        h = _post_attn(h, carry[0], mods[layer, 0], mods[layer, 1], ln_g[layer, 1],
                       attn_w_o[j].astype(BF16), w1[layer], w2[layer],
                       final_g if layer == depth - 1 else None)
    return h
```

```python
import functools

import numpy as np

import jax
import jax.numpy as jnp
from jax import lax
from jax.experimental import pallas as pl
from jax.experimental.pallas import tpu as pltpu

F32 = jnp.float32
BF16 = jnp.bfloat16

D_MODEL = 1024
SSM_GROUP = 16
SSM_GROUPS = D_MODEL // SSM_GROUP
SSM_STATE = 64
HEAD_DIM = 64
N_HEADS = D_MODEL // HEAD_DIM
DILATIONS = (1, 4, 16)
N_BRANCHES = len(DILATIONS)
ATT_BLOCK = 128
D_FF = 4 * D_MODEL
EPS = 1e-6
NEG = -1e30

LANES = 128
CHUNK = 16
GROUP_COLS = CHUNK * SSM_GROUP
PAIR_COLS = 2 * GROUP_COLS
N_PAIRS = SSM_GROUPS // 2
S5_TILE_TOKENS = 2 * CHUNK
VMEM_LIMIT = 56 * 1024 * 1024

HIGHEST = lax.Precision.HIGHEST
NT_DIMS = (((1,), (1,)), ((), ()))


def _cparams(sem):
    return pltpu.CompilerParams(dimension_semantics=sem, vmem_limit_bytes=VMEM_LIMIT)


def _resident(shape, index_map):
    return pl.BlockSpec(shape, index_map, pipeline_mode=pl.Buffered(1))


def _normmod(x, g, shift, scale):
    ms = jnp.mean(x * x, axis=-1, keepdims=True)
    return (x * lax.rsqrt(ms + EPS) * g) * (1.0 + scale) + shift


def _ada_kernel(c_ref, w_ref, b_ref, o_ref):
    c = c_ref[...]
    sc = c * jax.nn.sigmoid(c)
    o_ref[...] = jnp.dot(sc, w_ref[...], preferred_element_type=F32, precision=HIGHEST) + b_ref[...]


def _ada_mods(c, w, b):
    n, d, width = w.shape
    bsz = c.shape[0]
    tn = 1024
    return pl.pallas_call(
        _ada_kernel,
        grid=(n, width // tn),
        in_specs=[
            pl.BlockSpec((bsz, d), lambda i, j: (0, 0)),
            pl.BlockSpec((None, d, tn), lambda i, j: (i, 0, j)),
            pl.BlockSpec((None, 1, tn), lambda i, j: (i, 0, j)),
        ],
        out_specs=pl.BlockSpec((None, bsz, tn), lambda i, j: (i, 0, j)),
        out_shape=jax.ShapeDtypeStruct((n, bsz, width), F32),
        compiler_params=_cparams(("parallel", "parallel")),
        name="ada_mods",
    )(c, w, b.reshape(n, 1, width))


def _proj_kernel(h_ref, mod_ref, g_ref, w_ref, o_ref):
    d = D_MODEL
    u = _normmod(h_ref[...], g_ref[...], mod_ref[:, :d], mod_ref[:, d:2 * d]).astype(BF16)
    for n in range(o_ref.shape[1] // d):
        o_ref[:, n * d:(n + 1) * d] = jnp.dot(
            u, w_ref[:, n * d:(n + 1) * d], preferred_element_type=F32).astype(o_ref.dtype)


def _norm_proj(h, mod, g, w, tm=512):
    bsz, seq, d = h.shape
    width = w.shape[1]
    per_b = seq // tm
    return pl.pallas_call(
        _proj_kernel,
        grid=(bsz * per_b,),
        in_specs=[
            pl.BlockSpec((None, tm, d), lambda i: (i // per_b, i % per_b, 0)),
            pl.BlockSpec((None, 1, mod.shape[-1]), lambda i: (i // per_b, 0, 0)),
            pl.BlockSpec((1, d), lambda i: (0, 0)),
            _resident((d, width), lambda i: (0, 0)),
        ],
        out_specs=pl.BlockSpec((None, tm, width), lambda i: (i // per_b, i % per_b, 0)),
        out_shape=jax.ShapeDtypeStruct((bsz, seq, width), BF16),
        compiler_params=_cparams(("parallel",)),
        name="norm_proj",
    )(h, mod, g.reshape(1, d), w)


def _attn_block(qb, kcat, vcat, mask, old):
    blk = ATT_BLOCK
    lane = lax.broadcasted_iota(jnp.int32, (blk, LANES), 1)
    lo = lane < HEAD_DIM
    zero = jnp.zeros_like(qb)
    q2 = jnp.concatenate([jnp.where(lo, qb, zero), jnp.where(lo, zero, qb)], axis=0)
    q2 = q2 * jnp.asarray(HEAD_DIM ** -0.5, BF16)
    s = lax.dot_general(q2, kcat, NT_DIMS, preferred_element_type=F32)
    s = jnp.where(mask, s, NEG)
    ms, dens, ps, alphas = [], [], [], []
    for hh in range(2):
        sh = s[hh * blk:(hh + 1) * blk, :]
        m = jnp.max(sh, axis=-1, keepdims=True)
        if old is not None:
            m_old = old[0][:, hh:hh + 1]
            m = jnp.maximum(m, m_old)
        p = jnp.exp(sh - m)
        den = jnp.sum(p, axis=-1, keepdims=True)
        if old is not None:
            alpha = jnp.exp(m_old - m)
            den = alpha * old[0][:, 2 + hh:3 + hh] + den
            alphas.append(alpha)
        ms.append(m)
        dens.append(den)
        ps.append(p.astype(BF16))
    pv = jnp.dot(jnp.concatenate(ps, axis=0), vcat, preferred_element_type=F32)
    acc = jnp.where(lo, pv[:blk], pv[blk:])
    if old is not None:
        acc = jnp.where(lo, alphas[0], alphas[1]) * old[1] + acc
    stats = jnp.where(lane == 0, ms[0], jnp.where(lane == 1, ms[1], jnp.where(lane == 2, dens[0], dens[1])))
    return stats, acc


def _attn_kernel(q1, q2, q3, k1, k2, k3, v1, v2, v3, o_ref, qf2, kf2, vf2, qf3, kf3, vf3, acc_ref, st_ref):
    blk = ATT_BLOCK
    seq = q1.shape[0]
    nblk = seq // blk
    qq = lax.broadcasted_iota(jnp.int32, (2 * blk, 2 * blk), 0) & (blk - 1)
    kk = lax.broadcasted_iota(jnp.int32, (2 * blk, 2 * blk), 1)
    mask_pc = jnp.logical_or(jnp.logical_and(kk < blk, kk >= qq), jnp.logical_and(kk >= blk, kk - blk <= qq))
    mask_c = (lax.broadcasted_iota(jnp.int32, (2 * blk, blk), 1)
              <= (lax.broadcasted_iota(jnp.int32, (2 * blk, blk), 0) & (blk - 1)))

    st, acc = _attn_block(q1[0:blk, :], k1[0:blk, :], v1[0:blk, :], mask_c, None)
    st_ref[0:blk, :] = st
    acc_ref[0:blk, :] = acc

    def b0_body(n, carry):
        r0 = pl.multiple_of(n * blk, blk)
        rp = pl.multiple_of((n - 1) * blk, blk)
        st, acc = _attn_block(q1[pl.ds(r0, blk), :], k1[pl.ds(rp, 2 * blk), :], v1[pl.ds(rp, 2 * blk), :],
                              mask_pc, None)
        st_ref[pl.ds(r0, blk), :] = st
        acc_ref[pl.ds(r0, blk), :] = acc
        return carry

    lax.fori_loop(1, nblk, b0_body, 0)

    for src, dst in ((q2, qf2), (k2, kf2), (v2, vf2), (q3, qf3), (k3, kf3), (v3, vf3)):
        dst[...] = src[...].astype(F32)

    def dilated_block(qf, kf, vf, dil, r, n):
        start = r + dil * blk * n
        rows = pl.ds(start, blk, stride=dil)
        qb = qf[rows, :].astype(BF16)
        if n == 0:
            krows, mask = rows, mask_c
        else:
            krows, mask = pl.ds(r + dil * blk * (n - 1), 2 * blk, stride=dil), mask_pc
        st, acc = _attn_block(qb, kf[krows, :].astype(BF16), vf[krows, :].astype(BF16), mask,
                              (st_ref[rows, :], acc_ref[rows, :]))
        st_ref[rows, :] = st
        acc_ref[rows, :] = acc

    def b1_body(r, carry):
        dil = DILATIONS[1]
        for n in range(seq // dil // blk):
            dilated_block(qf2, kf2, vf2, dil, r, n)
        return carry

    lax.fori_loop(0, DILATIONS[1], b1_body, 0)

    def b2_body(r, carry):
        dil = DILATIONS[2]
        for n in range(seq // dil // blk):
            dilated_block(qf3, kf3, vf3, dil, r, n)
        return carry

    lax.fori_loop(0, DILATIONS[2], b2_body, 0)

    lane = lax.broadcasted_iota(jnp.int32, (seq, LANES), 1)
    st = st_ref[...]
    den = jnp.where(lane < HEAD_DIM, st[:, 2:3], st[:, 3:4])
    o_ref[...] = (acc_ref[...] / den).astype(o_ref.dtype)


def _attention(q, kv):
    bsz, seq, _ = q.shape
    pairs = D_MODEL // LANES

    def slab(col0):
        return pl.BlockSpec((None, seq, LANES), lambda b, hp: (b, 0, col0 + hp))

    in_specs = ([slab(i * pairs) for i in range(N_BRANCHES)]
                + [slab(i * pairs) for i in range(N_BRANCHES)]
                + [slab((N_BRANCHES + i) * pairs) for i in range(N_BRANCHES)])
    return pl.pallas_call(
        _attn_kernel,
        grid=(bsz, pairs),
        in_specs=in_specs,
        out_specs=slab(0),
        out_shape=jax.ShapeDtypeStruct((bsz, seq, D_MODEL), BF16),
        scratch_shapes=[pltpu.VMEM((seq, LANES), F32) for _ in range(8)],
        compiler_params=_cparams(("parallel", "parallel")),
        name="attention",
    )(q, q, q, kv, kv, kv, kv, kv, kv)


def _regroup8(tiles):
    t = list(tiles)
    lane = lax.broadcasted_iota(jnp.int32, t[0].shape, 1)
    piece = lane // SSM_GROUP
    for dist in (4, 2, 1):
        bit = (piece & dist) != 0
        shift = dist * SSM_GROUP
        for a in range(8):
            if a & dist:
                continue
            b = a | dist
            ta, tb = t[a], t[b]
            t[a] = jnp.where(bit, pltpu.roll(tb, shift, 1), ta)
            t[b] = jnp.where(bit, tb, pltpu.roll(ta, LANES - shift, 1))
    return t


def _s5_tile_perm(bsz):
    tt = S5_TILE_TOKENS
    n = bsz * tt
    perm = np.zeros((n, n), np.float32)
    for b in range(bsz):
        for c in range(tt // CHUNK):
            for t in range(CHUNK):
                perm[(t * (tt // CHUNK) + c) * bsz + b, b * tt + c * CHUNK + t] = 1.0
    return perm


def _s5_pre_kernel(h_ref, mod_ref, g_ref, perm_ref, z_ref, u_scr, up_scr):
    d = D_MODEL
    bsz, tt, _ = h_ref.shape
    g = g_ref[...]
    for b in range(bsz):
        u_scr[b * tt:(b + 1) * tt, :] = _normmod(
            h_ref[b], g, mod_ref[b, :, :d], mod_ref[b, :, d:2 * d]).astype(BF16)
    up_scr[...] = jnp.dot(perm_ref[...], u_scr[...], preferred_element_type=F32)
    rows = z_ref.shape[0]
    for j in range(d // LANES):
        for half in range(CHUNK // 8):
            tiles = [up_scr[(8 * half + tl) * rows:(8 * half + tl + 1) * rows, j * LANES:(j + 1) * LANES]
                     for tl in range(8)]
            outs = _regroup8(tiles)
            for gl in range(8):
                col = (8 * j + gl) * GROUP_COLS + half * LANES
                z_ref[:, col:col + LANES] = outs[gl].astype(z_ref.dtype)


def _s5_pre(h, mod, g, perm):
    bsz, seq, d = h.shape
    tt = S5_TILE_TOKENS
    rows = bsz * tt // CHUNK
    return pl.pallas_call(
        _s5_pre_kernel,
        grid=(seq // tt,),
        in_specs=[
            pl.BlockSpec((bsz, tt, d), lambda i: (0, i, 0)),
            pl.BlockSpec(mod.shape, lambda i: (0, 0, 0)),
            pl.BlockSpec((1, d), lambda i: (0, 0)),
            pl.BlockSpec(perm.shape, lambda i: (0, 0)),
        ],
        out_specs=pl.BlockSpec((rows, CHUNK * d), lambda i: (i, 0)),
        out_shape=jax.ShapeDtypeStruct((bsz * seq // CHUNK, CHUNK * d), BF16),
        scratch_shapes=[pltpu.VMEM((bsz * tt, d), BF16), pltpu.VMEM((bsz * tt, d), F32)],
        compiler_params=_cparams(("parallel",)),
        name="s5_pre",
    )(h, mod, g.reshape(1, d), perm)


def _s5_core_kernel(z_ref, win_ref, wt_ref, wx_ref, dec_ref, d_ref, o_ref, s_scr, x_scr, *, bsz):
    gc = GROUP_COLS
    nchunk = z_ref.shape[0] // bsz
    s_scr[...] = (jnp.dot(z_ref[:, :gc], win_ref[0], preferred_element_type=F32)
                  + jnp.dot(z_ref[:, gc:], win_ref[1], preferred_element_type=F32))
    ar, ai = dec_ref[:, :LANES], dec_ref[:, LANES:]

    def step(c, carry):
        xr, xi = carry
        rows = pl.ds(pl.multiple_of(c * bsz, bsz), bsz)
        x_scr[rows, :LANES] = xr
        x_scr[rows, LANES:] = xi
        return (ar * xr - ai * xi + s_scr[rows, :LANES], ar * xi + ai * xr + s_scr[rows, LANES:])

    zero = jnp.zeros((bsz, LANES), F32)
    lax.fori_loop(0, nchunk, step, (zero, zero))
    x = x_scr[...].astype(BF16)
    for gi in range(2):
        z = z_ref[:, gi * gc:(gi + 1) * gc]
        y = (jnp.dot(z, wt_ref[gi], preferred_element_type=F32)
             + jnp.dot(x, wx_ref[gi], preferred_element_type=F32))
        y = y + d_ref[:, gi * gc:(gi + 1) * gc] * z.astype(F32)
        o_ref[:, gi * gc:(gi + 1) * gc] = jax.nn.gelu(y).astype(o_ref.dtype)


def _s5_core(z, w_in, w_toep, w_x, dec, d_perm, bsz):
    nrow, width = z.shape
    wspec = pl.BlockSpec((2, GROUP_COLS, GROUP_COLS), lambda k: (k, 0, 0))
    return pl.pallas_call(
        functools.partial(_s5_core_kernel, bsz=bsz),
        grid=(N_PAIRS,),
        in_specs=[
            pl.BlockSpec((nrow, PAIR_COLS), lambda k: (0, k)),
            wspec, wspec, wspec,
            pl.BlockSpec((1, GROUP_COLS), lambda k: (0, k)),
            pl.BlockSpec((1, PAIR_COLS), lambda k: (0, k)),
        ],
        out_specs=pl.BlockSpec((nrow, PAIR_COLS), lambda k: (0, k)),
        out_shape=jax.ShapeDtypeStruct((nrow, width), BF16),
        scratch_shapes=[pltpu.VMEM((nrow, GROUP_COLS), F32), pltpu.VMEM((nrow, GROUP_COLS), F32)],
        compiler_params=_cparams(("parallel",)),
        name="s5_core",
    )(z, w_in, w_toep, w_x, dec, d_perm)


def _s5_weights(lam_re, lam_im, log_dt, b_re, b_im, c_re, c_im, d_skip):
    g, p, c16 = SSM_GROUPS, SSM_STATE, SSM_GROUP
    lam = lax.complex(lam_re.astype(F32), lam_im.astype(F32))
    dt = jnp.exp(log_dt.astype(F32))[:, None]
    steps = jnp.arange(CHUNK + 1, dtype=F32)
    apow = jnp.exp((lam * dt)[None] * steps[:, None, None])
    a = apow[1]
    bbar = ((a - 1.0) / lam)[..., None] * lax.complex(b_re.astype(F32), b_im.astype(F32))
    cmat = lax.complex(c_re.astype(F32), c_im.astype(F32))
    odd = (jnp.arange(g) % 2)[:, None, None]

    win = apow[CHUNK - 1::-1][..., None] * bbar[None]
    win = jnp.transpose(win, (1, 0, 3, 2)).reshape(g, GROUP_COLS, p)
    zeros = jnp.zeros_like(win.real)
    w_in = jnp.concatenate([
        jnp.where(odd == 0, win.real, zeros), jnp.where(odd == 1, win.real, zeros),
        jnp.where(odd == 0, win.imag, zeros), jnp.where(odd == 1, win.imag, zeros)], axis=-1)

    kern = jnp.einsum('gop,kgp,gpi->kgoi', cmat, apow[:CHUNK], bbar, precision=HIGHEST).real
    lag = jnp.arange(CHUNK)[None, :] - jnp.arange(CHUNK)[:, None]
    toep = jnp.where((lag >= 0)[:, :, None, None, None], kern[jnp.clip(lag, 0, CHUNK - 1)], 0.0)
    w_toep = jnp.transpose(toep, (2, 0, 4, 1, 3)).reshape(g, GROUP_COLS, GROUP_COLS)

    cw = cmat[None] * apow[1:, :, None, :]
    cw = jnp.transpose(cw, (1, 3, 0, 2)).reshape(g, p, GROUP_COLS)
    zx = jnp.zeros_like(cw.real)
    w_x = jnp.concatenate([
        jnp.where(odd == 0, cw.real, zx), jnp.where(odd == 1, cw.real, zx),
        jnp.where(odd == 0, -cw.imag, zx), jnp.where(odd == 1, -cw.imag, zx)], axis=1)

    a16 = apow[CHUNK].reshape(N_PAIRS, 2 * p)
    dec = jnp.concatenate([a16.real, a16.imag], axis=-1).reshape(1, N_PAIRS * GROUP_COLS)
    d_perm = jnp.broadcast_to(d_skip.astype(F32).reshape(g, 1, c16), (g, CHUNK, c16)).reshape(1, g * GROUP_COLS)
    return w_in.astype(BF16), w_toep.astype(BF16), w_x.astype(BF16), dec, d_perm


def _mlp(u, w1_ref, w2_ref):
    d = D_MODEL
    acc = None
    for k in range(D_FF // d):
        a = jnp.dot(u, w1_ref[:, k * d:(k + 1) * d], preferred_element_type=F32)
        a = jnp.square(jnp.maximum(a, 0.0)).astype(BF16)
        part = jnp.dot(a, w2_ref[k * d:(k + 1) * d, :], preferred_element_type=F32)
        acc = part if acc is None else acc + part
    return acc


def _final_norm(h, fg_ref):
    ms = jnp.mean(h * h, axis=-1, keepdims=True)
    return h * lax.rsqrt(ms + EPS) * fg_ref[...]


def _post_attn_kernel(*refs, final):
    if final:
        h_ref, y_ref, moda_ref, modm_ref, g_ref, wp_ref, w1_ref, w2_ref, fg_ref, o_ref = refs
    else:
        h_ref, y_ref, moda_ref, modm_ref, g_ref, wp_ref, w1_ref, w2_ref, o_ref = refs
    d = D_MODEL
    ymix = jnp.dot(y_ref[...], wp_ref[...], preferred_element_type=F32)
    h1 = h_ref[...] + moda_ref[:, 2 * d:3 * d] * ymix
    u = _normmod(h1, g_ref[...], modm_ref[:, :d], modm_ref[:, d:2 * d]).astype(BF16)
    h2 = h1 + modm_ref[:, 2 * d:3 * d] * _mlp(u, w1_ref, w2_ref)
    o_ref[...] = _final_norm(h2, fg_ref) if final else h2


def _post_attn(h, o, moda, modm, g2, w_o, w1, w2, final_g, tm=512):
    bsz, seq, d = h.shape
    per_b = seq // tm
    row_spec = pl.BlockSpec((None, tm, d), lambda i: (i // per_b, i % per_b, 0))
    final = final_g is not None
    in_specs = [
        row_spec, row_spec,
        pl.BlockSpec((None, 1, moda.shape[-1]), lambda i: (i // per_b, 0, 0)),
        pl.BlockSpec((None, 1, modm.shape[-1]), lambda i: (i // per_b, 0, 0)),
        pl.BlockSpec((1, d), lambda i: (0, 0)),
        _resident(w_o.shape, lambda i: (0, 0)),
        _resident((d, D_FF), lambda i: (0, 0)),
        _resident((D_FF, d), lambda i: (0, 0)),
    ]
    args = [h, o, moda, modm, g2.reshape(1, d), w_o, w1, w2]
    if final:
        in_specs.append(pl.BlockSpec((1, d), lambda i: (0, 0)))
        args.append(final_g.reshape(1, d))
    return pl.pallas_call(
        functools.partial(_post_attn_kernel, final=final),
        grid=(bsz * per_b,),
        in_specs=in_specs,
        out_specs=row_spec,
        out_shape=jax.ShapeDtypeStruct((bsz, seq, d), F32),
        compiler_params=_cparams(("parallel",)),
        name="post_attn_mlp",
    )(*args)


def _post_s5_kernel(h_ref, z_ref, moda_ref, modm_ref, g_ref, permt_ref, wp_ref, w1_ref, w2_ref, o_ref,
                    z_scr, h1_scr, u_scr):
    d = D_MODEL
    bsz, tt, _ = h_ref.shape
    rows = z_ref.shape[0]
    for j in range(d // LANES):
        for half in range(CHUNK // 8):
            tiles = []
            for gl in range(8):
                col = (8 * j + gl) * GROUP_COLS + half * LANES
                tiles.append(z_ref[:, col:col + LANES].astype(F32))
            outs = _regroup8(tiles)
            for tl in range(8):
                t = 8 * half + tl
                z_scr[t * rows:(t + 1) * rows, j * LANES:(j + 1) * LANES] = outs[tl].astype(BF16)
    zn = jnp.dot(permt_ref[...], z_scr[...], preferred_element_type=F32).astype(BF16)
    zz = jnp.dot(zn, wp_ref[...], preferred_element_type=F32)
    ymix = zz[:, :d] * jax.nn.sigmoid(zz[:, d:])
    g = g_ref[...]
    for b in range(bsz):
        sl = slice(b * tt, (b + 1) * tt)
        h1 = h_ref[b] + moda_ref[b, :, 2 * d:3 * d] * ymix[sl, :]
        h1_scr[sl, :] = h1
        u_scr[sl, :] = _normmod(h1, g, modm_ref[b, :, :d], modm_ref[b, :, d:2 * d]).astype(BF16)
    acc = _mlp(u_scr[...], w1_ref, w2_ref)
    for b in range(bsz):
        sl = slice(b * tt, (b + 1) * tt)
        o_ref[b] = h1_scr[sl, :] + modm_ref[b, :, 2 * d:3 * d] * acc[sl, :]


def _post_s5(h, zo, moda, modm, g2, permt, w_glu, w1, w2):
    bsz, seq, d = h.shape
    tt = S5_TILE_TOKENS
    rows = bsz * tt // CHUNK
    h_spec = pl.BlockSpec((bsz, tt, d), lambda i: (0, i, 0))
    return pl.pallas_call(
        _post_s5_kernel,
        grid=(seq // tt,),
        in_specs=[
            h_spec,
            pl.BlockSpec((rows, CHUNK * d), lambda i: (i, 0)),
            pl.BlockSpec(moda.shape, lambda i: (0, 0, 0)),
            pl.BlockSpec(modm.shape, lambda i: (0, 0, 0)),
            pl.BlockSpec((1, d), lambda i: (0, 0)),
            pl.BlockSpec(permt.shape, lambda i: (0, 0)),
            _resident(w_glu.shape, lambda i: (0, 0)),
            _resident((d, D_FF), lambda i: (0, 0)),
            _resident((D_FF, d), lambda i: (0, 0)),
        ],
        out_specs=h_spec,
        out_shape=jax.ShapeDtypeStruct((bsz, seq, d), F32),
        scratch_shapes=[pltpu.VMEM((bsz * tt, d), BF16), pltpu.VMEM((bsz * tt, d), F32),
                        pltpu.VMEM((bsz * tt, d), BF16)],
        compiler_params=_cparams(("parallel",)),
        name="post_s5_mlp",
    )(h, zo, moda, modm, g2.reshape(1, d), permt, w_glu, w1, w2)


def kernel(x, c, ln_g, ada_w, ada_b, ssm_lam_re, ssm_lam_im, ssm_log_dt, ssm_b_re, ssm_b_im, ssm_c_re, ssm_c_im, ssm_d, ssm_w_glu, kv_g, kv_ada_w, kv_ada_b, w_kv, attn_w_q, attn_w_o, mlp_w1, mlp_w2, final_g):
    bsz, seq, d = x.shape
    depth = ln_g.shape[0]
    n_s5 = ssm_lam_re.shape[0]
    assert d == D_MODEL and seq % (DILATIONS[-1] * ATT_BLOCK) == 0

    mods = _ada_mods(c, ada_w.reshape(depth * 2, d, 3 * d), ada_b.reshape(depth * 2, 3 * d))
    mods = mods.reshape(depth, 2, bsz, 1, 3 * d)
    kv_mod = _ada_mods(c, kv_ada_w[None], kv_ada_b[None]).reshape(bsz, 1, 2 * d)

    w1 = mlp_w1.astype(BF16)
    w2 = mlp_w2.astype(BF16)
    perm_np = _s5_tile_perm(bsz)
    perm = jnp.asarray(perm_np, BF16)
    permt = jnp.asarray(perm_np.T, BF16)

    h = x
    for layer in range(n_s5):
        w_in, w_toep, w_x, dec, d_perm = _s5_weights(
            ssm_lam_re[layer], ssm_lam_im[layer], ssm_log_dt[layer], ssm_b_re[layer], ssm_b_im[layer],
            ssm_c_re[layer], ssm_c_im[layer], ssm_d[layer])
        z = _s5_pre(h, mods[layer, 0], ln_g[layer, 0], perm)
        zo = _s5_core(z, w_in, w_toep, w_x, dec, d_perm, bsz)
        h = _post_s5(h, zo, mods[layer, 0], mods[layer, 1], ln_g[layer, 1], permt,
                     ssm_w_glu[layer].astype(BF16), w1[layer], w2[layer])

    kv = _norm_proj(h, kv_mod, kv_g, w_kv.astype(BF16))
    for layer in range(n_s5, depth):
        j = layer - n_s5
        q = _norm_proj(h, mods[layer, 0], ln_g[layer, 0], attn_w_q[j].astype(BF16))
        o = _attention(q, kv)
        h = _post_attn(h, o, mods[layer, 0], mods[layer, 1], ln_g[layer, 1],
                       attn_w_o[j].astype(BF16), w1[layer], w2[layer],
                       final_g if layer == depth - 1 else None)
    return h
```

```python
import functools

import numpy as np

import jax
import jax.numpy as jnp
from jax import lax
from jax.experimental import pallas as pl
from jax.experimental.pallas import tpu as pltpu

F32 = jnp.float32
BF16 = jnp.bfloat16

D_MODEL = 1024
SSM_GROUP = 16
SSM_GROUPS = D_MODEL // SSM_GROUP
SSM_STATE = 64
HEAD_DIM = 64
N_HEADS = D_MODEL // HEAD_DIM
DILATIONS = (1, 4, 16)
N_BRANCHES = len(DILATIONS)
ATT_BLOCK = 128
ATT_GROUP = 4
D_FF = 4 * D_MODEL
EPS = 1e-6
NEG = -1e30

LANES = 128
CHUNK = 16
GROUP_COLS = CHUNK * SSM_GROUP
PAIR_COLS = 2 * GROUP_COLS
N_PAIRS = SSM_GROUPS // 2
S5_TILE_TOKENS = 2 * CHUNK
VMEM_LIMIT = 56 * 1024 * 1024

HIGHEST = lax.Precision.HIGHEST
NT_DIMS = (((1,), (1,)), ((), ()))


def _cparams(sem):
    return pltpu.CompilerParams(dimension_semantics=sem, vmem_limit_bytes=VMEM_LIMIT)


def _resident(shape, index_map):
    return pl.BlockSpec(shape, index_map, pipeline_mode=pl.Buffered(1))


def _normmod(x, g, shift, scale):
    ms = jnp.mean(x * x, axis=-1, keepdims=True)
    return (x * lax.rsqrt(ms + EPS) * g) * (1.0 + scale) + shift


def _ada_kernel(c_ref, w_ref, b_ref, o_ref):
    c = c_ref[...]
    sc = c * jax.nn.sigmoid(c)
    o_ref[...] = jnp.dot(sc, w_ref[...], preferred_element_type=F32, precision=HIGHEST) + b_ref[...]


def _ada_mods(c, w, b):
    n, d, width = w.shape
    bsz = c.shape[0]
    tn = 1024
    return pl.pallas_call(
        _ada_kernel,
        grid=(n, width // tn),
        in_specs=[
            pl.BlockSpec((bsz, d), lambda i, j: (0, 0)),
            pl.BlockSpec((None, d, tn), lambda i, j: (i, 0, j)),
            pl.BlockSpec((None, 1, tn), lambda i, j: (i, 0, j)),
        ],
        out_specs=pl.BlockSpec((None, bsz, tn), lambda i, j: (i, 0, j)),
        out_shape=jax.ShapeDtypeStruct((n, bsz, width), F32),
        compiler_params=_cparams(("parallel", "parallel")),
        name="ada_mods",
    )(c, w, b.reshape(n, 1, width))


def _proj_kernel(h_ref, mod_ref, g_ref, w_ref, o_ref):
    d = D_MODEL
    u = _normmod(h_ref[...], g_ref[...], mod_ref[:, :d], mod_ref[:, d:2 * d]).astype(BF16)
    for n in range(o_ref.shape[1] // d):
        o_ref[:, n * d:(n + 1) * d] = jnp.dot(
            u, w_ref[:, n * d:(n + 1) * d], preferred_element_type=F32).astype(o_ref.dtype)


def _norm_proj(h, mod, g, w, tm=512):
    bsz, seq, d = h.shape
    width = w.shape[1]
    per_b = seq // tm
    return pl.pallas_call(
        _proj_kernel,
        grid=(bsz * per_b,),
        in_specs=[
            pl.BlockSpec((None, tm, d), lambda i: (i // per_b, i % per_b, 0)),
            pl.BlockSpec((None, 1, mod.shape[-1]), lambda i: (i // per_b, 0, 0)),
            pl.BlockSpec((1, d), lambda i: (0, 0)),
            _resident((d, width), lambda i: (0, 0)),
        ],
        out_specs=pl.BlockSpec((None, tm, width), lambda i: (i // per_b, i % per_b, 0)),
        out_shape=jax.ShapeDtypeStruct((bsz, seq, width), BF16),
        compiler_params=_cparams(("parallel",)),
        name="norm_proj",
    )(h, mod, g.reshape(1, d), w)


def _attn_group(blocks):
    blk = ATT_BLOCK
    lane = lax.broadcasted_iota(jnp.int32, (blk, LANES), 1)
    lo = lane < HEAD_DIM
    scores = []
    for qb, kcat, _, mask, _ in blocks:
        zero = jnp.zeros_like(qb)
        q2 = jnp.concatenate([jnp.where(lo, qb, zero), jnp.where(lo, zero, qb)], axis=0)
        q2 = q2 * jnp.asarray(HEAD_DIM ** -0.5, BF16)
        s = lax.dot_general(q2, kcat, NT_DIMS, preferred_element_type=F32)
        scores.append(jnp.where(mask, s, NEG))
    soft = []
    for s, (_, _, _, _, old) in zip(scores, blocks):
        ms, dens, ps, alphas = [], [], [], []
        for hh in range(2):
            tiles = [s[hh * blk:(hh + 1) * blk, t * LANES:(t + 1) * LANES] for t in range(s.shape[1] // LANES)]
            mx = tiles[0]
            for t in tiles[1:]:
                mx = jnp.maximum(mx, t)
            m = jnp.broadcast_to(jnp.max(mx, axis=-1, keepdims=True), (blk, LANES))
            if old is not None:
                m = jnp.maximum(m, old[0][hh])
            p = [jnp.exp(t - m) for t in tiles]
            psum = p[0]
            for t in p[1:]:
                psum = psum + t
            den = jnp.broadcast_to(jnp.sum(psum, axis=-1, keepdims=True), (blk, LANES))
            if old is not None:
                alpha = jnp.exp(old[0][hh] - m)
                den = alpha * old[0][2 + hh] + den
                alphas.append(alpha)
            ms.append(m)
            dens.append(den)
            ps.append(jnp.concatenate([t.astype(BF16) for t in p], axis=1))
        soft.append((ms, dens, jnp.concatenate(ps, axis=0), alphas))
    outs = []
    for (ms, dens, p, alphas), (_, _, vcat, _, old) in zip(soft, blocks):
        pv = jnp.dot(p, vcat, preferred_element_type=F32)
        acc = jnp.where(lo, pv[:blk], pv[blk:])
        if old is not None:
            acc = jnp.where(lo, alphas[0], alphas[1]) * old[1] + acc
        outs.append((ms + dens, acc))
    return outs


def _attn_kernel(q1, q2, q3, k1, k2, k3, v1, v2, v3, o_ref, tmp_nat, tmp_d4, qd2, kd2, vd2, qd3, kd3, vd3,
                 acc_ref, st_ref, acc2_ref, st2_ref):
    blk = ATT_BLOCK
    seq = q1.shape[0]
    nblk = seq // blk
    d1, d2 = DILATIONS[1], DILATIONS[2]
    quarter = seq // d1
    assert d2 == d1 * d1 and seq == d2 * blk and nblk % ATT_GROUP == 0 and quarter == ATT_GROUP * blk
    qq = lax.broadcasted_iota(jnp.int32, (2 * blk, 2 * blk), 0) & (blk - 1)
    kk = lax.broadcasted_iota(jnp.int32, (2 * blk, 2 * blk), 1)
    mask_pc = jnp.logical_or(jnp.logical_and(kk < blk, kk >= qq), jnp.logical_and(kk >= blk, kk - blk <= qq))
    mask_c = (lax.broadcasted_iota(jnp.int32, (2 * blk, blk), 1)
              <= (lax.broadcasted_iota(jnp.int32, (2 * blk, blk), 0) & (blk - 1)))

    def run_group(blocks, dests):
        for (st_dst, acc_dst, rows), (st, acc) in zip(dests, _attn_group(blocks)):
            for k in range(4):
                st_dst[k, rows, :] = st[k]
            acc_dst[rows, :] = acc

    def load_old(st_src, acc_src, rows):
        return [st_src[k, rows, :] for k in range(4)], acc_src[rows, :]

    def b0_group(ns, first):
        blocks, dests = [], []
        for n in ns:
            if first and n == 0:
                rows = pl.ds(0, blk)
                blocks.append((q1[rows, :], k1[rows, :], v1[rows, :], mask_c, None))
            else:
                rows = pl.ds(pl.multiple_of(n * blk, blk), blk)
                krows = pl.ds(pl.multiple_of((n - 1) * blk, blk), 2 * blk)
                blocks.append((q1[rows, :], k1[krows, :], v1[krows, :], mask_pc, None))
            dests.append((st_ref, acc_ref, rows))
        run_group(blocks, dests)

    b0_group(list(range(ATT_GROUP)), True)

    def b0_body(i, carry):
        b0_group([ATT_GROUP * i + j for j in range(ATT_GROUP)], False)
        return carry

    lax.fori_loop(1, nblk // ATT_GROUP, b0_body, 0)

    def regroup_rows(src, dst4, dst16):
        tmp_nat[...] = src[...].astype(F32)
        for a in range(d1):
            part = tmp_nat[pl.ds(a, quarter, stride=d1), :]
            if dst4 is not None:
                dst4[a * quarter:(a + 1) * quarter, :] = part.astype(dst4.dtype)
            else:
                tmp_d4[a * quarter:(a + 1) * quarter, :] = part
        if dst16 is not None:
            for a1 in range(d1):
                for a2 in range(d1):
                    r = d1 * a2 + a1
                    dst16[r * blk:(r + 1) * blk, :] = tmp_d4[
                        pl.ds(a1 * quarter + a2, blk, stride=d1), :].astype(dst16.dtype)

    for src, dst in ((q2, qd2), (k2, kd2), (v2, vd2)):
        regroup_rows(src, dst, None)
    for src, dst in ((q3, qd3), (k3, kd3), (v3, vd3)):
        regroup_rows(src, None, dst)

    def b1_body(a1, carry):
        base = a1 * quarter
        blocks, dests = [], []
        for n in range(ATT_GROUP):
            rows = pl.ds(a1 + d1 * blk * n, blk, stride=d1)
            qrows = pl.ds(pl.multiple_of(base + n * blk, blk), blk)
            if n == 0:
                krows, mask = qrows, mask_c
            else:
                krows, mask = pl.ds(pl.multiple_of(base + (n - 1) * blk, blk), 2 * blk), mask_pc
            blocks.append((qd2[qrows, :], kd2[krows, :], vd2[krows, :], mask,
                           load_old(st_ref, acc_ref, rows)))
            dests.append((st_ref, acc_ref, rows))
        run_group(blocks, dests)
        return carry

    lax.fori_loop(0, d1, b1_body, 0)

    for a in range(d1):
        acc2_ref[a * quarter:(a + 1) * quarter, :] = acc_ref[pl.ds(a, quarter, stride=d1), :]
        for k in range(4):
            st2_ref[k, a * quarter:(a + 1) * quarter, :] = st_ref[k, pl.ds(a, quarter, stride=d1), :]

    def b2_body(i, carry):
        blocks, dests = [], []
        for j in range(d1):
            qrows = pl.ds(pl.multiple_of((d1 * i + j) * blk, blk), blk)
            rows = pl.ds(j * quarter + i, blk, stride=d1)
            blocks.append((qd3[qrows, :], kd3[qrows, :], vd3[qrows, :], mask_c,
                           load_old(st2_ref, acc2_ref, rows)))
            dests.append((st2_ref, acc2_ref, rows))
        run_group(blocks, dests)
        return carry

    lax.fori_loop(0, d1, b2_body, 0)

    lane = lax.broadcasted_iota(jnp.int32, (quarter, LANES), 1)
    for a in range(d1):
        seg = slice(a * quarter, (a + 1) * quarter)
        den = jnp.where(lane < HEAD_DIM, st2_ref[2, seg, :], st2_ref[3, seg, :])
        tmp_nat[pl.ds(a, quarter, stride=d1), :] = acc2_ref[a * quarter:(a + 1) * quarter, :] / den
    o_ref[...] = tmp_nat[...].astype(o_ref.dtype)


def _attention(q, kv):
    bsz, seq, _ = q.shape
    pairs = D_MODEL // LANES

    def slab(col0):
        return pl.BlockSpec((None, seq, LANES), lambda b, hp: (b, 0, col0 + hp))

    in_specs = ([slab(i * pairs) for i in range(N_BRANCHES)]
                + [slab(i * pairs) for i in range(N_BRANCHES)]
                + [slab((N_BRANCHES + i) * pairs) for i in range(N_BRANCHES)])
    return pl.pallas_call(
        _attn_kernel,
        grid=(bsz, pairs),
        in_specs=in_specs,
        out_specs=slab(0),
        out_shape=jax.ShapeDtypeStruct((bsz, seq, D_MODEL), BF16),
        scratch_shapes=([pltpu.VMEM((seq, LANES), F32)] * 2 + [pltpu.VMEM((seq, LANES), BF16)] * 6
                        + [pltpu.VMEM((seq, LANES), F32), pltpu.VMEM((4, seq, LANES), F32)] * 2),
        compiler_params=_cparams(("parallel", "parallel")),
        name="attention",
    )(q, q, q, kv, kv, kv, kv, kv, kv)


def _regroup8(tiles):
    t = list(tiles)
    lane = lax.broadcasted_iota(jnp.int32, t[0].shape, 1)
    piece = lane // SSM_GROUP
    for dist in (4, 2, 1):
        bit = (piece & dist) != 0
        shift = dist * SSM_GROUP
        for a in range(8):
            if a & dist:
                continue
            b = a | dist
            ta, tb = t[a], t[b]
            t[a] = jnp.where(bit, pltpu.roll(tb, shift, 1), ta)
            t[b] = jnp.where(bit, tb, pltpu.roll(ta, LANES - shift, 1))
    return t


def _s5_tile_perm(bsz):
    tt = S5_TILE_TOKENS
    n = bsz * tt
    perm = np.zeros((n, n), np.float32)
    for b in range(bsz):
        for c in range(tt // CHUNK):
            for t in range(CHUNK):
                perm[(t * (tt // CHUNK) + c) * bsz + b, b * tt + c * CHUNK + t] = 1.0
    return perm


def _s5_pre_kernel(h_ref, mod_ref, g_ref, perm_ref, z_ref, u_scr, up_scr):
    d = D_MODEL
    bsz, tt, _ = h_ref.shape
    g = g_ref[...]
    for b in range(bsz):
        u_scr[b * tt:(b + 1) * tt, :] = _normmod(
            h_ref[b], g, mod_ref[b, :, :d], mod_ref[b, :, d:2 * d]).astype(BF16)
    up_scr[...] = jnp.dot(perm_ref[...], u_scr[...], preferred_element_type=F32)
    rows = z_ref.shape[0]
    for j in range(d // LANES):
        for half in range(CHUNK // 8):
            tiles = [up_scr[(8 * half + tl) * rows:(8 * half + tl + 1) * rows, j * LANES:(j + 1) * LANES]
                     for tl in range(8)]
            outs = _regroup8(tiles)
            for gl in range(8):
                col = (8 * j + gl) * GROUP_COLS + half * LANES
                z_ref[:, col:col + LANES] = outs[gl].astype(z_ref.dtype)


def _s5_pre(h, mod, g, perm):
    bsz, seq, d = h.shape
    tt = S5_TILE_TOKENS
    rows = bsz * tt // CHUNK
    return pl.pallas_call(
        _s5_pre_kernel,
        grid=(seq // tt,),
        in_specs=[
            pl.BlockSpec((bsz, tt, d), lambda i: (0, i, 0)),
            pl.BlockSpec(mod.shape, lambda i: (0, 0, 0)),
            pl.BlockSpec((1, d), lambda i: (0, 0)),
            pl.BlockSpec(perm.shape, lambda i: (0, 0)),
        ],
        out_specs=pl.BlockSpec((rows, CHUNK * d), lambda i: (i, 0)),
        out_shape=jax.ShapeDtypeStruct((bsz * seq // CHUNK, CHUNK * d), BF16),
        scratch_shapes=[pltpu.VMEM((bsz * tt, d), BF16), pltpu.VMEM((bsz * tt, d), F32)],
        compiler_params=_cparams(("parallel",)),
        name="s5_pre",
    )(h, mod, g.reshape(1, d), perm)


def _s5_core_kernel(z_ref, win_ref, wt_ref, wx_ref, dec_ref, d_ref, o_ref, s_scr, x_scr, *, bsz):
    gc = GROUP_COLS
    nchunk = z_ref.shape[0] // bsz
    s_scr[...] = (jnp.dot(z_ref[:, :gc], win_ref[0], preferred_element_type=F32)
                  + jnp.dot(z_ref[:, gc:], win_ref[1], preferred_element_type=F32))
    ar, ai = dec_ref[:, :LANES], dec_ref[:, LANES:]

    def step(c, carry):
        xr, xi = carry
        rows = pl.ds(pl.multiple_of(c * bsz, bsz), bsz)
        x_scr[rows, :LANES] = xr
        x_scr[rows, LANES:] = xi
        return (ar * xr - ai * xi + s_scr[rows, :LANES], ar * xi + ai * xr + s_scr[rows, LANES:])

    zero = jnp.zeros((bsz, LANES), F32)
    lax.fori_loop(0, nchunk, step, (zero, zero))
    x = x_scr[...].astype(BF16)
    for gi in range(2):
        z = z_ref[:, gi * gc:(gi + 1) * gc]
        y = (jnp.dot(z, wt_ref[gi], preferred_element_type=F32)
             + jnp.dot(x, wx_ref[gi], preferred_element_type=F32))
        y = y + d_ref[:, gi * gc:(gi + 1) * gc] * z.astype(F32)
        o_ref[:, gi * gc:(gi + 1) * gc] = jax.nn.gelu(y).astype(o_ref.dtype)


def _s5_core(z, w_in, w_toep, w_x, dec, d_perm, bsz):
    nrow, width = z.shape
    wspec = pl.BlockSpec((2, GROUP_COLS, GROUP_COLS), lambda k: (k, 0, 0))
    return pl.pallas_call(
        functools.partial(_s5_core_kernel, bsz=bsz),
        grid=(N_PAIRS,),
        in_specs=[
            pl.BlockSpec((nrow, PAIR_COLS), lambda k: (0, k)),
            wspec, wspec, wspec,
            pl.BlockSpec((1, GROUP_COLS), lambda k: (0, k)),
            pl.BlockSpec((1, PAIR_COLS), lambda k: (0, k)),
        ],
        out_specs=pl.BlockSpec((nrow, PAIR_COLS), lambda k: (0, k)),
        out_shape=jax.ShapeDtypeStruct((nrow, width), BF16),
        scratch_shapes=[pltpu.VMEM((nrow, GROUP_COLS), F32), pltpu.VMEM((nrow, GROUP_COLS), F32)],
        compiler_params=_cparams(("parallel",)),
        name="s5_core",
    )(z, w_in, w_toep, w_x, dec, d_perm)


def _s5_weights(lam_re, lam_im, log_dt, b_re, b_im, c_re, c_im, d_skip):
    g, p, c16 = SSM_GROUPS, SSM_STATE, SSM_GROUP
    lam = lax.complex(lam_re.astype(F32), lam_im.astype(F32))
    dt = jnp.exp(log_dt.astype(F32))[:, None]
    steps = jnp.arange(CHUNK + 1, dtype=F32)
    apow = jnp.exp((lam * dt)[None] * steps[:, None, None])
    a = apow[1]
    bbar = ((a - 1.0) / lam)[..., None] * lax.complex(b_re.astype(F32), b_im.astype(F32))
    cmat = lax.complex(c_re.astype(F32), c_im.astype(F32))
    odd = (jnp.arange(g) % 2)[:, None, None]

    win = apow[CHUNK - 1::-1][..., None] * bbar[None]
    win = jnp.transpose(win, (1, 0, 3, 2)).reshape(g, GROUP_COLS, p)
    zeros = jnp.zeros_like(win.real)
    w_in = jnp.concatenate([
        jnp.where(odd == 0, win.real, zeros), jnp.where(odd == 1, win.real, zeros),
        jnp.where(odd == 0, win.imag, zeros), jnp.where(odd == 1, win.imag, zeros)], axis=-1)

    kern = jnp.einsum('gop,kgp,gpi->kgoi', cmat, apow[:CHUNK], bbar, precision=HIGHEST).real
    lag = jnp.arange(CHUNK)[None, :] - jnp.arange(CHUNK)[:, None]
    toep = jnp.where((lag >= 0)[:, :, None, None, None], kern[jnp.clip(lag, 0, CHUNK - 1)], 0.0)
    w_toep = jnp.transpose(toep, (2, 0, 4, 1, 3)).reshape(g, GROUP_COLS, GROUP_COLS)

    cw = cmat[None] * apow[1:, :, None, :]
    cw = jnp.transpose(cw, (1, 3, 0, 2)).reshape(g, p, GROUP_COLS)
    zx = jnp.zeros_like(cw.real)
    w_x = jnp.concatenate([
        jnp.where(odd == 0, cw.real, zx), jnp.where(odd == 1, cw.real, zx),
        jnp.where(odd == 0, -cw.imag, zx), jnp.where(odd == 1, -cw.imag, zx)], axis=1)

    a16 = apow[CHUNK].reshape(N_PAIRS, 2 * p)
    dec = jnp.concatenate([a16.real, a16.imag], axis=-1).reshape(1, N_PAIRS * GROUP_COLS)
    d_perm = jnp.broadcast_to(d_skip.astype(F32).reshape(g, 1, c16), (g, CHUNK, c16)).reshape(1, g * GROUP_COLS)
    return w_in.astype(BF16), w_toep.astype(BF16), w_x.astype(BF16), dec, d_perm


def _mlp(u, w1_ref, w2_ref):
    d = D_MODEL
    acc = None
    for k in range(D_FF // d):
        a = jnp.dot(u, w1_ref[:, k * d:(k + 1) * d], preferred_element_type=F32)
        a = jnp.square(jnp.maximum(a, 0.0)).astype(BF16)
        part = jnp.dot(a, w2_ref[k * d:(k + 1) * d, :], preferred_element_type=F32)
        acc = part if acc is None else acc + part
    return acc


def _final_norm(h, fg_ref):
    ms = jnp.mean(h * h, axis=-1, keepdims=True)
    return h * lax.rsqrt(ms + EPS) * fg_ref[...]


def _post_attn_kernel(*refs, final):
    if final:
        h_ref, y_ref, moda_ref, modm_ref, g_ref, wp_ref, w1_ref, w2_ref, fg_ref, o_ref = refs
    else:
        h_ref, y_ref, moda_ref, modm_ref, g_ref, wp_ref, w1_ref, w2_ref, o_ref = refs
    d = D_MODEL
    ymix = jnp.dot(y_ref[...], wp_ref[...], preferred_element_type=F32)
    h1 = h_ref[...] + moda_ref[:, 2 * d:3 * d] * ymix
    u = _normmod(h1, g_ref[...], modm_ref[:, :d], modm_ref[:, d:2 * d]).astype(BF16)
    h2 = h1 + modm_ref[:, 2 * d:3 * d] * _mlp(u, w1_ref, w2_ref)
    o_ref[...] = _final_norm(h2, fg_ref) if final else h2


def _post_attn(h, o, moda, modm, g2, w_o, w1, w2, final_g, tm=512):
    bsz, seq, d = h.shape
    per_b = seq // tm
    row_spec = pl.BlockSpec((None, tm, d), lambda i: (i // per_b, i % per_b, 0))
    final = final_g is not None
    in_specs = [
        row_spec, row_spec,
        pl.BlockSpec((None, 1, moda.shape[-1]), lambda i: (i // per_b, 0, 0)),
        pl.BlockSpec((None, 1, modm.shape[-1]), lambda i: (i // per_b, 0, 0)),
        pl.BlockSpec((1, d), lambda i: (0, 0)),
        _resident(w_o.shape, lambda i: (0, 0)),
        _resident((d, D_FF), lambda i: (0, 0)),
        _resident((D_FF, d), lambda i: (0, 0)),
    ]
    args = [h, o, moda, modm, g2.reshape(1, d), w_o, w1, w2]
    if final:
        in_specs.append(pl.BlockSpec((1, d), lambda i: (0, 0)))
        args.append(final_g.reshape(1, d))
    return pl.pallas_call(
        functools.partial(_post_attn_kernel, final=final),
        grid=(bsz * per_b,),
        in_specs=in_specs,
        out_specs=row_spec,
        out_shape=jax.ShapeDtypeStruct((bsz, seq, d), F32),
        compiler_params=_cparams(("parallel",)),
        name="post_attn_mlp",
    )(*args)


def _post_s5_kernel(h_ref, z_ref, moda_ref, modm_ref, g_ref, permt_ref, wp_ref, w1_ref, w2_ref, o_ref,
                    z_scr, h1_scr, u_scr):
    d = D_MODEL
    bsz, tt, _ = h_ref.shape
    rows = z_ref.shape[0]
    for j in range(d // LANES):
        for half in range(CHUNK // 8):
            tiles = []
            for gl in range(8):
                col = (8 * j + gl) * GROUP_COLS + half * LANES
                tiles.append(z_ref[:, col:col + LANES].astype(F32))
            outs = _regroup8(tiles)
            for tl in range(8):
                t = 8 * half + tl
                z_scr[t * rows:(t + 1) * rows, j * LANES:(j + 1) * LANES] = outs[tl].astype(BF16)
    zn = jnp.dot(permt_ref[...], z_scr[...], preferred_element_type=F32).astype(BF16)
    zz = jnp.dot(zn, wp_ref[...], preferred_element_type=F32)
    ymix = zz[:, :d] * jax.nn.sigmoid(zz[:, d:])
    g = g_ref[...]
    for b in range(bsz):
        sl = slice(b * tt, (b + 1) * tt)
        h1 = h_ref[b] + moda_ref[b, :, 2 * d:3 * d] * ymix[sl, :]
        h1_scr[sl, :] = h1
        u_scr[sl, :] = _normmod(h1, g, modm_ref[b, :, :d], modm_ref[b, :, d:2 * d]).astype(BF16)
    acc = _mlp(u_scr[...], w1_ref, w2_ref)
    for b in range(bsz):
        sl = slice(b * tt, (b + 1) * tt)
        o_ref[b] = h1_scr[sl, :] + modm_ref[b, :, 2 * d:3 * d] * acc[sl, :]


def _post_s5(h, zo, moda, modm, g2, permt, w_glu, w1, w2):
    bsz, seq, d = h.shape
    tt = S5_TILE_TOKENS
    rows = bsz * tt // CHUNK
    h_spec = pl.BlockSpec((bsz, tt, d), lambda i: (0, i, 0))
    return pl.pallas_call(
        _post_s5_kernel,
        grid=(seq // tt,),
        in_specs=[
            h_spec,
            pl.BlockSpec((rows, CHUNK * d), lambda i: (i, 0)),
            pl.BlockSpec(moda.shape, lambda i: (0, 0, 0)),
            pl.BlockSpec(modm.shape, lambda i: (0, 0, 0)),
            pl.BlockSpec((1, d), lambda i: (0, 0)),
            pl.BlockSpec(permt.shape, lambda i: (0, 0)),
            _resident(w_glu.shape, lambda i: (0, 0)),
            _resident((d, D_FF), lambda i: (0, 0)),
            _resident((D_FF, d), lambda i: (0, 0)),
        ],
        out_specs=h_spec,
        out_shape=jax.ShapeDtypeStruct((bsz, seq, d), F32),
        scratch_shapes=[pltpu.VMEM((bsz * tt, d), BF16), pltpu.VMEM((bsz * tt, d), F32),
                        pltpu.VMEM((bsz * tt, d), BF16)],
        compiler_params=_cparams(("parallel",)),
        name="post_s5_mlp",
    )(h, zo, moda, modm, g2.reshape(1, d), permt, w_glu, w1, w2)


def kernel(x, c, ln_g, ada_w, ada_b, ssm_lam_re, ssm_lam_im, ssm_log_dt, ssm_b_re, ssm_b_im, ssm_c_re, ssm_c_im, ssm_d, ssm_w_glu, kv_g, kv_ada_w, kv_ada_b, w_kv, attn_w_q, attn_w_o, mlp_w1, mlp_w2, final_g):
    bsz, seq, d = x.shape
    depth = ln_g.shape[0]
    n_s5 = ssm_lam_re.shape[0]
    assert d == D_MODEL and seq % (DILATIONS[-1] * ATT_BLOCK) == 0

    mods = _ada_mods(c, ada_w.reshape(depth * 2, d, 3 * d), ada_b.reshape(depth * 2, 3 * d))
    mods = mods.reshape(depth, 2, bsz, 1, 3 * d)
    kv_mod = _ada_mods(c, kv_ada_w[None], kv_ada_b[None]).reshape(bsz, 1, 2 * d)

    w1 = mlp_w1.astype(BF16)
    w2 = mlp_w2.astype(BF16)
    perm_np = _s5_tile_perm(bsz)
    perm = jnp.asarray(perm_np, BF16)
    permt = jnp.asarray(perm_np.T, BF16)

    h = x
    for layer in range(n_s5):
        w_in, w_toep, w_x, dec, d_perm = _s5_weights(
            ssm_lam_re[layer], ssm_lam_im[layer], ssm_log_dt[layer], ssm_b_re[layer], ssm_b_im[layer],
            ssm_c_re[layer], ssm_c_im[layer], ssm_d[layer])
        z = _s5_pre(h, mods[layer, 0], ln_g[layer, 0], perm)
        zo = _s5_core(z, w_in, w_toep, w_x, dec, d_perm, bsz)
        h = _post_s5(h, zo, mods[layer, 0], mods[layer, 1], ln_g[layer, 1], permt,
                     ssm_w_glu[layer].astype(BF16), w1[layer], w2[layer])

    kv = _norm_proj(h, kv_mod, kv_g, w_kv.astype(BF16))
    for layer in range(n_s5, depth):
        j = layer - n_s5
        q = _norm_proj(h, mods[layer, 0], ln_g[layer, 0], attn_w_q[j].astype(BF16))
        o = _attention(q, kv)
        h = _post_attn(h, o, mods[layer, 0], mods[layer, 1], ln_g[layer, 1],
                       attn_w_o[j].astype(BF16), w1[layer], w2[layer],
                       final_g if layer == depth - 1 else None)
    return h
```

```python
import functools

import numpy as np

import jax
import jax.numpy as jnp
from jax import lax
from jax.experimental import pallas as pl
from jax.experimental.pallas import tpu as pltpu

F32 = jnp.float32
BF16 = jnp.bfloat16

D_MODEL = 1024
SSM_GROUP = 16
SSM_GROUPS = D_MODEL // SSM_GROUP
SSM_STATE = 64
HEAD_DIM = 64
N_HEADS = D_MODEL // HEAD_DIM
DILATIONS = (1, 4, 16)
N_BRANCHES = len(DILATIONS)
ATT_BLOCK = 128
ATT_GROUP = 8
D_FF = 4 * D_MODEL
EPS = 1e-6
NEG = -1e30

LANES = 128
CHUNK = 16
GROUP_COLS = CHUNK * SSM_GROUP
PAIR_COLS = 2 * GROUP_COLS
N_PAIRS = SSM_GROUPS // 2
S5_TILE_TOKENS = 2 * CHUNK
VMEM_LIMIT = 56 * 1024 * 1024

HIGHEST = lax.Precision.HIGHEST
NT_DIMS = (((1,), (1,)), ((), ()))


def _cparams(sem):
    return pltpu.CompilerParams(dimension_semantics=sem, vmem_limit_bytes=VMEM_LIMIT)


def _resident(shape, index_map):
    return pl.BlockSpec(shape, index_map, pipeline_mode=pl.Buffered(1))


def _normmod(x, g, shift, scale):
    ms = jnp.mean(x * x, axis=-1, keepdims=True)
    return (x * lax.rsqrt(ms + EPS) * g) * (1.0 + scale) + shift


def _ada_kernel(c_ref, w_ref, b_ref, o_ref):
    c = c_ref[...]
    sc = c * jax.nn.sigmoid(c)
    o_ref[...] = jnp.dot(sc, w_ref[...], preferred_element_type=F32, precision=HIGHEST) + b_ref[...]


def _ada_mods(c, w, b):
    n, d, width = w.shape
    bsz = c.shape[0]
    tn = 1024
    return pl.pallas_call(
        _ada_kernel,
        grid=(n, width // tn),
        in_specs=[
            pl.BlockSpec((bsz, d), lambda i, j: (0, 0)),
            pl.BlockSpec((None, d, tn), lambda i, j: (i, 0, j)),
            pl.BlockSpec((None, 1, tn), lambda i, j: (i, 0, j)),
        ],
        out_specs=pl.BlockSpec((None, bsz, tn), lambda i, j: (i, 0, j)),
        out_shape=jax.ShapeDtypeStruct((n, bsz, width), F32),
        compiler_params=_cparams(("parallel", "parallel")),
        name="ada_mods",
    )(c, w, b.reshape(n, 1, width))


def _proj_kernel(h_ref, mod_ref, g_ref, w_ref, o_ref):
    d = D_MODEL
    u = _normmod(h_ref[...], g_ref[...], mod_ref[:, :d], mod_ref[:, d:2 * d]).astype(BF16)
    for n in range(o_ref.shape[1] // d):
        o_ref[:, n * d:(n + 1) * d] = jnp.dot(
            u, w_ref[:, n * d:(n + 1) * d], preferred_element_type=F32).astype(o_ref.dtype)


def _norm_proj(h, mod, g, w, tm=512):
    bsz, seq, d = h.shape
    width = w.shape[1]
    per_b = seq // tm
    return pl.pallas_call(
        _proj_kernel,
        grid=(bsz * per_b,),
        in_specs=[
            pl.BlockSpec((None, tm, d), lambda i: (i // per_b, i % per_b, 0)),
            pl.BlockSpec((None, 1, mod.shape[-1]), lambda i: (i // per_b, 0, 0)),
            pl.BlockSpec((1, d), lambda i: (0, 0)),
            _resident((d, width), lambda i: (0, 0)),
        ],
        out_specs=pl.BlockSpec((None, tm, width), lambda i: (i // per_b, i % per_b, 0)),
        out_shape=jax.ShapeDtypeStruct((bsz, seq, width), BF16),
        compiler_params=_cparams(("parallel",)),
        name="norm_proj",
    )(h, mod, g.reshape(1, d), w)


def _attn_group(blocks):
    blk = ATT_BLOCK
    lane = lax.broadcasted_iota(jnp.int32, (blk, LANES), 1)
    lo = lane < HEAD_DIM
    scores = []
    for qb, kcat, _, mask, _ in blocks:
        zero = jnp.zeros_like(qb)
        q2 = jnp.concatenate([jnp.where(lo, qb, zero), jnp.where(lo, zero, qb)], axis=0)
        q2 = q2 * jnp.asarray(HEAD_DIM ** -0.5, BF16)
        s = lax.dot_general(q2, kcat, NT_DIMS, preferred_element_type=F32)
        scores.append(jnp.where(mask, s, NEG))
    soft = []
    for s, (_, _, _, _, old) in zip(scores, blocks):
        ms, ps, alphas = [], [], []
        for hh in range(2):
            tiles = [s[hh * blk:(hh + 1) * blk, t * LANES:(t + 1) * LANES] for t in range(s.shape[1] // LANES)]
            mx = tiles[0]
            for t in tiles[1:]:
                mx = jnp.maximum(mx, t)
            m = jnp.broadcast_to(jnp.max(mx, axis=-1, keepdims=True), (blk, LANES))
            if old is not None:
                m = jnp.maximum(m, old[0][hh])
                alphas.append(jnp.exp(old[0][hh] - m))
            ms.append(m)
            ps.append(jnp.concatenate([jnp.exp(t - m).astype(BF16) for t in tiles], axis=1))
        soft.append((ms, jnp.concatenate(ps, axis=0), alphas))
    outs = []
    for (ms, p, alphas), (_, _, vcat, _, old) in zip(soft, blocks):
        pv = jnp.dot(p, vcat, preferred_element_type=F32)
        acc = jnp.where(lo, pv[:blk, :LANES], pv[blk:, :LANES])
        dens = [pv[:blk, LANES:], pv[blk:, LANES:]]
        if old is not None:
            acc = jnp.where(lo, alphas[0], alphas[1]) * old[1] + acc
            dens = [alphas[hh] * old[0][2 + hh] + dens[hh] for hh in range(2)]
        outs.append((ms + dens, acc))
    return outs


def _attn_kernel(q1, q2, q3, k1, k2, k3, v1, v2, v3, o_ref, tmp_nat, tmp_d4, qd2, kd2, qd3, kd3, vd1, vd2, vd3,
                 acc_ref, st_ref, acc2_ref, st2_ref):
    blk = ATT_BLOCK
    seq = q1.shape[0]
    nblk = seq // blk
    d1, d2 = DILATIONS[1], DILATIONS[2]
    quarter = seq // d1
    per_res = quarter // blk
    assert d2 == d1 * d1 and seq == d2 * blk and nblk % ATT_GROUP == 0 and ATT_GROUP % per_res == 0
    ones = jnp.ones((seq, LANES), BF16)
    vd1[:, :LANES] = v1[...]
    for vd in (vd1, vd2, vd3):
        vd[:, LANES:] = ones
    qq = lax.broadcasted_iota(jnp.int32, (2 * blk, 2 * blk), 0) & (blk - 1)
    kk = lax.broadcasted_iota(jnp.int32, (2 * blk, 2 * blk), 1)
    mask_pc = jnp.logical_or(jnp.logical_and(kk < blk, kk >= qq), jnp.logical_and(kk >= blk, kk - blk <= qq))
    mask_c = (lax.broadcasted_iota(jnp.int32, (2 * blk, blk), 1)
              <= (lax.broadcasted_iota(jnp.int32, (2 * blk, blk), 0) & (blk - 1)))

    def run_group(blocks, dests):
        for (st_dst, acc_dst, rows), (st, acc) in zip(dests, _attn_group(blocks)):
            for k in range(4):
                st_dst[k, rows, :] = st[k]
            acc_dst[rows, :] = acc

    def load_old(st_src, acc_src, rows):
        return [st_src[k, rows, :] for k in range(4)], acc_src[rows, :]

    def b0_group(ns, first):
        blocks, dests = [], []
        for n in ns:
            if first and n == 0:
                rows = pl.ds(0, blk)
                blocks.append((q1[rows, :], k1[rows, :], vd1[rows, :], mask_c, None))
            else:
                rows = pl.ds(pl.multiple_of(n * blk, blk), blk)
                krows = pl.ds(pl.multiple_of((n - 1) * blk, blk), 2 * blk)
                blocks.append((q1[rows, :], k1[krows, :], vd1[krows, :], mask_pc, None))
            dests.append((st_ref, acc_ref, rows))
        run_group(blocks, dests)

    b0_group(list(range(ATT_GROUP)), True)

    def b0_body(i, carry):
        b0_group([ATT_GROUP * i + j for j in range(ATT_GROUP)], False)
        return carry

    lax.fori_loop(1, nblk // ATT_GROUP, b0_body, 0)

    def regroup_rows(src, dst4, dst16):
        tmp_nat[...] = src[...].astype(F32)
        for a in range(d1):
            part = tmp_nat[pl.ds(a, quarter, stride=d1), :]
            if dst4 is not None:
                dst4[a * quarter:(a + 1) * quarter, :LANES] = part.astype(dst4.dtype)
            else:
                tmp_d4[a * quarter:(a + 1) * quarter, :] = part
        if dst16 is not None:
            for a1 in range(d1):
                for a2 in range(d1):
                    r = d1 * a2 + a1
                    dst16[r * blk:(r + 1) * blk, :LANES] = tmp_d4[
                        pl.ds(a1 * quarter + a2, blk, stride=d1), :].astype(dst16.dtype)

    for src, dst in ((q2, qd2), (k2, kd2), (v2, vd2)):
        regroup_rows(src, dst, None)
    for src, dst in ((q3, qd3), (k3, kd3), (v3, vd3)):
        regroup_rows(src, None, dst)

    res_per_trip = ATT_GROUP // per_res

    def b1_body(i, carry):
        blocks, dests = [], []
        for jr in range(res_per_trip):
            a1 = res_per_trip * i + jr
            base = a1 * quarter
            for n in range(per_res):
                rows = pl.ds(a1 + d1 * blk * n, blk, stride=d1)
                qrows = pl.ds(pl.multiple_of(base + n * blk, blk), blk)
                if n == 0:
                    krows, mask = qrows, mask_c
                else:
                    krows, mask = pl.ds(pl.multiple_of(base + (n - 1) * blk, blk), 2 * blk), mask_pc
                blocks.append((qd2[qrows, :], kd2[krows, :], vd2[krows, :], mask,
                               load_old(st_ref, acc_ref, rows)))
                dests.append((st_ref, acc_ref, rows))
        run_group(blocks, dests)
        return carry

    lax.fori_loop(0, d1 // res_per_trip, b1_body, 0)

    for a in range(d1):
        acc2_ref[a * quarter:(a + 1) * quarter, :] = acc_ref[pl.ds(a, quarter, stride=d1), :]
        for k in range(4):
            st2_ref[k, a * quarter:(a + 1) * quarter, :] = st_ref[k, pl.ds(a, quarter, stride=d1), :]

    def b2_body(i, carry):
        blocks, dests = [], []
        for j in range(ATT_GROUP):
            a1, a2 = j % d1, (ATT_GROUP // d1) * i + j // d1
            qrows = pl.ds(pl.multiple_of((ATT_GROUP * i + j) * blk, blk), blk)
            rows = pl.ds(a1 * quarter + a2, blk, stride=d1)
            blocks.append((qd3[qrows, :], kd3[qrows, :], vd3[qrows, :], mask_c,
                           load_old(st2_ref, acc2_ref, rows)))
            dests.append((st2_ref, acc2_ref, rows))
        run_group(blocks, dests)
        return carry

    lax.fori_loop(0, d2 // ATT_GROUP, b2_body, 0)

    lane = lax.broadcasted_iota(jnp.int32, (quarter, LANES), 1)
    for a in range(d1):
        seg = slice(a * quarter, (a + 1) * quarter)
        den = jnp.where(lane < HEAD_DIM, st2_ref[2, seg, :], st2_ref[3, seg, :])
        tmp_nat[pl.ds(a, quarter, stride=d1), :] = acc2_ref[a * quarter:(a + 1) * quarter, :] / den
    o_ref[...] = tmp_nat[...].astype(o_ref.dtype)


def _attention(q, kv):
    bsz, seq, _ = q.shape
    pairs = D_MODEL // LANES

    def slab(col0):
        return pl.BlockSpec((None, seq, LANES), lambda b, hp: (b, 0, col0 + hp))

    in_specs = ([slab(i * pairs) for i in range(N_BRANCHES)]
                + [slab(i * pairs) for i in range(N_BRANCHES)]
                + [slab((N_BRANCHES + i) * pairs) for i in range(N_BRANCHES)])
    return pl.pallas_call(
        _attn_kernel,
        grid=(bsz, pairs),
        in_specs=in_specs,
        out_specs=slab(0),
        out_shape=jax.ShapeDtypeStruct((bsz, seq, D_MODEL), BF16),
        scratch_shapes=([pltpu.VMEM((seq, LANES), F32)] * 2 + [pltpu.VMEM((seq, LANES), BF16)] * 4
                        + [pltpu.VMEM((seq, 2 * LANES), BF16)] * 3
                        + [pltpu.VMEM((seq, LANES), F32), pltpu.VMEM((4, seq, LANES), F32)] * 2),
        compiler_params=_cparams(("parallel", "parallel")),
        name="attention",
    )(q, q, q, kv, kv, kv, kv, kv, kv)


def _regroup8(tiles):
    t = list(tiles)
    lane = lax.broadcasted_iota(jnp.int32, t[0].shape, 1)
    piece = lane // SSM_GROUP
    for dist in (4, 2, 1):
        bit = (piece & dist) != 0
        shift = dist * SSM_GROUP
        for a in range(8):
            if a & dist:
                continue
            b = a | dist
            ta, tb = t[a], t[b]
            t[a] = jnp.where(bit, pltpu.roll(tb, shift, 1), ta)
            t[b] = jnp.where(bit, tb, pltpu.roll(ta, LANES - shift, 1))
    return t


def _s5_tile_perm(bsz):
    tt = S5_TILE_TOKENS
    n = bsz * tt
    perm = np.zeros((n, n), np.float32)
    for b in range(bsz):
        for c in range(tt // CHUNK):
            for t in range(CHUNK):
                perm[(t * (tt // CHUNK) + c) * bsz + b, b * tt + c * CHUNK + t] = 1.0
    return perm


def _s5_pre_kernel(h_ref, mod_ref, g_ref, perm_ref, z_ref, u_scr, up_scr):
    d = D_MODEL
    bsz, tt, _ = h_ref.shape
    g = g_ref[...]
    for b in range(bsz):
        u_scr[b * tt:(b + 1) * tt, :] = _normmod(
            h_ref[b], g, mod_ref[b, :, :d], mod_ref[b, :, d:2 * d]).astype(BF16)
    up_scr[...] = jnp.dot(perm_ref[...], u_scr[...], preferred_element_type=F32)
    rows = z_ref.shape[0]
    for j in range(d // LANES):
        for half in range(CHUNK // 8):
            tiles = [up_scr[(8 * half + tl) * rows:(8 * half + tl + 1) * rows, j * LANES:(j + 1) * LANES]
                     for tl in range(8)]
            outs = _regroup8(tiles)
            for gl in range(8):
                col = (8 * j + gl) * GROUP_COLS + half * LANES
                z_ref[:, col:col + LANES] = outs[gl].astype(z_ref.dtype)


def _s5_pre(h, mod, g, perm):
    bsz, seq, d = h.shape
    tt = S5_TILE_TOKENS
    rows = bsz * tt // CHUNK
    return pl.pallas_call(
        _s5_pre_kernel,
        grid=(seq // tt,),
        in_specs=[
            pl.BlockSpec((bsz, tt, d), lambda i: (0, i, 0)),
            pl.BlockSpec(mod.shape, lambda i: (0, 0, 0)),
            pl.BlockSpec((1, d), lambda i: (0, 0)),
            pl.BlockSpec(perm.shape, lambda i: (0, 0)),
        ],
        out_specs=pl.BlockSpec((rows, CHUNK * d), lambda i: (i, 0)),
        out_shape=jax.ShapeDtypeStruct((bsz * seq // CHUNK, CHUNK * d), BF16),
        scratch_shapes=[pltpu.VMEM((bsz * tt, d), BF16), pltpu.VMEM((bsz * tt, d), F32)],
        compiler_params=_cparams(("parallel",)),
        name="s5_pre",
    )(h, mod, g.reshape(1, d), perm)


def _s5_core_kernel(z_ref, win_ref, wt_ref, wx_ref, dec_ref, d_ref, o_ref, s_scr, x_scr, *, bsz):
    gc = GROUP_COLS
    nchunk = z_ref.shape[0] // bsz
    s_scr[...] = (jnp.dot(z_ref[:, :gc], win_ref[0], preferred_element_type=F32)
                  + jnp.dot(z_ref[:, gc:], win_ref[1], preferred_element_type=F32))
    ar, ai = dec_ref[:, :LANES], dec_ref[:, LANES:]

    def step(c, carry):
        xr, xi = carry
        rows = pl.ds(pl.multiple_of(c * bsz, bsz), bsz)
        x_scr[rows, :LANES] = xr
        x_scr[rows, LANES:] = xi
        return (ar * xr - ai * xi + s_scr[rows, :LANES], ar * xi + ai * xr + s_scr[rows, LANES:])

    zero = jnp.zeros((bsz, LANES), F32)
    lax.fori_loop(0, nchunk, step, (zero, zero))
    x = x_scr[...].astype(BF16)
    for gi in range(2):
        z = z_ref[:, gi * gc:(gi + 1) * gc]
        y = (jnp.dot(z, wt_ref[gi], preferred_element_type=F32)
             + jnp.dot(x, wx_ref[gi], preferred_element_type=F32))
        y = y + d_ref[:, gi * gc:(gi + 1) * gc] * z.astype(F32)
        o_ref[:, gi * gc:(gi + 1) * gc] = jax.nn.gelu(y).astype(o_ref.dtype)


def _s5_core(z, w_in, w_toep, w_x, dec, d_perm, bsz):
    nrow, width = z.shape
    wspec = pl.BlockSpec((2, GROUP_COLS, GROUP_COLS), lambda k: (k, 0, 0))
    return pl.pallas_call(
        functools.partial(_s5_core_kernel, bsz=bsz),
        grid=(N_PAIRS,),
        in_specs=[
            pl.BlockSpec((nrow, PAIR_COLS), lambda k: (0, k)),
            wspec, wspec, wspec,
            pl.BlockSpec((1, GROUP_COLS), lambda k: (0, k)),
            pl.BlockSpec((1, PAIR_COLS), lambda k: (0, k)),
        ],
        out_specs=pl.BlockSpec((nrow, PAIR_COLS), lambda k: (0, k)),
        out_shape=jax.ShapeDtypeStruct((nrow, width), BF16),
        scratch_shapes=[pltpu.VMEM((nrow, GROUP_COLS), F32), pltpu.VMEM((nrow, GROUP_COLS), F32)],
        compiler_params=_cparams(("parallel",)),
        name="s5_core",
    )(z, w_in, w_toep, w_x, dec, d_perm)


def _s5_weights(lam_re, lam_im, log_dt, b_re, b_im, c_re, c_im, d_skip):
    g, p, c16 = SSM_GROUPS, SSM_STATE, SSM_GROUP
    lam = lax.complex(lam_re.astype(F32), lam_im.astype(F32))
    dt = jnp.exp(log_dt.astype(F32))[:, None]
    steps = jnp.arange(CHUNK + 1, dtype=F32)
    apow = jnp.exp((lam * dt)[None] * steps[:, None, None])
    a = apow[1]
    bbar = ((a - 1.0) / lam)[..., None] * lax.complex(b_re.astype(F32), b_im.astype(F32))
    cmat = lax.complex(c_re.astype(F32), c_im.astype(F32))
    odd = (jnp.arange(g) % 2)[:, None, None]

    win = apow[CHUNK - 1::-1][..., None] * bbar[None]
    win = jnp.transpose(win, (1, 0, 3, 2)).reshape(g, GROUP_COLS, p)
    zeros = jnp.zeros_like(win.real)
    w_in = jnp.concatenate([
        jnp.where(odd == 0, win.real, zeros), jnp.where(odd == 1, win.real, zeros),
        jnp.where(odd == 0, win.imag, zeros), jnp.where(odd == 1, win.imag, zeros)], axis=-1)

    kern = jnp.einsum('gop,kgp,gpi->kgoi', cmat, apow[:CHUNK], bbar, precision=HIGHEST).real
    lag = jnp.arange(CHUNK)[None, :] - jnp.arange(CHUNK)[:, None]
    toep = jnp.where((lag >= 0)[:, :, None, None, None], kern[jnp.clip(lag, 0, CHUNK - 1)], 0.0)
    w_toep = jnp.transpose(toep, (2, 0, 4, 1, 3)).reshape(g, GROUP_COLS, GROUP_COLS)

    cw = cmat[None] * apow[1:, :, None, :]
    cw = jnp.transpose(cw, (1, 3, 0, 2)).reshape(g, p, GROUP_COLS)
    zx = jnp.zeros_like(cw.real)
    w_x = jnp.concatenate([
        jnp.where(odd == 0, cw.real, zx), jnp.where(odd == 1, cw.real, zx),
        jnp.where(odd == 0, -cw.imag, zx), jnp.where(odd == 1, -cw.imag, zx)], axis=1)

    a16 = apow[CHUNK].reshape(N_PAIRS, 2 * p)
    dec = jnp.concatenate([a16.real, a16.imag], axis=-1).reshape(1, N_PAIRS * GROUP_COLS)
    d_perm = jnp.broadcast_to(d_skip.astype(F32).reshape(g, 1, c16), (g, CHUNK, c16)).reshape(1, g * GROUP_COLS)
    return w_in.astype(BF16), w_toep.astype(BF16), w_x.astype(BF16), dec, d_perm


def _mlp(u, w1_ref, w2_ref):
    d = D_MODEL
    acc = None
    for k in range(D_FF // d):
        a = jnp.dot(u, w1_ref[:, k * d:(k + 1) * d], preferred_element_type=F32)
        a = jnp.square(jnp.maximum(a, 0.0)).astype(BF16)
        part = jnp.dot(a, w2_ref[k * d:(k + 1) * d, :], preferred_element_type=F32)
        acc = part if acc is None else acc + part
    return acc


def _final_norm(h, fg_ref):
    ms = jnp.mean(h * h, axis=-1, keepdims=True)
    return h * lax.rsqrt(ms + EPS) * fg_ref[...]


def _post_attn_kernel(*refs, final):
    if final:
        h_ref, y_ref, moda_ref, modm_ref, g_ref, wp_ref, w1_ref, w2_ref, fg_ref, o_ref = refs
    else:
        h_ref, y_ref, moda_ref, modm_ref, g_ref, wp_ref, w1_ref, w2_ref, o_ref = refs
    d = D_MODEL
    ymix = jnp.dot(y_ref[...], wp_ref[...], preferred_element_type=F32)
    h1 = h_ref[...] + moda_ref[:, 2 * d:3 * d] * ymix
    u = _normmod(h1, g_ref[...], modm_ref[:, :d], modm_ref[:, d:2 * d]).astype(BF16)
    h2 = h1 + modm_ref[:, 2 * d:3 * d] * _mlp(u, w1_ref, w2_ref)
    o_ref[...] = _final_norm(h2, fg_ref) if final else h2


def _post_attn(h, o, moda, modm, g2, w_o, w1, w2, final_g, tm=512):
    bsz, seq, d = h.shape
    per_b = seq // tm
    row_spec = pl.BlockSpec((None, tm, d), lambda i: (i // per_b, i % per_b, 0))
    final = final_g is not None
    in_specs = [
        row_spec, row_spec,
        pl.BlockSpec((None, 1, moda.shape[-1]), lambda i: (i // per_b, 0, 0)),
        pl.BlockSpec((None, 1, modm.shape[-1]), lambda i: (i // per_b, 0, 0)),
        pl.BlockSpec((1, d), lambda i: (0, 0)),
        _resident(w_o.shape, lambda i: (0, 0)),
        _resident((d, D_FF), lambda i: (0, 0)),
        _resident((D_FF, d), lambda i: (0, 0)),
    ]
    args = [h, o, moda, modm, g2.reshape(1, d), w_o, w1, w2]
    if final:
        in_specs.append(pl.BlockSpec((1, d), lambda i: (0, 0)))
        args.append(final_g.reshape(1, d))
    return pl.pallas_call(
        functools.partial(_post_attn_kernel, final=final),
        grid=(bsz * per_b,),
        in_specs=in_specs,
        out_specs=row_spec,
        out_shape=jax.ShapeDtypeStruct((bsz, seq, d), F32),
        compiler_params=_cparams(("parallel",)),
        name="post_attn_mlp",
    )(*args)


def _post_s5_kernel(h_ref, z_ref, moda_ref, modm_ref, g_ref, permt_ref, wp_ref, w1_ref, w2_ref, o_ref,
                    z_scr, h1_scr, u_scr):
    d = D_MODEL
    bsz, tt, _ = h_ref.shape
    rows = z_ref.shape[0]
    for j in range(d // LANES):
        for half in range(CHUNK // 8):
            tiles = []
            for gl in range(8):
                col = (8 * j + gl) * GROUP_COLS + half * LANES
                tiles.append(z_ref[:, col:col + LANES].astype(F32))
            outs = _regroup8(tiles)
            for tl in range(8):
                t = 8 * half + tl
                z_scr[t * rows:(t + 1) * rows, j * LANES:(j + 1) * LANES] = outs[tl].astype(BF16)
    zn = jnp.dot(permt_ref[...], z_scr[...], preferred_element_type=F32).astype(BF16)
    zz = jnp.dot(zn, wp_ref[...], preferred_element_type=F32)
    ymix = zz[:, :d] * jax.nn.sigmoid(zz[:, d:])
    g = g_ref[...]
    for b in range(bsz):
        sl = slice(b * tt, (b + 1) * tt)
        h1 = h_ref[b] + moda_ref[b, :, 2 * d:3 * d] * ymix[sl, :]
        h1_scr[sl, :] = h1
        u_scr[sl, :] = _normmod(h1, g, modm_ref[b, :, :d], modm_ref[b, :, d:2 * d]).astype(BF16)
    acc = _mlp(u_scr[...], w1_ref, w2_ref)
    for b in range(bsz):
        sl = slice(b * tt, (b + 1) * tt)
        o_ref[b] = h1_scr[sl, :] + modm_ref[b, :, 2 * d:3 * d] * acc[sl, :]


def _post_s5(h, zo, moda, modm, g2, permt, w_glu, w1, w2):
    bsz, seq, d = h.shape
    tt = S5_TILE_TOKENS
    rows = bsz * tt // CHUNK
    h_spec = pl.BlockSpec((bsz, tt, d), lambda i: (0, i, 0))
    return pl.pallas_call(
        _post_s5_kernel,
        grid=(seq // tt,),
        in_specs=[
            h_spec,
            pl.BlockSpec((rows, CHUNK * d), lambda i: (i, 0)),
            pl.BlockSpec(moda.shape, lambda i: (0, 0, 0)),
            pl.BlockSpec(modm.shape, lambda i: (0, 0, 0)),
            pl.BlockSpec((1, d), lambda i: (0, 0)),
            pl.BlockSpec(permt.shape, lambda i: (0, 0)),
            _resident(w_glu.shape, lambda i: (0, 0)),
            _resident((d, D_FF), lambda i: (0, 0)),
            _resident((D_FF, d), lambda i: (0, 0)),
        ],
        out_specs=h_spec,
        out_shape=jax.ShapeDtypeStruct((bsz, seq, d), F32),
        scratch_shapes=[pltpu.VMEM((bsz * tt, d), BF16), pltpu.VMEM((bsz * tt, d), F32),
                        pltpu.VMEM((bsz * tt, d), BF16)],
        compiler_params=_cparams(("parallel",)),
        name="post_s5_mlp",
    )(h, zo, moda, modm, g2.reshape(1, d), permt, w_glu, w1, w2)


def kernel(x, c, ln_g, ada_w, ada_b, ssm_lam_re, ssm_lam_im, ssm_log_dt, ssm_b_re, ssm_b_im, ssm_c_re, ssm_c_im, ssm_d, ssm_w_glu, kv_g, kv_ada_w, kv_ada_b, w_kv, attn_w_q, attn_w_o, mlp_w1, mlp_w2, final_g):
    bsz, seq, d = x.shape
    depth = ln_g.shape[0]
    n_s5 = ssm_lam_re.shape[0]
    assert d == D_MODEL and seq % (DILATIONS[-1] * ATT_BLOCK) == 0

    mods = _ada_mods(c, ada_w.reshape(depth * 2, d, 3 * d), ada_b.reshape(depth * 2, 3 * d))
    mods = mods.reshape(depth, 2, bsz, 1, 3 * d)
    kv_mod = _ada_mods(c, kv_ada_w[None], kv_ada_b[None]).reshape(bsz, 1, 2 * d)

    w1 = mlp_w1.astype(BF16)
    w2 = mlp_w2.astype(BF16)
    perm_np = _s5_tile_perm(bsz)
    perm = jnp.asarray(perm_np, BF16)
    permt = jnp.asarray(perm_np.T, BF16)

    h = x
    for layer in range(n_s5):
        w_in, w_toep, w_x, dec, d_perm = _s5_weights(
            ssm_lam_re[layer], ssm_lam_im[layer], ssm_log_dt[layer], ssm_b_re[layer], ssm_b_im[layer],
            ssm_c_re[layer], ssm_c_im[layer], ssm_d[layer])
        z = _s5_pre(h, mods[layer, 0], ln_g[layer, 0], perm)
        zo = _s5_core(z, w_in, w_toep, w_x, dec, d_perm, bsz)
        h = _post_s5(h, zo, mods[layer, 0], mods[layer, 1], ln_g[layer, 1], permt,
                     ssm_w_glu[layer].astype(BF16), w1[layer], w2[layer])

    kv = _norm_proj(h, kv_mod, kv_g, w_kv.astype(BF16))
    for layer in range(n_s5, depth):
        j = layer - n_s5
        q = _norm_proj(h, mods[layer, 0], ln_g[layer, 0], attn_w_q[j].astype(BF16))
        o = _attention(q, kv)
        h = _post_attn(h, o, mods[layer, 0], mods[layer, 1], ln_g[layer, 1],
                       attn_w_o[j].astype(BF16), w1[layer], w2[layer],
                       final_g if layer == depth - 1 else None)
    return h
```

```python
import functools

import numpy as np

import jax
import jax.numpy as jnp
from jax import lax
from jax.experimental import pallas as pl
from jax.experimental.pallas import tpu as pltpu

F32 = jnp.float32
BF16 = jnp.bfloat16

D_MODEL = 1024
SSM_GROUP = 16
SSM_GROUPS = D_MODEL // SSM_GROUP
SSM_STATE = 64
HEAD_DIM = 64
N_HEADS = D_MODEL // HEAD_DIM
DILATIONS = (1, 4, 16)
N_BRANCHES = len(DILATIONS)
ATT_BLOCK = 128
ATT_GROUP = 8
D_FF = 4 * D_MODEL
EPS = 1e-6
NEG = -1e30

LANES = 128
CHUNK = 16
GROUP_COLS = CHUNK * SSM_GROUP
PAIR_COLS = 2 * GROUP_COLS
N_PAIRS = SSM_GROUPS // 2
S5_TILE_TOKENS = 2 * CHUNK
PROJ_TILE = 512
VMEM_LIMIT = 56 * 1024 * 1024

HIGHEST = lax.Precision.HIGHEST
NT_DIMS = (((1,), (1,)), ((), ()))


def _cparams(sem):
    return pltpu.CompilerParams(dimension_semantics=sem, vmem_limit_bytes=VMEM_LIMIT)


def _resident(shape, index_map):
    return pl.BlockSpec(shape, index_map, pipeline_mode=pl.Buffered(1))


def _normmod(x, g, shift, scale):
    ms = jnp.mean(x * x, axis=-1, keepdims=True)
    return (x * lax.rsqrt(ms + EPS) * g) * (1.0 + scale) + shift


def _ada_kernel(c_ref, w_ref, b_ref, o_ref):
    c = c_ref[...]
    sc = c * jax.nn.sigmoid(c)
    o_ref[...] = jnp.dot(sc, w_ref[...], preferred_element_type=F32, precision=HIGHEST) + b_ref[...]


def _ada_mods(c, w, b):
    n, d, width = w.shape
    bsz = c.shape[0]
    tn = 1024
    return pl.pallas_call(
        _ada_kernel,
        grid=(n, width // tn),
        in_specs=[
            pl.BlockSpec((bsz, d), lambda i, j: (0, 0)),
            pl.BlockSpec((None, d, tn), lambda i, j: (i, 0, j)),
            pl.BlockSpec((None, 1, tn), lambda i, j: (i, 0, j)),
        ],
        out_specs=pl.BlockSpec((None, bsz, tn), lambda i, j: (i, 0, j)),
        out_shape=jax.ShapeDtypeStruct((n, bsz, width), F32),
        compiler_params=_cparams(("parallel", "parallel")),
        name="ada_mods",
    )(c, w, b.reshape(n, 1, width))


def _proj_kernel(h_ref, mod_ref, g_ref, w_ref, o1_ref, o2_ref, o3_ref, us, us4, up4, up16):
    d = D_MODEL
    tm = h_ref.shape[0]
    d1 = DILATIONS[1]
    q4, q16 = tm // d1, tm // DILATIONS[2]
    cw = o1_ref.shape[1]
    u = _normmod(h_ref[...], g_ref[...], mod_ref[:, :d], mod_ref[:, d:2 * d])
    for l in range(d // LANES):
        us[l] = u[:, l * LANES:(l + 1) * LANES]
    ub = u.astype(BF16)
    for n in range(cw // d):
        cols = slice(n * d, (n + 1) * d)
        o1_ref[:, cols] = jnp.dot(ub, w_ref[:, cols], preferred_element_type=F32).astype(o1_ref.dtype)
    for l in range(d // LANES):
        for a in range(d1):
            part = us[l, pl.ds(a, q4, stride=d1), :]
            us4[l, a * q4:(a + 1) * q4, :] = part
            up4[a * q4:(a + 1) * q4, l * LANES:(l + 1) * LANES] = part.astype(BF16)
    for n in range(cw // d):
        res = jnp.dot(up4[...], w_ref[:, cw + n * d:cw + (n + 1) * d], preferred_element_type=F32)
        for a in range(d1):
            o2_ref[a, :, n * d:(n + 1) * d] = res[a * q4:(a + 1) * q4, :].astype(o2_ref.dtype)
    for l in range(d // LANES):
        for a1 in range(d1):
            for a2 in range(d1):
                r = d1 * a2 + a1
                up16[r * q16:(r + 1) * q16, l * LANES:(l + 1) * LANES] = us4[
                    l, pl.ds(a1 * q4 + a2, q16, stride=d1), :].astype(BF16)
    for n in range(cw // d):
        res = jnp.dot(up16[...], w_ref[:, 2 * cw + n * d:2 * cw + (n + 1) * d], preferred_element_type=F32)
        for r in range(DILATIONS[2]):
            o3_ref[r, :, n * d:(n + 1) * d] = res[r * q16:(r + 1) * q16, :].astype(o3_ref.dtype)


def _norm_proj(h, mod, g, w):
    bsz, seq, d = h.shape
    tm = PROJ_TILE
    cw = w.shape[1] // N_BRANCHES
    per_b = seq // tm
    d1, d2 = DILATIONS[1], DILATIONS[2]
    o1, o2, o3 = pl.pallas_call(
        _proj_kernel,
        grid=(bsz * per_b,),
        in_specs=[
            pl.BlockSpec((None, tm, d), lambda i: (i // per_b, i % per_b, 0)),
            pl.BlockSpec((None, 1, mod.shape[-1]), lambda i: (i // per_b, 0, 0)),
            pl.BlockSpec((1, d), lambda i: (0, 0)),
            _resident(w.shape, lambda i: (0, 0)),
        ],
        out_specs=[
            pl.BlockSpec((None, tm, cw), lambda i: (i // per_b, i % per_b, 0)),
            pl.BlockSpec((None, d1, None, tm // d1, cw), lambda i: (i // per_b, 0, i % per_b, 0, 0)),
            pl.BlockSpec((None, d2, None, tm // d2, cw), lambda i: (i // per_b, 0, i % per_b, 0, 0)),
        ],
        out_shape=[
            jax.ShapeDtypeStruct((bsz, seq, cw), BF16),
            jax.ShapeDtypeStruct((bsz, d1, per_b, tm // d1, cw), BF16),
            jax.ShapeDtypeStruct((bsz, d2, per_b, tm // d2, cw), BF16),
        ],
        scratch_shapes=[pltpu.VMEM((d // LANES, tm, LANES), F32), pltpu.VMEM((d // LANES, tm, LANES), F32),
                        pltpu.VMEM((tm, d), BF16), pltpu.VMEM((tm, d), BF16)],
        compiler_params=_cparams(("parallel",)),
        name="norm_proj",
    )(h, mod, g.reshape(1, d), w)
    return o1, o2.reshape(bsz, seq, cw), o3.reshape(bsz, seq, cw)


def _attn_group(blocks):
    blk = ATT_BLOCK
    lane = lax.broadcasted_iota(jnp.int32, (blk, LANES), 1)
    lo = lane < HEAD_DIM
    scores = []
    for qb, kcat, _, mask, _ in blocks:
        zero = jnp.zeros_like(qb)
        q2 = jnp.concatenate([jnp.where(lo, qb, zero), jnp.where(lo, zero, qb)], axis=0)
        q2 = q2 * jnp.asarray(HEAD_DIM ** -0.5, BF16)
        s = lax.dot_general(q2, kcat, NT_DIMS, preferred_element_type=F32)
        scores.append(jnp.where(mask, s, NEG))
    soft = []
    for s, (_, _, _, _, old) in zip(scores, blocks):
        ms, ps, alphas = [], [], []
        for hh in range(2):
            tiles = [s[hh * blk:(hh + 1) * blk, t * LANES:(t + 1) * LANES] for t in range(s.shape[1] // LANES)]
            mx = tiles[0]
            for t in tiles[1:]:
                mx = jnp.maximum(mx, t)
            m = jnp.broadcast_to(jnp.max(mx, axis=-1, keepdims=True), (blk, LANES))
            if old is not None:
                m = jnp.maximum(m, old[0][hh])
                alphas.append(jnp.exp(old[0][hh] - m))
            ms.append(m)
            ps.append(jnp.concatenate([jnp.exp(t - m).astype(BF16) for t in tiles], axis=1))
        soft.append((ms, jnp.concatenate(ps, axis=0), alphas))
    outs = []
    for (ms, p, alphas), (_, _, vcat, _, old) in zip(soft, blocks):
        pv = jnp.dot(p, vcat, preferred_element_type=F32)
        acc = jnp.where(lo, pv[:blk, :LANES], pv[blk:, :LANES])
        dens = [pv[:blk, LANES:], pv[blk:, LANES:]]
        if old is not None:
            acc = jnp.where(lo, alphas[0], alphas[1]) * old[1] + acc
            dens = [alphas[hh] * old[0][2 + hh] + dens[hh] for hh in range(2)]
        outs.append((ms + dens, acc))
    return outs


def _attn_kernel(q1, qd2, qd3, k1, kd2, kd3, v1, v2, v3, o_ref, tmp_nat, vd1, vd2, vd3,
                 acc_ref, st_ref, acc2_ref, st2_ref):
    blk = ATT_BLOCK
    seq = q1.shape[0]
    nblk = seq // blk
    d1, d2 = DILATIONS[1], DILATIONS[2]
    quarter = seq // d1
    per_res = quarter // blk
    assert d2 == d1 * d1 and seq == d2 * blk and nblk % ATT_GROUP == 0 and ATT_GROUP % per_res == 0
    ones = jnp.ones((seq, LANES), BF16)
    for v, vd in ((v1, vd1), (v2, vd2), (v3, vd3)):
        vd[:, :LANES] = v[...]
        vd[:, LANES:] = ones
    qq = lax.broadcasted_iota(jnp.int32, (2 * blk, 2 * blk), 0) & (blk - 1)
    kk = lax.broadcasted_iota(jnp.int32, (2 * blk, 2 * blk), 1)
    mask_pc = jnp.logical_or(jnp.logical_and(kk < blk, kk >= qq), jnp.logical_and(kk >= blk, kk - blk <= qq))
    mask_c = (lax.broadcasted_iota(jnp.int32, (2 * blk, blk), 1)
              <= (lax.broadcasted_iota(jnp.int32, (2 * blk, blk), 0) & (blk - 1)))

    def run_group(blocks, dests):
        for (st_dst, acc_dst, rows), (st, acc) in zip(dests, _attn_group(blocks)):
            for k in range(4):
                st_dst[k, rows, :] = st[k]
            acc_dst[rows, :] = acc

    def load_old(st_src, acc_src, rows):
        return [st_src[k, rows, :] for k in range(4)], acc_src[rows, :]

    def b0_group(ns, first):
        blocks, dests = [], []
        for n in ns:
            if first and n == 0:
                rows = pl.ds(0, blk)
                blocks.append((q1[rows, :], k1[rows, :], vd1[rows, :], mask_c, None))
            else:
                rows = pl.ds(pl.multiple_of(n * blk, blk), blk)
                krows = pl.ds(pl.multiple_of((n - 1) * blk, blk), 2 * blk)
                blocks.append((q1[rows, :], k1[krows, :], vd1[krows, :], mask_pc, None))
            dests.append((st_ref, acc_ref, rows))
        run_group(blocks, dests)

    b0_group(list(range(ATT_GROUP)), True)

    def b0_body(i, carry):
        b0_group([ATT_GROUP * i + j for j in range(ATT_GROUP)], False)
        return carry

    lax.fori_loop(1, nblk // ATT_GROUP, b0_body, 0)

    res_per_trip = ATT_GROUP // per_res

    def b1_body(i, carry):
        blocks, dests = [], []
        for jr in range(res_per_trip):
            a1 = res_per_trip * i + jr
            base = a1 * quarter
            for n in range(per_res):
                rows = pl.ds(a1 + d1 * blk * n, blk, stride=d1)
                qrows = pl.ds(pl.multiple_of(base + n * blk, blk), blk)
                if n == 0:
                    krows, mask = qrows, mask_c
                else:
                    krows, mask = pl.ds(pl.multiple_of(base + (n - 1) * blk, blk), 2 * blk), mask_pc
                blocks.append((qd2[qrows, :], kd2[krows, :], vd2[krows, :], mask,
                               load_old(st_ref, acc_ref, rows)))
                dests.append((st_ref, acc_ref, rows))
        run_group(blocks, dests)
        return carry

    lax.fori_loop(0, d1 // res_per_trip, b1_body, 0)

    for a in range(d1):
        acc2_ref[a * quarter:(a + 1) * quarter, :] = acc_ref[pl.ds(a, quarter, stride=d1), :]
        for k in range(4):
            st2_ref[k, a * quarter:(a + 1) * quarter, :] = st_ref[k, pl.ds(a, quarter, stride=d1), :]

    def b2_body(i, carry):
        blocks, dests = [], []
        for j in range(ATT_GROUP):
            a1, a2 = j % d1, (ATT_GROUP // d1) * i + j // d1
            qrows = pl.ds(pl.multiple_of((ATT_GROUP * i + j) * blk, blk), blk)
            rows = pl.ds(a1 * quarter + a2, blk, stride=d1)
            blocks.append((qd3[qrows, :], kd3[qrows, :], vd3[qrows, :], mask_c,
                           load_old(st2_ref, acc2_ref, rows)))
            dests.append((st2_ref, acc2_ref, rows))
        run_group(blocks, dests)
        return carry

    lax.fori_loop(0, d2 // ATT_GROUP, b2_body, 0)

    lane = lax.broadcasted_iota(jnp.int32, (quarter, LANES), 1)
    for a in range(d1):
        seg = slice(a * quarter, (a + 1) * quarter)
        den = jnp.where(lane < HEAD_DIM, st2_ref[2, seg, :], st2_ref[3, seg, :])
        tmp_nat[pl.ds(a, quarter, stride=d1), :] = acc2_ref[a * quarter:(a + 1) * quarter, :] / den
    o_ref[...] = tmp_nat[...].astype(o_ref.dtype)


def _attention(qs, kvs):
    bsz, seq, _ = qs[0].shape
    pairs = D_MODEL // LANES

    def slab(col0):
        return pl.BlockSpec((None, seq, LANES), lambda b, hp: (b, 0, col0 + hp))

    in_specs = [slab(0)] * N_BRANCHES + [slab(0)] * N_BRANCHES + [slab(pairs)] * N_BRANCHES
    return pl.pallas_call(
        _attn_kernel,
        grid=(bsz, pairs),
        in_specs=in_specs,
        out_specs=slab(0),
        out_shape=jax.ShapeDtypeStruct((bsz, seq, D_MODEL), BF16),
        scratch_shapes=([pltpu.VMEM((seq, LANES), F32)] + [pltpu.VMEM((seq, 2 * LANES), BF16)] * 3
                        + [pltpu.VMEM((seq, LANES), F32), pltpu.VMEM((4, seq, LANES), F32)] * 2),
        compiler_params=_cparams(("parallel", "parallel")),
        name="attention",
    )(*qs, *kvs, *kvs)


def _regroup8(tiles):
    t = list(tiles)
    lane = lax.broadcasted_iota(jnp.int32, t[0].shape, 1)
    piece = lane // SSM_GROUP
    for dist in (4, 2, 1):
        bit = (piece & dist) != 0
        shift = dist * SSM_GROUP
        for a in range(8):
            if a & dist:
                continue
            b = a | dist
            ta, tb = t[a], t[b]
            t[a] = jnp.where(bit, pltpu.roll(tb, shift, 1), ta)
            t[b] = jnp.where(bit, tb, pltpu.roll(ta, LANES - shift, 1))
    return t


def _s5_tile_perm(bsz):
    tt = S5_TILE_TOKENS
    n = bsz * tt
    perm = np.zeros((n, n), np.float32)
    for b in range(bsz):
        for c in range(tt // CHUNK):
            for t in range(CHUNK):
                perm[(t * (tt // CHUNK) + c) * bsz + b, b * tt + c * CHUNK + t] = 1.0
    return perm


def _s5_pre_kernel(h_ref, mod_ref, g_ref, perm_ref, z_ref, u_scr, up_scr):
    d = D_MODEL
    bsz, tt, _ = h_ref.shape
    g = g_ref[...]
    for b in range(bsz):
        u_scr[b * tt:(b + 1) * tt, :] = _normmod(
            h_ref[b], g, mod_ref[b, :, :d], mod_ref[b, :, d:2 * d]).astype(BF16)
    up_scr[...] = jnp.dot(perm_ref[...], u_scr[...], preferred_element_type=F32)
    rows = z_ref.shape[0]
    for j in range(d // LANES):
        for half in range(CHUNK // 8):
            tiles = [up_scr[(8 * half + tl) * rows:(8 * half + tl + 1) * rows, j * LANES:(j + 1) * LANES]
                     for tl in range(8)]
            outs = _regroup8(tiles)
            for gl in range(8):
                col = (8 * j + gl) * GROUP_COLS + half * LANES
                z_ref[:, col:col + LANES] = outs[gl].astype(z_ref.dtype)


def _s5_pre(h, mod, g, perm):
    bsz, seq, d = h.shape
    tt = S5_TILE_TOKENS
    rows = bsz * tt // CHUNK
    return pl.pallas_call(
        _s5_pre_kernel,
        grid=(seq // tt,),
        in_specs=[
            pl.BlockSpec((bsz, tt, d), lambda i: (0, i, 0)),
            pl.BlockSpec(mod.shape, lambda i: (0, 0, 0)),
            pl.BlockSpec((1, d), lambda i: (0, 0)),
            pl.BlockSpec(perm.shape, lambda i: (0, 0)),
        ],
        out_specs=pl.BlockSpec((rows, CHUNK * d), lambda i: (i, 0)),
        out_shape=jax.ShapeDtypeStruct((bsz * seq // CHUNK, CHUNK * d), BF16),
        scratch_shapes=[pltpu.VMEM((bsz * tt, d), BF16), pltpu.VMEM((bsz * tt, d), F32)],
        compiler_params=_cparams(("parallel",)),
        name="s5_pre",
    )(h, mod, g.reshape(1, d), perm)


def _s5_core_kernel(z_ref, win_ref, wt_ref, wx_ref, dec_ref, d_ref, o_ref, s_scr, x_scr, *, bsz):
    gc = GROUP_COLS
    nchunk = z_ref.shape[0] // bsz
    s_scr[...] = (jnp.dot(z_ref[:, :gc], win_ref[0], preferred_element_type=F32)
                  + jnp.dot(z_ref[:, gc:], win_ref[1], preferred_element_type=F32))
    ar, ai = dec_ref[:, :LANES], dec_ref[:, LANES:]

    def step(c, carry):
        xr, xi = carry
        rows = pl.ds(pl.multiple_of(c * bsz, bsz), bsz)
        x_scr[rows, :LANES] = xr
        x_scr[rows, LANES:] = xi
        return (ar * xr - ai * xi + s_scr[rows, :LANES], ar * xi + ai * xr + s_scr[rows, LANES:])

    zero = jnp.zeros((bsz, LANES), F32)
    lax.fori_loop(0, nchunk, step, (zero, zero))
    x = x_scr[...].astype(BF16)
    for gi in range(2):
        z = z_ref[:, gi * gc:(gi + 1) * gc]
        y = (jnp.dot(z, wt_ref[gi], preferred_element_type=F32)
             + jnp.dot(x, wx_ref[gi], preferred_element_type=F32))
        y = y + d_ref[:, gi * gc:(gi + 1) * gc] * z.astype(F32)
        o_ref[:, gi * gc:(gi + 1) * gc] = jax.nn.gelu(y).astype(o_ref.dtype)


def _s5_core(z, w_in, w_toep, w_x, dec, d_perm, bsz):
    nrow, width = z.shape
    wspec = pl.BlockSpec((2, GROUP_COLS, GROUP_COLS), lambda k: (k, 0, 0))
    return pl.pallas_call(
        functools.partial(_s5_core_kernel, bsz=bsz),
        grid=(N_PAIRS,),
        in_specs=[
            pl.BlockSpec((nrow, PAIR_COLS), lambda k: (0, k)),
            wspec, wspec, wspec,
            pl.BlockSpec((1, GROUP_COLS), lambda k: (0, k)),
            pl.BlockSpec((1, PAIR_COLS), lambda k: (0, k)),
        ],
        out_specs=pl.BlockSpec((nrow, PAIR_COLS), lambda k: (0, k)),
        out_shape=jax.ShapeDtypeStruct((nrow, width), BF16),
        scratch_shapes=[pltpu.VMEM((nrow, GROUP_COLS), F32), pltpu.VMEM((nrow, GROUP_COLS), F32)],
        compiler_params=_cparams(("parallel",)),
        name="s5_core",
    )(z, w_in, w_toep, w_x, dec, d_perm)


def _s5_weights(lam_re, lam_im, log_dt, b_re, b_im, c_re, c_im, d_skip):
    g, p, c16 = SSM_GROUPS, SSM_STATE, SSM_GROUP
    lam = lax.complex(lam_re.astype(F32), lam_im.astype(F32))
    dt = jnp.exp(log_dt.astype(F32))[:, None]
    steps = jnp.arange(CHUNK + 1, dtype=F32)
    apow = jnp.exp((lam * dt)[None] * steps[:, None, None])
    a = apow[1]
    bbar = ((a - 1.0) / lam)[..., None] * lax.complex(b_re.astype(F32), b_im.astype(F32))
    cmat = lax.complex(c_re.astype(F32), c_im.astype(F32))
    odd = (jnp.arange(g) % 2)[:, None, None]

    win = apow[CHUNK - 1::-1][..., None] * bbar[None]
    win = jnp.transpose(win, (1, 0, 3, 2)).reshape(g, GROUP_COLS, p)
    zeros = jnp.zeros_like(win.real)
    w_in = jnp.concatenate([
        jnp.where(odd == 0, win.real, zeros), jnp.where(odd == 1, win.real, zeros),
        jnp.where(odd == 0, win.imag, zeros), jnp.where(odd == 1, win.imag, zeros)], axis=-1)

    kern = jnp.einsum('gop,kgp,gpi->kgoi', cmat, apow[:CHUNK], bbar, precision=HIGHEST).real
    lag = jnp.arange(CHUNK)[None, :] - jnp.arange(CHUNK)[:, None]
    toep = jnp.where((lag >= 0)[:, :, None, None, None], kern[jnp.clip(lag, 0, CHUNK - 1)], 0.0)
    w_toep = jnp.transpose(toep, (2, 0, 4, 1, 3)).reshape(g, GROUP_COLS, GROUP_COLS)

    cw = cmat[None] * apow[1:, :, None, :]
    cw = jnp.transpose(cw, (1, 3, 0, 2)).reshape(g, p, GROUP_COLS)
    zx = jnp.zeros_like(cw.real)
    w_x = jnp.concatenate([
        jnp.where(odd == 0, cw.real, zx), jnp.where(odd == 1, cw.real, zx),
        jnp.where(odd == 0, -cw.imag, zx), jnp.where(odd == 1, -cw.imag, zx)], axis=1)

    a16 = apow[CHUNK].reshape(N_PAIRS, 2 * p)
    dec = jnp.concatenate([a16.real, a16.imag], axis=-1).reshape(1, N_PAIRS * GROUP_COLS)
    d_perm = jnp.broadcast_to(d_skip.astype(F32).reshape(g, 1, c16), (g, CHUNK, c16)).reshape(1, g * GROUP_COLS)
    return w_in.astype(BF16), w_toep.astype(BF16), w_x.astype(BF16), dec, d_perm


def _mlp(u, w1_ref, w2_ref):
    d = D_MODEL
    acc = None
    for k in range(D_FF // d):
        a = jnp.dot(u, w1_ref[:, k * d:(k + 1) * d], preferred_element_type=F32)
        a = jnp.square(jnp.maximum(a, 0.0)).astype(BF16)
        part = jnp.dot(a, w2_ref[k * d:(k + 1) * d, :], preferred_element_type=F32)
        acc = part if acc is None else acc + part
    return acc


def _final_norm(h, fg_ref):
    ms = jnp.mean(h * h, axis=-1, keepdims=True)
    return h * lax.rsqrt(ms + EPS) * fg_ref[...]


def _post_attn_kernel(*refs, final):
    if final:
        h_ref, y_ref, moda_ref, modm_ref, g_ref, wp_ref, w1_ref, w2_ref, fg_ref, o_ref = refs
    else:
        h_ref, y_ref, moda_ref, modm_ref, g_ref, wp_ref, w1_ref, w2_ref, o_ref = refs
    d = D_MODEL
    ymix = jnp.dot(y_ref[...], wp_ref[...], preferred_element_type=F32)
    h1 = h_ref[...] + moda_ref[:, 2 * d:3 * d] * ymix
    u = _normmod(h1, g_ref[...], modm_ref[:, :d], modm_ref[:, d:2 * d]).astype(BF16)
    h2 = h1 + modm_ref[:, 2 * d:3 * d] * _mlp(u, w1_ref, w2_ref)
    o_ref[...] = _final_norm(h2, fg_ref) if final else h2


def _post_attn(h, o, moda, modm, g2, w_o, w1, w2, final_g, tm=512):
    bsz, seq, d = h.shape
    per_b = seq // tm
    row_spec = pl.BlockSpec((None, tm, d), lambda i: (i // per_b, i % per_b, 0))
    final = final_g is not None
    in_specs = [
        row_spec, row_spec,
        pl.BlockSpec((None, 1, moda.shape[-1]), lambda i: (i // per_b, 0, 0)),
        pl.BlockSpec((None, 1, modm.shape[-1]), lambda i: (i // per_b, 0, 0)),
        pl.BlockSpec((1, d), lambda i: (0, 0)),
        _resident(w_o.shape, lambda i: (0, 0)),
        _resident((d, D_FF), lambda i: (0, 0)),
        _resident((D_FF, d), lambda i: (0, 0)),
    ]
    args = [h, o, moda, modm, g2.reshape(1, d), w_o, w1, w2]
    if final:
        in_specs.append(pl.BlockSpec((1, d), lambda i: (0, 0)))
        args.append(final_g.reshape(1, d))
    return pl.pallas_call(
        functools.partial(_post_attn_kernel, final=final),
        grid=(bsz * per_b,),
        in_specs=in_specs,
        out_specs=row_spec,
        out_shape=jax.ShapeDtypeStruct((bsz, seq, d), F32),
        compiler_params=_cparams(("parallel",)),
        name="post_attn_mlp",
    )(*args)


def _post_s5_kernel(h_ref, z_ref, moda_ref, modm_ref, g_ref, permt_ref, wp_ref, w1_ref, w2_ref, o_ref,
                    z_scr, h1_scr, u_scr):
    d = D_MODEL
    bsz, tt, _ = h_ref.shape
    rows = z_ref.shape[0]
    for j in range(d // LANES):
        for half in range(CHUNK // 8):
            tiles = []
            for gl in range(8):
                col = (8 * j + gl) * GROUP_COLS + half * LANES
                tiles.append(z_ref[:, col:col + LANES].astype(F32))
            outs = _regroup8(tiles)
            for tl in range(8):
                t = 8 * half + tl
                z_scr[t * rows:(t + 1) * rows, j * LANES:(j + 1) * LANES] = outs[tl].astype(BF16)
    zn = jnp.dot(permt_ref[...], z_scr[...], preferred_element_type=F32).astype(BF16)
    zz = jnp.dot(zn, wp_ref[...], preferred_element_type=F32)
    ymix = zz[:, :d] * jax.nn.sigmoid(zz[:, d:])
    g = g_ref[...]
    for b in range(bsz):
        sl = slice(b * tt, (b + 1) * tt)
        h1 = h_ref[b] + moda_ref[b, :, 2 * d:3 * d] * ymix[sl, :]
        h1_scr[sl, :] = h1
        u_scr[sl, :] = _normmod(h1, g, modm_ref[b, :, :d], modm_ref[b, :, d:2 * d]).astype(BF16)
    acc = _mlp(u_scr[...], w1_ref, w2_ref)
    for b in range(bsz):
        sl = slice(b * tt, (b + 1) * tt)
        o_ref[b] = h1_scr[sl, :] + modm_ref[b, :, 2 * d:3 * d] * acc[sl, :]


def _post_s5(h, zo, moda, modm, g2, permt, w_glu, w1, w2):
    bsz, seq, d = h.shape
    tt = S5_TILE_TOKENS
    rows = bsz * tt // CHUNK
    h_spec = pl.BlockSpec((bsz, tt, d), lambda i: (0, i, 0))
    return pl.pallas_call(
        _post_s5_kernel,
        grid=(seq // tt,),
        in_specs=[
            h_spec,
            pl.BlockSpec((rows, CHUNK * d), lambda i: (i, 0)),
            pl.BlockSpec(moda.shape, lambda i: (0, 0, 0)),
            pl.BlockSpec(modm.shape, lambda i: (0, 0, 0)),
            pl.BlockSpec((1, d), lambda i: (0, 0)),
            pl.BlockSpec(permt.shape, lambda i: (0, 0)),
            _resident(w_glu.shape, lambda i: (0, 0)),
            _resident((d, D_FF), lambda i: (0, 0)),
            _resident((D_FF, d), lambda i: (0, 0)),
        ],
        out_specs=h_spec,
        out_shape=jax.ShapeDtypeStruct((bsz, seq, d), F32),
        scratch_shapes=[pltpu.VMEM((bsz * tt, d), BF16), pltpu.VMEM((bsz * tt, d), F32),
                        pltpu.VMEM((bsz * tt, d), BF16)],
        compiler_params=_cparams(("parallel",)),
        name="post_s5_mlp",
    )(h, zo, moda, modm, g2.reshape(1, d), permt, w_glu, w1, w2)


def kernel(x, c, ln_g, ada_w, ada_b, ssm_lam_re, ssm_lam_im, ssm_log_dt, ssm_b_re, ssm_b_im, ssm_c_re, ssm_c_im, ssm_d, ssm_w_glu, kv_g, kv_ada_w, kv_ada_b, w_kv, attn_w_q, attn_w_o, mlp_w1, mlp_w2, final_g):
    bsz, seq, d = x.shape
    depth = ln_g.shape[0]
    n_s5 = ssm_lam_re.shape[0]
    assert d == D_MODEL and seq % (DILATIONS[-1] * ATT_BLOCK) == 0

    mods = _ada_mods(c, ada_w.reshape(depth * 2, d, 3 * d), ada_b.reshape(depth * 2, 3 * d))
    mods = mods.reshape(depth, 2, bsz, 1, 3 * d)
    kv_mod = _ada_mods(c, kv_ada_w[None], kv_ada_b[None]).reshape(bsz, 1, 2 * d)

    w1 = mlp_w1.astype(BF16)
    w2 = mlp_w2.astype(BF16)
    perm_np = _s5_tile_perm(bsz)
    perm = jnp.asarray(perm_np, BF16)
    permt = jnp.asarray(perm_np.T, BF16)

    h = x
    for layer in range(n_s5):
        w_in, w_toep, w_x, dec, d_perm = _s5_weights(
            ssm_lam_re[layer], ssm_lam_im[layer], ssm_log_dt[layer], ssm_b_re[layer], ssm_b_im[layer],
            ssm_c_re[layer], ssm_c_im[layer], ssm_d[layer])
        z = _s5_pre(h, mods[layer, 0], ln_g[layer, 0], perm)
        zo = _s5_core(z, w_in, w_toep, w_x, dec, d_perm, bsz)
        h = _post_s5(h, zo, mods[layer, 0], mods[layer, 1], ln_g[layer, 1], permt,
                     ssm_w_glu[layer].astype(BF16), w1[layer], w2[layer])

    w_kv_b = jnp.transpose(w_kv.reshape(d, 2, N_BRANCHES, d), (0, 2, 1, 3)).reshape(d, 2 * N_BRANCHES * d)
    kvs = _norm_proj(h, kv_mod, kv_g, w_kv_b.astype(BF16))
    for layer in range(n_s5, depth):
        j = layer - n_s5
        qs = _norm_proj(h, mods[layer, 0], ln_g[layer, 0], attn_w_q[j].astype(BF16))
        o = _attention(qs, kvs)
        h = _post_attn(h, o, mods[layer, 0], mods[layer, 1], ln_g[layer, 1],
                       attn_w_o[j].astype(BF16), w1[layer], w2[layer],
                       final_g if layer == depth - 1 else None)
    return h
```

```python
import functools

import numpy as np

import jax
import jax.numpy as jnp
from jax import lax
from jax.experimental import pallas as pl
from jax.experimental.pallas import tpu as pltpu

F32 = jnp.float32
BF16 = jnp.bfloat16

D_MODEL = 1024
SSM_GROUP = 16
SSM_GROUPS = D_MODEL // SSM_GROUP
SSM_STATE = 64
HEAD_DIM = 64
N_HEADS = D_MODEL // HEAD_DIM
DILATIONS = (1, 4, 16)
N_BRANCHES = len(DILATIONS)
ATT_BLOCK = 128
ATT_GROUP = 8
D_FF = 4 * D_MODEL
EPS = 1e-6
NEG = -1e30

LANES = 128
CHUNK = 16
GROUP_COLS = CHUNK * SSM_GROUP
PAIR_COLS = 2 * GROUP_COLS
N_PAIRS = SSM_GROUPS // 2
S5_TILE_TOKENS = 2 * CHUNK
PROJ_TILE = 512
VMEM_LIMIT = 56 * 1024 * 1024

HIGHEST = lax.Precision.HIGHEST
NT_DIMS = (((1,), (1,)), ((), ()))


def _cparams(sem):
    return pltpu.CompilerParams(dimension_semantics=sem, vmem_limit_bytes=VMEM_LIMIT)


def _resident(shape, index_map):
    return pl.BlockSpec(shape, index_map, pipeline_mode=pl.Buffered(1))


def _normmod(x, g, shift, scale):
    ms = jnp.mean(x * x, axis=-1, keepdims=True)
    return (x * lax.rsqrt(ms + EPS) * g) * (1.0 + scale) + shift


def _ada_kernel(c_ref, w_ref, b_ref, o_ref):
    c = c_ref[...]
    sc = c * jax.nn.sigmoid(c)
    o_ref[...] = jnp.dot(sc, w_ref[...], preferred_element_type=F32, precision=HIGHEST) + b_ref[...]


def _ada_mods(c, w, b):
    n, d, width = w.shape
    bsz = c.shape[0]
    tn = 1024
    return pl.pallas_call(
        _ada_kernel,
        grid=(n, width // tn),
        in_specs=[
            pl.BlockSpec((bsz, d), lambda i, j: (0, 0)),
            pl.BlockSpec((None, d, tn), lambda i, j: (i, 0, j)),
            pl.BlockSpec((None, 1, tn), lambda i, j: (i, 0, j)),
        ],
        out_specs=pl.BlockSpec((None, bsz, tn), lambda i, j: (i, 0, j)),
        out_shape=jax.ShapeDtypeStruct((n, bsz, width), F32),
        compiler_params=_cparams(("parallel", "parallel")),
        name="ada_mods",
    )(c, w, b.reshape(n, 1, width))


def _proj_kernel(h_ref, mod_ref, g_ref, w_ref, o1_ref, o2_ref, o3_ref, us, us4, up4, up16, *, wcols):
    d = D_MODEL
    tm = h_ref.shape[0]
    d1 = DILATIONS[1]
    q4, q16 = tm // d1, tm // DILATIONS[2]
    cw = o1_ref.shape[1]
    u = _normmod(h_ref[...], g_ref[...], mod_ref[:, :d], mod_ref[:, d:2 * d])
    for l in range(d // LANES):
        us[l] = u[:, l * LANES:(l + 1) * LANES]
    ub = u.astype(BF16)
    for n in range(cw // d):
        cols = slice(n * d, (n + 1) * d)
        o1_ref[:, cols] = jnp.dot(ub, w_ref[:, wcols[0][n]:wcols[0][n] + d],
                                  preferred_element_type=F32).astype(o1_ref.dtype)
    for l in range(d // LANES):
        for a in range(d1):
            part = us[l, pl.ds(a, q4, stride=d1), :]
            us4[l, a * q4:(a + 1) * q4, :] = part
            up4[a * q4:(a + 1) * q4, l * LANES:(l + 1) * LANES] = part.astype(BF16)
    for n in range(cw // d):
        res = jnp.dot(up4[...], w_ref[:, wcols[1][n]:wcols[1][n] + d], preferred_element_type=F32)
        for a in range(d1):
            o2_ref[a, :, n * d:(n + 1) * d] = res[a * q4:(a + 1) * q4, :].astype(o2_ref.dtype)
    for l in range(d // LANES):
        for a1 in range(d1):
            for a2 in range(d1):
                r = d1 * a2 + a1
                up16[r * q16:(r + 1) * q16, l * LANES:(l + 1) * LANES] = us4[
                    l, pl.ds(a1 * q4 + a2, q16, stride=d1), :].astype(BF16)
    for n in range(cw // d):
        res = jnp.dot(up16[...], w_ref[:, wcols[2][n]:wcols[2][n] + d], preferred_element_type=F32)
        for r in range(DILATIONS[2]):
            o3_ref[r, :, n * d:(n + 1) * d] = res[r * q16:(r + 1) * q16, :].astype(o3_ref.dtype)


def _norm_proj(h, mod, g, w, wcols):
    bsz, seq, d = h.shape
    tm = PROJ_TILE
    cw = len(wcols[0]) * d
    per_b = seq // tm
    d1, d2 = DILATIONS[1], DILATIONS[2]
    o1, o2, o3 = pl.pallas_call(
        functools.partial(_proj_kernel, wcols=wcols),
        grid=(bsz * per_b,),
        in_specs=[
            pl.BlockSpec((None, tm, d), lambda i: (i // per_b, i % per_b, 0)),
            pl.BlockSpec((None, 1, mod.shape[-1]), lambda i: (i // per_b, 0, 0)),
            pl.BlockSpec((1, d), lambda i: (0, 0)),
            _resident(w.shape, lambda i: (0, 0)),
        ],
        out_specs=[
            pl.BlockSpec((None, tm, cw), lambda i: (i // per_b, i % per_b, 0)),
            pl.BlockSpec((None, d1, None, tm // d1, cw), lambda i: (i // per_b, 0, i % per_b, 0, 0)),
            pl.BlockSpec((None, d2, None, tm // d2, cw), lambda i: (i // per_b, 0, i % per_b, 0, 0)),
        ],
        out_shape=[
            jax.ShapeDtypeStruct((bsz, seq, cw), BF16),
            jax.ShapeDtypeStruct((bsz, d1, per_b, tm // d1, cw), BF16),
            jax.ShapeDtypeStruct((bsz, d2, per_b, tm // d2, cw), BF16),
        ],
        scratch_shapes=[pltpu.VMEM((d // LANES, tm, LANES), F32), pltpu.VMEM((d // LANES, tm, LANES), F32),
                        pltpu.VMEM((tm, d), BF16), pltpu.VMEM((tm, d), BF16)],
        compiler_params=_cparams(("parallel",)),
        name="norm_proj",
    )(h, mod, g.reshape(1, d), w)
    return o1, o2.reshape(bsz, seq, cw), o3.reshape(bsz, seq, cw)


def _attn_group(blocks):
    blk = ATT_BLOCK
    lane = lax.broadcasted_iota(jnp.int32, (blk, LANES), 1)
    lo = lane < HEAD_DIM
    scores = []
    for qb, kcat, _, mask, _ in blocks:
        zero = jnp.zeros_like(qb)
        q2 = jnp.concatenate([jnp.where(lo, qb, zero), jnp.where(lo, zero, qb)], axis=0)
        q2 = q2 * jnp.asarray(HEAD_DIM ** -0.5, BF16)
        s = lax.dot_general(q2, kcat, NT_DIMS, preferred_element_type=F32)
        scores.append(jnp.where(mask, s, NEG))
    soft = []
    for s, (_, _, _, _, old) in zip(scores, blocks):
        ms, ps, alphas = [], [], []
        for hh in range(2):
            tiles = [s[hh * blk:(hh + 1) * blk, t * LANES:(t + 1) * LANES] for t in range(s.shape[1] // LANES)]
            mx = tiles[0]
            for t in tiles[1:]:
                mx = jnp.maximum(mx, t)
            m = jnp.broadcast_to(jnp.max(mx, axis=-1, keepdims=True), (blk, LANES))
            if old is not None:
                m = jnp.maximum(m, old[0][hh])
                alphas.append(jnp.exp(old[0][hh] - m))
            ms.append(m)
            ps.append(jnp.concatenate([jnp.exp(t - m).astype(BF16) for t in tiles], axis=1))
        soft.append((ms, jnp.concatenate(ps, axis=0), alphas))
    outs = []
    for (ms, p, alphas), (_, _, vcat, _, old) in zip(soft, blocks):
        pv = jnp.dot(p, vcat, preferred_element_type=F32)
        acc = jnp.where(lo, pv[:blk, :LANES], pv[blk:, :LANES])
        dens = [pv[:blk, LANES:], pv[blk:, LANES:]]
        if old is not None:
            acc = jnp.where(lo, alphas[0], alphas[1]) * old[1] + acc
            dens = [alphas[hh] * old[0][2 + hh] + dens[hh] for hh in range(2)]
        outs.append((ms + dens, acc))
    return outs


def _attn_kernel(q1, qd2, qd3, k1, kd2, kd3, v1, v2, v3, o_ref, tmp_nat, vd1, vd2, vd3,
                 acc_ref, st_ref, acc2_ref, st2_ref):
    blk = ATT_BLOCK
    seq = q1.shape[0]
    nblk = seq // blk
    d1, d2 = DILATIONS[1], DILATIONS[2]
    quarter = seq // d1
    per_res = quarter // blk
    assert d2 == d1 * d1 and seq == d2 * blk and nblk % ATT_GROUP == 0 and ATT_GROUP % per_res == 0
    @pl.when(jnp.logical_and(pl.program_id(0) == 0, pl.program_id(1) == 0))
    def _():
        for vd in (vd1, vd2, vd3):
            vd[:, LANES:] = jnp.ones((seq, LANES), BF16)

    for v, vd in ((v1, vd1), (v2, vd2), (v3, vd3)):
        vd[:, :LANES] = v[...]
    qq = lax.broadcasted_iota(jnp.int32, (2 * blk, 2 * blk), 0) & (blk - 1)
    kk = lax.broadcasted_iota(jnp.int32, (2 * blk, 2 * blk), 1)
    mask_pc = jnp.logical_or(jnp.logical_and(kk < blk, kk >= qq), jnp.logical_and(kk >= blk, kk - blk <= qq))
    mask_c = (lax.broadcasted_iota(jnp.int32, (2 * blk, blk), 1)
              <= (lax.broadcasted_iota(jnp.int32, (2 * blk, blk), 0) & (blk - 1)))

    def run_group(blocks, dests):
        for (st_dst, acc_dst, rows), (st, acc) in zip(dests, _attn_group(blocks)):
            for k in range(4):
                st_dst[k, rows, :] = st[k]
            acc_dst[rows, :] = acc

    def load_old(st_src, acc_src, rows):
        return [st_src[k, rows, :] for k in range(4)], acc_src[rows, :]

    def b0_group(ns, first):
        blocks, dests = [], []
        for n in ns:
            if first and n == 0:
                rows = pl.ds(0, blk)
                blocks.append((q1[rows, :], k1[rows, :], vd1[rows, :], mask_c, None))
            else:
                rows = pl.ds(pl.multiple_of(n * blk, blk), blk)
                krows = pl.ds(pl.multiple_of((n - 1) * blk, blk), 2 * blk)
                blocks.append((q1[rows, :], k1[krows, :], vd1[krows, :], mask_pc, None))
            dests.append((st_ref, acc_ref, rows))
        run_group(blocks, dests)

    b0_group(list(range(ATT_GROUP)), True)

    def b0_body(i, carry):
        b0_group([ATT_GROUP * i + j for j in range(ATT_GROUP)], False)
        return carry

    lax.fori_loop(1, nblk // ATT_GROUP, b0_body, 0)

    res_per_trip = ATT_GROUP // per_res

    def b1_body(i, carry):
        blocks, dests = [], []
        for jr in range(res_per_trip):
            a1 = res_per_trip * i + jr
            base = a1 * quarter
            for n in range(per_res):
                rows = pl.ds(a1 + d1 * blk * n, blk, stride=d1)
                qrows = pl.ds(pl.multiple_of(base + n * blk, blk), blk)
                if n == 0:
                    krows, mask = qrows, mask_c
                else:
                    krows, mask = pl.ds(pl.multiple_of(base + (n - 1) * blk, blk), 2 * blk), mask_pc
                blocks.append((qd2[qrows, :], kd2[krows, :], vd2[krows, :], mask,
                               load_old(st_ref, acc_ref, rows)))
                dests.append((st2_ref, acc2_ref, qrows))
        run_group(blocks, dests)
        return carry

    lax.fori_loop(0, d1 // res_per_trip, b1_body, 0)

    def b2_body(i, carry):
        blocks, dests = [], []
        for j in range(ATT_GROUP):
            a1, a2 = j % d1, (ATT_GROUP // d1) * i + j // d1
            qrows = pl.ds(pl.multiple_of((ATT_GROUP * i + j) * blk, blk), blk)
            rows = pl.ds(a1 * quarter + a2, blk, stride=d1)
            blocks.append((qd3[qrows, :], kd3[qrows, :], vd3[qrows, :], mask_c,
                           load_old(st2_ref, acc2_ref, rows)))
            dests.append((st2_ref, acc2_ref, rows))
        run_group(blocks, dests)
        return carry

    lax.fori_loop(0, d2 // ATT_GROUP, b2_body, 0)

    lane = lax.broadcasted_iota(jnp.int32, (quarter, LANES), 1)
    for a in range(d1):
        seg = slice(a * quarter, (a + 1) * quarter)
        den = jnp.where(lane < HEAD_DIM, st2_ref[2, seg, :], st2_ref[3, seg, :])
        tmp_nat[pl.ds(a, quarter, stride=d1), :] = acc2_ref[a * quarter:(a + 1) * quarter, :] / den
    o_ref[...] = tmp_nat[...].astype(o_ref.dtype)


def _attention(qs, kvs):
    bsz, seq, _ = qs[0].shape
    pairs = D_MODEL // LANES

    def slab(col0):
        return pl.BlockSpec((None, seq, LANES), lambda b, hp: (b, 0, col0 + hp))

    in_specs = [slab(0)] * N_BRANCHES + [slab(0)] * N_BRANCHES + [slab(pairs)] * N_BRANCHES
    return pl.pallas_call(
        _attn_kernel,
        grid=(bsz, pairs),
        in_specs=in_specs,
        out_specs=slab(0),
        out_shape=jax.ShapeDtypeStruct((bsz, seq, D_MODEL), BF16),
        scratch_shapes=([pltpu.VMEM((seq, LANES), F32)] + [pltpu.VMEM((seq, 2 * LANES), BF16)] * 3
                        + [pltpu.VMEM((seq, LANES), F32), pltpu.VMEM((4, seq, LANES), F32)] * 2),
        compiler_params=_cparams(("arbitrary", "arbitrary")),
        name="attention",
    )(*qs, *kvs, *kvs)


def _regroup8(tiles):
    t = list(tiles)
    lane = lax.broadcasted_iota(jnp.int32, t[0].shape, 1)
    piece = lane // SSM_GROUP
    for dist in (4, 2, 1):
        bit = (piece & dist) != 0
        shift = dist * SSM_GROUP
        for a in range(8):
            if a & dist:
                continue
            b = a | dist
            ta, tb = t[a], t[b]
            t[a] = jnp.where(bit, pltpu.roll(tb, shift, 1), ta)
            t[b] = jnp.where(bit, tb, pltpu.roll(ta, LANES - shift, 1))
    return t


def _s5_tile_perm(bsz):
    tt = S5_TILE_TOKENS
    n = bsz * tt
    perm = np.zeros((n, n), np.float32)
    for b in range(bsz):
        for c in range(tt // CHUNK):
            for t in range(CHUNK):
                perm[(t * (tt // CHUNK) + c) * bsz + b, b * tt + c * CHUNK + t] = 1.0
    return perm


def _s5_pre_kernel(h_ref, mod_ref, g_ref, perm_ref, z_ref, u_scr, up_scr):
    d = D_MODEL
    bsz, tt, _ = h_ref.shape
    g = g_ref[...]
    for b in range(bsz):
        u_scr[b * tt:(b + 1) * tt, :] = _normmod(
            h_ref[b], g, mod_ref[b, :, :d], mod_ref[b, :, d:2 * d]).astype(BF16)
    up_scr[...] = jnp.dot(perm_ref[...], u_scr[...], preferred_element_type=F32)
    rows = z_ref.shape[0]
    for j in range(d // LANES):
        for half in range(CHUNK // 8):
            tiles = [up_scr[(8 * half + tl) * rows:(8 * half + tl + 1) * rows, j * LANES:(j + 1) * LANES]
                     for tl in range(8)]
            outs = _regroup8(tiles)
            for gl in range(8):
                col = (8 * j + gl) * GROUP_COLS + half * LANES
                z_ref[:, col:col + LANES] = outs[gl].astype(z_ref.dtype)


def _s5_pre(h, mod, g, perm):
    bsz, seq, d = h.shape
    tt = S5_TILE_TOKENS
    rows = bsz * tt // CHUNK
    return pl.pallas_call(
        _s5_pre_kernel,
        grid=(seq // tt,),
        in_specs=[
            pl.BlockSpec((bsz, tt, d), lambda i: (0, i, 0)),
            pl.BlockSpec(mod.shape, lambda i: (0, 0, 0)),
            pl.BlockSpec((1, d), lambda i: (0, 0)),
            pl.BlockSpec(perm.shape, lambda i: (0, 0)),
        ],
        out_specs=pl.BlockSpec((rows, CHUNK * d), lambda i: (i, 0)),
        out_shape=jax.ShapeDtypeStruct((bsz * seq // CHUNK, CHUNK * d), BF16),
        scratch_shapes=[pltpu.VMEM((bsz * tt, d), BF16), pltpu.VMEM((bsz * tt, d), F32)],
        compiler_params=_cparams(("parallel",)),
        name="s5_pre",
    )(h, mod, g.reshape(1, d), perm)


def _s5_core_kernel(z_ref, win_ref, wt_ref, wx_ref, dec_ref, d_ref, o_ref, s_scr, x_scr, *, bsz):
    gc = GROUP_COLS
    nchunk = z_ref.shape[0] // bsz
    s_scr[...] = (jnp.dot(z_ref[:, :gc], win_ref[0], preferred_element_type=F32)
                  + jnp.dot(z_ref[:, gc:], win_ref[1], preferred_element_type=F32))
    ar, ai = dec_ref[:, :LANES], dec_ref[:, LANES:]

    def step(c, carry):
        xr, xi = carry
        rows = pl.ds(pl.multiple_of(c * bsz, bsz), bsz)
        x_scr[rows, :LANES] = xr
        x_scr[rows, LANES:] = xi
        return (ar * xr - ai * xi + s_scr[rows, :LANES], ar * xi + ai * xr + s_scr[rows, LANES:])

    zero = jnp.zeros((bsz, LANES), F32)
    lax.fori_loop(0, nchunk, step, (zero, zero))
    x = x_scr[...].astype(BF16)
    for gi in range(2):
        z = z_ref[:, gi * gc:(gi + 1) * gc]
        y = (jnp.dot(z, wt_ref[gi], preferred_element_type=F32)
             + jnp.dot(x, wx_ref[gi], preferred_element_type=F32))
        y = y + d_ref[:, gi * gc:(gi + 1) * gc] * z.astype(F32)
        o_ref[:, gi * gc:(gi + 1) * gc] = jax.nn.gelu(y).astype(o_ref.dtype)


def _s5_core(z, w_in, w_toep, w_x, dec, d_perm, bsz):
    nrow, width = z.shape
    wspec = pl.BlockSpec((2, GROUP_COLS, GROUP_COLS), lambda k: (k, 0, 0))
    return pl.pallas_call(
        functools.partial(_s5_core_kernel, bsz=bsz),
        grid=(N_PAIRS,),
        in_specs=[
            pl.BlockSpec((nrow, PAIR_COLS), lambda k: (0, k)),
            wspec, wspec, wspec,
            pl.BlockSpec((1, GROUP_COLS), lambda k: (0, k)),
            pl.BlockSpec((1, PAIR_COLS), lambda k: (0, k)),
        ],
        out_specs=pl.BlockSpec((nrow, PAIR_COLS), lambda k: (0, k)),
        out_shape=jax.ShapeDtypeStruct((nrow, width), BF16),
        scratch_shapes=[pltpu.VMEM((nrow, GROUP_COLS), F32), pltpu.VMEM((nrow, GROUP_COLS), F32)],
        compiler_params=_cparams(("parallel",)),
        name="s5_core",
    )(z, w_in, w_toep, w_x, dec, d_perm)


def _s5_weights(lam_re, lam_im, log_dt, b_re, b_im, c_re, c_im, d_skip):
    g, p, c16 = SSM_GROUPS, SSM_STATE, SSM_GROUP
    lam = lax.complex(lam_re.astype(F32), lam_im.astype(F32))
    dt = jnp.exp(log_dt.astype(F32))[:, None]
    steps = jnp.arange(CHUNK + 1, dtype=F32)
    apow = jnp.exp((lam * dt)[None] * steps[:, None, None])
    a = apow[1]
    bbar = ((a - 1.0) / lam)[..., None] * lax.complex(b_re.astype(F32), b_im.astype(F32))
    cmat = lax.complex(c_re.astype(F32), c_im.astype(F32))
    odd = (jnp.arange(g) % 2)[:, None, None]

    win = apow[CHUNK - 1::-1][..., None] * bbar[None]
    win = jnp.transpose(win, (1, 0, 3, 2)).reshape(g, GROUP_COLS, p)
    zeros = jnp.zeros_like(win.real)
    w_in = jnp.concatenate([
        jnp.where(odd == 0, win.real, zeros), jnp.where(odd == 1, win.real, zeros),
        jnp.where(odd == 0, win.imag, zeros), jnp.where(odd == 1, win.imag, zeros)], axis=-1)

    kern = jnp.einsum('gop,kgp,gpi->kgoi', cmat, apow[:CHUNK], bbar, precision=HIGHEST).real
    idx = jnp.arange(CHUNK)
    onehot = (idx[None, None, :] - idx[None, :, None] == idx[:, None, None]).astype(F32)
    w_toep = jnp.einsum('kst,kgoi->gsito', onehot, kern, precision=HIGHEST)
    w_toep = w_toep.reshape(g, GROUP_COLS, GROUP_COLS)

    cw = cmat[None] * apow[1:, :, None, :]
    cw = jnp.transpose(cw, (1, 3, 0, 2)).reshape(g, p, GROUP_COLS)
    zx = jnp.zeros_like(cw.real)
    w_x = jnp.concatenate([
        jnp.where(odd == 0, cw.real, zx), jnp.where(odd == 1, cw.real, zx),
        jnp.where(odd == 0, -cw.imag, zx), jnp.where(odd == 1, -cw.imag, zx)], axis=1)

    a16 = apow[CHUNK].reshape(N_PAIRS, 2 * p)
    dec = jnp.concatenate([a16.real, a16.imag], axis=-1).reshape(1, N_PAIRS * GROUP_COLS)
    d_perm = jnp.broadcast_to(d_skip.astype(F32).reshape(g, 1, c16), (g, CHUNK, c16)).reshape(1, g * GROUP_COLS)
    return w_in.astype(BF16), w_toep.astype(BF16), w_x.astype(BF16), dec, d_perm


def _mlp(u, w1_ref, w2_ref):
    d = D_MODEL
    acc = None
    for k in range(D_FF // d):
        a = jnp.dot(u, w1_ref[:, k * d:(k + 1) * d], preferred_element_type=F32)
        a = jnp.square(jnp.maximum(a, 0.0)).astype(BF16)
        part = jnp.dot(a, w2_ref[k * d:(k + 1) * d, :], preferred_element_type=F32)
        acc = part if acc is None else acc + part
    return acc


def _final_norm(h, fg_ref):
    ms = jnp.mean(h * h, axis=-1, keepdims=True)
    return h * lax.rsqrt(ms + EPS) * fg_ref[...]


def _post_attn_kernel(*refs, final):
    if final:
        h_ref, y_ref, moda_ref, modm_ref, g_ref, wp_ref, w1_ref, w2_ref, fg_ref, o_ref = refs
    else:
        h_ref, y_ref, moda_ref, modm_ref, g_ref, wp_ref, w1_ref, w2_ref, o_ref = refs
    d = D_MODEL
    ymix = jnp.dot(y_ref[...], wp_ref[...], preferred_element_type=F32)
    h1 = h_ref[...] + moda_ref[:, 2 * d:3 * d] * ymix
    u = _normmod(h1, g_ref[...], modm_ref[:, :d], modm_ref[:, d:2 * d]).astype(BF16)
    h2 = h1 + modm_ref[:, 2 * d:3 * d] * _mlp(u, w1_ref, w2_ref)
    o_ref[...] = _final_norm(h2, fg_ref) if final else h2


def _post_attn(h, o, moda, modm, g2, w_o, w1, w2, final_g, tm=512):
    bsz, seq, d = h.shape
    per_b = seq // tm
    row_spec = pl.BlockSpec((None, tm, d), lambda i: (i // per_b, i % per_b, 0))
    final = final_g is not None
    in_specs = [
        row_spec, row_spec,
        pl.BlockSpec((None, 1, moda.shape[-1]), lambda i: (i // per_b, 0, 0)),
        pl.BlockSpec((None, 1, modm.shape[-1]), lambda i: (i // per_b, 0, 0)),
        pl.BlockSpec((1, d), lambda i: (0, 0)),
        _resident(w_o.shape, lambda i: (0, 0)),
        _resident((d, D_FF), lambda i: (0, 0)),
        _resident((D_FF, d), lambda i: (0, 0)),
    ]
    args = [h, o, moda, modm, g2.reshape(1, d), w_o, w1, w2]
    if final:
        in_specs.append(pl.BlockSpec((1, d), lambda i: (0, 0)))
        args.append(final_g.reshape(1, d))
    return pl.pallas_call(
        functools.partial(_post_attn_kernel, final=final),
        grid=(bsz * per_b,),
        in_specs=in_specs,
        out_specs=row_spec,
        out_shape=jax.ShapeDtypeStruct((bsz, seq, d), F32),
        compiler_params=_cparams(("parallel",)),
        name="post_attn_mlp",
    )(*args)


def _post_s5_kernel(h_ref, z_ref, moda_ref, modm_ref, g_ref, permt_ref, wp_ref, w1_ref, w2_ref, o_ref,
                    z_scr, h1_scr, u_scr):
    d = D_MODEL
    bsz, tt, _ = h_ref.shape
    rows = z_ref.shape[0]
    for j in range(d // LANES):
        for half in range(CHUNK // 8):
            tiles = []
            for gl in range(8):
                col = (8 * j + gl) * GROUP_COLS + half * LANES
                tiles.append(z_ref[:, col:col + LANES].astype(F32))
            outs = _regroup8(tiles)
            for tl in range(8):
                t = 8 * half + tl
                z_scr[t * rows:(t + 1) * rows, j * LANES:(j + 1) * LANES] = outs[tl].astype(BF16)
    zn = jnp.dot(permt_ref[...], z_scr[...], preferred_element_type=F32).astype(BF16)
    zz = jnp.dot(zn, wp_ref[...], preferred_element_type=F32)
    ymix = zz[:, :d] * jax.nn.sigmoid(zz[:, d:])
    g = g_ref[...]
    for b in range(bsz):
        sl = slice(b * tt, (b + 1) * tt)
        h1 = h_ref[b] + moda_ref[b, :, 2 * d:3 * d] * ymix[sl, :]
        h1_scr[sl, :] = h1
        u_scr[sl, :] = _normmod(h1, g, modm_ref[b, :, :d], modm_ref[b, :, d:2 * d]).astype(BF16)
    acc = _mlp(u_scr[...], w1_ref, w2_ref)
    for b in range(bsz):
        sl = slice(b * tt, (b + 1) * tt)
        o_ref[b] = h1_scr[sl, :] + modm_ref[b, :, 2 * d:3 * d] * acc[sl, :]


def _post_s5(h, zo, moda, modm, g2, permt, w_glu, w1, w2):
    bsz, seq, d = h.shape
    tt = S5_TILE_TOKENS
    rows = bsz * tt // CHUNK
    h_spec = pl.BlockSpec((bsz, tt, d), lambda i: (0, i, 0))
    return pl.pallas_call(
        _post_s5_kernel,
        grid=(seq // tt,),
        in_specs=[
            h_spec,
            pl.BlockSpec((rows, CHUNK * d), lambda i: (i, 0)),
            pl.BlockSpec(moda.shape, lambda i: (0, 0, 0)),
            pl.BlockSpec(modm.shape, lambda i: (0, 0, 0)),
            pl.BlockSpec((1, d), lambda i: (0, 0)),
            pl.BlockSpec(permt.shape, lambda i: (0, 0)),
            _resident(w_glu.shape, lambda i: (0, 0)),
            _resident((d, D_FF), lambda i: (0, 0)),
            _resident((D_FF, d), lambda i: (0, 0)),
        ],
        out_specs=h_spec,
        out_shape=jax.ShapeDtypeStruct((bsz, seq, d), F32),
        scratch_shapes=[pltpu.VMEM((bsz * tt, d), BF16), pltpu.VMEM((bsz * tt, d), F32),
                        pltpu.VMEM((bsz * tt, d), BF16)],
        compiler_params=_cparams(("parallel",)),
        name="post_s5_mlp",
    )(h, zo, moda, modm, g2.reshape(1, d), permt, w_glu, w1, w2)


def kernel(x, c, ln_g, ada_w, ada_b, ssm_lam_re, ssm_lam_im, ssm_log_dt, ssm_b_re, ssm_b_im, ssm_c_re, ssm_c_im, ssm_d, ssm_w_glu, kv_g, kv_ada_w, kv_ada_b, w_kv, attn_w_q, attn_w_o, mlp_w1, mlp_w2, final_g):
    bsz, seq, d = x.shape
    depth = ln_g.shape[0]
    n_s5 = ssm_lam_re.shape[0]
    assert d == D_MODEL and seq % (DILATIONS[-1] * ATT_BLOCK) == 0

    mods = _ada_mods(c, ada_w.reshape(depth * 2, d, 3 * d), ada_b.reshape(depth * 2, 3 * d))
    mods = mods.reshape(depth, 2, bsz, 1, 3 * d)
    kv_mod = _ada_mods(c, kv_ada_w[None], kv_ada_b[None]).reshape(bsz, 1, 2 * d)

    w1 = mlp_w1.astype(BF16)
    w2 = mlp_w2.astype(BF16)
    perm_np = _s5_tile_perm(bsz)
    perm = jnp.asarray(perm_np, BF16)
    permt = jnp.asarray(perm_np.T, BF16)

    h = x
    for layer in range(n_s5):
        w_in, w_toep, w_x, dec, d_perm = _s5_weights(
            ssm_lam_re[layer], ssm_lam_im[layer], ssm_log_dt[layer], ssm_b_re[layer], ssm_b_im[layer],
            ssm_c_re[layer], ssm_c_im[layer], ssm_d[layer])
        z = _s5_pre(h, mods[layer, 0], ln_g[layer, 0], perm)
        zo = _s5_core(z, w_in, w_toep, w_x, dec, d_perm, bsz)
        h = _post_s5(h, zo, mods[layer, 0], mods[layer, 1], ln_g[layer, 1], permt,
                     ssm_w_glu[layer].astype(BF16), w1[layer], w2[layer])

    kv_cols = tuple((i * d, (N_BRANCHES + i) * d) for i in range(N_BRANCHES))
    q_cols = tuple((i * d,) for i in range(N_BRANCHES))
    kvs = _norm_proj(h, kv_mod, kv_g, w_kv.astype(BF16), kv_cols)
    for layer in range(n_s5, depth):
        j = layer - n_s5
        qs = _norm_proj(h, mods[layer, 0], ln_g[layer, 0], attn_w_q[j].astype(BF16), q_cols)
        o = _attention(qs, kvs)
        h = _post_attn(h, o, mods[layer, 0], mods[layer, 1], ln_g[layer, 1],
                       attn_w_o[j].astype(BF16), w1[layer], w2[layer],
                       final_g if layer == depth - 1 else None)
    return h
```

```python
import functools
import math

import numpy as np

import jax
import jax.numpy as jnp
from jax import lax
from jax.experimental import pallas as pl
from jax.experimental.pallas import tpu as pltpu

F32 = jnp.float32
BF16 = jnp.bfloat16

D_MODEL = 1024
SSM_GROUP = 16
SSM_GROUPS = D_MODEL // SSM_GROUP
SSM_STATE = 64
HEAD_DIM = 64
N_HEADS = D_MODEL // HEAD_DIM
DILATIONS = (1, 4, 16)
N_BRANCHES = len(DILATIONS)
ATT_BLOCK = 128
ATT_GROUP = 8
N_STATS = 3
Q_SCALE = HEAD_DIM ** -0.5 * math.log2(math.e)
D_FF = 4 * D_MODEL
EPS = 1e-6
NEG = -1e30

LANES = 128
CHUNK = 16
GROUP_COLS = CHUNK * SSM_GROUP
PAIR_COLS = 2 * GROUP_COLS
N_PAIRS = SSM_GROUPS // 2
S5_TILE_TOKENS = 2 * CHUNK
MLP_ROWS = 512
PROJ_TILE = 512
VMEM_LIMIT = 56 * 1024 * 1024

HIGHEST = lax.Precision.HIGHEST
NT_DIMS = (((1,), (1,)), ((), ()))


def _cparams(sem):
    return pltpu.CompilerParams(dimension_semantics=sem, vmem_limit_bytes=VMEM_LIMIT)


def _resident(shape, index_map):
    return pl.BlockSpec(shape, index_map, pipeline_mode=pl.Buffered(1))


def _normmod(x, g, shift, scale):
    ms = jnp.mean(x * x, axis=-1, keepdims=True)
    return (x * lax.rsqrt(ms + EPS) * g) * (1.0 + scale) + shift


def _ada_kernel(c_ref, w_ref, b_ref, o_ref):
    c = c_ref[...]
    sc = c * jax.nn.sigmoid(c)
    o_ref[...] = jnp.dot(sc, w_ref[...], preferred_element_type=F32, precision=HIGHEST) + b_ref[...]


def _ada_mods(c, w, b):
    n, d, width = w.shape
    bsz = c.shape[0]
    tn = 1024
    return pl.pallas_call(
        _ada_kernel,
        grid=(n, width // tn),
        in_specs=[
            pl.BlockSpec((bsz, d), lambda i, j: (0, 0)),
            pl.BlockSpec((None, d, tn), lambda i, j: (i, 0, j)),
            pl.BlockSpec((None, 1, tn), lambda i, j: (i, 0, j)),
        ],
        out_specs=pl.BlockSpec((None, bsz, tn), lambda i, j: (i, 0, j)),
        out_shape=jax.ShapeDtypeStruct((n, bsz, width), F32),
        compiler_params=_cparams(("parallel", "parallel")),
        name="ada_mods",
    )(c, w, b.reshape(n, 1, width))


def _proj_kernel(h_ref, mod_ref, g_ref, w_ref, o1_ref, o2_ref, o3_ref, us, us4, up4, up16, *, wcols, out_scale):
    d = D_MODEL
    tm = h_ref.shape[0]
    d1 = DILATIONS[1]
    q4, q16 = tm // d1, tm // DILATIONS[2]
    cw = o1_ref.shape[1]
    u = _normmod(h_ref[...], g_ref[...], mod_ref[:, :d], mod_ref[:, d:2 * d])
    for l in range(d // LANES):
        us[l] = u[:, l * LANES:(l + 1) * LANES]
    ub = u.astype(BF16)
    for n in range(cw // d):
        cols = slice(n * d, (n + 1) * d)
        res = jnp.dot(ub, w_ref[:, wcols[0][n]:wcols[0][n] + d], preferred_element_type=F32)
        o1_ref[:, cols] = (res * out_scale).astype(o1_ref.dtype)
    for l in range(d // LANES):
        for a in range(d1):
            part = us[l, pl.ds(a, q4, stride=d1), :]
            us4[l, a * q4:(a + 1) * q4, :] = part
            up4[a * q4:(a + 1) * q4, l * LANES:(l + 1) * LANES] = part.astype(BF16)
    for n in range(cw // d):
        res = jnp.dot(up4[...], w_ref[:, wcols[1][n]:wcols[1][n] + d], preferred_element_type=F32)
        for a in range(d1):
            o2_ref[a, :, n * d:(n + 1) * d] = (res[a * q4:(a + 1) * q4, :] * out_scale).astype(o2_ref.dtype)
    for l in range(d // LANES):
        for a1 in range(d1):
            for a2 in range(d1):
                r = d1 * a2 + a1
                up16[r * q16:(r + 1) * q16, l * LANES:(l + 1) * LANES] = us4[
                    l, pl.ds(a1 * q4 + a2, q16, stride=d1), :].astype(BF16)
    for n in range(cw // d):
        res = jnp.dot(up16[...], w_ref[:, wcols[2][n]:wcols[2][n] + d], preferred_element_type=F32)
        for r in range(DILATIONS[2]):
            o3_ref[r, :, n * d:(n + 1) * d] = (res[r * q16:(r + 1) * q16, :] * out_scale).astype(o3_ref.dtype)


def _norm_proj(h, mod, g, w, wcols, out_scale):
    bsz, seq, d = h.shape
    tm = PROJ_TILE
    cw = len(wcols[0]) * d
    per_b = seq // tm
    d1, d2 = DILATIONS[1], DILATIONS[2]
    o1, o2, o3 = pl.pallas_call(
        functools.partial(_proj_kernel, wcols=wcols, out_scale=out_scale),
        grid=(bsz * per_b,),
        in_specs=[
            pl.BlockSpec((None, tm, d), lambda i: (i // per_b, i % per_b, 0)),
            pl.BlockSpec((None, 1, mod.shape[-1]), lambda i: (i // per_b, 0, 0)),
            pl.BlockSpec((1, d), lambda i: (0, 0)),
            _resident(w.shape, lambda i: (0, 0)),
        ],
        out_specs=[
            pl.BlockSpec((None, tm, cw), lambda i: (i // per_b, i % per_b, 0)),
            pl.BlockSpec((None, d1, None, tm // d1, cw), lambda i: (i // per_b, 0, i % per_b, 0, 0)),
            pl.BlockSpec((None, d2, None, tm // d2, cw), lambda i: (i // per_b, 0, i % per_b, 0, 0)),
        ],
        out_shape=[
            jax.ShapeDtypeStruct((bsz, seq, cw), BF16),
            jax.ShapeDtypeStruct((bsz, d1, per_b, tm // d1, cw), BF16),
            jax.ShapeDtypeStruct((bsz, d2, per_b, tm // d2, cw), BF16),
        ],
        scratch_shapes=[pltpu.VMEM((d // LANES, tm, LANES), F32), pltpu.VMEM((d // LANES, tm, LANES), F32),
                        pltpu.VMEM((tm, d), BF16), pltpu.VMEM((tm, d), BF16)],
        compiler_params=_cparams(("parallel",)),
        name="norm_proj",
    )(h, mod, g.reshape(1, d), w)
    return o1, o2.reshape(bsz, seq, cw), o3.reshape(bsz, seq, cw)


def _attn_group(blocks):
    blk = ATT_BLOCK
    lane = lax.broadcasted_iota(jnp.int32, (blk, LANES), 1)
    lo = lane < HEAD_DIM
    scores = []
    for qb, kcat, _, mask, _ in blocks:
        zero = jnp.zeros_like(qb)
        q2 = jnp.concatenate([jnp.where(lo, qb, zero), jnp.where(lo, zero, qb)], axis=0)
        s = lax.dot_general(q2, kcat, NT_DIMS, preferred_element_type=F32)
        scores.append(jnp.where(mask, s, NEG))
    soft = []
    for s, (_, _, _, _, old) in zip(scores, blocks):
        ms, ps = [], []
        for hh in range(2):
            tiles = [s[hh * blk:(hh + 1) * blk, t * LANES:(t + 1) * LANES] for t in range(s.shape[1] // LANES)]
            mx = tiles[0]
            for t in tiles[1:]:
                mx = jnp.maximum(mx, t)
            m = jnp.broadcast_to(jnp.max(mx, axis=-1, keepdims=True), (blk, LANES))
            if old is not None:
                m = jnp.maximum(m, old[0][hh])
            ms.append(m)
            ps.append(jnp.concatenate([jnp.exp2(t - m).astype(BF16) for t in tiles], axis=1))
        alpha = None
        if old is not None:
            alpha = jnp.exp2(jnp.where(lo, old[0][0], old[0][1]) - jnp.where(lo, ms[0], ms[1]))
        soft.append((ms, jnp.concatenate(ps, axis=0), alpha))
    outs = []
    for (ms, p, alpha), (_, _, vcat, _, old) in zip(soft, blocks):
        pv = jnp.dot(p, vcat, preferred_element_type=F32)
        acc = jnp.where(lo, pv[:blk, :LANES], pv[blk:, :LANES])
        den = jnp.where(lo, pv[:blk, LANES:], pv[blk:, LANES:])
        if old is not None:
            acc = alpha * old[1] + acc
            den = alpha * old[0][2] + den
        outs.append((ms + [den], acc))
    return outs


def _attn_kernel(q1, qd2, qd3, k1, kd2, kd3, v1, v2, v3, o_ref, tmp_nat, vd1, vd2, vd3,
                 acc_ref, st_ref, acc2_ref, st2_ref):
    blk = ATT_BLOCK
    seq = q1.shape[0]
    nblk = seq // blk
    d1, d2 = DILATIONS[1], DILATIONS[2]
    quarter = seq // d1
    per_res = quarter // blk
    assert d2 == d1 * d1 and seq == d2 * blk and nblk % ATT_GROUP == 0 and ATT_GROUP % per_res == 0
    @pl.when(jnp.logical_and(pl.program_id(0) == 0, pl.program_id(1) == 0))
    def _():
        for vd in (vd1, vd2, vd3):
            vd[:, LANES:] = jnp.ones((seq, LANES), BF16)

    for v, vd in ((v1, vd1), (v2, vd2), (v3, vd3)):
        vd[:, :LANES] = v[...]
    qq = lax.broadcasted_iota(jnp.int32, (2 * blk, 2 * blk), 0) & (blk - 1)
    kk = lax.broadcasted_iota(jnp.int32, (2 * blk, 2 * blk), 1)
    mask_pc = jnp.logical_or(jnp.logical_and(kk < blk, kk >= qq), jnp.logical_and(kk >= blk, kk - blk <= qq))
    mask_c = (lax.broadcasted_iota(jnp.int32, (2 * blk, blk), 1)
              <= (lax.broadcasted_iota(jnp.int32, (2 * blk, blk), 0) & (blk - 1)))

    def run_group(blocks, dests):
        for (st_dst, acc_dst, rows), (st, acc) in zip(dests, _attn_group(blocks)):
            for k in range(N_STATS):
                st_dst[k, rows, :] = st[k]
            acc_dst[rows, :] = acc

    def load_old(st_src, acc_src, rows):
        return [st_src[k, rows, :] for k in range(N_STATS)], acc_src[rows, :]

    def b0_group(ns, first):
        blocks, dests = [], []
        for n in ns:
            if first and n == 0:
                rows = pl.ds(0, blk)
                blocks.append((q1[rows, :], k1[rows, :], vd1[rows, :], mask_c, None))
            else:
                rows = pl.ds(pl.multiple_of(n * blk, blk), blk)
                krows = pl.ds(pl.multiple_of((n - 1) * blk, blk), 2 * blk)
                blocks.append((q1[rows, :], k1[krows, :], vd1[krows, :], mask_pc, None))
            dests.append((st_ref, acc_ref, rows))
        run_group(blocks, dests)

    b0_group(list(range(ATT_GROUP)), True)

    def b0_body(i, carry):
        b0_group([ATT_GROUP * i + j for j in range(ATT_GROUP)], False)
        return carry

    lax.fori_loop(1, nblk // ATT_GROUP, b0_body, 0)

    res_per_trip = ATT_GROUP // per_res

    def b1_body(i, carry):
        blocks, dests = [], []
        for jr in range(res_per_trip):
            a1 = res_per_trip * i + jr
            base = a1 * quarter
            for n in range(per_res):
                rows = pl.ds(a1 + d1 * blk * n, blk, stride=d1)
                qrows = pl.ds(pl.multiple_of(base + n * blk, blk), blk)
                if n == 0:
                    krows, mask = qrows, mask_c
                else:
                    krows, mask = pl.ds(pl.multiple_of(base + (n - 1) * blk, blk), 2 * blk), mask_pc
                blocks.append((qd2[qrows, :], kd2[krows, :], vd2[krows, :], mask,
                               load_old(st_ref, acc_ref, rows)))
                dests.append((st2_ref, acc2_ref, qrows))
        run_group(blocks, dests)
        return carry

    lax.fori_loop(0, d1 // res_per_trip, b1_body, 0)

    def b2_body(i, carry):
        blocks, dests = [], []
        for j in range(ATT_GROUP):
            a1, a2 = j % d1, (ATT_GROUP // d1) * i + j // d1
            qrows = pl.ds(pl.multiple_of((ATT_GROUP * i + j) * blk, blk), blk)
            rows = pl.ds(a1 * quarter + a2, blk, stride=d1)
            blocks.append((qd3[qrows, :], kd3[qrows, :], vd3[qrows, :], mask_c,
                           load_old(st2_ref, acc2_ref, rows)))
            dests.append((st2_ref, acc2_ref, rows))
        run_group(blocks, dests)
        return carry

    lax.fori_loop(0, d2 // ATT_GROUP, b2_body, 0)

    for a in range(d1):
        seg = slice(a * quarter, (a + 1) * quarter)
        tmp_nat[pl.ds(a, quarter, stride=d1), :] = acc2_ref[seg, :] / st2_ref[N_STATS - 1, seg, :]
    o_ref[...] = tmp_nat[...].astype(o_ref.dtype)


def _attention(qs, kvs):
    bsz, seq, _ = qs[0].shape
    pairs = D_MODEL // LANES

    def slab(col0):
        return pl.BlockSpec((None, seq, LANES), lambda b, hp: (b, 0, col0 + hp))

    in_specs = [slab(0)] * N_BRANCHES + [slab(0)] * N_BRANCHES + [slab(pairs)] * N_BRANCHES
    return pl.pallas_call(
        _attn_kernel,
        grid=(bsz, pairs),
        in_specs=in_specs,
        out_specs=slab(0),
        out_shape=jax.ShapeDtypeStruct((bsz, seq, D_MODEL), BF16),
        scratch_shapes=([pltpu.VMEM((seq, LANES), F32)] + [pltpu.VMEM((seq, 2 * LANES), BF16)] * 3
                        + [pltpu.VMEM((seq, LANES), F32), pltpu.VMEM((N_STATS, seq, LANES), F32)] * 2),
        compiler_params=_cparams(("arbitrary", "arbitrary")),
        name="attention",
    )(*qs, *kvs, *kvs)


def _regroup8(tiles):
    t = list(tiles)
    lane = lax.broadcasted_iota(jnp.int32, t[0].shape, 1)
    piece = lane // SSM_GROUP
    for dist in (4, 2, 1):
        bit = (piece & dist) != 0
        shift = dist * SSM_GROUP
        for a in range(8):
            if a & dist:
                continue
            b = a | dist
            ta, tb = t[a], t[b]
            t[a] = jnp.where(bit, pltpu.roll(tb, shift, 1), ta)
            t[b] = jnp.where(bit, tb, pltpu.roll(ta, LANES - shift, 1))
    return t


def _s5_tile_perm(bsz):
    tt = S5_TILE_TOKENS
    n = bsz * tt
    perm = np.zeros((n, n), np.float32)
    for b in range(bsz):
        for c in range(tt // CHUNK):
            for t in range(CHUNK):
                perm[(t * (tt // CHUNK) + c) * bsz + b, b * tt + c * CHUNK + t] = 1.0
    return perm


def _s5_pre_kernel(h_ref, mod_ref, g_ref, perm_ref, z_ref, u_scr, up_scr):
    d = D_MODEL
    bsz, tt, _ = h_ref.shape
    g = g_ref[...]
    for b in range(bsz):
        u_scr[b * tt:(b + 1) * tt, :] = _normmod(
            h_ref[b], g, mod_ref[b, :, :d], mod_ref[b, :, d:2 * d]).astype(BF16)
    up_scr[...] = jnp.dot(perm_ref[...], u_scr[...], preferred_element_type=F32)
    rows = z_ref.shape[0]
    for j in range(d // LANES):
        for half in range(CHUNK // 8):
            tiles = [up_scr[(8 * half + tl) * rows:(8 * half + tl + 1) * rows, j * LANES:(j + 1) * LANES]
                     for tl in range(8)]
            outs = _regroup8(tiles)
            for gl in range(8):
                col = (8 * j + gl) * GROUP_COLS + half * LANES
                z_ref[:, col:col + LANES] = outs[gl].astype(z_ref.dtype)


def _s5_pre(h, mod, g, perm):
    bsz, seq, d = h.shape
    tt = S5_TILE_TOKENS
    rows = bsz * tt // CHUNK
    return pl.pallas_call(
        _s5_pre_kernel,
        grid=(seq // tt,),
        in_specs=[
            pl.BlockSpec((bsz, tt, d), lambda i: (0, i, 0)),
            pl.BlockSpec(mod.shape, lambda i: (0, 0, 0)),
            pl.BlockSpec((1, d), lambda i: (0, 0)),
            pl.BlockSpec(perm.shape, lambda i: (0, 0)),
        ],
        out_specs=pl.BlockSpec((rows, CHUNK * d), lambda i: (i, 0)),
        out_shape=jax.ShapeDtypeStruct((bsz * seq // CHUNK, CHUNK * d), BF16),
        scratch_shapes=[pltpu.VMEM((bsz * tt, d), BF16), pltpu.VMEM((bsz * tt, d), F32)],
        compiler_params=_cparams(("parallel",)),
        name="s5_pre",
    )(h, mod, g.reshape(1, d), perm)


def _s5_core_kernel(z_ref, win_ref, wt_ref, wx_ref, dec_ref, d_ref, o_ref, s_scr, x_scr, *, bsz):
    gc = GROUP_COLS
    nchunk = z_ref.shape[0] // bsz
    s_scr[...] = (jnp.dot(z_ref[:, :gc], win_ref[0], preferred_element_type=F32)
                  + jnp.dot(z_ref[:, gc:], win_ref[1], preferred_element_type=F32))
    ar, ai = dec_ref[:, :LANES], dec_ref[:, LANES:]

    def step(c, carry):
        xr, xi = carry
        rows = pl.ds(pl.multiple_of(c * bsz, bsz), bsz)
        x_scr[rows, :LANES] = xr
        x_scr[rows, LANES:] = xi
        return (ar * xr - ai * xi + s_scr[rows, :LANES], ar * xi + ai * xr + s_scr[rows, LANES:])

    zero = jnp.zeros((bsz, LANES), F32)
    lax.fori_loop(0, nchunk, step, (zero, zero))
    x = x_scr[...].astype(BF16)
    for gi in range(2):
        z = z_ref[:, gi * gc:(gi + 1) * gc]
        y = (jnp.dot(z, wt_ref[gi], preferred_element_type=F32)
             + jnp.dot(x, wx_ref[gi], preferred_element_type=F32))
        y = y + d_ref[:, gi * gc:(gi + 1) * gc] * z.astype(F32)
        o_ref[:, gi * gc:(gi + 1) * gc] = jax.nn.gelu(y).astype(o_ref.dtype)


def _s5_core(z, w_in, w_toep, w_x, dec, d_perm, bsz):
    nrow, width = z.shape
    wspec = pl.BlockSpec((2, GROUP_COLS, GROUP_COLS), lambda k: (k, 0, 0))
    return pl.pallas_call(
        functools.partial(_s5_core_kernel, bsz=bsz),
        grid=(N_PAIRS,),
        in_specs=[
            pl.BlockSpec((nrow, PAIR_COLS), lambda k: (0, k)),
            wspec, wspec, wspec,
            pl.BlockSpec((1, GROUP_COLS), lambda k: (0, k)),
            pl.BlockSpec((1, PAIR_COLS), lambda k: (0, k)),
        ],
        out_specs=pl.BlockSpec((nrow, PAIR_COLS), lambda k: (0, k)),
        out_shape=jax.ShapeDtypeStruct((nrow, width), BF16),
        scratch_shapes=[pltpu.VMEM((nrow, GROUP_COLS), F32), pltpu.VMEM((nrow, GROUP_COLS), F32)],
        compiler_params=_cparams(("parallel",)),
        name="s5_core",
    )(z, w_in, w_toep, w_x, dec, d_perm)


def _s5_weights(lam_re, lam_im, log_dt, b_re, b_im, c_re, c_im, d_skip):
    g, p, c16 = SSM_GROUPS, SSM_STATE, SSM_GROUP
    lam = lax.complex(lam_re.astype(F32), lam_im.astype(F32))
    dt = jnp.exp(log_dt.astype(F32))[:, None]
    steps = jnp.arange(CHUNK + 1, dtype=F32)
    apow = jnp.exp((lam * dt)[None] * steps[:, None, None])
    a = apow[1]
    bbar = ((a - 1.0) / lam)[..., None] * lax.complex(b_re.astype(F32), b_im.astype(F32))
    cmat = lax.complex(c_re.astype(F32), c_im.astype(F32))
    odd = (jnp.arange(g) % 2)[:, None, None]

    win = apow[CHUNK - 1::-1][..., None] * bbar[None]
    win = jnp.transpose(win, (1, 0, 3, 2)).reshape(g, GROUP_COLS, p)
    zeros = jnp.zeros_like(win.real)
    w_in = jnp.concatenate([
        jnp.where(odd == 0, win.real, zeros), jnp.where(odd == 1, win.real, zeros),
        jnp.where(odd == 0, win.imag, zeros), jnp.where(odd == 1, win.imag, zeros)], axis=-1)

    kern = jnp.einsum('gop,kgp,gpi->kgoi', cmat, apow[:CHUNK], bbar, precision=HIGHEST).real
    idx = jnp.arange(CHUNK)
    onehot = (idx[None, None, :] - idx[None, :, None] == idx[:, None, None]).astype(F32)
    w_toep = jnp.einsum('kst,kgoi->gsito', onehot, kern, precision=HIGHEST)
    w_toep = w_toep.reshape(g, GROUP_COLS, GROUP_COLS)

    cw = cmat[None] * apow[1:, :, None, :]
    cw = jnp.transpose(cw, (1, 3, 0, 2)).reshape(g, p, GROUP_COLS)
    zx = jnp.zeros_like(cw.real)
    w_x = jnp.concatenate([
        jnp.where(odd == 0, cw.real, zx), jnp.where(odd == 1, cw.real, zx),
        jnp.where(odd == 0, -cw.imag, zx), jnp.where(odd == 1, -cw.imag, zx)], axis=1)

    a16 = apow[CHUNK].reshape(N_PAIRS, 2 * p)
    dec = jnp.concatenate([a16.real, a16.imag], axis=-1).reshape(1, N_PAIRS * GROUP_COLS)
    d_perm = jnp.broadcast_to(d_skip.astype(F32).reshape(g, 1, c16), (g, CHUNK, c16)).reshape(1, g * GROUP_COLS)
    return w_in.astype(BF16), w_toep.astype(BF16), w_x.astype(BF16), dec, d_perm


def _mlp(u, w1_ref, w2_ref):
    d = D_MODEL
    acc = None
    for k in range(D_FF // d):
        a = jnp.dot(u, w1_ref[:, k * d:(k + 1) * d], preferred_element_type=F32)
        a = jnp.square(jnp.maximum(a, 0.0)).astype(BF16)
        part = jnp.dot(a, w2_ref[k * d:(k + 1) * d, :], preferred_element_type=F32)
        acc = part if acc is None else acc + part
    return acc


def _final_norm(h, fg_ref):
    ms = jnp.mean(h * h, axis=-1, keepdims=True)
    return h * lax.rsqrt(ms + EPS) * fg_ref[...]


def _post_attn_kernel(*refs, final):
    if final:
        h_ref, y_ref, moda_ref, modm_ref, g_ref, wp_ref, w1_ref, w2_ref, fg_ref, o_ref = refs
    else:
        h_ref, y_ref, moda_ref, modm_ref, g_ref, wp_ref, w1_ref, w2_ref, o_ref = refs
    d = D_MODEL
    ymix = jnp.dot(y_ref[...], wp_ref[...], preferred_element_type=F32)
    h1 = h_ref[...] + moda_ref[:, 2 * d:3 * d] * ymix
    u = _normmod(h1, g_ref[...], modm_ref[:, :d], modm_ref[:, d:2 * d]).astype(BF16)
    h2 = h1 + modm_ref[:, 2 * d:3 * d] * _mlp(u, w1_ref, w2_ref)
    o_ref[...] = _final_norm(h2, fg_ref) if final else h2


def _post_attn(h, o, moda, modm, g2, w_o, w1, w2, final_g, tm=MLP_ROWS):
    bsz, seq, d = h.shape
    per_b = seq // tm
    row_spec = pl.BlockSpec((None, tm, d), lambda i: (i // per_b, i % per_b, 0))
    final = final_g is not None
    in_specs = [
        row_spec, row_spec,
        pl.BlockSpec((None, 1, moda.shape[-1]), lambda i: (i // per_b, 0, 0)),
        pl.BlockSpec((None, 1, modm.shape[-1]), lambda i: (i // per_b, 0, 0)),
        pl.BlockSpec((1, d), lambda i: (0, 0)),
        _resident(w_o.shape, lambda i: (0, 0)),
        _resident((d, D_FF), lambda i: (0, 0)),
        _resident((D_FF, d), lambda i: (0, 0)),
    ]
    args = [h, o, moda, modm, g2.reshape(1, d), w_o, w1, w2]
    if final:
        in_specs.append(pl.BlockSpec((1, d), lambda i: (0, 0)))
        args.append(final_g.reshape(1, d))
    return pl.pallas_call(
        functools.partial(_post_attn_kernel, final=final),
        grid=(bsz * per_b,),
        in_specs=in_specs,
        out_specs=row_spec,
        out_shape=jax.ShapeDtypeStruct((bsz, seq, d), F32),
        compiler_params=_cparams(("parallel",)),
        name="post_attn_mlp",
    )(*args)


def _post_s5_kernel(h_ref, z_ref, moda_ref, modm_ref, g_ref, permt_ref, wp_ref, w1_ref, w2_ref, o_ref,
                    z_scr, h1_scr, u_scr):
    d = D_MODEL
    bsz, tt, _ = h_ref.shape
    rows = z_ref.shape[0]
    for j in range(d // LANES):
        for half in range(CHUNK // 8):
            tiles = []
            for gl in range(8):
                col = (8 * j + gl) * GROUP_COLS + half * LANES
                tiles.append(z_ref[:, col:col + LANES].astype(F32))
            outs = _regroup8(tiles)
            for tl in range(8):
                t = 8 * half + tl
                z_scr[t * rows:(t + 1) * rows, j * LANES:(j + 1) * LANES] = outs[tl].astype(BF16)
    zn = jnp.dot(permt_ref[...], z_scr[...], preferred_element_type=F32).astype(BF16)
    zz = jnp.dot(zn, wp_ref[...], preferred_element_type=F32)
    ymix = zz[:, :d] * jax.nn.sigmoid(zz[:, d:])
    g = g_ref[...]
    for b in range(bsz):
        sl = slice(b * tt, (b + 1) * tt)
        h1 = h_ref[b] + moda_ref[b, :, 2 * d:3 * d] * ymix[sl, :]
        h1_scr[sl, :] = h1
        u_scr[sl, :] = _normmod(h1, g, modm_ref[b, :, :d], modm_ref[b, :, d:2 * d]).astype(BF16)
    acc = _mlp(u_scr[...], w1_ref, w2_ref)
    for b in range(bsz):
        sl = slice(b * tt, (b + 1) * tt)
        o_ref[b] = h1_scr[sl, :] + modm_ref[b, :, 2 * d:3 * d] * acc[sl, :]


def _post_s5(h, zo, moda, modm, g2, permt, w_glu, w1, w2):
    bsz, seq, d = h.shape
    tt = S5_TILE_TOKENS
    rows = bsz * tt // CHUNK
    h_spec = pl.BlockSpec((bsz, tt, d), lambda i: (0, i, 0))
    return pl.pallas_call(
        _post_s5_kernel,
        grid=(seq // tt,),
        in_specs=[
            h_spec,
            pl.BlockSpec((rows, CHUNK * d), lambda i: (i, 0)),
            pl.BlockSpec(moda.shape, lambda i: (0, 0, 0)),
            pl.BlockSpec(modm.shape, lambda i: (0, 0, 0)),
            pl.BlockSpec((1, d), lambda i: (0, 0)),
            pl.BlockSpec(permt.shape, lambda i: (0, 0)),
            _resident(w_glu.shape, lambda i: (0, 0)),
            _resident((d, D_FF), lambda i: (0, 0)),
            _resident((D_FF, d), lambda i: (0, 0)),
        ],
        out_specs=h_spec,
        out_shape=jax.ShapeDtypeStruct((bsz, seq, d), F32),
        scratch_shapes=[pltpu.VMEM((bsz * tt, d), BF16), pltpu.VMEM((bsz * tt, d), F32),
                        pltpu.VMEM((bsz * tt, d), BF16)],
        compiler_params=_cparams(("parallel",)),
        name="post_s5_mlp",
    )(h, zo, moda, modm, g2.reshape(1, d), permt, w_glu, w1, w2)


def kernel(x, c, ln_g, ada_w, ada_b, ssm_lam_re, ssm_lam_im, ssm_log_dt, ssm_b_re, ssm_b_im, ssm_c_re, ssm_c_im, ssm_d, ssm_w_glu, kv_g, kv_ada_w, kv_ada_b, w_kv, attn_w_q, attn_w_o, mlp_w1, mlp_w2, final_g):
    bsz, seq, d = x.shape
    depth = ln_g.shape[0]
    n_s5 = ssm_lam_re.shape[0]
    assert d == D_MODEL and seq % (DILATIONS[-1] * ATT_BLOCK) == 0

    mods = _ada_mods(c, ada_w.reshape(depth * 2, d, 3 * d), ada_b.reshape(depth * 2, 3 * d))
    mods = mods.reshape(depth, 2, bsz, 1, 3 * d)
    kv_mod = _ada_mods(c, kv_ada_w[None], kv_ada_b[None]).reshape(bsz, 1, 2 * d)

    w1 = mlp_w1.astype(BF16)
    w2 = mlp_w2.astype(BF16)
    perm_np = _s5_tile_perm(bsz)
    perm = jnp.asarray(perm_np, BF16)
    permt = jnp.asarray(perm_np.T, BF16)

    h = x
    for layer in range(n_s5):
        w_in, w_toep, w_x, dec, d_perm = _s5_weights(
            ssm_lam_re[layer], ssm_lam_im[layer], ssm_log_dt[layer], ssm_b_re[layer], ssm_b_im[layer],
            ssm_c_re[layer], ssm_c_im[layer], ssm_d[layer])
        z = _s5_pre(h, mods[layer, 0], ln_g[layer, 0], perm)
        zo = _s5_core(z, w_in, w_toep, w_x, dec, d_perm, bsz)
        h = _post_s5(h, zo, mods[layer, 0], mods[layer, 1], ln_g[layer, 1], permt,
                     ssm_w_glu[layer].astype(BF16), w1[layer], w2[layer])

    kv_cols = tuple((i * d, (N_BRANCHES + i) * d) for i in range(N_BRANCHES))
    q_cols = tuple((i * d,) for i in range(N_BRANCHES))
    kvs = _norm_proj(h, kv_mod, kv_g, w_kv.astype(BF16), kv_cols, 1.0)
    for layer in range(n_s5, depth):
        j = layer - n_s5
        qs = _norm_proj(h, mods[layer, 0], ln_g[layer, 0], attn_w_q[j].astype(BF16), q_cols, Q_SCALE)
        o = _attention(qs, kvs)
        h = _post_attn(h, o, mods[layer, 0], mods[layer, 1], ln_g[layer, 1],
                       attn_w_o[j].astype(BF16), w1[layer], w2[layer],
                       final_g if layer == depth - 1 else None)
    return h
```

```python
import functools
import math

import numpy as np

import jax
import jax.numpy as jnp
from jax import lax
from jax.experimental import pallas as pl
from jax.experimental.pallas import tpu as pltpu

F32 = jnp.float32
BF16 = jnp.bfloat16

D_MODEL = 1024
SSM_GROUP = 16
SSM_GROUPS = D_MODEL // SSM_GROUP
SSM_STATE = 64
HEAD_DIM = 64
N_HEADS = D_MODEL // HEAD_DIM
DILATIONS = (1, 4, 16)
N_BRANCHES = len(DILATIONS)
ATT_BLOCK = 128
ATT_GROUP = 16
N_STATS = 3
Q_SCALE = HEAD_DIM ** -0.5 * math.log2(math.e)
D_FF = 4 * D_MODEL
EPS = 1e-6
NEG = -1e30

LANES = 128
CHUNK = 16
GROUP_COLS = CHUNK * SSM_GROUP
PAIR_COLS = 2 * GROUP_COLS
N_PAIRS = SSM_GROUPS // 2
S5_TILE_TOKENS = 2 * CHUNK
MLP_ROWS = 512
PROJ_TILE = 512
VMEM_LIMIT = 56 * 1024 * 1024

HIGHEST = lax.Precision.HIGHEST
NT_DIMS = (((1,), (1,)), ((), ()))


def _cparams(sem):
    return pltpu.CompilerParams(dimension_semantics=sem, vmem_limit_bytes=VMEM_LIMIT)


def _resident(shape, index_map):
    return pl.BlockSpec(shape, index_map, pipeline_mode=pl.Buffered(1))


def _normmod(x, g, shift, scale):
    ms = jnp.mean(x * x, axis=-1, keepdims=True)
    return (x * lax.rsqrt(ms + EPS) * g) * (1.0 + scale) + shift


def _ada_kernel(c_ref, w_ref, b_ref, o_ref):
    c = c_ref[...]
    sc = c * jax.nn.sigmoid(c)
    o_ref[...] = jnp.dot(sc, w_ref[...], preferred_element_type=F32, precision=HIGHEST) + b_ref[...]


def _ada_mods(c, w, b):
    n, d, width = w.shape
    bsz = c.shape[0]
    tn = 1024
    return pl.pallas_call(
        _ada_kernel,
        grid=(n, width // tn),
        in_specs=[
            pl.BlockSpec((bsz, d), lambda i, j: (0, 0)),
            pl.BlockSpec((None, d, tn), lambda i, j: (i, 0, j)),
            pl.BlockSpec((None, 1, tn), lambda i, j: (i, 0, j)),
        ],
        out_specs=pl.BlockSpec((None, bsz, tn), lambda i, j: (i, 0, j)),
        out_shape=jax.ShapeDtypeStruct((n, bsz, width), F32),
        compiler_params=_cparams(("parallel", "parallel")),
        name="ada_mods",
    )(c, w, b.reshape(n, 1, width))


def _proj_kernel(h_ref, mod_ref, g_ref, w_ref, o1_ref, o2_ref, o3_ref, us, us4, up4, up16, *, wcols, out_scale):
    d = D_MODEL
    tm = h_ref.shape[0]
    d1 = DILATIONS[1]
    q4, q16 = tm // d1, tm // DILATIONS[2]
    cw = o1_ref.shape[1]
    u = _normmod(h_ref[...], g_ref[...], mod_ref[:, :d], mod_ref[:, d:2 * d])
    for l in range(d // LANES):
        us[l] = u[:, l * LANES:(l + 1) * LANES]
    ub = u.astype(BF16)
    for n in range(cw // d):
        cols = slice(n * d, (n + 1) * d)
        res = jnp.dot(ub, w_ref[:, wcols[0][n]:wcols[0][n] + d], preferred_element_type=F32)
        o1_ref[:, cols] = (res * out_scale).astype(o1_ref.dtype)
    for l in range(d // LANES):
        for a in range(d1):
            part = us[l, pl.ds(a, q4, stride=d1), :]
            us4[l, a * q4:(a + 1) * q4, :] = part
            up4[a * q4:(a + 1) * q4, l * LANES:(l + 1) * LANES] = part.astype(BF16)
    for n in range(cw // d):
        res = jnp.dot(up4[...], w_ref[:, wcols[1][n]:wcols[1][n] + d], preferred_element_type=F32)
        for a in range(d1):
            o2_ref[a, :, n * d:(n + 1) * d] = (res[a * q4:(a + 1) * q4, :] * out_scale).astype(o2_ref.dtype)
    for l in range(d // LANES):
        for a1 in range(d1):
            for a2 in range(d1):
                r = d1 * a2 + a1
                up16[r * q16:(r + 1) * q16, l * LANES:(l + 1) * LANES] = us4[
                    l, pl.ds(a1 * q4 + a2, q16, stride=d1), :].astype(BF16)
    for n in range(cw // d):
        res = jnp.dot(up16[...], w_ref[:, wcols[2][n]:wcols[2][n] + d], preferred_element_type=F32)
        for r in range(DILATIONS[2]):
            o3_ref[r, :, n * d:(n + 1) * d] = (res[r * q16:(r + 1) * q16, :] * out_scale).astype(o3_ref.dtype)


def _norm_proj(h, mod, g, w, wcols, out_scale):
    bsz, seq, d = h.shape
    tm = PROJ_TILE
    cw = len(wcols[0]) * d
    per_b = seq // tm
    d1, d2 = DILATIONS[1], DILATIONS[2]
    o1, o2, o3 = pl.pallas_call(
        functools.partial(_proj_kernel, wcols=wcols, out_scale=out_scale),
        grid=(bsz * per_b,),
        in_specs=[
            pl.BlockSpec((None, tm, d), lambda i: (i // per_b, i % per_b, 0)),
            pl.BlockSpec((None, 1, mod.shape[-1]), lambda i: (i // per_b, 0, 0)),
            pl.BlockSpec((1, d), lambda i: (0, 0)),
            _resident(w.shape, lambda i: (0, 0)),
        ],
        out_specs=[
            pl.BlockSpec((None, tm, cw), lambda i: (i // per_b, i % per_b, 0)),
            pl.BlockSpec((None, d1, None, tm // d1, cw), lambda i: (i // per_b, 0, i % per_b, 0, 0)),
            pl.BlockSpec((None, d2, None, tm // d2, cw), lambda i: (i // per_b, 0, i % per_b, 0, 0)),
        ],
        out_shape=[
            jax.ShapeDtypeStruct((bsz, seq, cw), BF16),
            jax.ShapeDtypeStruct((bsz, d1, per_b, tm // d1, cw), BF16),
            jax.ShapeDtypeStruct((bsz, d2, per_b, tm // d2, cw), BF16),
        ],
        scratch_shapes=[pltpu.VMEM((d // LANES, tm, LANES), F32), pltpu.VMEM((d // LANES, tm, LANES), F32),
                        pltpu.VMEM((tm, d), BF16), pltpu.VMEM((tm, d), BF16)],
        compiler_params=_cparams(("parallel",)),
        name="norm_proj",
    )(h, mod, g.reshape(1, d), w)
    return o1, o2.reshape(bsz, seq, cw), o3.reshape(bsz, seq, cw)


def _attn_group(blocks):
    blk = ATT_BLOCK
    lane = lax.broadcasted_iota(jnp.int32, (blk, LANES), 1)
    lo = lane < HEAD_DIM
    scores = []
    for qb, kcat, _, mask, _ in blocks:
        zero = jnp.zeros_like(qb)
        q2 = jnp.concatenate([jnp.where(lo, qb, zero), jnp.where(lo, zero, qb)], axis=0)
        s = lax.dot_general(q2, kcat, NT_DIMS, preferred_element_type=F32)
        scores.append(s + mask[...])
    soft = []
    for s, (_, _, _, _, old) in zip(scores, blocks):
        ms, ps = [], []
        for hh in range(2):
            tiles = [s[hh * blk:(hh + 1) * blk, t * LANES:(t + 1) * LANES] for t in range(s.shape[1] // LANES)]
            mx = tiles[0]
            for t in tiles[1:]:
                mx = jnp.maximum(mx, t)
            m = jnp.broadcast_to(jnp.max(mx, axis=-1, keepdims=True), (blk, LANES))
            if old is not None:
                m = jnp.maximum(m, old[0][hh])
            ms.append(m)
            ps.append(jnp.concatenate([jnp.exp2(t - m).astype(BF16) for t in tiles], axis=1))
        alpha = None
        if old is not None:
            alpha = jnp.exp2(jnp.where(lo, old[0][0], old[0][1]) - jnp.where(lo, ms[0], ms[1]))
        soft.append((ms, jnp.concatenate(ps, axis=0), alpha))
    outs = []
    for (ms, p, alpha), (_, _, vcat, _, old) in zip(soft, blocks):
        pv = jnp.dot(p, vcat, preferred_element_type=F32)
        acc = jnp.where(lo, pv[:blk, :LANES], pv[blk:, :LANES])
        den = jnp.where(lo, pv[:blk, LANES:], pv[blk:, LANES:])
        if old is not None:
            acc = alpha * old[1] + acc
            den = alpha * old[0][2] + den
        outs.append((ms + [den], acc))
    return outs


def _attn_kernel(q1, qd2, qd3, k1, kd2, kd3, v1, v2, v3, o_ref, tmp_nat, vd1, vd2, vd3,
                 acc_ref, st_ref, acc2_ref, st2_ref, mask_pc, mask_c):
    blk = ATT_BLOCK
    seq = q1.shape[0]
    nblk = seq // blk
    d1, d2 = DILATIONS[1], DILATIONS[2]
    quarter = seq // d1
    per_res = quarter // blk
    assert d2 == d1 * d1 and seq == d2 * blk and nblk % ATT_GROUP == 0 and ATT_GROUP % per_res == 0
    @pl.when(jnp.logical_and(pl.program_id(0) == 0, pl.program_id(1) == 0))
    def _():
        for vd in (vd1, vd2, vd3):
            vd[:, LANES:] = jnp.ones((seq, LANES), BF16)
        qq = lax.broadcasted_iota(jnp.int32, (2 * blk, 2 * blk), 0) & (blk - 1)
        kk = lax.broadcasted_iota(jnp.int32, (2 * blk, 2 * blk), 1)
        valid = jnp.logical_or(jnp.logical_and(kk < blk, kk >= qq), jnp.logical_and(kk >= blk, kk - blk <= qq))
        mask_pc[...] = jnp.where(valid, 0.0, NEG)
        valid_c = (lax.broadcasted_iota(jnp.int32, (2 * blk, blk), 1)
                   <= (lax.broadcasted_iota(jnp.int32, (2 * blk, blk), 0) & (blk - 1)))
        mask_c[...] = jnp.where(valid_c, 0.0, NEG)

    for v, vd in ((v1, vd1), (v2, vd2), (v3, vd3)):
        vd[:, :LANES] = v[...]

    def run_group(blocks, dests):
        for (st_dst, acc_dst, rows), (st, acc) in zip(dests, _attn_group(blocks)):
            for k in range(N_STATS):
                st_dst[k, rows, :] = st[k]
            acc_dst[rows, :] = acc

    def load_old(st_src, acc_src, rows):
        return [st_src[k, rows, :] for k in range(N_STATS)], acc_src[rows, :]

    def b0_group(ns, first):
        blocks, dests = [], []
        for n in ns:
            if first and n == 0:
                rows = pl.ds(0, blk)
                blocks.append((q1[rows, :], k1[rows, :], vd1[rows, :], mask_c, None))
            else:
                rows = pl.ds(pl.multiple_of(n * blk, blk), blk)
                krows = pl.ds(pl.multiple_of((n - 1) * blk, blk), 2 * blk)
                blocks.append((q1[rows, :], k1[krows, :], vd1[krows, :], mask_pc, None))
            dests.append((st_ref, acc_ref, rows))
        run_group(blocks, dests)

    b0_group(list(range(ATT_GROUP)), True)

    def b0_body(i, carry):
        b0_group([ATT_GROUP * i + j for j in range(ATT_GROUP)], False)
        return carry

    lax.fori_loop(1, nblk // ATT_GROUP, b0_body, 0)

    res_per_trip = ATT_GROUP // per_res

    def b1_body(i, carry):
        blocks, dests = [], []
        for jr in range(res_per_trip):
            a1 = res_per_trip * i + jr
            base = a1 * quarter
            for n in range(per_res):
                rows = pl.ds(a1 + d1 * blk * n, blk, stride=d1)
                qrows = pl.ds(pl.multiple_of(base + n * blk, blk), blk)
                if n == 0:
                    krows, mask = qrows, mask_c
                else:
                    krows, mask = pl.ds(pl.multiple_of(base + (n - 1) * blk, blk), 2 * blk), mask_pc
                blocks.append((qd2[qrows, :], kd2[krows, :], vd2[krows, :], mask,
                               load_old(st_ref, acc_ref, rows)))
                dests.append((st2_ref, acc2_ref, qrows))
        run_group(blocks, dests)
        return carry

    lax.fori_loop(0, d1 // res_per_trip, b1_body, 0)

    def b2_body(i, carry):
        blocks, dests = [], []
        for j in range(ATT_GROUP):
            a1, a2 = j % d1, (ATT_GROUP // d1) * i + j // d1
            qrows = pl.ds(pl.multiple_of((ATT_GROUP * i + j) * blk, blk), blk)
            rows = pl.ds(a1 * quarter + a2, blk, stride=d1)
            blocks.append((qd3[qrows, :], kd3[qrows, :], vd3[qrows, :], mask_c,
                           load_old(st2_ref, acc2_ref, rows)))
            dests.append((st2_ref, acc2_ref, rows))
        run_group(blocks, dests)
        return carry

    lax.fori_loop(0, d2 // ATT_GROUP, b2_body, 0)

    for a in range(d1):
        seg = slice(a * quarter, (a + 1) * quarter)
        tmp_nat[pl.ds(a, quarter, stride=d1), :] = acc2_ref[seg, :] / st2_ref[N_STATS - 1, seg, :]
    o_ref[...] = tmp_nat[...].astype(o_ref.dtype)


def _attention(qs, kvs):
    bsz, seq, _ = qs[0].shape
    pairs = D_MODEL // LANES

    def slab(col0):
        return pl.BlockSpec((None, seq, LANES), lambda b, hp: (b, 0, col0 + hp))

    in_specs = [slab(0)] * N_BRANCHES + [slab(0)] * N_BRANCHES + [slab(pairs)] * N_BRANCHES
    return pl.pallas_call(
        _attn_kernel,
        grid=(bsz, pairs),
        in_specs=in_specs,
        out_specs=slab(0),
        out_shape=jax.ShapeDtypeStruct((bsz, seq, D_MODEL), BF16),
        scratch_shapes=([pltpu.VMEM((seq, LANES), F32)] + [pltpu.VMEM((seq, 2 * LANES), BF16)] * 3
                        + [pltpu.VMEM((seq, LANES), F32), pltpu.VMEM((N_STATS, seq, LANES), F32)] * 2
                        + [pltpu.VMEM((2 * ATT_BLOCK, 2 * ATT_BLOCK), F32), pltpu.VMEM((2 * ATT_BLOCK, ATT_BLOCK), F32)]),
        compiler_params=_cparams(("arbitrary", "arbitrary")),
        name="attention",
    )(*qs, *kvs, *kvs)


def _regroup8(tiles):
    t = list(tiles)
    lane = lax.broadcasted_iota(jnp.int32, t[0].shape, 1)
    piece = lane // SSM_GROUP
    for dist in (4, 2, 1):
        bit = (piece & dist) != 0
        shift = dist * SSM_GROUP
        for a in range(8):
            if a & dist:
                continue
            b = a | dist
            ta, tb = t[a], t[b]
            t[a] = jnp.where(bit, pltpu.roll(tb, shift, 1), ta)
            t[b] = jnp.where(bit, tb, pltpu.roll(ta, LANES - shift, 1))
    return t


def _s5_tile_perm(bsz):
    tt = S5_TILE_TOKENS
    n = bsz * tt
    perm = np.zeros((n, n), np.float32)
    for b in range(bsz):
        for c in range(tt // CHUNK):
            for t in range(CHUNK):
                perm[(t * (tt // CHUNK) + c) * bsz + b, b * tt + c * CHUNK + t] = 1.0
    return perm


def _s5_pre_kernel(h_ref, mod_ref, g_ref, perm_ref, z_ref, u_scr, up_scr):
    d = D_MODEL
    bsz, tt, _ = h_ref.shape
    g = g_ref[...]
    for b in range(bsz):
        u_scr[b * tt:(b + 1) * tt, :] = _normmod(
            h_ref[b], g, mod_ref[b, :, :d], mod_ref[b, :, d:2 * d]).astype(BF16)
    up_scr[...] = jnp.dot(perm_ref[...], u_scr[...], preferred_element_type=F32)
    rows = z_ref.shape[0]
    for j in range(d // LANES):
        for half in range(CHUNK // 8):
            tiles = [up_scr[(8 * half + tl) * rows:(8 * half + tl + 1) * rows, j * LANES:(j + 1) * LANES]
                     for tl in range(8)]
            outs = _regroup8(tiles)
            for gl in range(8):
                col = (8 * j + gl) * GROUP_COLS + half * LANES
                z_ref[:, col:col + LANES] = outs[gl].astype(z_ref.dtype)


def _s5_pre(h, mod, g, perm):
    bsz, seq, d = h.shape
    tt = S5_TILE_TOKENS
    rows = bsz * tt // CHUNK
    return pl.pallas_call(
        _s5_pre_kernel,
        grid=(seq // tt,),
        in_specs=[
            pl.BlockSpec((bsz, tt, d), lambda i: (0, i, 0)),
            pl.BlockSpec(mod.shape, lambda i: (0, 0, 0)),
            pl.BlockSpec((1, d), lambda i: (0, 0)),
            pl.BlockSpec(perm.shape, lambda i: (0, 0)),
        ],
        out_specs=pl.BlockSpec((rows, CHUNK * d), lambda i: (i, 0)),
        out_shape=jax.ShapeDtypeStruct((bsz * seq // CHUNK, CHUNK * d), BF16),
        scratch_shapes=[pltpu.VMEM((bsz * tt, d), BF16), pltpu.VMEM((bsz * tt, d), F32)],
        compiler_params=_cparams(("parallel",)),
        name="s5_pre",
    )(h, mod, g.reshape(1, d), perm)


def _s5_core_kernel(z_ref, win_ref, wt_ref, wx_ref, dec_ref, d_ref, o_ref, s_scr, x_scr, *, bsz):
    gc = GROUP_COLS
    nchunk = z_ref.shape[0] // bsz
    s_scr[...] = (jnp.dot(z_ref[:, :gc], win_ref[0], preferred_element_type=F32)
                  + jnp.dot(z_ref[:, gc:], win_ref[1], preferred_element_type=F32))
    ar, ai = dec_ref[:, :LANES], dec_ref[:, LANES:]

    def step(c, carry):
        xr, xi = carry
        rows = pl.ds(pl.multiple_of(c * bsz, bsz), bsz)
        x_scr[rows, :LANES] = xr
        x_scr[rows, LANES:] = xi
        return (ar * xr - ai * xi + s_scr[rows, :LANES], ar * xi + ai * xr + s_scr[rows, LANES:])

    zero = jnp.zeros((bsz, LANES), F32)
    lax.fori_loop(0, nchunk, step, (zero, zero))
    x = x_scr[...].astype(BF16)
    for gi in range(2):
        z = z_ref[:, gi * gc:(gi + 1) * gc]
        y = (jnp.dot(z, wt_ref[gi], preferred_element_type=F32)
             + jnp.dot(x, wx_ref[gi], preferred_element_type=F32))
        y = y + d_ref[:, gi * gc:(gi + 1) * gc] * z.astype(F32)
        o_ref[:, gi * gc:(gi + 1) * gc] = jax.nn.gelu(y).astype(o_ref.dtype)


def _s5_core(z, w_in, w_toep, w_x, dec, d_perm, bsz):
    nrow, width = z.shape
    wspec = pl.BlockSpec((2, GROUP_COLS, GROUP_COLS), lambda k: (k, 0, 0))
    return pl.pallas_call(
        functools.partial(_s5_core_kernel, bsz=bsz),
        grid=(N_PAIRS,),
        in_specs=[
            pl.BlockSpec((nrow, PAIR_COLS), lambda k: (0, k)),
            wspec, wspec, wspec,
            pl.BlockSpec((1, GROUP_COLS), lambda k: (0, k)),
            pl.BlockSpec((1, PAIR_COLS), lambda k: (0, k)),
        ],
        out_specs=pl.BlockSpec((nrow, PAIR_COLS), lambda k: (0, k)),
        out_shape=jax.ShapeDtypeStruct((nrow, width), BF16),
        scratch_shapes=[pltpu.VMEM((nrow, GROUP_COLS), F32), pltpu.VMEM((nrow, GROUP_COLS), F32)],
        compiler_params=_cparams(("parallel",)),
        name="s5_core",
    )(z, w_in, w_toep, w_x, dec, d_perm)


def _s5_weights(lam_re, lam_im, log_dt, b_re, b_im, c_re, c_im, d_skip):
    g, p, c16 = SSM_GROUPS, SSM_STATE, SSM_GROUP
    lam = lax.complex(lam_re.astype(F32), lam_im.astype(F32))
    dt = jnp.exp(log_dt.astype(F32))[:, None]
    steps = jnp.arange(CHUNK + 1, dtype=F32)
    apow = jnp.exp((lam * dt)[None] * steps[:, None, None])
    a = apow[1]
    bbar = ((a - 1.0) / lam)[..., None] * lax.complex(b_re.astype(F32), b_im.astype(F32))
    cmat = lax.complex(c_re.astype(F32), c_im.astype(F32))
    odd = (jnp.arange(g) % 2)[:, None, None]

    win = apow[CHUNK - 1::-1][..., None] * bbar[None]
    win = jnp.transpose(win, (1, 0, 3, 2)).reshape(g, GROUP_COLS, p)
    zeros = jnp.zeros_like(win.real)
    w_in = jnp.concatenate([
        jnp.where(odd == 0, win.real, zeros), jnp.where(odd == 1, win.real, zeros),
        jnp.where(odd == 0, win.imag, zeros), jnp.where(odd == 1, win.imag, zeros)], axis=-1)

    kern = jnp.einsum('gop,kgp,gpi->kgoi', cmat, apow[:CHUNK], bbar, precision=HIGHEST).real
    idx = jnp.arange(CHUNK)
    onehot = (idx[None, None, :] - idx[None, :, None] == idx[:, None, None]).astype(F32)
    w_toep = jnp.einsum('kst,kgoi->gsito', onehot, kern, precision=HIGHEST)
    w_toep = w_toep.reshape(g, GROUP_COLS, GROUP_COLS)

    cw = cmat[None] * apow[1:, :, None, :]
    cw = jnp.transpose(cw, (1, 3, 0, 2)).reshape(g, p, GROUP_COLS)
    zx = jnp.zeros_like(cw.real)
    w_x = jnp.concatenate([
        jnp.where(odd == 0, cw.real, zx), jnp.where(odd == 1, cw.real, zx),
        jnp.where(odd == 0, -cw.imag, zx), jnp.where(odd == 1, -cw.imag, zx)], axis=1)

    a16 = apow[CHUNK].reshape(N_PAIRS, 2 * p)
    dec = jnp.concatenate([a16.real, a16.imag], axis=-1).reshape(1, N_PAIRS * GROUP_COLS)
    d_perm = jnp.broadcast_to(d_skip.astype(F32).reshape(g, 1, c16), (g, CHUNK, c16)).reshape(1, g * GROUP_COLS)
    return w_in.astype(BF16), w_toep.astype(BF16), w_x.astype(BF16), dec, d_perm


def _mlp(u, w1_ref, w2_ref):
    d = D_MODEL
    acc = None
    for k in range(D_FF // d):
        a = jnp.dot(u, w1_ref[:, k * d:(k + 1) * d], preferred_element_type=F32)
        a = jnp.square(jnp.maximum(a, 0.0)).astype(BF16)
        part = jnp.dot(a, w2_ref[k * d:(k + 1) * d, :], preferred_element_type=F32)
        acc = part if acc is None else acc + part
    return acc


def _final_norm(h, fg_ref):
    ms = jnp.mean(h * h, axis=-1, keepdims=True)
    return h * lax.rsqrt(ms + EPS) * fg_ref[...]


def _post_attn_kernel(*refs, final):
    if final:
        h_ref, y_ref, moda_ref, modm_ref, g_ref, wp_ref, w1_ref, w2_ref, fg_ref, o_ref = refs
    else:
        h_ref, y_ref, moda_ref, modm_ref, g_ref, wp_ref, w1_ref, w2_ref, o_ref = refs
    d = D_MODEL
    ymix = jnp.dot(y_ref[...], wp_ref[...], preferred_element_type=F32)
    h1 = h_ref[...] + moda_ref[:, 2 * d:3 * d] * ymix
    u = _normmod(h1, g_ref[...], modm_ref[:, :d], modm_ref[:, d:2 * d]).astype(BF16)
    h2 = h1 + modm_ref[:, 2 * d:3 * d] * _mlp(u, w1_ref, w2_ref)
    o_ref[...] = _final_norm(h2, fg_ref) if final else h2


def _post_attn(h, o, moda, modm, g2, w_o, w1, w2, final_g, tm=MLP_ROWS):
    bsz, seq, d = h.shape
    per_b = seq // tm
    row_spec = pl.BlockSpec((None, tm, d), lambda i: (i // per_b, i % per_b, 0))
    final = final_g is not None
    in_specs = [
        row_spec, row_spec,
        pl.BlockSpec((None, 1, moda.shape[-1]), lambda i: (i // per_b, 0, 0)),
        pl.BlockSpec((None, 1, modm.shape[-1]), lambda i: (i // per_b, 0, 0)),
        pl.BlockSpec((1, d), lambda i: (0, 0)),
        _resident(w_o.shape, lambda i: (0, 0)),
        _resident((d, D_FF), lambda i: (0, 0)),
        _resident((D_FF, d), lambda i: (0, 0)),
    ]
    args = [h, o, moda, modm, g2.reshape(1, d), w_o, w1, w2]
    if final:
        in_specs.append(pl.BlockSpec((1, d), lambda i: (0, 0)))
        args.append(final_g.reshape(1, d))
    return pl.pallas_call(
        functools.partial(_post_attn_kernel, final=final),
        grid=(bsz * per_b,),
        in_specs=in_specs,
        out_specs=row_spec,
        out_shape=jax.ShapeDtypeStruct((bsz, seq, d), F32),
        compiler_params=_cparams(("parallel",)),
        name="post_attn_mlp",
    )(*args)


def _post_s5_kernel(h_ref, z_ref, moda_ref, modm_ref, g_ref, permt_ref, wp_ref, w1_ref, w2_ref, o_ref,
                    z_scr, h1_scr, u_scr):
    d = D_MODEL
    bsz, tt, _ = h_ref.shape
    rows = z_ref.shape[0]
    for j in range(d // LANES):
        for half in range(CHUNK // 8):
            tiles = []
            for gl in range(8):
                col = (8 * j + gl) * GROUP_COLS + half * LANES
                tiles.append(z_ref[:, col:col + LANES].astype(F32))
            outs = _regroup8(tiles)
            for tl in range(8):
                t = 8 * half + tl
                z_scr[t * rows:(t + 1) * rows, j * LANES:(j + 1) * LANES] = outs[tl].astype(BF16)
    zn = jnp.dot(permt_ref[...], z_scr[...], preferred_element_type=F32).astype(BF16)
    zz = jnp.dot(zn, wp_ref[...], preferred_element_type=F32)
    ymix = zz[:, :d] * jax.nn.sigmoid(zz[:, d:])
    g = g_ref[...]
    for b in range(bsz):
        sl = slice(b * tt, (b + 1) * tt)
        h1 = h_ref[b] + moda_ref[b, :, 2 * d:3 * d] * ymix[sl, :]
        h1_scr[sl, :] = h1
        u_scr[sl, :] = _normmod(h1, g, modm_ref[b, :, :d], modm_ref[b, :, d:2 * d]).astype(BF16)
    acc = _mlp(u_scr[...], w1_ref, w2_ref)
    for b in range(bsz):
        sl = slice(b * tt, (b + 1) * tt)
        o_ref[b] = h1_scr[sl, :] + modm_ref[b, :, 2 * d:3 * d] * acc[sl, :]


def _post_s5(h, zo, moda, modm, g2, permt, w_glu, w1, w2):
    bsz, seq, d = h.shape
    tt = S5_TILE_TOKENS
    rows = bsz * tt // CHUNK
    h_spec = pl.BlockSpec((bsz, tt, d), lambda i: (0, i, 0))
    return pl.pallas_call(
        _post_s5_kernel,
        grid=(seq // tt,),
        in_specs=[
            h_spec,
            pl.BlockSpec((rows, CHUNK * d), lambda i: (i, 0)),
            pl.BlockSpec(moda.shape, lambda i: (0, 0, 0)),
            pl.BlockSpec(modm.shape, lambda i: (0, 0, 0)),
            pl.BlockSpec((1, d), lambda i: (0, 0)),
            pl.BlockSpec(permt.shape, lambda i: (0, 0)),
            _resident(w_glu.shape, lambda i: (0, 0)),
            _resident((d, D_FF), lambda i: (0, 0)),
            _resident((D_FF, d), lambda i: (0, 0)),
        ],
        out_specs=h_spec,
        out_shape=jax.ShapeDtypeStruct((bsz, seq, d), F32),
        scratch_shapes=[pltpu.VMEM((bsz * tt, d), BF16), pltpu.VMEM((bsz * tt, d), F32),
                        pltpu.VMEM((bsz * tt, d), BF16)],
        compiler_params=_cparams(("parallel",)),
        name="post_s5_mlp",
    )(h, zo, moda, modm, g2.reshape(1, d), permt, w_glu, w1, w2)


def kernel(x, c, ln_g, ada_w, ada_b, ssm_lam_re, ssm_lam_im, ssm_log_dt, ssm_b_re, ssm_b_im, ssm_c_re, ssm_c_im, ssm_d, ssm_w_glu, kv_g, kv_ada_w, kv_ada_b, w_kv, attn_w_q, attn_w_o, mlp_w1, mlp_w2, final_g):
    bsz, seq, d = x.shape
    depth = ln_g.shape[0]
    n_s5 = ssm_lam_re.shape[0]
    assert d == D_MODEL and seq % (DILATIONS[-1] * ATT_BLOCK) == 0

    mods = _ada_mods(c, ada_w.reshape(depth * 2, d, 3 * d), ada_b.reshape(depth * 2, 3 * d))
    mods = mods.reshape(depth, 2, bsz, 1, 3 * d)
    kv_mod = _ada_mods(c, kv_ada_w[None], kv_ada_b[None]).reshape(bsz, 1, 2 * d)

    w1 = mlp_w1.astype(BF16)
    w2 = mlp_w2.astype(BF16)
    perm_np = _s5_tile_perm(bsz)
    perm = jnp.asarray(perm_np, BF16)
    permt = jnp.asarray(perm_np.T, BF16)

    h = x
    for layer in range(n_s5):
        w_in, w_toep, w_x, dec, d_perm = _s5_weights(
            ssm_lam_re[layer], ssm_lam_im[layer], ssm_log_dt[layer], ssm_b_re[layer], ssm_b_im[layer],
            ssm_c_re[layer], ssm_c_im[layer], ssm_d[layer])
        z = _s5_pre(h, mods[layer, 0], ln_g[layer, 0], perm)
        zo = _s5_core(z, w_in, w_toep, w_x, dec, d_perm, bsz)
        h = _post_s5(h, zo, mods[layer, 0], mods[layer, 1], ln_g[layer, 1], permt,
                     ssm_w_glu[layer].astype(BF16), w1[layer], w2[layer])

    kv_cols = tuple((i * d, (N_BRANCHES + i) * d) for i in range(N_BRANCHES))
    q_cols = tuple((i * d,) for i in range(N_BRANCHES))
    kvs = _norm_proj(h, kv_mod, kv_g, w_kv.astype(BF16), kv_cols, 1.0)
    for layer in range(n_s5, depth):
        j = layer - n_s5
        qs = _norm_proj(h, mods[layer, 0], ln_g[layer, 0], attn_w_q[j].astype(BF16), q_cols, Q_SCALE)
        o = _attention(qs, kvs)
        h = _post_attn(h, o, mods[layer, 0], mods[layer, 1], ln_g[layer, 1],
                       attn_w_o[j].astype(BF16), w1[layer], w2[layer],
                       final_g if layer == depth - 1 else None)
    return h
```

```python
import functools
import math

import numpy as np

import jax
import jax.numpy as jnp
from jax import lax
from jax.experimental import pallas as pl
from jax.experimental.pallas import tpu as pltpu

F32 = jnp.float32
BF16 = jnp.bfloat16

D_MODEL = 1024
SSM_GROUP = 16
SSM_GROUPS = D_MODEL // SSM_GROUP
SSM_STATE = 64
HEAD_DIM = 64
N_HEADS = D_MODEL // HEAD_DIM
DILATIONS = (1, 4, 16)
N_BRANCHES = len(DILATIONS)
ATT_BLOCK = 128
ATT_GROUP = 16
N_STATS = 3
Q_SCALE = HEAD_DIM ** -0.5 * math.log2(math.e)
D_FF = 4 * D_MODEL
EPS = 1e-6
NEG = -1e30

LANES = 128
CHUNK = 16
GROUP_COLS = CHUNK * SSM_GROUP
PAIR_COLS = 2 * GROUP_COLS
N_PAIRS = SSM_GROUPS // 2
S5_TILE_TOKENS = 2 * CHUNK
MLP_ROWS = 512
PROJ_TILE = 512
VMEM_LIMIT = 56 * 1024 * 1024

HIGHEST = lax.Precision.HIGHEST
NT_DIMS = (((1,), (1,)), ((), ()))


def _cparams(sem):
    return pltpu.CompilerParams(dimension_semantics=sem, vmem_limit_bytes=VMEM_LIMIT)


def _layer_spec(stack, layer):
    zeros = (0,) * (stack.ndim - 1)
    return pl.BlockSpec((None,) + stack.shape[1:], lambda i: (layer,) + zeros, pipeline_mode=pl.Buffered(1))


def _normmod(x, g, shift, scale):
    ms = jnp.mean(x * x, axis=-1, keepdims=True)
    return (x * lax.rsqrt(ms + EPS) * g) * (1.0 + scale) + shift


def _ada_kernel(c_ref, w_ref, b_ref, o_ref):
    c = c_ref[...]
    sc = c * jax.nn.sigmoid(c)
    o_ref[...] = jnp.dot(sc, w_ref[...], preferred_element_type=F32, precision=HIGHEST) + b_ref[...]


def _ada_mods(c, w, b):
    n, d, width = w.shape
    bsz = c.shape[0]
    tn = width
    return pl.pallas_call(
        _ada_kernel,
        grid=(n, width // tn),
        in_specs=[
            pl.BlockSpec((bsz, d), lambda i, j: (0, 0)),
            pl.BlockSpec((None, d, tn), lambda i, j: (i, 0, j)),
            pl.BlockSpec((None, 1, tn), lambda i, j: (i, 0, j)),
        ],
        out_specs=pl.BlockSpec((None, bsz, tn), lambda i, j: (i, 0, j)),
        out_shape=jax.ShapeDtypeStruct((n, bsz, width), F32),
        compiler_params=_cparams(("parallel", "parallel")),
        name="ada_mods",
    )(c, w, b.reshape(n, 1, width))


def _proj_kernel(h_ref, mod_ref, g_ref, w_ref, o1_ref, o2_ref, o3_ref, us, us4, up4, up16, *, wcols, out_scale):
    d = D_MODEL
    tm = h_ref.shape[0]
    d1 = DILATIONS[1]
    q4, q16 = tm // d1, tm // DILATIONS[2]
    cw = o1_ref.shape[1]
    u = _normmod(h_ref[...], g_ref[...], mod_ref[:, :d], mod_ref[:, d:2 * d])
    for l in range(d // LANES):
        us[l] = u[:, l * LANES:(l + 1) * LANES]
    ub = u.astype(BF16)
    for n in range(cw // d):
        cols = slice(n * d, (n + 1) * d)
        res = jnp.dot(ub, w_ref[:, wcols[0][n]:wcols[0][n] + d], preferred_element_type=F32)
        o1_ref[:, cols] = (res * out_scale).astype(o1_ref.dtype)
    for l in range(d // LANES):
        for a in range(d1):
            part = us[l, pl.ds(a, q4, stride=d1), :]
            us4[l, a * q4:(a + 1) * q4, :] = part
            up4[a * q4:(a + 1) * q4, l * LANES:(l + 1) * LANES] = part.astype(BF16)
    for n in range(cw // d):
        res = jnp.dot(up4[...], w_ref[:, wcols[1][n]:wcols[1][n] + d], preferred_element_type=F32)
        for a in range(d1):
            o2_ref[a, :, n * d:(n + 1) * d] = (res[a * q4:(a + 1) * q4, :] * out_scale).astype(o2_ref.dtype)
    for l in range(d // LANES):
        for a1 in range(d1):
            for a2 in range(d1):
                r = d1 * a2 + a1
                up16[r * q16:(r + 1) * q16, l * LANES:(l + 1) * LANES] = us4[
                    l, pl.ds(a1 * q4 + a2, q16, stride=d1), :].astype(BF16)
    for n in range(cw // d):
        res = jnp.dot(up16[...], w_ref[:, wcols[2][n]:wcols[2][n] + d], preferred_element_type=F32)
        for r in range(DILATIONS[2]):
            o3_ref[r, :, n * d:(n + 1) * d] = (res[r * q16:(r + 1) * q16, :] * out_scale).astype(o3_ref.dtype)


def _norm_proj(h, mod, g, w, wcols, out_scale):
    bsz, seq, d = h.shape
    tm = PROJ_TILE
    cw = len(wcols[0]) * d
    per_b = seq // tm
    d1, d2 = DILATIONS[1], DILATIONS[2]
    o1, o2, o3 = pl.pallas_call(
        functools.partial(_proj_kernel, wcols=wcols, out_scale=out_scale),
        grid=(bsz * per_b,),
        in_specs=[
            pl.BlockSpec((None, tm, d), lambda i: (i // per_b, i % per_b, 0)),
            pl.BlockSpec((None, 1, mod.shape[-1]), lambda i: (i // per_b, 0, 0)),
            pl.BlockSpec((1, d), lambda i: (0, 0)),
            _layer_spec(*w),
        ],
        out_specs=[
            pl.BlockSpec((None, tm, cw), lambda i: (i // per_b, i % per_b, 0)),
            pl.BlockSpec((None, d1, None, tm // d1, cw), lambda i: (i // per_b, 0, i % per_b, 0, 0)),
            pl.BlockSpec((None, d2, None, tm // d2, cw), lambda i: (i // per_b, 0, i % per_b, 0, 0)),
        ],
        out_shape=[
            jax.ShapeDtypeStruct((bsz, seq, cw), BF16),
            jax.ShapeDtypeStruct((bsz, d1, per_b, tm // d1, cw), BF16),
            jax.ShapeDtypeStruct((bsz, d2, per_b, tm // d2, cw), BF16),
        ],
        scratch_shapes=[pltpu.VMEM((d // LANES, tm, LANES), F32), pltpu.VMEM((d // LANES, tm, LANES), F32),
                        pltpu.VMEM((tm, d), BF16), pltpu.VMEM((tm, d), BF16)],
        compiler_params=_cparams(("parallel",)),
        name="norm_proj",
    )(h, mod, g.reshape(1, d), w[0])
    return o1, o2.reshape(bsz, seq, cw), o3.reshape(bsz, seq, cw)


def _attn_group(blocks):
    blk = ATT_BLOCK
    lane = lax.broadcasted_iota(jnp.int32, (blk, LANES), 1)
    lo = lane < HEAD_DIM
    scores = []
    for qb, kcat, _, mask, _ in blocks:
        zero = jnp.zeros_like(qb)
        q2 = jnp.concatenate([jnp.where(lo, qb, zero), jnp.where(lo, zero, qb)], axis=0)
        s = lax.dot_general(q2, kcat, NT_DIMS, preferred_element_type=F32)
        scores.append(s + mask[...])
    soft = []
    for s, (_, _, _, _, old) in zip(scores, blocks):
        ms, ps = [], []
        for hh in range(2):
            tiles = [s[hh * blk:(hh + 1) * blk, t * LANES:(t + 1) * LANES] for t in range(s.shape[1] // LANES)]
            mx = tiles[0]
            for t in tiles[1:]:
                mx = jnp.maximum(mx, t)
            m = jnp.broadcast_to(jnp.max(mx, axis=-1, keepdims=True), (blk, LANES))
            if old is not None:
                m = jnp.maximum(m, old[0][hh])
            ms.append(m)
            ps.append(jnp.concatenate([jnp.exp2(t - m).astype(BF16) for t in tiles], axis=1))
        alpha = None
        if old is not None:
            alpha = jnp.exp2(jnp.where(lo, old[0][0], old[0][1]) - jnp.where(lo, ms[0], ms[1]))
        soft.append((ms, jnp.concatenate(ps, axis=0), alpha))
    outs = []
    for (ms, p, alpha), (_, _, vcat, _, old) in zip(soft, blocks):
        pv = jnp.dot(p, vcat, preferred_element_type=F32)
        acc = jnp.where(lo, pv[:blk, :LANES], pv[blk:, :LANES])
        den = jnp.where(lo, pv[:blk, LANES:], pv[blk:, LANES:])
        if old is not None:
            acc = alpha * old[1] + acc
            den = alpha * old[0][2] + den
        outs.append((ms + [den], acc))
    return outs


def _attn_kernel(q1, qd2, qd3, k1, kd2, kd3, v1, v2, v3, o_ref, tmp_nat, vd1, vd2, vd3,
                 acc_ref, st_ref, acc2_ref, st2_ref, mask_pc, mask_c):
    blk = ATT_BLOCK
    seq = q1.shape[0]
    nblk = seq // blk
    d1, d2 = DILATIONS[1], DILATIONS[2]
    quarter = seq // d1
    per_res = quarter // blk
    assert d2 == d1 * d1 and seq == d2 * blk and nblk % ATT_GROUP == 0 and ATT_GROUP % per_res == 0
    @pl.when(jnp.logical_and(pl.program_id(0) == 0, pl.program_id(1) == 0))
    def _():
        for vd in (vd1, vd2, vd3):
            vd[:, LANES:] = jnp.ones((seq, LANES), BF16)
        qq = lax.broadcasted_iota(jnp.int32, (2 * blk, 2 * blk), 0) & (blk - 1)
        kk = lax.broadcasted_iota(jnp.int32, (2 * blk, 2 * blk), 1)
        valid = jnp.logical_or(jnp.logical_and(kk < blk, kk >= qq), jnp.logical_and(kk >= blk, kk - blk <= qq))
        mask_pc[...] = jnp.where(valid, 0.0, NEG)
        valid_c = (lax.broadcasted_iota(jnp.int32, (2 * blk, blk), 1)
                   <= (lax.broadcasted_iota(jnp.int32, (2 * blk, blk), 0) & (blk - 1)))
        mask_c[...] = jnp.where(valid_c, 0.0, NEG)

    for v, vd in ((v1, vd1), (v2, vd2), (v3, vd3)):
        vd[:, :LANES] = v[...]

    def run_group(blocks, dests):
        for (st_dst, acc_dst, rows), (st, acc) in zip(dests, _attn_group(blocks)):
            for k in range(N_STATS):
                st_dst[k, rows, :] = st[k]
            acc_dst[rows, :] = acc

    def load_old(st_src, acc_src, rows):
        return [st_src[k, rows, :] for k in range(N_STATS)], acc_src[rows, :]

    def b0_group(ns, first):
        blocks, dests = [], []
        for n in ns:
            if first and n == 0:
                rows = pl.ds(0, blk)
                blocks.append((q1[rows, :], k1[rows, :], vd1[rows, :], mask_c, None))
            else:
                rows = pl.ds(pl.multiple_of(n * blk, blk), blk)
                krows = pl.ds(pl.multiple_of((n - 1) * blk, blk), 2 * blk)
                blocks.append((q1[rows, :], k1[krows, :], vd1[krows, :], mask_pc, None))
            dests.append((st_ref, acc_ref, rows))
        run_group(blocks, dests)

    b0_group(list(range(ATT_GROUP)), True)

    def b0_body(i, carry):
        b0_group([ATT_GROUP * i + j for j in range(ATT_GROUP)], False)
        return carry

    lax.fori_loop(1, nblk // ATT_GROUP, b0_body, 0)

    res_per_trip = ATT_GROUP // per_res

    def b1_body(i, carry):
        blocks, dests = [], []
        for jr in range(res_per_trip):
            a1 = res_per_trip * i + jr
            base = a1 * quarter
            for n in range(per_res):
                rows = pl.ds(a1 + d1 * blk * n, blk, stride=d1)
                qrows = pl.ds(pl.multiple_of(base + n * blk, blk), blk)
                if n == 0:
                    krows, mask = qrows, mask_c
                else:
                    krows, mask = pl.ds(pl.multiple_of(base + (n - 1) * blk, blk), 2 * blk), mask_pc
                blocks.append((qd2[qrows, :], kd2[krows, :], vd2[krows, :], mask,
                               load_old(st_ref, acc_ref, rows)))
                dests.append((st2_ref, acc2_ref, qrows))
        run_group(blocks, dests)
        return carry

    lax.fori_loop(0, d1 // res_per_trip, b1_body, 0)

    def b2_body(i, carry):
        blocks, dests = [], []
        for j in range(ATT_GROUP):
            a1, a2 = j % d1, (ATT_GROUP // d1) * i + j // d1
            qrows = pl.ds(pl.multiple_of((ATT_GROUP * i + j) * blk, blk), blk)
            rows = pl.ds(a1 * quarter + a2, blk, stride=d1)
            blocks.append((qd3[qrows, :], kd3[qrows, :], vd3[qrows, :], mask_c,
                           load_old(st2_ref, acc2_ref, rows)))
            dests.append((st2_ref, acc2_ref, rows))
        run_group(blocks, dests)
        return carry

    lax.fori_loop(0, d2 // ATT_GROUP, b2_body, 0)

    for a in range(d1):
        seg = slice(a * quarter, (a + 1) * quarter)
        tmp_nat[pl.ds(a, quarter, stride=d1), :] = acc2_ref[seg, :] / st2_ref[N_STATS - 1, seg, :]
    o_ref[...] = tmp_nat[...].astype(o_ref.dtype)


def _attention(qs, kvs):
    bsz, seq, _ = qs[0].shape
    pairs = D_MODEL // LANES

    def slab(col0):
        return pl.BlockSpec((None, seq, LANES), lambda b, hp: (b, 0, col0 + hp))

    in_specs = [slab(0)] * N_BRANCHES + [slab(0)] * N_BRANCHES + [slab(pairs)] * N_BRANCHES
    return pl.pallas_call(
        _attn_kernel,
        grid=(bsz, pairs),
        in_specs=in_specs,
        out_specs=slab(0),
        out_shape=jax.ShapeDtypeStruct((bsz, seq, D_MODEL), BF16),
        scratch_shapes=([pltpu.VMEM((seq, LANES), F32)] + [pltpu.VMEM((seq, 2 * LANES), BF16)] * 3
                        + [pltpu.VMEM((seq, LANES), F32), pltpu.VMEM((N_STATS, seq, LANES), F32)] * 2
                        + [pltpu.VMEM((2 * ATT_BLOCK, 2 * ATT_BLOCK), F32), pltpu.VMEM((2 * ATT_BLOCK, ATT_BLOCK), F32)]),
        compiler_params=_cparams(("arbitrary", "arbitrary")),
        name="attention",
    )(*qs, *kvs, *kvs)


def _regroup8(tiles):
    t = list(tiles)
    lane = lax.broadcasted_iota(jnp.int32, t[0].shape, 1)
    piece = lane // SSM_GROUP
    for dist in (4, 2, 1):
        bit = (piece & dist) != 0
        shift = dist * SSM_GROUP
        for a in range(8):
            if a & dist:
                continue
            b = a | dist
            ta, tb = t[a], t[b]
            t[a] = jnp.where(bit, pltpu.roll(tb, shift, 1), ta)
            t[b] = jnp.where(bit, tb, pltpu.roll(ta, LANES - shift, 1))
    return t


def _s5_tile_perm(bsz):
    tt = S5_TILE_TOKENS
    n = bsz * tt
    perm = np.zeros((n, n), np.float32)
    for b in range(bsz):
        for c in range(tt // CHUNK):
            for t in range(CHUNK):
                perm[(t * (tt // CHUNK) + c) * bsz + b, b * tt + c * CHUNK + t] = 1.0
    return perm


def _s5_pre_kernel(h_ref, mod_ref, g_ref, perm_ref, z_ref, u_scr, up_scr):
    d = D_MODEL
    bsz, tt, _ = h_ref.shape
    g = g_ref[...]
    for b in range(bsz):
        u_scr[b * tt:(b + 1) * tt, :] = _normmod(
            h_ref[b], g, mod_ref[b, :, :d], mod_ref[b, :, d:2 * d]).astype(BF16)
    up_scr[...] = jnp.dot(perm_ref[...], u_scr[...], preferred_element_type=F32)
    rows = z_ref.shape[0]
    for j in range(d // LANES):
        for half in range(CHUNK // 8):
            tiles = [up_scr[(8 * half + tl) * rows:(8 * half + tl + 1) * rows, j * LANES:(j + 1) * LANES]
                     for tl in range(8)]
            outs = _regroup8(tiles)
            for gl in range(8):
                col = (8 * j + gl) * GROUP_COLS + half * LANES
                z_ref[:, col:col + LANES] = outs[gl].astype(z_ref.dtype)


def _s5_pre(h, mod, g, perm):
    bsz, seq, d = h.shape
    tt = S5_TILE_TOKENS
    rows = bsz * tt // CHUNK
    return pl.pallas_call(
        _s5_pre_kernel,
        grid=(seq // tt,),
        in_specs=[
            pl.BlockSpec((bsz, tt, d), lambda i: (0, i, 0)),
            pl.BlockSpec(mod.shape, lambda i: (0, 0, 0)),
            pl.BlockSpec((1, d), lambda i: (0, 0)),
            pl.BlockSpec(perm.shape, lambda i: (0, 0)),
        ],
        out_specs=pl.BlockSpec((rows, CHUNK * d), lambda i: (i, 0)),
        out_shape=jax.ShapeDtypeStruct((bsz * seq // CHUNK, CHUNK * d), BF16),
        scratch_shapes=[pltpu.VMEM((bsz * tt, d), BF16), pltpu.VMEM((bsz * tt, d), F32)],
        compiler_params=_cparams(("parallel",)),
        name="s5_pre",
    )(h, mod, g.reshape(1, d), perm)


def _s5_core_kernel(z_ref, win_ref, wt_ref, wx_ref, dec_ref, d_ref, o_ref, s_scr, x_scr, *, bsz):
    gc = GROUP_COLS
    nchunk = z_ref.shape[0] // bsz
    s_scr[...] = (jnp.dot(z_ref[:, :gc], win_ref[0], preferred_element_type=F32)
                  + jnp.dot(z_ref[:, gc:], win_ref[1], preferred_element_type=F32))
    ar, ai = dec_ref[:, :LANES], dec_ref[:, LANES:]

    def step(c, carry):
        xr, xi = carry
        rows = pl.ds(pl.multiple_of(c * bsz, bsz), bsz)
        x_scr[rows, :LANES] = xr
        x_scr[rows, LANES:] = xi
        return (ar * xr - ai * xi + s_scr[rows, :LANES], ar * xi + ai * xr + s_scr[rows, LANES:])

    zero = jnp.zeros((bsz, LANES), F32)
    lax.fori_loop(0, nchunk, step, (zero, zero))
    x = x_scr[...].astype(BF16)
    for gi in range(2):
        z = z_ref[:, gi * gc:(gi + 1) * gc]
        y = (jnp.dot(z, wt_ref[gi], preferred_element_type=F32)
             + jnp.dot(x, wx_ref[gi], preferred_element_type=F32))
        y = y + d_ref[:, gi * gc:(gi + 1) * gc] * z.astype(F32)
        o_ref[:, gi * gc:(gi + 1) * gc] = jax.nn.gelu(y).astype(o_ref.dtype)


def _s5_core(z, tables, layer, bsz):
    nrow, width = z.shape
    w_in, w_toep, w_x, dec, d_perm = tables
    wspec = pl.BlockSpec((None, 2, GROUP_COLS, GROUP_COLS), lambda k: (layer, k, 0, 0))
    return pl.pallas_call(
        functools.partial(_s5_core_kernel, bsz=bsz),
        grid=(N_PAIRS,),
        in_specs=[
            pl.BlockSpec((nrow, PAIR_COLS), lambda k: (0, k)),
            wspec, wspec, wspec,
            pl.BlockSpec((None, 1, GROUP_COLS), lambda k: (layer, 0, k)),
            pl.BlockSpec((None, 1, PAIR_COLS), lambda k: (layer, 0, k)),
        ],
        out_specs=pl.BlockSpec((nrow, PAIR_COLS), lambda k: (0, k)),
        out_shape=jax.ShapeDtypeStruct((nrow, width), BF16),
        scratch_shapes=[pltpu.VMEM((nrow, GROUP_COLS), F32), pltpu.VMEM((nrow, GROUP_COLS), F32)],
        compiler_params=_cparams(("parallel",)),
        name="s5_core",
    )(z, w_in, w_toep, w_x, dec, d_perm)


def _s5_weights(lam_re, lam_im, log_dt, b_re, b_im, c_re, c_im, d_skip):
    g, p, c16 = SSM_GROUPS, SSM_STATE, SSM_GROUP
    lam = lax.complex(lam_re.astype(F32), lam_im.astype(F32))
    dt = jnp.exp(log_dt.astype(F32))[:, None]
    steps = jnp.arange(CHUNK + 1, dtype=F32)
    apow = jnp.exp((lam * dt)[None] * steps[:, None, None])
    a = apow[1]
    bbar = ((a - 1.0) / lam)[..., None] * lax.complex(b_re.astype(F32), b_im.astype(F32))
    cmat = lax.complex(c_re.astype(F32), c_im.astype(F32))
    odd = (jnp.arange(g) % 2)[:, None, None]

    win = apow[CHUNK - 1::-1][..., None] * bbar[None]
    win = jnp.transpose(win, (1, 0, 3, 2)).reshape(g, GROUP_COLS, p)
    zeros = jnp.zeros_like(win.real)
    w_in = jnp.concatenate([
        jnp.where(odd == 0, win.real, zeros), jnp.where(odd == 1, win.real, zeros),
        jnp.where(odd == 0, win.imag, zeros), jnp.where(odd == 1, win.imag, zeros)], axis=-1)

    kern = jnp.einsum('gop,kgp,gpi->kgoi', cmat, apow[:CHUNK], bbar, precision=HIGHEST).real
    idx = jnp.arange(CHUNK)
    onehot = (idx[None, None, :] - idx[None, :, None] == idx[:, None, None]).astype(F32)
    w_toep = jnp.einsum('kst,kgoi->gsito', onehot, kern, precision=HIGHEST)
    w_toep = w_toep.reshape(g, GROUP_COLS, GROUP_COLS)

    cw = cmat[None] * apow[1:, :, None, :]
    cw = jnp.transpose(cw, (1, 3, 0, 2)).reshape(g, p, GROUP_COLS)
    zx = jnp.zeros_like(cw.real)
    w_x = jnp.concatenate([
        jnp.where(odd == 0, cw.real, zx), jnp.where(odd == 1, cw.real, zx),
        jnp.where(odd == 0, -cw.imag, zx), jnp.where(odd == 1, -cw.imag, zx)], axis=1)

    a16 = apow[CHUNK].reshape(N_PAIRS, 2 * p)
    dec = jnp.concatenate([a16.real, a16.imag], axis=-1).reshape(1, N_PAIRS * GROUP_COLS)
    d_perm = jnp.broadcast_to(d_skip.astype(F32).reshape(g, 1, c16), (g, CHUNK, c16)).reshape(1, g * GROUP_COLS)
    return w_in.astype(BF16), w_toep.astype(BF16), w_x.astype(BF16), dec, d_perm


def _mlp(u, w1_ref, w2_ref):
    d = D_MODEL
    acc = None
    for k in range(D_FF // d):
        a = jnp.dot(u, w1_ref[:, k * d:(k + 1) * d], preferred_element_type=F32)
        a = jnp.square(jnp.maximum(a, 0.0)).astype(BF16)
        part = jnp.dot(a, w2_ref[k * d:(k + 1) * d, :], preferred_element_type=F32)
        acc = part if acc is None else acc + part
    return acc


def _final_norm(h, fg_ref):
    ms = jnp.mean(h * h, axis=-1, keepdims=True)
    return h * lax.rsqrt(ms + EPS) * fg_ref[...]


def _post_attn_kernel(*refs, final):
    if final:
        h_ref, y_ref, moda_ref, modm_ref, g_ref, wp_ref, w1_ref, w2_ref, fg_ref, o_ref = refs
    else:
        h_ref, y_ref, moda_ref, modm_ref, g_ref, wp_ref, w1_ref, w2_ref, o_ref = refs
    d = D_MODEL
    ymix = jnp.dot(y_ref[...], wp_ref[...], preferred_element_type=F32)
    h1 = h_ref[...] + moda_ref[:, 2 * d:3 * d] * ymix
    u = _normmod(h1, g_ref[...], modm_ref[:, :d], modm_ref[:, d:2 * d]).astype(BF16)
    h2 = h1 + modm_ref[:, 2 * d:3 * d] * _mlp(u, w1_ref, w2_ref)
    o_ref[...] = _final_norm(h2, fg_ref) if final else h2


def _post_attn(h, o, moda, modm, g2, w_o, w1, w2, final_g, tm=MLP_ROWS):
    bsz, seq, d = h.shape
    per_b = seq // tm
    row_spec = pl.BlockSpec((None, tm, d), lambda i: (i // per_b, i % per_b, 0))
    final = final_g is not None
    in_specs = [
        row_spec, row_spec,
        pl.BlockSpec((None, 1, moda.shape[-1]), lambda i: (i // per_b, 0, 0)),
        pl.BlockSpec((None, 1, modm.shape[-1]), lambda i: (i // per_b, 0, 0)),
        pl.BlockSpec((1, d), lambda i: (0, 0)),
        _layer_spec(*w_o), _layer_spec(*w1), _layer_spec(*w2),
    ]
    args = [h, o, moda, modm, g2.reshape(1, d), w_o[0], w1[0], w2[0]]
    if final:
        in_specs.append(pl.BlockSpec((1, d), lambda i: (0, 0)))
        args.append(final_g.reshape(1, d))
    return pl.pallas_call(
        functools.partial(_post_attn_kernel, final=final),
        grid=(bsz * per_b,),
        in_specs=in_specs,
        out_specs=row_spec,
        out_shape=jax.ShapeDtypeStruct((bsz, seq, d), F32),
        compiler_params=_cparams(("parallel",)),
        name="post_attn_mlp",
    )(*args)


def _post_s5_kernel(h_ref, z_ref, moda_ref, modm_ref, g_ref, permt_ref, wp_ref, w1_ref, w2_ref, o_ref,
                    z_scr, h1_scr, u_scr):
    d = D_MODEL
    bsz, tt, _ = h_ref.shape
    rows = z_ref.shape[0]
    for j in range(d // LANES):
        for half in range(CHUNK // 8):
            tiles = []
            for gl in range(8):
                col = (8 * j + gl) * GROUP_COLS + half * LANES
                tiles.append(z_ref[:, col:col + LANES].astype(F32))
            outs = _regroup8(tiles)
            for tl in range(8):
                t = 8 * half + tl
                z_scr[t * rows:(t + 1) * rows, j * LANES:(j + 1) * LANES] = outs[tl].astype(BF16)
    zn = jnp.dot(permt_ref[...], z_scr[...], preferred_element_type=F32).astype(BF16)
    zz = jnp.dot(zn, wp_ref[...], preferred_element_type=F32)
    ymix = zz[:, :d] * jax.nn.sigmoid(zz[:, d:])
    g = g_ref[...]
    for b in range(bsz):
        sl = slice(b * tt, (b + 1) * tt)
        h1 = h_ref[b] + moda_ref[b, :, 2 * d:3 * d] * ymix[sl, :]
        h1_scr[sl, :] = h1
        u_scr[sl, :] = _normmod(h1, g, modm_ref[b, :, :d], modm_ref[b, :, d:2 * d]).astype(BF16)
    acc = _mlp(u_scr[...], w1_ref, w2_ref)
    for b in range(bsz):
        sl = slice(b * tt, (b + 1) * tt)
        o_ref[b] = h1_scr[sl, :] + modm_ref[b, :, 2 * d:3 * d] * acc[sl, :]


def _post_s5(h, zo, moda, modm, g2, permt, w_glu, w1, w2):
    bsz, seq, d = h.shape
    tt = S5_TILE_TOKENS
    rows = bsz * tt // CHUNK
    h_spec = pl.BlockSpec((bsz, tt, d), lambda i: (0, i, 0))
    return pl.pallas_call(
        _post_s5_kernel,
        grid=(seq // tt,),
        in_specs=[
            h_spec,
            pl.BlockSpec((rows, CHUNK * d), lambda i: (i, 0)),
            pl.BlockSpec(moda.shape, lambda i: (0, 0, 0)),
            pl.BlockSpec(modm.shape, lambda i: (0, 0, 0)),
            pl.BlockSpec((1, d), lambda i: (0, 0)),
            pl.BlockSpec(permt.shape, lambda i: (0, 0)),
            _layer_spec(*w_glu), _layer_spec(*w1), _layer_spec(*w2),
        ],
        out_specs=h_spec,
        out_shape=jax.ShapeDtypeStruct((bsz, seq, d), F32),
        scratch_shapes=[pltpu.VMEM((bsz * tt, d), BF16), pltpu.VMEM((bsz * tt, d), F32),
                        pltpu.VMEM((bsz * tt, d), BF16)],
        compiler_params=_cparams(("parallel",)),
        name="post_s5_mlp",
    )(h, zo, moda, modm, g2.reshape(1, d), permt, w_glu[0], w1[0], w2[0])


def kernel(x, c, ln_g, ada_w, ada_b, ssm_lam_re, ssm_lam_im, ssm_log_dt, ssm_b_re, ssm_b_im, ssm_c_re, ssm_c_im, ssm_d, ssm_w_glu, kv_g, kv_ada_w, kv_ada_b, w_kv, attn_w_q, attn_w_o, mlp_w1, mlp_w2, final_g):
    bsz, seq, d = x.shape
    depth = ln_g.shape[0]
    n_s5 = ssm_lam_re.shape[0]
    assert d == D_MODEL and seq % (DILATIONS[-1] * ATT_BLOCK) == 0

    mods = _ada_mods(c, ada_w.reshape(depth * 2, d, 3 * d), ada_b.reshape(depth * 2, 3 * d))
    mods = mods.reshape(depth, 2, bsz, 1, 3 * d)
    kv_mod = _ada_mods(c, kv_ada_w[None], kv_ada_b[None]).reshape(bsz, 1, 2 * d)

    w1 = mlp_w1.astype(BF16)
    w2 = mlp_w2.astype(BF16)
    w_glu = ssm_w_glu.astype(BF16)
    w_q = attn_w_q.astype(BF16)
    w_o = attn_w_o.astype(BF16)
    perm_np = _s5_tile_perm(bsz)
    perm = jnp.asarray(perm_np, BF16)
    permt = jnp.asarray(perm_np.T, BF16)

    tables = jax.vmap(_s5_weights)(ssm_lam_re, ssm_lam_im, ssm_log_dt, ssm_b_re, ssm_b_im,
                                   ssm_c_re, ssm_c_im, ssm_d)
    h = x
    for layer in range(n_s5):
        z = _s5_pre(h, mods[layer, 0], ln_g[layer, 0], perm)
        zo = _s5_core(z, tables, layer, bsz)
        h = _post_s5(h, zo, mods[layer, 0], mods[layer, 1], ln_g[layer, 1], permt,
                     (w_glu, layer), (w1, layer), (w2, layer))

    kv_cols = tuple((i * d, (N_BRANCHES + i) * d) for i in range(N_BRANCHES))
    q_cols = tuple((i * d,) for i in range(N_BRANCHES))
    kvs = _norm_proj(h, kv_mod, kv_g, (w_kv.astype(BF16)[None], 0), kv_cols, 1.0)
    for layer in range(n_s5, depth):
        j = layer - n_s5
        qs = _norm_proj(h, mods[layer, 0], ln_g[layer, 0], (w_q, j), q_cols, Q_SCALE)
        o = _attention(qs, kvs)
        h = _post_attn(h, o, mods[layer, 0], mods[layer, 1], ln_g[layer, 1],
                       (w_o, j), (w1, layer), (w2, layer),
                       final_g if layer == depth - 1 else None)
    return h
```

```python
import functools
import math

import numpy as np

import jax
import jax.numpy as jnp
from jax import lax
from jax.experimental import pallas as pl
from jax.experimental.pallas import tpu as pltpu

F32 = jnp.float32
BF16 = jnp.bfloat16

D_MODEL = 1024
SSM_GROUP = 16
SSM_GROUPS = D_MODEL // SSM_GROUP
SSM_STATE = 64
HEAD_DIM = 64
N_HEADS = D_MODEL // HEAD_DIM
DILATIONS = (1, 4, 16)
N_BRANCHES = len(DILATIONS)
ATT_BLOCK = 128
ATT_GROUP = 16
N_STATS = 3
Q_SCALE = HEAD_DIM ** -0.5 * math.log2(math.e)
D_FF = 4 * D_MODEL
EPS = 1e-6
NEG = -1e30

LANES = 128
CHUNK = 16
GROUP_COLS = CHUNK * SSM_GROUP
PAIR_COLS = 2 * GROUP_COLS
N_PAIRS = SSM_GROUPS // 2
S5_TILE_TOKENS = 2 * CHUNK
MLP_ROWS = 512
PROJ_TILE = 512
VMEM_LIMIT = 56 * 1024 * 1024

HIGHEST = lax.Precision.HIGHEST
NT_DIMS = (((1,), (1,)), ((), ()))


def _cparams(sem):
    return pltpu.CompilerParams(dimension_semantics=sem, vmem_limit_bytes=VMEM_LIMIT)


def _layer_spec(stack, layer):
    zeros = (0,) * (stack.ndim - 1)
    return pl.BlockSpec((None,) + stack.shape[1:], lambda i: (layer,) + zeros, pipeline_mode=pl.Buffered(1))


def _normmod(x, g, shift, scale):
    ms = jnp.mean(x * x, axis=-1, keepdims=True)
    return (x * lax.rsqrt(ms + EPS) * g) * (1.0 + scale) + shift


def _ada_kernel(c_ref, w_ref, b_ref, o_ref):
    c = c_ref[...]
    sc = (c * jax.nn.sigmoid(c)).astype(BF16)
    o_ref[...] = jnp.dot(sc, w_ref[...].astype(BF16), preferred_element_type=F32) + b_ref[...]


def _ada_mods(c, w, b):
    n, d, width = w.shape
    bsz = c.shape[0]
    tn = width
    return pl.pallas_call(
        _ada_kernel,
        grid=(n, width // tn),
        in_specs=[
            pl.BlockSpec((bsz, d), lambda i, j: (0, 0)),
            pl.BlockSpec((None, d, tn), lambda i, j: (i, 0, j)),
            pl.BlockSpec((None, 1, tn), lambda i, j: (i, 0, j)),
        ],
        out_specs=pl.BlockSpec((None, bsz, tn), lambda i, j: (i, 0, j)),
        out_shape=jax.ShapeDtypeStruct((n, bsz, width), F32),
        compiler_params=_cparams(("parallel", "parallel")),
        name="ada_mods",
    )(c, w, b.reshape(n, 1, width))


def _proj_kernel(h_ref, mod_ref, g_ref, w_ref, o1_ref, o2_ref, o3_ref, us, us4, up4, up16, *, wcols, out_scale):
    d = D_MODEL
    tm = h_ref.shape[0]
    d1 = DILATIONS[1]
    q4, q16 = tm // d1, tm // DILATIONS[2]
    cw = o1_ref.shape[1]
    u = _normmod(h_ref[...], g_ref[...], mod_ref[:, :d], mod_ref[:, d:2 * d])
    for l in range(d // LANES):
        us[l] = u[:, l * LANES:(l + 1) * LANES]
    ub = u.astype(BF16)
    for n in range(cw // d):
        cols = slice(n * d, (n + 1) * d)
        res = jnp.dot(ub, w_ref[:, wcols[0][n]:wcols[0][n] + d], preferred_element_type=F32)
        o1_ref[:, cols] = (res * out_scale).astype(o1_ref.dtype)
    for l in range(d // LANES):
        for a in range(d1):
            part = us[l, pl.ds(a, q4, stride=d1), :]
            us4[l, a * q4:(a + 1) * q4, :] = part
            up4[a * q4:(a + 1) * q4, l * LANES:(l + 1) * LANES] = part.astype(BF16)
    for n in range(cw // d):
        res = jnp.dot(up4[...], w_ref[:, wcols[1][n]:wcols[1][n] + d], preferred_element_type=F32)
        for a in range(d1):
            o2_ref[a, :, n * d:(n + 1) * d] = (res[a * q4:(a + 1) * q4, :] * out_scale).astype(o2_ref.dtype)
    for l in range(d // LANES):
        for a1 in range(d1):
            for a2 in range(d1):
                r = d1 * a2 + a1
                up16[r * q16:(r + 1) * q16, l * LANES:(l + 1) * LANES] = us4[
                    l, pl.ds(a1 * q4 + a2, q16, stride=d1), :].astype(BF16)
    for n in range(cw // d):
        res = jnp.dot(up16[...], w_ref[:, wcols[2][n]:wcols[2][n] + d], preferred_element_type=F32)
        for r in range(DILATIONS[2]):
            o3_ref[r, :, n * d:(n + 1) * d] = (res[r * q16:(r + 1) * q16, :] * out_scale).astype(o3_ref.dtype)


def _norm_proj(h, mod, g, w, wcols, out_scale):
    bsz, seq, d = h.shape
    tm = PROJ_TILE
    cw = len(wcols[0]) * d
    per_b = seq // tm
    d1, d2 = DILATIONS[1], DILATIONS[2]
    o1, o2, o3 = pl.pallas_call(
        functools.partial(_proj_kernel, wcols=wcols, out_scale=out_scale),
        grid=(bsz * per_b,),
        in_specs=[
            pl.BlockSpec((None, tm, d), lambda i: (i // per_b, i % per_b, 0)),
            pl.BlockSpec((None, 1, mod.shape[-1]), lambda i: (i // per_b, 0, 0)),
            pl.BlockSpec((1, d), lambda i: (0, 0)),
            _layer_spec(*w),
        ],
        out_specs=[
            pl.BlockSpec((None, tm, cw), lambda i: (i // per_b, i % per_b, 0)),
            pl.BlockSpec((None, d1, None, tm // d1, cw), lambda i: (i // per_b, 0, i % per_b, 0, 0)),
            pl.BlockSpec((None, d2, None, tm // d2, cw), lambda i: (i // per_b, 0, i % per_b, 0, 0)),
        ],
        out_shape=[
            jax.ShapeDtypeStruct((bsz, seq, cw), BF16),
            jax.ShapeDtypeStruct((bsz, d1, per_b, tm // d1, cw), BF16),
            jax.ShapeDtypeStruct((bsz, d2, per_b, tm // d2, cw), BF16),
        ],
        scratch_shapes=[pltpu.VMEM((d // LANES, tm, LANES), F32), pltpu.VMEM((d // LANES, tm, LANES), F32),
                        pltpu.VMEM((tm, d), BF16), pltpu.VMEM((tm, d), BF16)],
        compiler_params=_cparams(("parallel",)),
        name="norm_proj",
    )(h, mod, g.reshape(1, d), w[0])
    return o1, o2.reshape(bsz, seq, cw), o3.reshape(bsz, seq, cw)


def _attn_group(blocks):
    blk = ATT_BLOCK
    lane = lax.broadcasted_iota(jnp.int32, (blk, LANES), 1)
    lo = lane < HEAD_DIM
    scores = []
    for qb, kcat, _, mask, _ in blocks:
        zero = jnp.zeros_like(qb)
        q2 = jnp.concatenate([jnp.where(lo, qb, zero), jnp.where(lo, zero, qb)], axis=0)
        s = lax.dot_general(q2, kcat, NT_DIMS, preferred_element_type=F32)
        scores.append(s + mask[...])
    soft = []
    for s, (_, _, _, _, old) in zip(scores, blocks):
        ms, ps = [], []
        for hh in range(2):
            tiles = [s[hh * blk:(hh + 1) * blk, t * LANES:(t + 1) * LANES] for t in range(s.shape[1] // LANES)]
            mx = tiles[0]
            for t in tiles[1:]:
                mx = jnp.maximum(mx, t)
            m = jnp.broadcast_to(jnp.max(mx, axis=-1, keepdims=True), (blk, LANES))
            if old is not None:
                m = jnp.maximum(m, old[0][hh])
            ms.append(m)
            ps.append(jnp.concatenate([jnp.exp2(t - m).astype(BF16) for t in tiles], axis=1))
        alpha = None
        if old is not None:
            alpha = jnp.exp2(jnp.where(lo, old[0][0], old[0][1]) - jnp.where(lo, ms[0], ms[1]))
        soft.append((ms, jnp.concatenate(ps, axis=0), alpha))
    outs = []
    for (ms, p, alpha), (_, _, vcat, _, old) in zip(soft, blocks):
        pv = jnp.dot(p, vcat, preferred_element_type=F32)
        acc = jnp.where(lo, pv[:blk, :LANES], pv[blk:, :LANES])
        den = jnp.where(lo, pv[:blk, LANES:], pv[blk:, LANES:])
        if old is not None:
            acc = alpha * old[1] + acc
            den = alpha * old[0][2] + den
        outs.append((ms + [den], acc))
    return outs


def _attn_kernel(q1, qd2, qd3, k1, kd2, kd3, v1, v2, v3, o_ref, tmp_nat, vd1, vd2, vd3,
                 acc_ref, st_ref, acc2_ref, st2_ref, mask_pc, mask_c):
    blk = ATT_BLOCK
    seq = q1.shape[0]
    nblk = seq // blk
    d1, d2 = DILATIONS[1], DILATIONS[2]
    quarter = seq // d1
    per_res = quarter // blk
    assert d2 == d1 * d1 and seq == d2 * blk and nblk % ATT_GROUP == 0 and ATT_GROUP % per_res == 0
    @pl.when(jnp.logical_and(pl.program_id(0) == 0, pl.program_id(1) == 0))
    def _():
        for vd in (vd1, vd2, vd3):
            vd[:, LANES:] = jnp.ones((seq, LANES), BF16)
        qq = lax.broadcasted_iota(jnp.int32, (2 * blk, 2 * blk), 0) & (blk - 1)
        kk = lax.broadcasted_iota(jnp.int32, (2 * blk, 2 * blk), 1)
        valid = jnp.logical_or(jnp.logical_and(kk < blk, kk >= qq), jnp.logical_and(kk >= blk, kk - blk <= qq))
        mask_pc[...] = jnp.where(valid, 0.0, NEG)
        valid_c = (lax.broadcasted_iota(jnp.int32, (2 * blk, blk), 1)
                   <= (lax.broadcasted_iota(jnp.int32, (2 * blk, blk), 0) & (blk - 1)))
        mask_c[...] = jnp.where(valid_c, 0.0, NEG)

    for v, vd in ((v1, vd1), (v2, vd2), (v3, vd3)):
        vd[:, :LANES] = v[...]

    def run_group(blocks, dests):
        for (st_dst, acc_dst, rows), (st, acc) in zip(dests, _attn_group(blocks)):
            for k in range(N_STATS):
                st_dst[k, rows, :] = st[k]
            acc_dst[rows, :] = acc

    def load_old(st_src, acc_src, rows):
        return [st_src[k, rows, :] for k in range(N_STATS)], acc_src[rows, :]

    def b0_group(ns, first):
        blocks, dests = [], []
        for n in ns:
            if first and n == 0:
                rows = pl.ds(0, blk)
                blocks.append((q1[rows, :], k1[rows, :], vd1[rows, :], mask_c, None))
            else:
                rows = pl.ds(pl.multiple_of(n * blk, blk), blk)
                krows = pl.ds(pl.multiple_of((n - 1) * blk, blk), 2 * blk)
                blocks.append((q1[rows, :], k1[krows, :], vd1[krows, :], mask_pc, None))
            dests.append((st_ref, acc_ref, rows))
        run_group(blocks, dests)

    b0_group(list(range(ATT_GROUP)), True)

    def b0_body(i, carry):
        b0_group([ATT_GROUP * i + j for j in range(ATT_GROUP)], False)
        return carry

    lax.fori_loop(1, nblk // ATT_GROUP, b0_body, 0)

    res_per_trip = ATT_GROUP // per_res

    def b1_body(i, carry):
        blocks, dests = [], []
        for jr in range(res_per_trip):
            a1 = res_per_trip * i + jr
            base = a1 * quarter
            for n in range(per_res):
                rows = pl.ds(a1 + d1 * blk * n, blk, stride=d1)
                qrows = pl.ds(pl.multiple_of(base + n * blk, blk), blk)
                if n == 0:
                    krows, mask = qrows, mask_c
                else:
                    krows, mask = pl.ds(pl.multiple_of(base + (n - 1) * blk, blk), 2 * blk), mask_pc
                blocks.append((qd2[qrows, :], kd2[krows, :], vd2[krows, :], mask,
                               load_old(st_ref, acc_ref, rows)))
                dests.append((st2_ref, acc2_ref, qrows))
        run_group(blocks, dests)
        return carry

    lax.fori_loop(0, d1 // res_per_trip, b1_body, 0)

    def b2_body(i, carry):
        blocks, dests = [], []
        for j in range(ATT_GROUP):
            a1, a2 = j % d1, (ATT_GROUP // d1) * i + j // d1
            qrows = pl.ds(pl.multiple_of((ATT_GROUP * i + j) * blk, blk), blk)
            rows = pl.ds(a1 * quarter + a2, blk, stride=d1)
            blocks.append((qd3[qrows, :], kd3[qrows, :], vd3[qrows, :], mask_c,
                           load_old(st2_ref, acc2_ref, rows)))
            dests.append((st2_ref, acc2_ref, rows))
        run_group(blocks, dests)
        return carry

    lax.fori_loop(0, d2 // ATT_GROUP, b2_body, 0)

    for a in range(d1):
        seg = slice(a * quarter, (a + 1) * quarter)
        tmp_nat[pl.ds(a, quarter, stride=d1), :] = acc2_ref[seg, :] / st2_ref[N_STATS - 1, seg, :]
    o_ref[...] = tmp_nat[...].astype(o_ref.dtype)


def _attention(qs, kvs):
    bsz, seq, _ = qs[0].shape
    pairs = D_MODEL // LANES

    def slab(col0):
        return pl.BlockSpec((None, seq, LANES), lambda b, hp: (b, 0, col0 + hp))

    in_specs = [slab(0)] * N_BRANCHES + [slab(0)] * N_BRANCHES + [slab(pairs)] * N_BRANCHES
    return pl.pallas_call(
        _attn_kernel,
        grid=(bsz, pairs),
        in_specs=in_specs,
        out_specs=slab(0),
        out_shape=jax.ShapeDtypeStruct((bsz, seq, D_MODEL), BF16),
        scratch_shapes=([pltpu.VMEM((seq, LANES), F32)] + [pltpu.VMEM((seq, 2 * LANES), BF16)] * 3
                        + [pltpu.VMEM((seq, LANES), F32), pltpu.VMEM((N_STATS, seq, LANES), F32)] * 2
                        + [pltpu.VMEM((2 * ATT_BLOCK, 2 * ATT_BLOCK), F32), pltpu.VMEM((2 * ATT_BLOCK, ATT_BLOCK), F32)]),
        compiler_params=_cparams(("arbitrary", "arbitrary")),
        name="attention",
    )(*qs, *kvs, *kvs)


def _regroup8(tiles):
    t = list(tiles)
    lane = lax.broadcasted_iota(jnp.int32, t[0].shape, 1)
    piece = lane // SSM_GROUP
    for dist in (4, 2, 1):
        bit = (piece & dist) != 0
        shift = dist * SSM_GROUP
        for a in range(8):
            if a & dist:
                continue
            b = a | dist
            ta, tb = t[a], t[b]
            t[a] = jnp.where(bit, pltpu.roll(tb, shift, 1), ta)
            t[b] = jnp.where(bit, tb, pltpu.roll(ta, LANES - shift, 1))
    return t


def _s5_tile_perm(bsz):
    tt = S5_TILE_TOKENS
    n = bsz * tt
    perm = np.zeros((n, n), np.float32)
    for b in range(bsz):
        for c in range(tt // CHUNK):
            for t in range(CHUNK):
                perm[(t * (tt // CHUNK) + c) * bsz + b, b * tt + c * CHUNK + t] = 1.0
    return perm


def _s5_pre_kernel(h_ref, mod_ref, g_ref, perm_ref, z_ref, u_scr, up_scr):
    d = D_MODEL
    bsz, tt, _ = h_ref.shape
    g = g_ref[...]
    for b in range(bsz):
        u_scr[b * tt:(b + 1) * tt, :] = _normmod(
            h_ref[b], g, mod_ref[b, :, :d], mod_ref[b, :, d:2 * d]).astype(BF16)
    up_scr[...] = jnp.dot(perm_ref[...], u_scr[...], preferred_element_type=F32)
    rows = z_ref.shape[0]
    for j in range(d // LANES):
        for half in range(CHUNK // 8):
            tiles = [up_scr[(8 * half + tl) * rows:(8 * half + tl + 1) * rows, j * LANES:(j + 1) * LANES]
                     for tl in range(8)]
            outs = _regroup8(tiles)
            for gl in range(8):
                col = (8 * j + gl) * GROUP_COLS + half * LANES
                z_ref[:, col:col + LANES] = outs[gl].astype(z_ref.dtype)


def _s5_pre(h, mod, g, perm):
    bsz, seq, d = h.shape
    tt = S5_TILE_TOKENS
    rows = bsz * tt // CHUNK
    return pl.pallas_call(
        _s5_pre_kernel,
        grid=(seq // tt,),
        in_specs=[
            pl.BlockSpec((bsz, tt, d), lambda i: (0, i, 0)),
            pl.BlockSpec(mod.shape, lambda i: (0, 0, 0)),
            pl.BlockSpec((1, d), lambda i: (0, 0)),
            pl.BlockSpec(perm.shape, lambda i: (0, 0)),
        ],
        out_specs=pl.BlockSpec((rows, CHUNK * d), lambda i: (i, 0)),
        out_shape=jax.ShapeDtypeStruct((bsz * seq // CHUNK, CHUNK * d), BF16),
        scratch_shapes=[pltpu.VMEM((bsz * tt, d), BF16), pltpu.VMEM((bsz * tt, d), F32)],
        compiler_params=_cparams(("parallel",)),
        name="s5_pre",
    )(h, mod, g.reshape(1, d), perm)


def _s5_core_kernel(z_ref, win_ref, wt_ref, wx_ref, dec_ref, d_ref, o_ref, s_scr, x_scr, *, bsz):
    gc = GROUP_COLS
    nchunk = z_ref.shape[0] // bsz
    s_scr[...] = (jnp.dot(z_ref[:, :gc], win_ref[0], preferred_element_type=F32)
                  + jnp.dot(z_ref[:, gc:], win_ref[1], preferred_element_type=F32))
    ar, ai = dec_ref[:, :LANES], dec_ref[:, LANES:]

    def step(c, carry):
        xr, xi = carry
        rows = pl.ds(pl.multiple_of(c * bsz, bsz), bsz)
        x_scr[rows, :LANES] = xr
        x_scr[rows, LANES:] = xi
        return (ar * xr - ai * xi + s_scr[rows, :LANES], ar * xi + ai * xr + s_scr[rows, LANES:])

    zero = jnp.zeros((bsz, LANES), F32)
    lax.fori_loop(0, nchunk, step, (zero, zero))
    x = x_scr[...].astype(BF16)
    for gi in range(2):
        z = z_ref[:, gi * gc:(gi + 1) * gc]
        y = (jnp.dot(z, wt_ref[gi], preferred_element_type=F32)
             + jnp.dot(x, wx_ref[gi], preferred_element_type=F32))
        y = y + d_ref[:, gi * gc:(gi + 1) * gc] * z.astype(F32)
        o_ref[:, gi * gc:(gi + 1) * gc] = jax.nn.gelu(y).astype(o_ref.dtype)


def _s5_core(z, tables, layer, bsz):
    nrow, width = z.shape
    w_in, w_toep, w_x, dec, d_perm = tables
    wspec = pl.BlockSpec((None, 2, GROUP_COLS, GROUP_COLS), lambda k: (layer, k, 0, 0))
    return pl.pallas_call(
        functools.partial(_s5_core_kernel, bsz=bsz),
        grid=(N_PAIRS,),
        in_specs=[
            pl.BlockSpec((nrow, PAIR_COLS), lambda k: (0, k)),
            wspec, wspec, wspec,
            pl.BlockSpec((None, 1, GROUP_COLS), lambda k: (layer, 0, k)),
            pl.BlockSpec((None, 1, PAIR_COLS), lambda k: (layer, 0, k)),
        ],
        out_specs=pl.BlockSpec((nrow, PAIR_COLS), lambda k: (0, k)),
        out_shape=jax.ShapeDtypeStruct((nrow, width), BF16),
        scratch_shapes=[pltpu.VMEM((nrow, GROUP_COLS), F32), pltpu.VMEM((nrow, GROUP_COLS), F32)],
        compiler_params=_cparams(("parallel",)),
        name="s5_core",
    )(z, w_in, w_toep, w_x, dec, d_perm)


def _s5_weights(lam_re, lam_im, log_dt, b_re, b_im, c_re, c_im, d_skip):
    g, p, c16 = SSM_GROUPS, SSM_STATE, SSM_GROUP
    lam = lax.complex(lam_re.astype(F32), lam_im.astype(F32))
    dt = jnp.exp(log_dt.astype(F32))[:, None]
    steps = jnp.arange(CHUNK + 1, dtype=F32)
    apow = jnp.exp((lam * dt)[None] * steps[:, None, None])
    a = apow[1]
    bbar = ((a - 1.0) / lam)[..., None] * lax.complex(b_re.astype(F32), b_im.astype(F32))
    cmat = lax.complex(c_re.astype(F32), c_im.astype(F32))
    odd = (jnp.arange(g) % 2)[:, None, None]

    win = apow[CHUNK - 1::-1][..., None] * bbar[None]
    win = jnp.transpose(win, (1, 0, 3, 2)).reshape(g, GROUP_COLS, p)
    zeros = jnp.zeros_like(win.real)
    w_in = jnp.concatenate([
        jnp.where(odd == 0, win.real, zeros), jnp.where(odd == 1, win.real, zeros),
        jnp.where(odd == 0, win.imag, zeros), jnp.where(odd == 1, win.imag, zeros)], axis=-1)

    kern = jnp.einsum('gop,kgp,gpi->kgoi', cmat, apow[:CHUNK], bbar, precision=HIGHEST).real
    idx = jnp.arange(CHUNK)
    onehot = (idx[None, None, :] - idx[None, :, None] == idx[:, None, None]).astype(F32)
    w_toep = jnp.einsum('kst,kgoi->gsito', onehot, kern, precision=HIGHEST)
    w_toep = w_toep.reshape(g, GROUP_COLS, GROUP_COLS)

    cw = cmat[None] * apow[1:, :, None, :]
    cw = jnp.transpose(cw, (1, 3, 0, 2)).reshape(g, p, GROUP_COLS)
    zx = jnp.zeros_like(cw.real)
    w_x = jnp.concatenate([
        jnp.where(odd == 0, cw.real, zx), jnp.where(odd == 1, cw.real, zx),
        jnp.where(odd == 0, -cw.imag, zx), jnp.where(odd == 1, -cw.imag, zx)], axis=1)

    a16 = apow[CHUNK].reshape(N_PAIRS, 2 * p)
    dec = jnp.concatenate([a16.real, a16.imag], axis=-1).reshape(1, N_PAIRS * GROUP_COLS)
    d_perm = jnp.broadcast_to(d_skip.astype(F32).reshape(g, 1, c16), (g, CHUNK, c16)).reshape(1, g * GROUP_COLS)
    return w_in.astype(BF16), w_toep.astype(BF16), w_x.astype(BF16), dec, d_perm


def _mlp(u, w1_ref, w2_ref):
    d = D_MODEL
    acc = None
    for k in range(D_FF // d):
        a = jnp.dot(u, w1_ref[:, k * d:(k + 1) * d], preferred_element_type=F32)
        a = jnp.square(jnp.maximum(a, 0.0)).astype(BF16)
        part = jnp.dot(a, w2_ref[k * d:(k + 1) * d, :], preferred_element_type=F32)
        acc = part if acc is None else acc + part
    return acc


def _final_norm(h, fg_ref):
    ms = jnp.mean(h * h, axis=-1, keepdims=True)
    return h * lax.rsqrt(ms + EPS) * fg_ref[...]


def _post_attn_kernel(*refs, final):
    if final:
        h_ref, y_ref, moda_ref, modm_ref, g_ref, wp_ref, w1_ref, w2_ref, fg_ref, o_ref = refs
    else:
        h_ref, y_ref, moda_ref, modm_ref, g_ref, wp_ref, w1_ref, w2_ref, o_ref = refs
    d = D_MODEL
    ymix = jnp.dot(y_ref[...], wp_ref[...], preferred_element_type=F32)
    h1 = h_ref[...] + moda_ref[:, 2 * d:3 * d] * ymix
    u = _normmod(h1, g_ref[...], modm_ref[:, :d], modm_ref[:, d:2 * d]).astype(BF16)
    h2 = h1 + modm_ref[:, 2 * d:3 * d] * _mlp(u, w1_ref, w2_ref)
    o_ref[...] = _final_norm(h2, fg_ref) if final else h2


def _post_attn(h, o, moda, modm, g2, w_o, w1, w2, final_g, tm=MLP_ROWS):
    bsz, seq, d = h.shape
    per_b = seq // tm
    row_spec = pl.BlockSpec((None, tm, d), lambda i: (i // per_b, i % per_b, 0))
    final = final_g is not None
    in_specs = [
        row_spec, row_spec,
        pl.BlockSpec((None, 1, moda.shape[-1]), lambda i: (i // per_b, 0, 0)),
        pl.BlockSpec((None, 1, modm.shape[-1]), lambda i: (i // per_b, 0, 0)),
        pl.BlockSpec((1, d), lambda i: (0, 0)),
        _layer_spec(*w_o), _layer_spec(*w1), _layer_spec(*w2),
    ]
    args = [h, o, moda, modm, g2.reshape(1, d), w_o[0], w1[0], w2[0]]
    if final:
        in_specs.append(pl.BlockSpec((1, d), lambda i: (0, 0)))
        args.append(final_g.reshape(1, d))
    return pl.pallas_call(
        functools.partial(_post_attn_kernel, final=final),
        grid=(bsz * per_b,),
        in_specs=in_specs,
        out_specs=row_spec,
        out_shape=jax.ShapeDtypeStruct((bsz, seq, d), F32),
        compiler_params=_cparams(("parallel",)),
        name="post_attn_mlp",
    )(*args)


def _post_s5_kernel(h_ref, z_ref, moda_ref, modm_ref, g_ref, permt_ref, wp_ref, w1_ref, w2_ref, o_ref,
                    z_scr, h1_scr, u_scr):
    d = D_MODEL
    bsz, tt, _ = h_ref.shape
    rows = z_ref.shape[0]
    for j in range(d // LANES):
        for half in range(CHUNK // 8):
            tiles = []
            for gl in range(8):
                col = (8 * j + gl) * GROUP_COLS + half * LANES
                tiles.append(z_ref[:, col:col + LANES].astype(F32))
            outs = _regroup8(tiles)
            for tl in range(8):
                t = 8 * half + tl
                z_scr[t * rows:(t + 1) * rows, j * LANES:(j + 1) * LANES] = outs[tl].astype(BF16)
    zn = jnp.dot(permt_ref[...], z_scr[...], preferred_element_type=F32).astype(BF16)
    zz = jnp.dot(zn, wp_ref[...], preferred_element_type=F32)
    ymix = zz[:, :d] * jax.nn.sigmoid(zz[:, d:])
    g = g_ref[...]
    for b in range(bsz):
        sl = slice(b * tt, (b + 1) * tt)
        h1 = h_ref[b] + moda_ref[b, :, 2 * d:3 * d] * ymix[sl, :]
        h1_scr[sl, :] = h1
        u_scr[sl, :] = _normmod(h1, g, modm_ref[b, :, :d], modm_ref[b, :, d:2 * d]).astype(BF16)
    acc = _mlp(u_scr[...], w1_ref, w2_ref)
    for b in range(bsz):
        sl = slice(b * tt, (b + 1) * tt)
        o_ref[b] = h1_scr[sl, :] + modm_ref[b, :, 2 * d:3 * d] * acc[sl, :]


def _post_s5(h, zo, moda, modm, g2, permt, w_glu, w1, w2):
    bsz, seq, d = h.shape
    tt = S5_TILE_TOKENS
    rows = bsz * tt // CHUNK
    h_spec = pl.BlockSpec((bsz, tt, d), lambda i: (0, i, 0))
    return pl.pallas_call(
        _post_s5_kernel,
        grid=(seq // tt,),
        in_specs=[
            h_spec,
            pl.BlockSpec((rows, CHUNK * d), lambda i: (i, 0)),
            pl.BlockSpec(moda.shape, lambda i: (0, 0, 0)),
            pl.BlockSpec(modm.shape, lambda i: (0, 0, 0)),
            pl.BlockSpec((1, d), lambda i: (0, 0)),
            pl.BlockSpec(permt.shape, lambda i: (0, 0)),
            _layer_spec(*w_glu), _layer_spec(*w1), _layer_spec(*w2),
        ],
        out_specs=h_spec,
        out_shape=jax.ShapeDtypeStruct((bsz, seq, d), F32),
        scratch_shapes=[pltpu.VMEM((bsz * tt, d), BF16), pltpu.VMEM((bsz * tt, d), F32),
                        pltpu.VMEM((bsz * tt, d), BF16)],
        compiler_params=_cparams(("parallel",)),
        name="post_s5_mlp",
    )(h, zo, moda, modm, g2.reshape(1, d), permt, w_glu[0], w1[0], w2[0])


def kernel(x, c, ln_g, ada_w, ada_b, ssm_lam_re, ssm_lam_im, ssm_log_dt, ssm_b_re, ssm_b_im, ssm_c_re, ssm_c_im, ssm_d, ssm_w_glu, kv_g, kv_ada_w, kv_ada_b, w_kv, attn_w_q, attn_w_o, mlp_w1, mlp_w2, final_g):
    bsz, seq, d = x.shape
    depth = ln_g.shape[0]
    n_s5 = ssm_lam_re.shape[0]
    assert d == D_MODEL and seq % (DILATIONS[-1] * ATT_BLOCK) == 0

    mods = _ada_mods(c, ada_w.reshape(depth * 2, d, 3 * d), ada_b.reshape(depth * 2, 3 * d))
    mods = mods.reshape(depth, 2, bsz, 1, 3 * d)
    kv_mod = _ada_mods(c, kv_ada_w[None], kv_ada_b[None]).reshape(bsz, 1, 2 * d)

    w1 = mlp_w1.astype(BF16)
    w2 = mlp_w2.astype(BF16)
    w_glu = ssm_w_glu.astype(BF16)
    w_q = attn_w_q.astype(BF16)
    w_o = attn_w_o.astype(BF16)
    perm_np = _s5_tile_perm(bsz)
    perm = jnp.asarray(perm_np, BF16)
    permt = jnp.asarray(perm_np.T, BF16)

    tables = jax.vmap(_s5_weights)(ssm_lam_re, ssm_lam_im, ssm_log_dt, ssm_b_re, ssm_b_im,
                                   ssm_c_re, ssm_c_im, ssm_d)
    h = x
    for layer in range(n_s5):
        z = _s5_pre(h, mods[layer, 0], ln_g[layer, 0], perm)
        zo = _s5_core(z, tables, layer, bsz)
        h = _post_s5(h, zo, mods[layer, 0], mods[layer, 1], ln_g[layer, 1], permt,
                     (w_glu, layer), (w1, layer), (w2, layer))

    kv_cols = tuple((i * d, (N_BRANCHES + i) * d) for i in range(N_BRANCHES))
    q_cols = tuple((i * d,) for i in range(N_BRANCHES))
    kvs = _norm_proj(h, kv_mod, kv_g, (w_kv.astype(BF16)[None], 0), kv_cols, 1.0)
    for layer in range(n_s5, depth):
        j = layer - n_s5
        qs = _norm_proj(h, mods[layer, 0], ln_g[layer, 0], (w_q, j), q_cols, Q_SCALE)
        o = _attention(qs, kvs)
        h = _post_attn(h, o, mods[layer, 0], mods[layer, 1], ln_g[layer, 1],
                       (w_o, j), (w1, layer), (w2, layer),
                       final_g if layer == depth - 1 else None)
    return h
```

```python
import functools
import math

import numpy as np

import jax
import jax.numpy as jnp
from jax import lax
from jax.experimental import pallas as pl
from jax.experimental.pallas import tpu as pltpu

F32 = jnp.float32
BF16 = jnp.bfloat16

D_MODEL = 1024
SSM_GROUP = 16
SSM_GROUPS = D_MODEL // SSM_GROUP
SSM_STATE = 64
HEAD_DIM = 64
N_HEADS = D_MODEL // HEAD_DIM
DILATIONS = (1, 4, 16)
N_BRANCHES = len(DILATIONS)
ATT_BLOCK = 128
ATT_GROUP = 16
N_STATS = 3
Q_SCALE = HEAD_DIM ** -0.5 * math.log2(math.e)
D_FF = 4 * D_MODEL
EPS = 1e-6
NEG = -1e30

LANES = 128
CHUNK = 16
GROUP_COLS = CHUNK * SSM_GROUP
PAIR_COLS = 2 * GROUP_COLS
N_PAIRS = SSM_GROUPS // 2
S5_TILE_TOKENS = 2 * CHUNK
MLP_ROWS = 512
PROJ_TILE = 512
VMEM_LIMIT = 56 * 1024 * 1024

HIGHEST = lax.Precision.HIGHEST
NT_DIMS = (((1,), (1,)), ((), ()))


def _cparams(sem):
    return pltpu.CompilerParams(dimension_semantics=sem, vmem_limit_bytes=VMEM_LIMIT)


def _layer_spec(stack, layer):
    zeros = (0,) * (stack.ndim - 1)
    return pl.BlockSpec((None,) + stack.shape[1:], lambda i: (layer,) + zeros, pipeline_mode=pl.Buffered(1))


def _normmod(x, g, shift, scale):
    ms = jnp.mean(x * x, axis=-1, keepdims=True)
    return (x * lax.rsqrt(ms + EPS) * g) * (1.0 + scale) + shift


def _ada_kernel(c_ref, w_ref, b_ref, o_ref):
    c = c_ref[...]
    sc = (c * jax.nn.sigmoid(c)).astype(BF16)
    o_ref[...] = jnp.dot(sc, w_ref[...].astype(BF16), preferred_element_type=F32) + b_ref[...]


def _ada_mods(c, w, b):
    n, d, width = w.shape
    bsz = c.shape[0]
    tn = width
    return pl.pallas_call(
        _ada_kernel,
        grid=(n, width // tn),
        in_specs=[
            pl.BlockSpec((bsz, d), lambda i, j: (0, 0)),
            pl.BlockSpec((None, d, tn), lambda i, j: (i, 0, j)),
            pl.BlockSpec((None, 1, tn), lambda i, j: (i, 0, j)),
        ],
        out_specs=pl.BlockSpec((None, bsz, tn), lambda i, j: (i, 0, j)),
        out_shape=jax.ShapeDtypeStruct((n, bsz, width), F32),
        compiler_params=_cparams(("parallel", "parallel")),
        name="ada_mods",
    )(c, w, b.reshape(n, 1, width))


def _proj_kernel(h_ref, mod_ref, g_ref, w_ref, o1_ref, o2_ref, o3_ref, us, us4, up4, up16, *, wcols, out_scale):
    d = D_MODEL
    tm = h_ref.shape[0]
    d1 = DILATIONS[1]
    q4, q16 = tm // d1, tm // DILATIONS[2]
    cw = o1_ref.shape[1]
    u = _normmod(h_ref[...], g_ref[...], mod_ref[:, :d], mod_ref[:, d:2 * d])
    for l in range(d // LANES):
        us[l] = u[:, l * LANES:(l + 1) * LANES]
    ub = u.astype(BF16)
    for n in range(cw // d):
        cols = slice(n * d, (n + 1) * d)
        res = jnp.dot(ub, w_ref[:, wcols[0][n]:wcols[0][n] + d], preferred_element_type=F32)
        o1_ref[:, cols] = (res * out_scale).astype(o1_ref.dtype)
    for l in range(d // LANES):
        for a in range(d1):
            part = us[l, pl.ds(a, q4, stride=d1), :]
            us4[l, a * q4:(a + 1) * q4, :] = part
            up4[a * q4:(a + 1) * q4, l * LANES:(l + 1) * LANES] = part.astype(BF16)
    for n in range(cw // d):
        res = jnp.dot(up4[...], w_ref[:, wcols[1][n]:wcols[1][n] + d], preferred_element_type=F32)
        for a in range(d1):
            o2_ref[a, :, n * d:(n + 1) * d] = (res[a * q4:(a + 1) * q4, :] * out_scale).astype(o2_ref.dtype)
    for l in range(d // LANES):
        for a1 in range(d1):
            for a2 in range(d1):
                r = d1 * a2 + a1
                up16[r * q16:(r + 1) * q16, l * LANES:(l + 1) * LANES] = us4[
                    l, pl.ds(a1 * q4 + a2, q16, stride=d1), :].astype(BF16)
    for n in range(cw // d):
        res = jnp.dot(up16[...], w_ref[:, wcols[2][n]:wcols[2][n] + d], preferred_element_type=F32)
        for r in range(DILATIONS[2]):
            o3_ref[r, :, n * d:(n + 1) * d] = (res[r * q16:(r + 1) * q16, :] * out_scale).astype(o3_ref.dtype)


def _norm_proj(h, mod, g, w, wcols, out_scale):
    bsz, seq, d = h.shape
    tm = PROJ_TILE
    cw = len(wcols[0]) * d
    per_b = seq // tm
    d1, d2 = DILATIONS[1], DILATIONS[2]
    o1, o2, o3 = pl.pallas_call(
        functools.partial(_proj_kernel, wcols=wcols, out_scale=out_scale),
        grid=(bsz * per_b,),
        in_specs=[
            pl.BlockSpec((None, tm, d), lambda i: (i // per_b, i % per_b, 0)),
            pl.BlockSpec((None, 1, mod.shape[-1]), lambda i: (i // per_b, 0, 0)),
            pl.BlockSpec((1, d), lambda i: (0, 0)),
            _layer_spec(*w),
        ],
        out_specs=[
            pl.BlockSpec((None, tm, cw), lambda i: (i // per_b, i % per_b, 0)),
            pl.BlockSpec((None, d1, None, tm // d1, cw), lambda i: (i // per_b, 0, i % per_b, 0, 0)),
            pl.BlockSpec((None, d2, None, tm // d2, cw), lambda i: (i // per_b, 0, i % per_b, 0, 0)),
        ],
        out_shape=[
            jax.ShapeDtypeStruct((bsz, seq, cw), BF16),
            jax.ShapeDtypeStruct((bsz, d1, per_b, tm // d1, cw), BF16),
            jax.ShapeDtypeStruct((bsz, d2, per_b, tm // d2, cw), BF16),
        ],
        scratch_shapes=[pltpu.VMEM((d // LANES, tm, LANES), F32), pltpu.VMEM((d // LANES, tm, LANES), F32),
                        pltpu.VMEM((tm, d), BF16), pltpu.VMEM((tm, d), BF16)],
        compiler_params=_cparams(("parallel",)),
        name="norm_proj",
    )(h, mod, g.reshape(1, d), w[0])
    return o1, o2.reshape(bsz, seq, cw), o3.reshape(bsz, seq, cw)


def _attn_group(blocks):
    blk = ATT_BLOCK
    lane = lax.broadcasted_iota(jnp.int32, (blk, LANES), 1)
    lo = lane < HEAD_DIM
    scores = []
    for qb, kcat, _, mask, _ in blocks:
        zero = jnp.zeros_like(qb)
        q2 = jnp.concatenate([jnp.where(lo, qb, zero), jnp.where(lo, zero, qb)], axis=0)
        s = lax.dot_general(q2, kcat, NT_DIMS, preferred_element_type=F32)
        scores.append(s + mask[...])
    soft = []
    for s, (_, _, _, _, old) in zip(scores, blocks):
        ms, ps = [], []
        for hh in range(2):
            tiles = [s[hh * blk:(hh + 1) * blk, t * LANES:(t + 1) * LANES] for t in range(s.shape[1] // LANES)]
            mx = tiles[0]
            for t in tiles[1:]:
                mx = jnp.maximum(mx, t)
            m = jnp.broadcast_to(jnp.max(mx, axis=-1, keepdims=True), (blk, LANES))
            if old is not None:
                m = jnp.maximum(m, old[0][hh])
            ms.append(m)
            ps.append(jnp.concatenate([jnp.exp2(t - m).astype(BF16) for t in tiles], axis=1))
        alpha = None
        if old is not None:
            alpha = jnp.exp2(jnp.where(lo, old[0][0], old[0][1]) - jnp.where(lo, ms[0], ms[1]))
        soft.append((ms, jnp.concatenate(ps, axis=0), alpha))
    outs = []
    for (ms, p, alpha), (_, _, vcat, _, old) in zip(soft, blocks):
        pv = jnp.dot(p, vcat, preferred_element_type=F32)
        acc = jnp.where(lo, pv[:blk, :LANES], pv[blk:, :LANES])
        den = jnp.where(lo, pv[:blk, LANES:], pv[blk:, LANES:])
        if old is not None:
            acc = alpha * old[1] + acc
            den = alpha * old[0][2] + den
        outs.append((ms + [den], acc))
    return outs


def _attn_kernel(q1, qd2, qd3, k1, kd2, kd3, v1, v2, v3, o_ref, tmp_nat, vd1, vd2, vd3,
                 acc_ref, st_ref, acc2_ref, st2_ref, mask_pc, mask_c):
    blk = ATT_BLOCK
    seq = q1.shape[0]
    nblk = seq // blk
    d1, d2 = DILATIONS[1], DILATIONS[2]
    quarter = seq // d1
    per_res = quarter // blk
    assert d2 == d1 * d1 and seq == d2 * blk and nblk % ATT_GROUP == 0 and ATT_GROUP % per_res == 0
    @pl.when(jnp.logical_and(pl.program_id(0) == 0, pl.program_id(1) == 0))
    def _():
        for vd in (vd1, vd2, vd3):
            vd[:, LANES:] = jnp.ones((seq, LANES), BF16)
        qq = lax.broadcasted_iota(jnp.int32, (2 * blk, 2 * blk), 0) & (blk - 1)
        kk = lax.broadcasted_iota(jnp.int32, (2 * blk, 2 * blk), 1)
        valid = jnp.logical_or(jnp.logical_and(kk < blk, kk >= qq), jnp.logical_and(kk >= blk, kk - blk <= qq))
        mask_pc[...] = jnp.where(valid, 0.0, NEG)
        valid_c = (lax.broadcasted_iota(jnp.int32, (2 * blk, blk), 1)
                   <= (lax.broadcasted_iota(jnp.int32, (2 * blk, blk), 0) & (blk - 1)))
        mask_c[...] = jnp.where(valid_c, 0.0, NEG)

    for v, vd in ((v1, vd1), (v2, vd2), (v3, vd3)):
        vd[:, :LANES] = v[...]

    def run_group(blocks, dests):
        for (st_dst, acc_dst, rows), (st, acc) in zip(dests, _attn_group(blocks)):
            for k in range(N_STATS):
                st_dst[k, rows, :] = st[k]
            acc_dst[rows, :] = acc

    def load_old(st_src, acc_src, rows):
        return [st_src[k, rows, :] for k in range(N_STATS)], acc_src[rows, :]

    def b0_group(ns, first):
        blocks, dests = [], []
        for n in ns:
            if first and n == 0:
                rows = pl.ds(0, blk)
                blocks.append((q1[rows, :], k1[rows, :], vd1[rows, :], mask_c, None))
            else:
                rows = pl.ds(pl.multiple_of(n * blk, blk), blk)
                krows = pl.ds(pl.multiple_of((n - 1) * blk, blk), 2 * blk)
                blocks.append((q1[rows, :], k1[krows, :], vd1[krows, :], mask_pc, None))
            dests.append((st_ref, acc_ref, rows))
        run_group(blocks, dests)

    b0_group(list(range(ATT_GROUP)), True)

    def b0_body(i, carry):
        b0_group([ATT_GROUP * i + j for j in range(ATT_GROUP)], False)
        return carry

    lax.fori_loop(1, nblk // ATT_GROUP, b0_body, 0)

    res_per_trip = ATT_GROUP // per_res

    def b1_body(i, carry):
        blocks, dests = [], []
        for jr in range(res_per_trip):
            a1 = res_per_trip * i + jr
            base = a1 * quarter
            for n in range(per_res):
                rows = pl.ds(a1 + d1 * blk * n, blk, stride=d1)
                qrows = pl.ds(pl.multiple_of(base + n * blk, blk), blk)
                if n == 0:
                    krows, mask = qrows, mask_c
                else:
                    krows, mask = pl.ds(pl.multiple_of(base + (n - 1) * blk, blk), 2 * blk), mask_pc
                blocks.append((qd2[qrows, :], kd2[krows, :], vd2[krows, :], mask,
                               load_old(st_ref, acc_ref, rows)))
                dests.append((st2_ref, acc2_ref, qrows))
        run_group(blocks, dests)
        return carry

    lax.fori_loop(0, d1 // res_per_trip, b1_body, 0)

    def b2_body(i, carry):
        blocks, dests = [], []
        for j in range(ATT_GROUP):
            a1, a2 = j % d1, (ATT_GROUP // d1) * i + j // d1
            qrows = pl.ds(pl.multiple_of((ATT_GROUP * i + j) * blk, blk), blk)
            rows = pl.ds(a1 * quarter + a2, blk, stride=d1)
            blocks.append((qd3[qrows, :], kd3[qrows, :], vd3[qrows, :], mask_c,
                           load_old(st2_ref, acc2_ref, rows)))
            dests.append((st2_ref, acc2_ref, rows))
        run_group(blocks, dests)
        return carry

    lax.fori_loop(0, d2 // ATT_GROUP, b2_body, 0)

    for a in range(d1):
        seg = slice(a * quarter, (a + 1) * quarter)
        tmp_nat[pl.ds(a, quarter, stride=d1), :] = acc2_ref[seg, :] / st2_ref[N_STATS - 1, seg, :]
    o_ref[...] = tmp_nat[...].astype(o_ref.dtype)


def _attention(qs, kvs):
    bsz, seq, _ = qs[0].shape
    pairs = D_MODEL // LANES

    def slab(col0):
        return pl.BlockSpec((None, seq, LANES), lambda b, hp: (b, 0, col0 + hp))

    in_specs = [slab(0)] * N_BRANCHES + [slab(0)] * N_BRANCHES + [slab(pairs)] * N_BRANCHES
    return pl.pallas_call(
        _attn_kernel,
        grid=(bsz, pairs),
        in_specs=in_specs,
        out_specs=slab(0),
        out_shape=jax.ShapeDtypeStruct((bsz, seq, D_MODEL), BF16),
        scratch_shapes=([pltpu.VMEM((seq, LANES), F32)] + [pltpu.VMEM((seq, 2 * LANES), BF16)] * 3
                        + [pltpu.VMEM((seq, LANES), F32), pltpu.VMEM((N_STATS, seq, LANES), F32)] * 2
                        + [pltpu.VMEM((2 * ATT_BLOCK, 2 * ATT_BLOCK), F32), pltpu.VMEM((2 * ATT_BLOCK, ATT_BLOCK), F32)]),
        compiler_params=_cparams(("arbitrary", "arbitrary")),
        name="attention",
    )(*qs, *kvs, *kvs)


def _regroup8(tiles):
    t = list(tiles)
    lane = lax.broadcasted_iota(jnp.int32, t[0].shape, 1)
    piece = lane // SSM_GROUP
    for dist in (4, 2, 1):
        bit = (piece & dist) != 0
        shift = dist * SSM_GROUP
        for a in range(8):
            if a & dist:
                continue
            b = a | dist
            ta, tb = t[a], t[b]
            t[a] = jnp.where(bit, pltpu.roll(tb, shift, 1), ta)
            t[b] = jnp.where(bit, tb, pltpu.roll(ta, LANES - shift, 1))
    return t


def _s5_tile_perm(bsz):
    tt = S5_TILE_TOKENS
    n = bsz * tt
    perm = np.zeros((n, n), np.float32)
    for b in range(bsz):
        for c in range(tt // CHUNK):
            for t in range(CHUNK):
                perm[(t * (tt // CHUNK) + c) * bsz + b, b * tt + c * CHUNK + t] = 1.0
    return perm


def _s5_pre_kernel(h_ref, mod_ref, g_ref, perm_ref, z_ref, u_scr, up_scr):
    d = D_MODEL
    bsz, tt, _ = h_ref.shape
    g = g_ref[...]
    for b in range(bsz):
        u_scr[b * tt:(b + 1) * tt, :] = _normmod(
            h_ref[b], g, mod_ref[b, :, :d], mod_ref[b, :, d:2 * d]).astype(BF16)
    up_scr[...] = jnp.dot(perm_ref[...], u_scr[...], preferred_element_type=F32)
    rows = z_ref.shape[0]
    for j in range(d // LANES):
        for half in range(CHUNK // 8):
            tiles = [up_scr[(8 * half + tl) * rows:(8 * half + tl + 1) * rows, j * LANES:(j + 1) * LANES]
                     for tl in range(8)]
            outs = _regroup8(tiles)
            for gl in range(8):
                col = (8 * j + gl) * GROUP_COLS + half * LANES
                z_ref[:, col:col + LANES] = outs[gl].astype(z_ref.dtype)


def _s5_pre(h, mod, g, perm):
    bsz, seq, d = h.shape
    tt = S5_TILE_TOKENS
    rows = bsz * tt // CHUNK
    return pl.pallas_call(
        _s5_pre_kernel,
        grid=(seq // tt,),
        in_specs=[
            pl.BlockSpec((bsz, tt, d), lambda i: (0, i, 0)),
            pl.BlockSpec(mod.shape, lambda i: (0, 0, 0)),
            pl.BlockSpec((1, d), lambda i: (0, 0)),
            pl.BlockSpec(perm.shape, lambda i: (0, 0)),
        ],
        out_specs=pl.BlockSpec((rows, CHUNK * d), lambda i: (i, 0)),
        out_shape=jax.ShapeDtypeStruct((bsz * seq // CHUNK, CHUNK * d), BF16),
        scratch_shapes=[pltpu.VMEM((bsz * tt, d), BF16), pltpu.VMEM((bsz * tt, d), F32)],
        compiler_params=_cparams(("parallel",)),
        name="s5_pre",
    )(h, mod, g.reshape(1, d), perm)


def _s5_core_kernel(z_ref, win_ref, wt_ref, wx_ref, dec_ref, d_ref, o_ref, s_scr, x_scr, *, bsz):
    gc = GROUP_COLS
    nchunk = z_ref.shape[0] // bsz
    s_scr[...] = (jnp.dot(z_ref[:, :gc], win_ref[0], preferred_element_type=F32)
                  + jnp.dot(z_ref[:, gc:], win_ref[1], preferred_element_type=F32))
    ar, ai = dec_ref[:, :LANES], dec_ref[:, LANES:]

    def step(c, carry):
        xr, xi = carry
        rows = pl.ds(pl.multiple_of(c * bsz, bsz), bsz)
        x_scr[rows, :LANES] = xr
        x_scr[rows, LANES:] = xi
        return (ar * xr - ai * xi + s_scr[rows, :LANES], ar * xi + ai * xr + s_scr[rows, LANES:])

    zero = jnp.zeros((bsz, LANES), F32)
    lax.fori_loop(0, nchunk, step, (zero, zero))
    x = x_scr[...].astype(BF16)
    for gi in range(2):
        z = z_ref[:, gi * gc:(gi + 1) * gc]
        y = (jnp.dot(z, wt_ref[gi], preferred_element_type=F32)
             + jnp.dot(x, wx_ref[gi], preferred_element_type=F32))
        y = y + d_ref[:, gi * gc:(gi + 1) * gc] * z.astype(F32)
        o_ref[:, gi * gc:(gi + 1) * gc] = jax.nn.gelu(y).astype(o_ref.dtype)


def _s5_core(z, tables, layer, bsz):
    nrow, width = z.shape
    w_in, w_toep, w_x, dec, d_perm = tables
    wspec = pl.BlockSpec((None, 2, GROUP_COLS, GROUP_COLS), lambda k: (layer, k, 0, 0))
    return pl.pallas_call(
        functools.partial(_s5_core_kernel, bsz=bsz),
        grid=(N_PAIRS,),
        in_specs=[
            pl.BlockSpec((nrow, PAIR_COLS), lambda k: (0, k)),
            wspec, wspec, wspec,
            pl.BlockSpec((None, 1, GROUP_COLS), lambda k: (layer, 0, k)),
            pl.BlockSpec((None, 1, PAIR_COLS), lambda k: (layer, 0, k)),
        ],
        out_specs=pl.BlockSpec((nrow, PAIR_COLS), lambda k: (0, k)),
        out_shape=jax.ShapeDtypeStruct((nrow, width), BF16),
        scratch_shapes=[pltpu.VMEM((nrow, GROUP_COLS), F32), pltpu.VMEM((nrow, GROUP_COLS), F32)],
        compiler_params=_cparams(("parallel",)),
        name="s5_core",
    )(z, w_in, w_toep, w_x, dec, d_perm)


def _s5_weights(lam_re, lam_im, log_dt, b_re, b_im, c_re, c_im, d_skip):
    g, p, c16 = SSM_GROUPS, SSM_STATE, SSM_GROUP
    lam = lax.complex(lam_re.astype(F32), lam_im.astype(F32))
    dt = jnp.exp(log_dt.astype(F32))[:, None]
    steps = jnp.arange(CHUNK + 1, dtype=F32)
    apow = jnp.exp((lam * dt)[None] * steps[:, None, None])
    a = apow[1]
    bbar = ((a - 1.0) / lam)[..., None] * lax.complex(b_re.astype(F32), b_im.astype(F32))
    cmat = lax.complex(c_re.astype(F32), c_im.astype(F32))
    odd = (jnp.arange(g) % 2)[:, None, None]

    win = apow[CHUNK - 1::-1][..., None] * bbar[None]

    def lay_in(x):
        return jnp.transpose(x.astype(BF16), (1, 0, 3, 2)).reshape(g, GROUP_COLS, p)

    win_re, win_im = lay_in(win.real), lay_in(win.imag)
    zeros = jnp.zeros_like(win_re)
    w_in = jnp.concatenate([
        jnp.where(odd == 0, win_re, zeros), jnp.where(odd == 1, win_re, zeros),
        jnp.where(odd == 0, win_im, zeros), jnp.where(odd == 1, win_im, zeros)], axis=-1)

    kern = jnp.einsum('gop,kgp,gpi->kgoi', cmat, apow[:CHUNK], bbar, precision=HIGHEST).real
    idx = jnp.arange(CHUNK)
    onehot = (idx[None, None, :] - idx[None, :, None] == idx[:, None, None]).astype(BF16)
    w_toep = jnp.einsum('kst,kgoi->gsito', onehot, kern.astype(BF16),
                        preferred_element_type=BF16)
    w_toep = w_toep.reshape(g, GROUP_COLS, GROUP_COLS)

    cw = cmat[None] * apow[1:, :, None, :]

    def lay_x(x):
        return jnp.transpose(x.astype(BF16), (1, 3, 0, 2)).reshape(g, p, GROUP_COLS)

    cw_re, cw_im = lay_x(cw.real), lay_x(-cw.imag)
    zx = jnp.zeros_like(cw_re)
    w_x = jnp.concatenate([
        jnp.where(odd == 0, cw_re, zx), jnp.where(odd == 1, cw_re, zx),
        jnp.where(odd == 0, cw_im, zx), jnp.where(odd == 1, cw_im, zx)], axis=1)

    a16 = apow[CHUNK].reshape(N_PAIRS, 2 * p)
    dec = jnp.concatenate([a16.real, a16.imag], axis=-1).reshape(1, N_PAIRS * GROUP_COLS)
    d_perm = jnp.broadcast_to(d_skip.astype(F32).reshape(g, 1, c16), (g, CHUNK, c16)).reshape(1, g * GROUP_COLS)
    return w_in, w_toep, w_x, dec, d_perm


def _mlp(u, w1_ref, w2_ref):
    d = D_MODEL
    acc = None
    for k in range(D_FF // d):
        a = jnp.dot(u, w1_ref[:, k * d:(k + 1) * d], preferred_element_type=F32)
        a = jnp.square(jnp.maximum(a, 0.0)).astype(BF16)
        part = jnp.dot(a, w2_ref[k * d:(k + 1) * d, :], preferred_element_type=F32)
        acc = part if acc is None else acc + part
    return acc


def _final_norm(h, fg_ref):
    ms = jnp.mean(h * h, axis=-1, keepdims=True)
    return h * lax.rsqrt(ms + EPS) * fg_ref[...]


def _post_attn_kernel(*refs, final):
    if final:
        h_ref, y_ref, moda_ref, modm_ref, g_ref, wp_ref, w1_ref, w2_ref, fg_ref, o_ref = refs
    else:
        h_ref, y_ref, moda_ref, modm_ref, g_ref, wp_ref, w1_ref, w2_ref, o_ref = refs
    d = D_MODEL
    ymix = jnp.dot(y_ref[...], wp_ref[...], preferred_element_type=F32)
    h1 = h_ref[...] + moda_ref[:, 2 * d:3 * d] * ymix
    u = _normmod(h1, g_ref[...], modm_ref[:, :d], modm_ref[:, d:2 * d]).astype(BF16)
    h2 = h1 + modm_ref[:, 2 * d:3 * d] * _mlp(u, w1_ref, w2_ref)
    o_ref[...] = _final_norm(h2, fg_ref) if final else h2


def _post_attn(h, o, moda, modm, g2, w_o, w1, w2, final_g, tm=MLP_ROWS):
    bsz, seq, d = h.shape
    per_b = seq // tm
    row_spec = pl.BlockSpec((None, tm, d), lambda i: (i // per_b, i % per_b, 0))
    final = final_g is not None
    in_specs = [
        row_spec, row_spec,
        pl.BlockSpec((None, 1, moda.shape[-1]), lambda i: (i // per_b, 0, 0)),
        pl.BlockSpec((None, 1, modm.shape[-1]), lambda i: (i // per_b, 0, 0)),
        pl.BlockSpec((1, d), lambda i: (0, 0)),
        _layer_spec(*w_o), _layer_spec(*w1), _layer_spec(*w2),
    ]
    args = [h, o, moda, modm, g2.reshape(1, d), w_o[0], w1[0], w2[0]]
    if final:
        in_specs.append(pl.BlockSpec((1, d), lambda i: (0, 0)))
        args.append(final_g.reshape(1, d))
    return pl.pallas_call(
        functools.partial(_post_attn_kernel, final=final),
        grid=(bsz * per_b,),
        in_specs=in_specs,
        out_specs=row_spec,
        out_shape=jax.ShapeDtypeStruct((bsz, seq, d), F32),
        compiler_params=_cparams(("parallel",)),
        name="post_attn_mlp",
    )(*args)


def _post_s5_kernel(h_ref, z_ref, moda_ref, modm_ref, g_ref, permt_ref, wp_ref, w1_ref, w2_ref, o_ref,
                    z_scr, h1_scr, u_scr):
    d = D_MODEL
    bsz, tt, _ = h_ref.shape
    rows = z_ref.shape[0]
    for j in range(d // LANES):
        for half in range(CHUNK // 8):
            tiles = []
            for gl in range(8):
                col = (8 * j + gl) * GROUP_COLS + half * LANES
                tiles.append(z_ref[:, col:col + LANES].astype(F32))
            outs = _regroup8(tiles)
            for tl in range(8):
                t = 8 * half + tl
                z_scr[t * rows:(t + 1) * rows, j * LANES:(j + 1) * LANES] = outs[tl].astype(BF16)
    zn = jnp.dot(permt_ref[...], z_scr[...], preferred_element_type=F32).astype(BF16)
    zz = jnp.dot(zn, wp_ref[...], preferred_element_type=F32)
    ymix = zz[:, :d] * jax.nn.sigmoid(zz[:, d:])
    g = g_ref[...]
    for b in range(bsz):
        sl = slice(b * tt, (b + 1) * tt)
        h1 = h_ref[b] + moda_ref[b, :, 2 * d:3 * d] * ymix[sl, :]
        h1_scr[sl, :] = h1
        u_scr[sl, :] = _normmod(h1, g, modm_ref[b, :, :d], modm_ref[b, :, d:2 * d]).astype(BF16)
    acc = _mlp(u_scr[...], w1_ref, w2_ref)
    for b in range(bsz):
        sl = slice(b * tt, (b + 1) * tt)
        o_ref[b] = h1_scr[sl, :] + modm_ref[b, :, 2 * d:3 * d] * acc[sl, :]


def _post_s5(h, zo, moda, modm, g2, permt, w_glu, w1, w2):
    bsz, seq, d = h.shape
    tt = S5_TILE_TOKENS
    rows = bsz * tt // CHUNK
    h_spec = pl.BlockSpec((bsz, tt, d), lambda i: (0, i, 0))
    return pl.pallas_call(
        _post_s5_kernel,
        grid=(seq // tt,),
        in_specs=[
            h_spec,
            pl.BlockSpec((rows, CHUNK * d), lambda i: (i, 0)),
            pl.BlockSpec(moda.shape, lambda i: (0, 0, 0)),
            pl.BlockSpec(modm.shape, lambda i: (0, 0, 0)),
            pl.BlockSpec((1, d), lambda i: (0, 0)),
            pl.BlockSpec(permt.shape, lambda i: (0, 0)),
            _layer_spec(*w_glu), _layer_spec(*w1), _layer_spec(*w2),
        ],
        out_specs=h_spec,
        out_shape=jax.ShapeDtypeStruct((bsz, seq, d), F32),
        scratch_shapes=[pltpu.VMEM((bsz * tt, d), BF16), pltpu.VMEM((bsz * tt, d), F32),
                        pltpu.VMEM((bsz * tt, d), BF16)],
        compiler_params=_cparams(("parallel",)),
        name="post_s5_mlp",
    )(h, zo, moda, modm, g2.reshape(1, d), permt, w_glu[0], w1[0], w2[0])


def kernel(x, c, ln_g, ada_w, ada_b, ssm_lam_re, ssm_lam_im, ssm_log_dt, ssm_b_re, ssm_b_im, ssm_c_re, ssm_c_im, ssm_d, ssm_w_glu, kv_g, kv_ada_w, kv_ada_b, w_kv, attn_w_q, attn_w_o, mlp_w1, mlp_w2, final_g):
    bsz, seq, d = x.shape
    depth = ln_g.shape[0]
    n_s5 = ssm_lam_re.shape[0]
    assert d == D_MODEL and seq % (DILATIONS[-1] * ATT_BLOCK) == 0

    mods = _ada_mods(c, ada_w.reshape(depth * 2, d, 3 * d), ada_b.reshape(depth * 2, 3 * d))
    mods = mods.reshape(depth, 2, bsz, 1, 3 * d)
    kv_mod = _ada_mods(c, kv_ada_w[None], kv_ada_b[None]).reshape(bsz, 1, 2 * d)

    w1 = mlp_w1.astype(BF16)
    w2 = mlp_w2.astype(BF16)
    w_glu = ssm_w_glu.astype(BF16)
    w_q = attn_w_q.astype(BF16)
    w_o = attn_w_o.astype(BF16)
    perm_np = _s5_tile_perm(bsz)
    perm = jnp.asarray(perm_np, BF16)
    permt = jnp.asarray(perm_np.T, BF16)

    tables = jax.vmap(_s5_weights)(ssm_lam_re, ssm_lam_im, ssm_log_dt, ssm_b_re, ssm_b_im,
                                   ssm_c_re, ssm_c_im, ssm_d)
    h = x
    for layer in range(n_s5):
        z = _s5_pre(h, mods[layer, 0], ln_g[layer, 0], perm)
        zo = _s5_core(z, tables, layer, bsz)
        h = _post_s5(h, zo, mods[layer, 0], mods[layer, 1], ln_g[layer, 1], permt,
                     (w_glu, layer), (w1, layer), (w2, layer))

    kv_cols = tuple((i * d, (N_BRANCHES + i) * d) for i in range(N_BRANCHES))
    q_cols = tuple((i * d,) for i in range(N_BRANCHES))
    kvs = _norm_proj(h, kv_mod, kv_g, (w_kv.astype(BF16)[None], 0), kv_cols, 1.0)
    for layer in range(n_s5, depth):
        j = layer - n_s5
        qs = _norm_proj(h, mods[layer, 0], ln_g[layer, 0], (w_q, j), q_cols, Q_SCALE)
        o = _attention(qs, kvs)
        h = _post_attn(h, o, mods[layer, 0], mods[layer, 1], ln_g[layer, 1],
                       (w_o, j), (w1, layer), (w2, layer),
                       final_g if layer == depth - 1 else None)
    return h
```

```python
import functools
import math

import numpy as np

import jax
import jax.numpy as jnp
from jax import lax
from jax.experimental import pallas as pl
from jax.experimental.pallas import tpu as pltpu

F32 = jnp.float32
BF16 = jnp.bfloat16

D_MODEL = 1024
SSM_GROUP = 16
SSM_GROUPS = D_MODEL // SSM_GROUP
SSM_STATE = 64
HEAD_DIM = 64
N_HEADS = D_MODEL // HEAD_DIM
DILATIONS = (1, 4, 16)
N_BRANCHES = len(DILATIONS)
ATT_BLOCK = 128
ATT_GROUP = 16
N_STATS = 3
Q_SCALE = HEAD_DIM ** -0.5 * math.log2(math.e)
D_FF = 4 * D_MODEL
EPS = 1e-6
NEG = -1e30

LANES = 128
CHUNK = 16
GROUP_COLS = CHUNK * SSM_GROUP
PAIR_COLS = 2 * GROUP_COLS
N_PAIRS = SSM_GROUPS // 2
S5_TILE_TOKENS = 2 * CHUNK
MLP_ROWS = 512
PROJ_TILE = 512
VMEM_LIMIT = 56 * 1024 * 1024

HIGHEST = lax.Precision.HIGHEST
NT_DIMS = (((1,), (1,)), ((), ()))


def _cparams(sem):
    return pltpu.CompilerParams(dimension_semantics=sem, vmem_limit_bytes=VMEM_LIMIT)


def _layer_spec(stack, layer):
    zeros = (0,) * (stack.ndim - 1)
    return pl.BlockSpec((None,) + stack.shape[1:], lambda i: (layer,) + zeros, pipeline_mode=pl.Buffered(1))


def _normmod(x, g, shift, scale):
    ms = jnp.mean(x * x, axis=-1, keepdims=True)
    return (x * lax.rsqrt(ms + EPS) * g) * (1.0 + scale) + shift


def _ada_kernel(c_ref, w_ref, b_ref, o_ref):
    c = c_ref[...]
    sc = (c * jax.nn.sigmoid(c)).astype(BF16)
    o_ref[...] = jnp.dot(sc, w_ref[...].astype(BF16), preferred_element_type=F32) + b_ref[...]


def _ada_mods(c, w, b):
    n, d, width = w.shape
    bsz = c.shape[0]
    return pl.pallas_call(
        _ada_kernel,
        grid=(n,),
        in_specs=[
            pl.BlockSpec((bsz, d), lambda i: (0, 0)),
            pl.BlockSpec((None, d, width), lambda i: (i, 0, 0)),
            pl.BlockSpec((None, 1, width), lambda i: (i, 0, 0)),
        ],
        out_specs=pl.BlockSpec((None, bsz, width), lambda i: (i, 0, 0)),
        out_shape=jax.ShapeDtypeStruct((n, bsz, width), F32),
        compiler_params=_cparams(("parallel",)),
        name="ada_mods",
    )(c, w, b.reshape(n, 1, width))


def _proj_kernel(h_ref, mod_ref, g_ref, w_ref, o1_ref, o2_ref, o3_ref, us, us4, up4, up16, *, wcols, out_scale):
    d = D_MODEL
    tm = h_ref.shape[0]
    d1 = DILATIONS[1]
    q4, q16 = tm // d1, tm // DILATIONS[2]
    cw = o1_ref.shape[1]
    u = _normmod(h_ref[...], g_ref[...], mod_ref[:, :d], mod_ref[:, d:2 * d])
    for l in range(d // LANES):
        us[l] = u[:, l * LANES:(l + 1) * LANES]
    ub = u.astype(BF16)
    for n in range(cw // d):
        cols = slice(n * d, (n + 1) * d)
        res = jnp.dot(ub, w_ref[:, wcols[0][n]:wcols[0][n] + d], preferred_element_type=F32)
        o1_ref[:, cols] = (res * out_scale).astype(o1_ref.dtype)
    for l in range(d // LANES):
        for a in range(d1):
            part = us[l, pl.ds(a, q4, stride=d1), :]
            us4[l, a * q4:(a + 1) * q4, :] = part
            up4[a * q4:(a + 1) * q4, l * LANES:(l + 1) * LANES] = part.astype(BF16)
    for n in range(cw // d):
        res = jnp.dot(up4[...], w_ref[:, wcols[1][n]:wcols[1][n] + d], preferred_element_type=F32)
        for a in range(d1):
            o2_ref[a, :, n * d:(n + 1) * d] = (res[a * q4:(a + 1) * q4, :] * out_scale).astype(o2_ref.dtype)
    for l in range(d // LANES):
        for a1 in range(d1):
            for a2 in range(d1):
                r = d1 * a2 + a1
                up16[r * q16:(r + 1) * q16, l * LANES:(l + 1) * LANES] = us4[
                    l, pl.ds(a1 * q4 + a2, q16, stride=d1), :].astype(BF16)
    for n in range(cw // d):
        res = jnp.dot(up16[...], w_ref[:, wcols[2][n]:wcols[2][n] + d], preferred_element_type=F32)
        for r in range(DILATIONS[2]):
            o3_ref[r, :, n * d:(n + 1) * d] = (res[r * q16:(r + 1) * q16, :] * out_scale).astype(o3_ref.dtype)


def _norm_proj(h, mod, g, w, wcols, out_scale):
    bsz, seq, d = h.shape
    tm = PROJ_TILE
    cw = len(wcols[0]) * d
    per_b = seq // tm
    d1, d2 = DILATIONS[1], DILATIONS[2]
    o1, o2, o3 = pl.pallas_call(
        functools.partial(_proj_kernel, wcols=wcols, out_scale=out_scale),
        grid=(bsz * per_b,),
        in_specs=[
            pl.BlockSpec((None, tm, d), lambda i: (i // per_b, i % per_b, 0)),
            pl.BlockSpec((None, 1, mod.shape[-1]), lambda i: (i // per_b, 0, 0)),
            pl.BlockSpec((1, d), lambda i: (0, 0)),
            _layer_spec(*w),
        ],
        out_specs=[
            pl.BlockSpec((None, tm, cw), lambda i: (i // per_b, i % per_b, 0)),
            pl.BlockSpec((None, d1, None, tm // d1, cw), lambda i: (i // per_b, 0, i % per_b, 0, 0)),
            pl.BlockSpec((None, d2, None, tm // d2, cw), lambda i: (i // per_b, 0, i % per_b, 0, 0)),
        ],
        out_shape=[
            jax.ShapeDtypeStruct((bsz, seq, cw), BF16),
            jax.ShapeDtypeStruct((bsz, d1, per_b, tm // d1, cw), BF16),
            jax.ShapeDtypeStruct((bsz, d2, per_b, tm // d2, cw), BF16),
        ],
        scratch_shapes=[pltpu.VMEM((d // LANES, tm, LANES), F32), pltpu.VMEM((d // LANES, tm, LANES), F32),
                        pltpu.VMEM((tm, d), BF16), pltpu.VMEM((tm, d), BF16)],
        compiler_params=_cparams(("parallel",)),
        name="norm_proj",
    )(h, mod, g.reshape(1, d), w[0])
    return o1, o2.reshape(bsz, seq, cw), o3.reshape(bsz, seq, cw)


def _attn_group(blocks):
    blk = ATT_BLOCK
    lane = lax.broadcasted_iota(jnp.int32, (blk, LANES), 1)
    lo = lane < HEAD_DIM
    scores = []
    for qb, kcat, _, mask, _ in blocks:
        zero = jnp.zeros_like(qb)
        q2 = jnp.concatenate([jnp.where(lo, qb, zero), jnp.where(lo, zero, qb)], axis=0)
        s = lax.dot_general(q2, kcat, NT_DIMS, preferred_element_type=F32)
        scores.append(s + mask[...])
    soft = []
    for s, (_, _, _, _, old) in zip(scores, blocks):
        ms, ps = [], []
        for hh in range(2):
            tiles = [s[hh * blk:(hh + 1) * blk, t * LANES:(t + 1) * LANES] for t in range(s.shape[1] // LANES)]
            mx = tiles[0]
            for t in tiles[1:]:
                mx = jnp.maximum(mx, t)
            m = jnp.broadcast_to(jnp.max(mx, axis=-1, keepdims=True), (blk, LANES))
            if old is not None:
                m = jnp.maximum(m, old[0][hh])
            ms.append(m)
            ps.append(jnp.concatenate([jnp.exp2(t - m).astype(BF16) for t in tiles], axis=1))
        alpha = None
        if old is not None:
            alpha = jnp.exp2(jnp.where(lo, old[0][0], old[0][1]) - jnp.where(lo, ms[0], ms[1]))
        soft.append((ms, jnp.concatenate(ps, axis=0), alpha))
    outs = []
    for (ms, p, alpha), (_, _, vcat, _, old) in zip(soft, blocks):
        pv = jnp.dot(p, vcat, preferred_element_type=F32)
        acc = jnp.where(lo, pv[:blk, :LANES], pv[blk:, :LANES])
        den = jnp.where(lo, pv[:blk, LANES:], pv[blk:, LANES:])
        if old is not None:
            acc = alpha * old[1] + acc
            den = alpha * old[0][2] + den
        outs.append((ms + [den], acc))
    return outs


def _attn_kernel(q1, qd2, qd3, k1, kd2, kd3, v1, v2, v3, o_ref, tmp_nat, vd1, vd2, vd3,
                 acc_ref, st_ref, acc2_ref, st2_ref, mask_pc, mask_c):
    blk = ATT_BLOCK
    seq = q1.shape[0]
    nblk = seq // blk
    d1, d2 = DILATIONS[1], DILATIONS[2]
    quarter = seq // d1
    per_res = quarter // blk
    assert d2 == d1 * d1 and seq == d2 * blk and nblk % ATT_GROUP == 0 and ATT_GROUP % per_res == 0

    @pl.when(jnp.logical_and(pl.program_id(0) == 0, pl.program_id(1) == 0))
    def _():
        for vd in (vd1, vd2, vd3):
            vd[:, LANES:] = jnp.ones((seq, LANES), BF16)
        qq = lax.broadcasted_iota(jnp.int32, (2 * blk, 2 * blk), 0) & (blk - 1)
        kk = lax.broadcasted_iota(jnp.int32, (2 * blk, 2 * blk), 1)
        valid = jnp.logical_or(jnp.logical_and(kk < blk, kk >= qq), jnp.logical_and(kk >= blk, kk - blk <= qq))
        mask_pc[...] = jnp.where(valid, 0.0, NEG)
        valid_c = (lax.broadcasted_iota(jnp.int32, (2 * blk, blk), 1)
                   <= (lax.broadcasted_iota(jnp.int32, (2 * blk, blk), 0) & (blk - 1)))
        mask_c[...] = jnp.where(valid_c, 0.0, NEG)

    for v, vd in ((v1, vd1), (v2, vd2), (v3, vd3)):
        vd[:, :LANES] = v[...]

    def run_group(blocks, dests):
        for (st_dst, acc_dst, rows), (st, acc) in zip(dests, _attn_group(blocks)):
            for k in range(N_STATS):
                st_dst[k, rows, :] = st[k]
            acc_dst[rows, :] = acc

    def load_old(st_src, acc_src, rows):
        return [st_src[k, rows, :] for k in range(N_STATS)], acc_src[rows, :]

    def b0_group(ns, first):
        blocks, dests = [], []
        for n in ns:
            if first and n == 0:
                rows = pl.ds(0, blk)
                blocks.append((q1[rows, :], k1[rows, :], vd1[rows, :], mask_c, None))
            else:
                rows = pl.ds(pl.multiple_of(n * blk, blk), blk)
                krows = pl.ds(pl.multiple_of((n - 1) * blk, blk), 2 * blk)
                blocks.append((q1[rows, :], k1[krows, :], vd1[krows, :], mask_pc, None))
            dests.append((st_ref, acc_ref, rows))
        run_group(blocks, dests)

    b0_group(list(range(ATT_GROUP)), True)

    def b0_body(i, carry):
        b0_group([ATT_GROUP * i + j for j in range(ATT_GROUP)], False)
        return carry

    lax.fori_loop(1, nblk // ATT_GROUP, b0_body, 0)

    res_per_trip = ATT_GROUP // per_res

    def b1_body(i, carry):
        blocks, dests = [], []
        for jr in range(res_per_trip):
            a1 = res_per_trip * i + jr
            base = a1 * quarter
            for n in range(per_res):
                rows = pl.ds(a1 + d1 * blk * n, blk, stride=d1)
                qrows = pl.ds(pl.multiple_of(base + n * blk, blk), blk)
                if n == 0:
                    krows, mask = qrows, mask_c
                else:
                    krows, mask = pl.ds(pl.multiple_of(base + (n - 1) * blk, blk), 2 * blk), mask_pc
                blocks.append((qd2[qrows, :], kd2[krows, :], vd2[krows, :], mask,
                               load_old(st_ref, acc_ref, rows)))
                dests.append((st2_ref, acc2_ref, qrows))
        run_group(blocks, dests)
        return carry

    lax.fori_loop(0, d1 // res_per_trip, b1_body, 0)

    def b2_body(i, carry):
        blocks, dests = [], []
        for j in range(ATT_GROUP):
            a1, a2 = j % d1, (ATT_GROUP // d1) * i + j // d1
            qrows = pl.ds(pl.multiple_of((ATT_GROUP * i + j) * blk, blk), blk)
            rows = pl.ds(a1 * quarter + a2, blk, stride=d1)
            blocks.append((qd3[qrows, :], kd3[qrows, :], vd3[qrows, :], mask_c,
                           load_old(st2_ref, acc2_ref, rows)))
            dests.append((st2_ref, acc2_ref, rows))
        run_group(blocks, dests)
        return carry

    lax.fori_loop(0, d2 // ATT_GROUP, b2_body, 0)

    for a in range(d1):
        seg = slice(a * quarter, (a + 1) * quarter)
        tmp_nat[pl.ds(a, quarter, stride=d1), :] = acc2_ref[seg, :] / st2_ref[N_STATS - 1, seg, :]
    o_ref[...] = tmp_nat[...].astype(o_ref.dtype)


def _attention(qs, kvs):
    bsz, seq, _ = qs[0].shape
    pairs = D_MODEL // LANES

    def slab(col0):
        return pl.BlockSpec((None, seq, LANES), lambda b, hp: (b, 0, col0 + hp))

    in_specs = [slab(0)] * N_BRANCHES + [slab(0)] * N_BRANCHES + [slab(pairs)] * N_BRANCHES
    return pl.pallas_call(
        _attn_kernel,
        grid=(bsz, pairs),
        in_specs=in_specs,
        out_specs=slab(0),
        out_shape=jax.ShapeDtypeStruct((bsz, seq, D_MODEL), BF16),
        scratch_shapes=([pltpu.VMEM((seq, LANES), F32)] + [pltpu.VMEM((seq, 2 * LANES), BF16)] * 3
                        + [pltpu.VMEM((seq, LANES), F32), pltpu.VMEM((N_STATS, seq, LANES), F32)] * 2
                        + [pltpu.VMEM((2 * ATT_BLOCK, 2 * ATT_BLOCK), F32), pltpu.VMEM((2 * ATT_BLOCK, ATT_BLOCK), F32)]),
        compiler_params=_cparams(("arbitrary", "arbitrary")),
        name="attention",
    )(*qs, *kvs, *kvs)


def _regroup8(tiles):
    t = list(tiles)
    lane = lax.broadcasted_iota(jnp.int32, t[0].shape, 1)
    piece = lane // SSM_GROUP
    for dist in (4, 2, 1):
        bit = (piece & dist) != 0
        shift = dist * SSM_GROUP
        for a in range(8):
            if a & dist:
                continue
            b = a | dist
            ta, tb = t[a], t[b]
            t[a] = jnp.where(bit, pltpu.roll(tb, shift, 1), ta)
            t[b] = jnp.where(bit, tb, pltpu.roll(ta, LANES - shift, 1))
    return t


def _s5_tile_perm(bsz):
    tt = S5_TILE_TOKENS
    n = bsz * tt
    perm = np.zeros((n, n), np.float32)
    for b in range(bsz):
        for c in range(tt // CHUNK):
            for t in range(CHUNK):
                perm[(t * (tt // CHUNK) + c) * bsz + b, b * tt + c * CHUNK + t] = 1.0
    return perm


def _s5_pre_kernel(h_ref, mod_ref, g_ref, perm_ref, z_ref, u_scr, up_scr):
    d = D_MODEL
    bsz, tt, _ = h_ref.shape
    g = g_ref[...]
    for b in range(bsz):
        u_scr[b * tt:(b + 1) * tt, :] = _normmod(
            h_ref[b], g, mod_ref[b, :, :d], mod_ref[b, :, d:2 * d]).astype(BF16)
    up_scr[...] = jnp.dot(perm_ref[...], u_scr[...], preferred_element_type=F32)
    rows = z_ref.shape[0]
    for j in range(d // LANES):
        for half in range(CHUNK // 8):
            tiles = [up_scr[(8 * half + tl) * rows:(8 * half + tl + 1) * rows, j * LANES:(j + 1) * LANES]
                     for tl in range(8)]
            outs = _regroup8(tiles)
            for gl in range(8):
                col = (8 * j + gl) * GROUP_COLS + half * LANES
                z_ref[:, col:col + LANES] = outs[gl].astype(z_ref.dtype)


def _s5_pre(h, mod, g, perm):
    bsz, seq, d = h.shape
    tt = S5_TILE_TOKENS
    rows = bsz * tt // CHUNK
    return pl.pallas_call(
        _s5_pre_kernel,
        grid=(seq // tt,),
        in_specs=[
            pl.BlockSpec((bsz, tt, d), lambda i: (0, i, 0)),
            pl.BlockSpec(mod.shape, lambda i: (0, 0, 0)),
            pl.BlockSpec((1, d), lambda i: (0, 0)),
            pl.BlockSpec(perm.shape, lambda i: (0, 0)),
        ],
        out_specs=pl.BlockSpec((rows, CHUNK * d), lambda i: (i, 0)),
        out_shape=jax.ShapeDtypeStruct((bsz * seq // CHUNK, CHUNK * d), BF16),
        scratch_shapes=[pltpu.VMEM((bsz * tt, d), BF16), pltpu.VMEM((bsz * tt, d), F32)],
        compiler_params=_cparams(("parallel",)),
        name="s5_pre",
    )(h, mod, g.reshape(1, d), perm)


def _s5_core_kernel(z_ref, win_ref, wt_ref, wx_ref, dec_ref, d_ref, o_ref, s_scr, x_scr, *, bsz):
    gc = GROUP_COLS
    nchunk = z_ref.shape[0] // bsz
    s_scr[...] = (jnp.dot(z_ref[:, :gc], win_ref[0], preferred_element_type=F32)
                  + jnp.dot(z_ref[:, gc:], win_ref[1], preferred_element_type=F32))
    ar, ai = dec_ref[:, :LANES], dec_ref[:, LANES:]

    def step(c, carry):
        xr, xi = carry
        rows = pl.ds(pl.multiple_of(c * bsz, bsz), bsz)
        x_scr[rows, :LANES] = xr
        x_scr[rows, LANES:] = xi
        return (ar * xr - ai * xi + s_scr[rows, :LANES], ar * xi + ai * xr + s_scr[rows, LANES:])

    zero = jnp.zeros((bsz, LANES), F32)
    lax.fori_loop(0, nchunk, step, (zero, zero))
    x = x_scr[...].astype(BF16)
    for gi in range(2):
        z = z_ref[:, gi * gc:(gi + 1) * gc]
        y = (jnp.dot(z, wt_ref[gi], preferred_element_type=F32)
             + jnp.dot(x, wx_ref[gi], preferred_element_type=F32))
        y = y + d_ref[:, gi * gc:(gi + 1) * gc] * z.astype(F32)
        o_ref[:, gi * gc:(gi + 1) * gc] = jax.nn.gelu(y).astype(o_ref.dtype)


def _s5_core(z, tables, layer, bsz):
    nrow, width = z.shape
    w_in, w_toep, w_x, dec, d_perm = tables
    wspec = pl.BlockSpec((None, 2, GROUP_COLS, GROUP_COLS), lambda k: (layer, k, 0, 0))
    return pl.pallas_call(
        functools.partial(_s5_core_kernel, bsz=bsz),
        grid=(N_PAIRS,),
        in_specs=[
            pl.BlockSpec((nrow, PAIR_COLS), lambda k: (0, k)),
            wspec, wspec, wspec,
            pl.BlockSpec((None, 1, GROUP_COLS), lambda k: (layer, 0, k)),
            pl.BlockSpec((None, 1, PAIR_COLS), lambda k: (layer, 0, k)),
        ],
        out_specs=pl.BlockSpec((nrow, PAIR_COLS), lambda k: (0, k)),
        out_shape=jax.ShapeDtypeStruct((nrow, width), BF16),
        scratch_shapes=[pltpu.VMEM((nrow, GROUP_COLS), F32), pltpu.VMEM((nrow, GROUP_COLS), F32)],
        compiler_params=_cparams(("parallel",)),
        name="s5_core",
    )(z, w_in, w_toep, w_x, dec, d_perm)


def _s5_weights(lam_re, lam_im, log_dt, b_re, b_im, c_re, c_im, d_skip):
    g, p, c16 = SSM_GROUPS, SSM_STATE, SSM_GROUP
    lam = lax.complex(lam_re.astype(F32), lam_im.astype(F32))
    dt = jnp.exp(log_dt.astype(F32))[:, None]
    steps = jnp.arange(CHUNK + 1, dtype=F32)
    apow = jnp.exp((lam * dt)[None] * steps[:, None, None])
    a = apow[1]
    bbar = ((a - 1.0) / lam)[..., None] * lax.complex(b_re.astype(F32), b_im.astype(F32))
    cmat = lax.complex(c_re.astype(F32), c_im.astype(F32))
    odd = (jnp.arange(g) % 2)[:, None, None]

    win = apow[CHUNK - 1::-1][..., None] * bbar[None]

    def lay_in(x):
        return jnp.transpose(x.astype(BF16), (1, 0, 3, 2)).reshape(g, GROUP_COLS, p)

    win_re, win_im = lay_in(win.real), lay_in(win.imag)
    zeros = jnp.zeros_like(win_re)
    w_in = jnp.concatenate([
        jnp.where(odd == 0, win_re, zeros), jnp.where(odd == 1, win_re, zeros),
        jnp.where(odd == 0, win_im, zeros), jnp.where(odd == 1, win_im, zeros)], axis=-1)

    kern = jnp.einsum('gop,kgp,gpi->kgoi', cmat, apow[:CHUNK], bbar, precision=HIGHEST).real
    idx = jnp.arange(CHUNK)
    onehot = (idx[None, None, :] - idx[None, :, None] == idx[:, None, None]).astype(BF16)
    w_toep = jnp.einsum('kst,kgoi->gsito', onehot, kern.astype(BF16),
                        preferred_element_type=BF16)
    w_toep = w_toep.reshape(g, GROUP_COLS, GROUP_COLS)

    cw = cmat[None] * apow[1:, :, None, :]

    def lay_x(x):
        return jnp.transpose(x.astype(BF16), (1, 3, 0, 2)).reshape(g, p, GROUP_COLS)

    cw_re, cw_im = lay_x(cw.real), lay_x(-cw.imag)
    zx = jnp.zeros_like(cw_re)
    w_x = jnp.concatenate([
        jnp.where(odd == 0, cw_re, zx), jnp.where(odd == 1, cw_re, zx),
        jnp.where(odd == 0, cw_im, zx), jnp.where(odd == 1, cw_im, zx)], axis=1)

    a16 = apow[CHUNK].reshape(N_PAIRS, 2 * p)
    dec = jnp.concatenate([a16.real, a16.imag], axis=-1).reshape(1, N_PAIRS * GROUP_COLS)
    d_perm = jnp.broadcast_to(d_skip.astype(F32).reshape(g, 1, c16), (g, CHUNK, c16)).reshape(1, g * GROUP_COLS)
    return w_in, w_toep, w_x, dec, d_perm


def _mlp(u, w1_ref, w2_ref):
    d = D_MODEL
    acc = None
    for k in range(D_FF // d):
        a = jnp.dot(u, w1_ref[:, k * d:(k + 1) * d], preferred_element_type=F32)
        a = jnp.square(jnp.maximum(a, 0.0)).astype(BF16)
        part = jnp.dot(a, w2_ref[k * d:(k + 1) * d, :], preferred_element_type=F32)
        acc = part if acc is None else acc + part
    return acc


def _final_norm(h, fg_ref):
    ms = jnp.mean(h * h, axis=-1, keepdims=True)
    return h * lax.rsqrt(ms + EPS) * fg_ref[...]


def _post_attn_kernel(*refs, final):
    if final:
        h_ref, y_ref, moda_ref, modm_ref, g_ref, wp_ref, w1_ref, w2_ref, fg_ref, o_ref = refs
    else:
        h_ref, y_ref, moda_ref, modm_ref, g_ref, wp_ref, w1_ref, w2_ref, o_ref = refs
    d = D_MODEL
    ymix = jnp.dot(y_ref[...], wp_ref[...], preferred_element_type=F32)
    h1 = h_ref[...] + moda_ref[:, 2 * d:3 * d] * ymix
    u = _normmod(h1, g_ref[...], modm_ref[:, :d], modm_ref[:, d:2 * d]).astype(BF16)
    h2 = h1 + modm_ref[:, 2 * d:3 * d] * _mlp(u, w1_ref, w2_ref)
    o_ref[...] = _final_norm(h2, fg_ref) if final else h2


def _post_attn(h, o, moda, modm, g2, w_o, w1, w2, final_g):
    bsz, seq, d = h.shape
    tm = MLP_ROWS
    per_b = seq // tm
    row_spec = pl.BlockSpec((None, tm, d), lambda i: (i // per_b, i % per_b, 0))
    final = final_g is not None
    in_specs = [
        row_spec, row_spec,
        pl.BlockSpec((None, 1, moda.shape[-1]), lambda i: (i // per_b, 0, 0)),
        pl.BlockSpec((None, 1, modm.shape[-1]), lambda i: (i // per_b, 0, 0)),
        pl.BlockSpec((1, d), lambda i: (0, 0)),
        _layer_spec(*w_o), _layer_spec(*w1), _layer_spec(*w2),
    ]
    args = [h, o, moda, modm, g2.reshape(1, d), w_o[0], w1[0], w2[0]]
    if final:
        in_specs.append(pl.BlockSpec((1, d), lambda i: (0, 0)))
        args.append(final_g.reshape(1, d))
    return pl.pallas_call(
        functools.partial(_post_attn_kernel, final=final),
        grid=(bsz * per_b,),
        in_specs=in_specs,
        out_specs=row_spec,
        out_shape=jax.ShapeDtypeStruct((bsz, seq, d), F32),
        compiler_params=_cparams(("parallel",)),
        name="post_attn_mlp",
    )(*args)


def _post_s5_kernel(h_ref, z_ref, moda_ref, modm_ref, g_ref, permt_ref, wp_ref, w1_ref, w2_ref, o_ref,
                    z_scr, h1_scr, u_scr):
    d = D_MODEL
    bsz, tt, _ = h_ref.shape
    rows = z_ref.shape[0]
    for j in range(d // LANES):
        for half in range(CHUNK // 8):
            tiles = []
            for gl in range(8):
                col = (8 * j + gl) * GROUP_COLS + half * LANES
                tiles.append(z_ref[:, col:col + LANES].astype(F32))
            outs = _regroup8(tiles)
            for tl in range(8):
                t = 8 * half + tl
                z_scr[t * rows:(t + 1) * rows, j * LANES:(j + 1) * LANES] = outs[tl].astype(BF16)
    zn = jnp.dot(permt_ref[...], z_scr[...], preferred_element_type=F32).astype(BF16)
    zz = jnp.dot(zn, wp_ref[...], preferred_element_type=F32)
    ymix = zz[:, :d] * jax.nn.sigmoid(zz[:, d:])
    g = g_ref[...]
    for b in range(bsz):
        sl = slice(b * tt, (b + 1) * tt)
        h1 = h_ref[b] + moda_ref[b, :, 2 * d:3 * d] * ymix[sl, :]
        h1_scr[sl, :] = h1
        u_scr[sl, :] = _normmod(h1, g, modm_ref[b, :, :d], modm_ref[b, :, d:2 * d]).astype(BF16)
    acc = _mlp(u_scr[...], w1_ref, w2_ref)
    for b in range(bsz):
        sl = slice(b * tt, (b + 1) * tt)
        o_ref[b] = h1_scr[sl, :] + modm_ref[b, :, 2 * d:3 * d] * acc[sl, :]


def _post_s5(h, zo, moda, modm, g2, permt, w_glu, w1, w2):
    bsz, seq, d = h.shape
    tt = S5_TILE_TOKENS
    rows = bsz * tt // CHUNK
    h_spec = pl.BlockSpec((bsz, tt, d), lambda i: (0, i, 0))
    return pl.pallas_call(
        _post_s5_kernel,
        grid=(seq // tt,),
        in_specs=[
            h_spec,
            pl.BlockSpec((rows, CHUNK * d), lambda i: (i, 0)),
            pl.BlockSpec(moda.shape, lambda i: (0, 0, 0)),
            pl.BlockSpec(modm.shape, lambda i: (0, 0, 0)),
            pl.BlockSpec((1, d), lambda i: (0, 0)),
            pl.BlockSpec(permt.shape, lambda i: (0, 0)),
            _layer_spec(*w_glu), _layer_spec(*w1), _layer_spec(*w2),
        ],
        out_specs=h_spec,
        out_shape=jax.ShapeDtypeStruct((bsz, seq, d), F32),
        scratch_shapes=[pltpu.VMEM((bsz * tt, d), BF16), pltpu.VMEM((bsz * tt, d), F32),
                        pltpu.VMEM((bsz * tt, d), BF16)],
        compiler_params=_cparams(("parallel",)),
        name="post_s5_mlp",
    )(h, zo, moda, modm, g2.reshape(1, d), permt, w_glu[0], w1[0], w2[0])


def kernel(x, c, ln_g, ada_w, ada_b, ssm_lam_re, ssm_lam_im, ssm_log_dt, ssm_b_re, ssm_b_im, ssm_c_re, ssm_c_im, ssm_d, ssm_w_glu, kv_g, kv_ada_w, kv_ada_b, w_kv, attn_w_q, attn_w_o, mlp_w1, mlp_w2, final_g):
    bsz, seq, d = x.shape
    depth = ln_g.shape[0]
    n_s5 = ssm_lam_re.shape[0]
    assert d == D_MODEL and seq % (DILATIONS[-1] * ATT_BLOCK) == 0

    mods = _ada_mods(c, ada_w.reshape(depth * 2, d, 3 * d), ada_b.reshape(depth * 2, 3 * d))
    mods = mods.reshape(depth, 2, bsz, 1, 3 * d)
    kv_mod = _ada_mods(c, kv_ada_w[None], kv_ada_b[None]).reshape(bsz, 1, 2 * d)

    w1 = mlp_w1.astype(BF16)
    w2 = mlp_w2.astype(BF16)
    w_glu = ssm_w_glu.astype(BF16)
    w_q = attn_w_q.astype(BF16)
    w_o = attn_w_o.astype(BF16)
    perm_np = _s5_tile_perm(bsz)
    perm = jnp.asarray(perm_np, BF16)
    permt = jnp.asarray(perm_np.T, BF16)

    tables = jax.vmap(_s5_weights)(ssm_lam_re, ssm_lam_im, ssm_log_dt, ssm_b_re, ssm_b_im,
                                   ssm_c_re, ssm_c_im, ssm_d)
    h = x
    for layer in range(n_s5):
        z = _s5_pre(h, mods[layer, 0], ln_g[layer, 0], perm)
        zo = _s5_core(z, tables, layer, bsz)
        h = _post_s5(h, zo, mods[layer, 0], mods[layer, 1], ln_g[layer, 1], permt,
                     (w_glu, layer), (w1, layer), (w2, layer))

    kv_cols = tuple((i * d, (N_BRANCHES + i) * d) for i in range(N_BRANCHES))
    q_cols = tuple((i * d,) for i in range(N_BRANCHES))
    kvs = _norm_proj(h, kv_mod, kv_g, (w_kv.astype(BF16)[None], 0), kv_cols, 1.0)
    for layer in range(n_s5, depth):
        j = layer - n_s5
        qs = _norm_proj(h, mods[layer, 0], ln_g[layer, 0], (w_q, j), q_cols, Q_SCALE)
        o = _attention(qs, kvs)
        h = _post_attn(h, o, mods[layer, 0], mods[layer, 1], ln_g[layer, 1],
                       (w_o, j), (w1, layer), (w2, layer),
                       final_g if layer == depth - 1 else None)
    return h
```

```python
import functools
import math

import numpy as np

import jax
import jax.numpy as jnp
from jax import lax
from jax.experimental import pallas as pl
from jax.experimental.pallas import tpu as pltpu

F32 = jnp.float32
BF16 = jnp.bfloat16

D_MODEL = 1024
SSM_GROUP = 16
SSM_GROUPS = D_MODEL // SSM_GROUP
SSM_STATE = 64
HEAD_DIM = 64
N_HEADS = D_MODEL // HEAD_DIM
DILATIONS = (1, 4, 16)
N_BRANCHES = len(DILATIONS)
ATT_BLOCK = 128
ATT_GROUP = 16
ATT_PAIRS_PER_STEP = 2
N_STATS = 3
Q_SCALE = HEAD_DIM ** -0.5 * math.log2(math.e)
D_FF = 4 * D_MODEL
EPS = 1e-6
NEG = -1e30

LANES = 128
CHUNK = 16
GROUP_COLS = CHUNK * SSM_GROUP
PAIR_COLS = 2 * GROUP_COLS
N_PAIRS = SSM_GROUPS // 2
S5_TILE_TOKENS = 2 * CHUNK
MLP_ROWS = 512
KV_PROJ_TILE = 512
Q_PROJ_TILE = 1024
VMEM_LIMIT = 56 * 1024 * 1024

HIGHEST = lax.Precision.HIGHEST
NT_DIMS = (((1,), (1,)), ((), ()))


def _cparams(sem):
    return pltpu.CompilerParams(dimension_semantics=sem, vmem_limit_bytes=VMEM_LIMIT)


def _layer_spec(stack, layer):
    zeros = (0,) * (stack.ndim - 1)
    return pl.BlockSpec((None,) + stack.shape[1:], lambda i: (layer,) + zeros, pipeline_mode=pl.Buffered(1))


def _normmod(x, g, shift, scale):
    ms = jnp.mean(x * x, axis=-1, keepdims=True)
    return (x * lax.rsqrt(ms + EPS) * g) * (1.0 + scale) + shift


def _ada_kernel(c_ref, w_ref, b_ref, o_ref):
    c = c_ref[...]
    sc = (c * jax.nn.sigmoid(c)).astype(BF16)
    o_ref[...] = jnp.dot(sc, w_ref[...].astype(BF16), preferred_element_type=F32) + b_ref[...]


def _ada_mods(c, w, b):
    n, d, width = w.shape
    bsz = c.shape[0]
    return pl.pallas_call(
        _ada_kernel,
        grid=(n,),
        in_specs=[
            pl.BlockSpec((bsz, d), lambda i: (0, 0)),
            pl.BlockSpec((None, d, width), lambda i: (i, 0, 0)),
            pl.BlockSpec((None, 1, width), lambda i: (i, 0, 0)),
        ],
        out_specs=pl.BlockSpec((None, bsz, width), lambda i: (i, 0, 0)),
        out_shape=jax.ShapeDtypeStruct((n, bsz, width), F32),
        compiler_params=_cparams(("parallel",)),
        name="ada_mods",
    )(c, w, b.reshape(n, 1, width))


def _proj_kernel(h_ref, mod_ref, g_ref, w_ref, o1_ref, o2_ref, o3_ref, us, us4, up4, up16, *, wcols, out_scale):
    d = D_MODEL
    tm = h_ref.shape[0]
    d1 = DILATIONS[1]
    q4, q16 = tm // d1, tm // DILATIONS[2]
    cw = o1_ref.shape[1]
    u = _normmod(h_ref[...], g_ref[...], mod_ref[:, :d], mod_ref[:, d:2 * d])
    for l in range(d // LANES):
        us[l] = u[:, l * LANES:(l + 1) * LANES]
    ub = u.astype(BF16)
    for n in range(cw // d):
        cols = slice(n * d, (n + 1) * d)
        res = jnp.dot(ub, w_ref[:, wcols[0][n]:wcols[0][n] + d], preferred_element_type=F32)
        o1_ref[:, cols] = (res * out_scale).astype(o1_ref.dtype)
    for l in range(d // LANES):
        for a in range(d1):
            part = us[l, pl.ds(a, q4, stride=d1), :]
            us4[l, a * q4:(a + 1) * q4, :] = part
            up4[a * q4:(a + 1) * q4, l * LANES:(l + 1) * LANES] = part.astype(BF16)
    for n in range(cw // d):
        res = jnp.dot(up4[...], w_ref[:, wcols[1][n]:wcols[1][n] + d], preferred_element_type=F32)
        for a in range(d1):
            o2_ref[a, :, n * d:(n + 1) * d] = (res[a * q4:(a + 1) * q4, :] * out_scale).astype(o2_ref.dtype)
    for l in range(d // LANES):
        for a1 in range(d1):
            for a2 in range(d1):
                r = d1 * a2 + a1
                up16[r * q16:(r + 1) * q16, l * LANES:(l + 1) * LANES] = us4[
                    l, pl.ds(a1 * q4 + a2, q16, stride=d1), :].astype(BF16)
    for n in range(cw // d):
        res = jnp.dot(up16[...], w_ref[:, wcols[2][n]:wcols[2][n] + d], preferred_element_type=F32)
        for r in range(DILATIONS[2]):
            o3_ref[r, :, n * d:(n + 1) * d] = (res[r * q16:(r + 1) * q16, :] * out_scale).astype(o3_ref.dtype)


def _norm_proj(h, mod, g, w, wcols, out_scale, tm):
    bsz, seq, d = h.shape
    cw = len(wcols[0]) * d
    per_b = seq // tm
    d1, d2 = DILATIONS[1], DILATIONS[2]
    o1, o2, o3 = pl.pallas_call(
        functools.partial(_proj_kernel, wcols=wcols, out_scale=out_scale),
        grid=(bsz * per_b,),
        in_specs=[
            pl.BlockSpec((None, tm, d), lambda i: (i // per_b, i % per_b, 0)),
            pl.BlockSpec((None, 1, mod.shape[-1]), lambda i: (i // per_b, 0, 0)),
            pl.BlockSpec((1, d), lambda i: (0, 0)),
            _layer_spec(*w),
        ],
        out_specs=[
            pl.BlockSpec((None, tm, cw), lambda i: (i // per_b, i % per_b, 0)),
            pl.BlockSpec((None, d1, None, tm // d1, cw), lambda i: (i // per_b, 0, i % per_b, 0, 0)),
            pl.BlockSpec((None, d2, None, tm // d2, cw), lambda i: (i // per_b, 0, i % per_b, 0, 0)),
        ],
        out_shape=[
            jax.ShapeDtypeStruct((bsz, seq, cw), BF16),
            jax.ShapeDtypeStruct((bsz, d1, per_b, tm // d1, cw), BF16),
            jax.ShapeDtypeStruct((bsz, d2, per_b, tm // d2, cw), BF16),
        ],
        scratch_shapes=[pltpu.VMEM((d // LANES, tm, LANES), F32), pltpu.VMEM((d // LANES, tm, LANES), F32),
                        pltpu.VMEM((tm, d), BF16), pltpu.VMEM((tm, d), BF16)],
        compiler_params=_cparams(("parallel",)),
        name="norm_proj",
    )(h, mod, g.reshape(1, d), w[0])
    return o1, o2.reshape(bsz, seq, cw), o3.reshape(bsz, seq, cw)


def _attn_group(blocks):
    blk = ATT_BLOCK
    lane = lax.broadcasted_iota(jnp.int32, (blk, LANES), 1)
    lo = lane < HEAD_DIM
    scores = []
    for qb, kcat, _, mask, _ in blocks:
        zero = jnp.zeros_like(qb)
        q2 = jnp.concatenate([jnp.where(lo, qb, zero), jnp.where(lo, zero, qb)], axis=0)
        s = lax.dot_general(q2, kcat, NT_DIMS, preferred_element_type=F32)
        scores.append(s + mask[...])
    soft = []
    for s, (_, _, _, _, old) in zip(scores, blocks):
        ms, ps = [], []
        for hh in range(2):
            tiles = [s[hh * blk:(hh + 1) * blk, t * LANES:(t + 1) * LANES] for t in range(s.shape[1] // LANES)]
            mx = tiles[0]
            for t in tiles[1:]:
                mx = jnp.maximum(mx, t)
            m = jnp.broadcast_to(jnp.max(mx, axis=-1, keepdims=True), (blk, LANES))
            if old is not None:
                m = jnp.maximum(m, old[0][hh])
            ms.append(m)
            ps.append(jnp.concatenate([jnp.exp2(t - m).astype(BF16) for t in tiles], axis=1))
        alpha = None
        if old is not None:
            alpha = jnp.exp2(jnp.where(lo, old[0][0], old[0][1]) - jnp.where(lo, ms[0], ms[1]))
        soft.append((ms, jnp.concatenate(ps, axis=0), alpha))
    outs = []
    for (ms, p, alpha), (_, _, vcat, _, old) in zip(soft, blocks):
        pv = jnp.dot(p, vcat, preferred_element_type=F32)
        acc = jnp.where(lo, pv[:blk, :LANES], pv[blk:, :LANES])
        den = jnp.where(lo, pv[:blk, LANES:], pv[blk:, LANES:])
        if old is not None:
            acc = alpha * old[1] + acc
            den = alpha * old[0][2] + den
        outs.append((ms + [den], acc))
    return outs


def _attn_kernel(*refs):
    n_in = 3 * N_BRANCHES
    ins, o_ref, scratch = refs[:n_in], refs[n_in], refs[n_in + 1:]
    for half in range(ATT_PAIRS_PER_STEP):
        lanes = pl.ds(half * LANES, LANES)
        _attn_pair(*[r.at[:, lanes] for r in ins], o_ref.at[:, lanes], *scratch)


def _attn_pair(q1, qd2, qd3, k1, kd2, kd3, v1, v2, v3, o_ref, tmp_nat, vd1, vd2, vd3,
               acc_ref, st_ref, acc2_ref, st2_ref, mask_pc, mask_c):
    blk = ATT_BLOCK
    seq = q1.shape[0]
    nblk = seq // blk
    d1, d2 = DILATIONS[1], DILATIONS[2]
    quarter = seq // d1
    per_res = quarter // blk
    assert d2 == d1 * d1 and seq == d2 * blk and nblk % ATT_GROUP == 0 and ATT_GROUP % per_res == 0

    @pl.when(jnp.logical_and(pl.program_id(0) == 0, pl.program_id(1) == 0))
    def _():
        for vd in (vd1, vd2, vd3):
            vd[:, LANES:] = jnp.ones((seq, LANES), BF16)
        qq = lax.broadcasted_iota(jnp.int32, (2 * blk, 2 * blk), 0) & (blk - 1)
        kk = lax.broadcasted_iota(jnp.int32, (2 * blk, 2 * blk), 1)
        valid = jnp.logical_or(jnp.logical_and(kk < blk, kk >= qq), jnp.logical_and(kk >= blk, kk - blk <= qq))
        mask_pc[...] = jnp.where(valid, 0.0, NEG)
        valid_c = (lax.broadcasted_iota(jnp.int32, (2 * blk, blk), 1)
                   <= (lax.broadcasted_iota(jnp.int32, (2 * blk, blk), 0) & (blk - 1)))
        mask_c[...] = jnp.where(valid_c, 0.0, NEG)

    for v, vd in ((v1, vd1), (v2, vd2), (v3, vd3)):
        vd[:, :LANES] = v[...]

    def run_group(blocks, dests):
        for (st_dst, acc_dst, rows), (st, acc) in zip(dests, _attn_group(blocks)):
            for k in range(N_STATS):
                st_dst[k, rows, :] = st[k]
            acc_dst[rows, :] = acc

    def load_old(st_src, acc_src, rows):
        return [st_src[k, rows, :] for k in range(N_STATS)], acc_src[rows, :]

    def b0_group(ns, first):
        blocks, dests = [], []
        for n in ns:
            if first and n == 0:
                rows = pl.ds(0, blk)
                blocks.append((q1[rows, :], k1[rows, :], vd1[rows, :], mask_c, None))
            else:
                rows = pl.ds(pl.multiple_of(n * blk, blk), blk)
                krows = pl.ds(pl.multiple_of((n - 1) * blk, blk), 2 * blk)
                blocks.append((q1[rows, :], k1[krows, :], vd1[krows, :], mask_pc, None))
            dests.append((st_ref, acc_ref, rows))
        run_group(blocks, dests)

    b0_group(list(range(ATT_GROUP)), True)

    def b0_body(i, carry):
        b0_group([ATT_GROUP * i + j for j in range(ATT_GROUP)], False)
        return carry

    lax.fori_loop(1, nblk // ATT_GROUP, b0_body, 0)

    res_per_trip = ATT_GROUP // per_res

    def b1_body(i, carry):
        blocks, dests = [], []
        for jr in range(res_per_trip):
            a1 = res_per_trip * i + jr
            base = a1 * quarter
            for n in range(per_res):
                rows = pl.ds(a1 + d1 * blk * n, blk, stride=d1)
                qrows = pl.ds(pl.multiple_of(base + n * blk, blk), blk)
                if n == 0:
                    krows, mask = qrows, mask_c
                else:
                    krows, mask = pl.ds(pl.multiple_of(base + (n - 1) * blk, blk), 2 * blk), mask_pc
                blocks.append((qd2[qrows, :], kd2[krows, :], vd2[krows, :], mask,
                               load_old(st_ref, acc_ref, rows)))
                dests.append((st2_ref, acc2_ref, qrows))
        run_group(blocks, dests)
        return carry

    lax.fori_loop(0, d1 // res_per_trip, b1_body, 0)

    def b2_body(i, carry):
        blocks, dests = [], []
        for j in range(ATT_GROUP):
            a1, a2 = j % d1, (ATT_GROUP // d1) * i + j // d1
            qrows = pl.ds(pl.multiple_of((ATT_GROUP * i + j) * blk, blk), blk)
            rows = pl.ds(a1 * quarter + a2, blk, stride=d1)
            blocks.append((qd3[qrows, :], kd3[qrows, :], vd3[qrows, :], mask_c,
                           load_old(st2_ref, acc2_ref, rows)))
            dests.append((st2_ref, acc2_ref, rows))
        run_group(blocks, dests)
        return carry

    lax.fori_loop(0, d2 // ATT_GROUP, b2_body, 0)

    for a in range(d1):
        seg = slice(a * quarter, (a + 1) * quarter)
        tmp_nat[pl.ds(a, quarter, stride=d1), :] = acc2_ref[seg, :] / st2_ref[N_STATS - 1, seg, :]
    o_ref[...] = tmp_nat[...].astype(o_ref.dtype)


def _attention(qs, kvs):
    bsz, seq, _ = qs[0].shape
    steps = D_MODEL // (ATT_PAIRS_PER_STEP * LANES)

    def slab(col0):
        return pl.BlockSpec((None, seq, ATT_PAIRS_PER_STEP * LANES), lambda b, hp: (b, 0, col0 + hp))

    in_specs = [slab(0)] * N_BRANCHES + [slab(0)] * N_BRANCHES + [slab(steps)] * N_BRANCHES
    return pl.pallas_call(
        _attn_kernel,
        grid=(bsz, steps),
        in_specs=in_specs,
        out_specs=slab(0),
        out_shape=jax.ShapeDtypeStruct((bsz, seq, D_MODEL), BF16),
        scratch_shapes=([pltpu.VMEM((seq, LANES), F32)] + [pltpu.VMEM((seq, 2 * LANES), BF16)] * 3
                        + [pltpu.VMEM((seq, LANES), F32), pltpu.VMEM((N_STATS, seq, LANES), F32)] * 2
                        + [pltpu.VMEM((2 * ATT_BLOCK, 2 * ATT_BLOCK), F32), pltpu.VMEM((2 * ATT_BLOCK, ATT_BLOCK), F32)]),
        compiler_params=_cparams(("arbitrary", "arbitrary")),
        name="attention",
    )(*qs, *kvs, *kvs)


def _regroup8(tiles):
    t = list(tiles)
    lane = lax.broadcasted_iota(jnp.int32, t[0].shape, 1)
    piece = lane // SSM_GROUP
    for dist in (4, 2, 1):
        bit = (piece & dist) != 0
        shift = dist * SSM_GROUP
        for a in range(8):
            if a & dist:
                continue
            b = a | dist
            ta, tb = t[a], t[b]
            t[a] = jnp.where(bit, pltpu.roll(tb, shift, 1), ta)
            t[b] = jnp.where(bit, tb, pltpu.roll(ta, LANES - shift, 1))
    return t


def _s5_tile_perm(bsz):
    tt = S5_TILE_TOKENS
    n = bsz * tt
    perm = np.zeros((n, n), np.float32)
    for b in range(bsz):
        for c in range(tt // CHUNK):
            for t in range(CHUNK):
                perm[(t * (tt // CHUNK) + c) * bsz + b, b * tt + c * CHUNK + t] = 1.0
    return perm


def _s5_pre_kernel(h_ref, mod_ref, g_ref, perm_ref, z_ref, u_scr, up_scr):
    d = D_MODEL
    bsz, tt, _ = h_ref.shape
    g = g_ref[...]
    for b in range(bsz):
        u_scr[b * tt:(b + 1) * tt, :] = _normmod(
            h_ref[b], g, mod_ref[b, :, :d], mod_ref[b, :, d:2 * d]).astype(BF16)
    up_scr[...] = jnp.dot(perm_ref[...], u_scr[...], preferred_element_type=F32)
    rows = z_ref.shape[0]
    for j in range(d // LANES):
        for half in range(CHUNK // 8):
            tiles = [up_scr[(8 * half + tl) * rows:(8 * half + tl + 1) * rows, j * LANES:(j + 1) * LANES]
                     for tl in range(8)]
            outs = _regroup8(tiles)
            for gl in range(8):
                col = (8 * j + gl) * GROUP_COLS + half * LANES
                z_ref[:, col:col + LANES] = outs[gl].astype(z_ref.dtype)


def _s5_pre(h, mod, g, perm):
    bsz, seq, d = h.shape
    tt = S5_TILE_TOKENS
    rows = bsz * tt // CHUNK
    return pl.pallas_call(
        _s5_pre_kernel,
        grid=(seq // tt,),
        in_specs=[
            pl.BlockSpec((bsz, tt, d), lambda i: (0, i, 0)),
            pl.BlockSpec(mod.shape, lambda i: (0, 0, 0)),
            pl.BlockSpec((1, d), lambda i: (0, 0)),
            pl.BlockSpec(perm.shape, lambda i: (0, 0)),
        ],
        out_specs=pl.BlockSpec((rows, CHUNK * d), lambda i: (i, 0)),
        out_shape=jax.ShapeDtypeStruct((bsz * seq // CHUNK, CHUNK * d), BF16),
        scratch_shapes=[pltpu.VMEM((bsz * tt, d), BF16), pltpu.VMEM((bsz * tt, d), F32)],
        compiler_params=_cparams(("parallel",)),
        name="s5_pre",
    )(h, mod, g.reshape(1, d), perm)


def _s5_core_kernel(z_ref, win_ref, wt_ref, wx_ref, dec_ref, d_ref, o_ref, s_scr, x_scr, *, bsz):
    gc = GROUP_COLS
    nchunk = z_ref.shape[0] // bsz
    s_scr[...] = jnp.dot(z_ref[...], win_ref[...].reshape(2 * gc, gc), preferred_element_type=F32)
    ar, ai = dec_ref[:, :LANES], dec_ref[:, LANES:]

    def step(c, carry):
        xr, xi = carry
        rows = pl.ds(pl.multiple_of(c * bsz, bsz), bsz)
        x_scr[rows, :LANES] = xr
        x_scr[rows, LANES:] = xi
        return (ar * xr - ai * xi + s_scr[rows, :LANES], ar * xi + ai * xr + s_scr[rows, LANES:])

    zero = jnp.zeros((bsz, LANES), F32)
    lax.fori_loop(0, nchunk, step, (zero, zero))
    x = x_scr[...].astype(BF16)
    for gi in range(2):
        z = z_ref[:, gi * gc:(gi + 1) * gc]
        y = jnp.dot(jnp.concatenate([z, x], axis=1), jnp.concatenate([wt_ref[gi], wx_ref[gi]], axis=0),
                    preferred_element_type=F32)
        y = y + d_ref[:, gi * gc:(gi + 1) * gc] * z.astype(F32)
        o_ref[:, gi * gc:(gi + 1) * gc] = jax.nn.gelu(y).astype(o_ref.dtype)


def _s5_core(z, tables, layer, bsz):
    nrow, width = z.shape
    w_in, w_toep, w_x, dec, d_perm = tables
    wspec = pl.BlockSpec((None, 2, GROUP_COLS, GROUP_COLS), lambda k: (layer, k, 0, 0))
    return pl.pallas_call(
        functools.partial(_s5_core_kernel, bsz=bsz),
        grid=(N_PAIRS,),
        in_specs=[
            pl.BlockSpec((nrow, PAIR_COLS), lambda k: (0, k)),
            wspec, wspec, wspec,
            pl.BlockSpec((None, 1, GROUP_COLS), lambda k: (layer, 0, k)),
            pl.BlockSpec((None, 1, PAIR_COLS), lambda k: (layer, 0, k)),
        ],
        out_specs=pl.BlockSpec((nrow, PAIR_COLS), lambda k: (0, k)),
        out_shape=jax.ShapeDtypeStruct((nrow, width), BF16),
        scratch_shapes=[pltpu.VMEM((nrow, GROUP_COLS), F32), pltpu.VMEM((nrow, GROUP_COLS), F32)],
        compiler_params=_cparams(("parallel",)),
        name="s5_core",
    )(z, w_in, w_toep, w_x, dec, d_perm)


def _s5_weights(lam_re, lam_im, log_dt, b_re, b_im, c_re, c_im, d_skip):
    g, p, c16 = SSM_GROUPS, SSM_STATE, SSM_GROUP
    lam = lax.complex(lam_re.astype(F32), lam_im.astype(F32))
    dt = jnp.exp(log_dt.astype(F32))[:, None]
    steps = jnp.arange(CHUNK + 1, dtype=F32)
    apow = jnp.exp((lam * dt)[None] * steps[:, None, None])
    a = apow[1]
    bbar = ((a - 1.0) / lam)[..., None] * lax.complex(b_re.astype(F32), b_im.astype(F32))
    cmat = lax.complex(c_re.astype(F32), c_im.astype(F32))
    odd = (jnp.arange(g) % 2)[:, None, None]

    win = apow[CHUNK - 1::-1][..., None] * bbar[None]

    def lay_in(x):
        return jnp.transpose(x.astype(BF16), (1, 0, 3, 2)).reshape(g, GROUP_COLS, p)

    win_re, win_im = lay_in(win.real), lay_in(win.imag)
    zeros = jnp.zeros_like(win_re)
    w_in = jnp.concatenate([
        jnp.where(odd == 0, win_re, zeros), jnp.where(odd == 1, win_re, zeros),
        jnp.where(odd == 0, win_im, zeros), jnp.where(odd == 1, win_im, zeros)], axis=-1)

    kern = jnp.einsum('gop,kgp,gpi->kgoi', cmat, apow[:CHUNK], bbar, precision=HIGHEST).real
    idx = jnp.arange(CHUNK)
    onehot = (idx[None, None, :] - idx[None, :, None] == idx[:, None, None]).astype(BF16)
    w_toep = jnp.einsum('kst,kgoi->gsito', onehot, kern.astype(BF16),
                        preferred_element_type=BF16)
    w_toep = w_toep.reshape(g, GROUP_COLS, GROUP_COLS)

    cw = cmat[None] * apow[1:, :, None, :]

    def lay_x(x):
        return jnp.transpose(x.astype(BF16), (1, 3, 0, 2)).reshape(g, p, GROUP_COLS)

    cw_re, cw_im = lay_x(cw.real), lay_x(-cw.imag)
    zx = jnp.zeros_like(cw_re)
    w_x = jnp.concatenate([
        jnp.where(odd == 0, cw_re, zx), jnp.where(odd == 1, cw_re, zx),
        jnp.where(odd == 0, cw_im, zx), jnp.where(odd == 1, cw_im, zx)], axis=1)

    a16 = apow[CHUNK].reshape(N_PAIRS, 2 * p)
    dec = jnp.concatenate([a16.real, a16.imag], axis=-1).reshape(1, N_PAIRS * GROUP_COLS)
    d_perm = jnp.broadcast_to(d_skip.astype(F32).reshape(g, 1, c16), (g, CHUNK, c16)).reshape(1, g * GROUP_COLS)
    return w_in, w_toep, w_x, dec, d_perm


def _mlp(u, w1_ref, w2_ref):
    d = D_MODEL
    acc = None
    for k in range(D_FF // d):
        a = jnp.dot(u, w1_ref[:, k * d:(k + 1) * d], preferred_element_type=F32)
        a = jnp.square(jnp.maximum(a, 0.0)).astype(BF16)
        part = jnp.dot(a, w2_ref[k * d:(k + 1) * d, :], preferred_element_type=F32)
        acc = part if acc is None else acc + part
    return acc


def _final_norm(h, fg_ref):
    ms = jnp.mean(h * h, axis=-1, keepdims=True)
    return h * lax.rsqrt(ms + EPS) * fg_ref[...]


def _post_attn_kernel(*refs, final):
    if final:
        h_ref, y_ref, moda_ref, modm_ref, g_ref, wp_ref, w1_ref, w2_ref, fg_ref, o_ref = refs
    else:
        h_ref, y_ref, moda_ref, modm_ref, g_ref, wp_ref, w1_ref, w2_ref, o_ref = refs
    d = D_MODEL
    ymix = jnp.dot(y_ref[...], wp_ref[...], preferred_element_type=F32)
    h1 = h_ref[...] + moda_ref[:, 2 * d:3 * d] * ymix
    u = _normmod(h1, g_ref[...], modm_ref[:, :d], modm_ref[:, d:2 * d]).astype(BF16)
    h2 = h1 + modm_ref[:, 2 * d:3 * d] * _mlp(u, w1_ref, w2_ref)
    o_ref[...] = _final_norm(h2, fg_ref) if final else h2


def _post_attn(h, o, moda, modm, g2, w_o, w1, w2, final_g):
    bsz, seq, d = h.shape
    tm = MLP_ROWS
    per_b = seq // tm
    row_spec = pl.BlockSpec((None, tm, d), lambda i: (i // per_b, i % per_b, 0))
    final = final_g is not None
    in_specs = [
        row_spec, row_spec,
        pl.BlockSpec((None, 1, moda.shape[-1]), lambda i: (i // per_b, 0, 0)),
        pl.BlockSpec((None, 1, modm.shape[-1]), lambda i: (i // per_b, 0, 0)),
        pl.BlockSpec((1, d), lambda i: (0, 0)),
        _layer_spec(*w_o), _layer_spec(*w1), _layer_spec(*w2),
    ]
    args = [h, o, moda, modm, g2.reshape(1, d), w_o[0], w1[0], w2[0]]
    if final:
        in_specs.append(pl.BlockSpec((1, d), lambda i: (0, 0)))
        args.append(final_g.reshape(1, d))
    return pl.pallas_call(
        functools.partial(_post_attn_kernel, final=final),
        grid=(bsz * per_b,),
        in_specs=in_specs,
        out_specs=row_spec,
        out_shape=jax.ShapeDtypeStruct((bsz, seq, d), F32),
        compiler_params=_cparams(("parallel",)),
        name="post_attn_mlp",
    )(*args)


def _post_s5_kernel(h_ref, z_ref, moda_ref, modm_ref, g_ref, permt_ref, wp_ref, w1_ref, w2_ref, o_ref,
                    z_scr, h1_scr, u_scr):
    d = D_MODEL
    bsz, tt, _ = h_ref.shape
    rows = z_ref.shape[0]
    for j in range(d // LANES):
        for half in range(CHUNK // 8):
            tiles = []
            for gl in range(8):
                col = (8 * j + gl) * GROUP_COLS + half * LANES
                tiles.append(z_ref[:, col:col + LANES].astype(F32))
            outs = _regroup8(tiles)
            for tl in range(8):
                t = 8 * half + tl
                z_scr[t * rows:(t + 1) * rows, j * LANES:(j + 1) * LANES] = outs[tl].astype(BF16)
    zn = jnp.dot(permt_ref[...], z_scr[...], preferred_element_type=F32).astype(BF16)
    zz = jnp.dot(zn, wp_ref[...], preferred_element_type=F32)
    ymix = zz[:, :d] * jax.nn.sigmoid(zz[:, d:])
    g = g_ref[...]
    for b in range(bsz):
        sl = slice(b * tt, (b + 1) * tt)
        h1 = h_ref[b] + moda_ref[b, :, 2 * d:3 * d] * ymix[sl, :]
        h1_scr[sl, :] = h1
        u_scr[sl, :] = _normmod(h1, g, modm_ref[b, :, :d], modm_ref[b, :, d:2 * d]).astype(BF16)
    acc = _mlp(u_scr[...], w1_ref, w2_ref)
    for b in range(bsz):
        sl = slice(b * tt, (b + 1) * tt)
        o_ref[b] = h1_scr[sl, :] + modm_ref[b, :, 2 * d:3 * d] * acc[sl, :]


def _post_s5(h, zo, moda, modm, g2, permt, w_glu, w1, w2):
    bsz, seq, d = h.shape
    tt = S5_TILE_TOKENS
    rows = bsz * tt // CHUNK
    h_spec = pl.BlockSpec((bsz, tt, d), lambda i: (0, i, 0))
    return pl.pallas_call(
        _post_s5_kernel,
        grid=(seq // tt,),
        in_specs=[
            h_spec,
            pl.BlockSpec((rows, CHUNK * d), lambda i: (i, 0)),
            pl.BlockSpec(moda.shape, lambda i: (0, 0, 0)),
            pl.BlockSpec(modm.shape, lambda i: (0, 0, 0)),
            pl.BlockSpec((1, d), lambda i: (0, 0)),
            pl.BlockSpec(permt.shape, lambda i: (0, 0)),
            _layer_spec(*w_glu), _layer_spec(*w1), _layer_spec(*w2),
        ],
        out_specs=h_spec,
        out_shape=jax.ShapeDtypeStruct((bsz, seq, d), F32),
        scratch_shapes=[pltpu.VMEM((bsz * tt, d), BF16), pltpu.VMEM((bsz * tt, d), F32),
                        pltpu.VMEM((bsz * tt, d), BF16)],
        compiler_params=_cparams(("parallel",)),
        name="post_s5_mlp",
    )(h, zo, moda, modm, g2.reshape(1, d), permt, w_glu[0], w1[0], w2[0])


def kernel(x, c, ln_g, ada_w, ada_b, ssm_lam_re, ssm_lam_im, ssm_log_dt, ssm_b_re, ssm_b_im, ssm_c_re, ssm_c_im, ssm_d, ssm_w_glu, kv_g, kv_ada_w, kv_ada_b, w_kv, attn_w_q, attn_w_o, mlp_w1, mlp_w2, final_g):
    bsz, seq, d = x.shape
    depth = ln_g.shape[0]
    n_s5 = ssm_lam_re.shape[0]
    assert d == D_MODEL and seq % (DILATIONS[-1] * ATT_BLOCK) == 0

    mods = _ada_mods(c, ada_w.reshape(depth * 2, d, 3 * d), ada_b.reshape(depth * 2, 3 * d))
    mods = mods.reshape(depth, 2, bsz, 1, 3 * d)
    kv_mod = _ada_mods(c, kv_ada_w[None], kv_ada_b[None]).reshape(bsz, 1, 2 * d)

    w1 = mlp_w1.astype(BF16)
    w2 = mlp_w2.astype(BF16)
    w_glu = ssm_w_glu.astype(BF16)
    w_q = attn_w_q.astype(BF16)
    w_o = attn_w_o.astype(BF16)
    perm_np = _s5_tile_perm(bsz)
    perm = jnp.asarray(perm_np, BF16)
    permt = jnp.asarray(perm_np.T, BF16)

    tables = jax.vmap(_s5_weights)(ssm_lam_re, ssm_lam_im, ssm_log_dt, ssm_b_re, ssm_b_im,
                                   ssm_c_re, ssm_c_im, ssm_d)
    h = x
    for layer in range(n_s5):
        z = _s5_pre(h, mods[layer, 0], ln_g[layer, 0], perm)
        zo = _s5_core(z, tables, layer, bsz)
        h = _post_s5(h, zo, mods[layer, 0], mods[layer, 1], ln_g[layer, 1], permt,
                     (w_glu, layer), (w1, layer), (w2, layer))

    kv_cols = tuple((i * d, (N_BRANCHES + i) * d) for i in range(N_BRANCHES))
    q_cols = tuple((i * d,) for i in range(N_BRANCHES))
    kvs = _norm_proj(h, kv_mod, kv_g, (w_kv.astype(BF16)[None], 0), kv_cols, 1.0, KV_PROJ_TILE)
    for layer in range(n_s5, depth):
        j = layer - n_s5
        qs = _norm_proj(h, mods[layer, 0], ln_g[layer, 0], (w_q, j), q_cols, Q_SCALE, Q_PROJ_TILE)
        o = _attention(qs, kvs)
        h = _post_attn(h, o, mods[layer, 0], mods[layer, 1], ln_g[layer, 1],
                       (w_o, j), (w1, layer), (w2, layer),
                       final_g if layer == depth - 1 else None)
    return h
```

```python
import functools
import math

import numpy as np

import jax
import jax.numpy as jnp
from jax import lax
from jax.experimental import pallas as pl
from jax.experimental.pallas import tpu as pltpu

F32 = jnp.float32
BF16 = jnp.bfloat16

D_MODEL = 1024
SSM_GROUP = 16
SSM_GROUPS = D_MODEL // SSM_GROUP
SSM_STATE = 64
HEAD_DIM = 64
N_HEADS = D_MODEL // HEAD_DIM
DILATIONS = (1, 4, 16)
N_BRANCHES = len(DILATIONS)
ATT_BLOCK = 128
ATT_GROUP = 16
ATT_PAIRS_PER_STEP = 2
N_STATS = 3
Q_SCALE = HEAD_DIM ** -0.5 * math.log2(math.e)
D_FF = 4 * D_MODEL
EPS = 1e-6
NEG = -1e30

LANES = 128
CHUNK = 16
GROUP_COLS = CHUNK * SSM_GROUP
PAIR_COLS = 2 * GROUP_COLS
N_PAIRS = SSM_GROUPS // 2
S5_TILE_TOKENS = 2 * CHUNK
S5_PRE_SUBTILES = 2
MLP_ROWS = 1024
KV_PROJ_TILE = 512
Q_PROJ_TILE = 1024
VMEM_LIMIT = 56 * 1024 * 1024

HIGHEST = lax.Precision.HIGHEST
NT_DIMS = (((1,), (1,)), ((), ()))


def _cparams(sem):
    return pltpu.CompilerParams(dimension_semantics=sem, vmem_limit_bytes=VMEM_LIMIT)


def _layer_spec(stack, layer):
    zeros = (0,) * (stack.ndim - 1)
    return pl.BlockSpec((None,) + stack.shape[1:], lambda i: (layer,) + zeros, pipeline_mode=pl.Buffered(1))


def _normmod(x, g, shift, scale):
    ms = jnp.mean(x * x, axis=-1, keepdims=True)
    return (x * lax.rsqrt(ms + EPS) * g) * (1.0 + scale) + shift


def _ada_kernel(c_ref, w_ref, b_ref, o_ref):
    c = c_ref[...]
    sc = (c * jax.nn.sigmoid(c)).astype(BF16)
    o_ref[...] = jnp.dot(sc, w_ref[...].astype(BF16), preferred_element_type=F32) + b_ref[...]


def _ada_mods(c, w, b):
    n, d, width = w.shape
    bsz = c.shape[0]
    return pl.pallas_call(
        _ada_kernel,
        grid=(n,),
        in_specs=[
            pl.BlockSpec((bsz, d), lambda i: (0, 0)),
            pl.BlockSpec((None, d, width), lambda i: (i, 0, 0)),
            pl.BlockSpec((None, 1, width), lambda i: (i, 0, 0)),
        ],
        out_specs=pl.BlockSpec((None, bsz, width), lambda i: (i, 0, 0)),
        out_shape=jax.ShapeDtypeStruct((n, bsz, width), F32),
        compiler_params=_cparams(("parallel",)),
        name="ada_mods",
    )(c, w, b.reshape(n, 1, width))


def _proj_kernel(h_ref, mod_ref, g_ref, w_ref, o1_ref, o2_ref, o3_ref, us, us4, up4, up16, *, wcols, out_scale):
    d = D_MODEL
    tm = h_ref.shape[0]
    d1 = DILATIONS[1]
    q4, q16 = tm // d1, tm // DILATIONS[2]
    cw = o1_ref.shape[1]
    u = _normmod(h_ref[...], g_ref[...], mod_ref[:, :d], mod_ref[:, d:2 * d])
    for l in range(d // LANES):
        us[l] = u[:, l * LANES:(l + 1) * LANES]
    ub = u.astype(BF16)
    for n in range(cw // d):
        cols = slice(n * d, (n + 1) * d)
        res = jnp.dot(ub, w_ref[:, wcols[0][n]:wcols[0][n] + d], preferred_element_type=F32)
        o1_ref[:, cols] = (res * out_scale).astype(o1_ref.dtype)
    for l in range(d // LANES):
        for a in range(d1):
            part = us[l, pl.ds(a, q4, stride=d1), :]
            us4[l, a * q4:(a + 1) * q4, :] = part
            up4[a * q4:(a + 1) * q4, l * LANES:(l + 1) * LANES] = part.astype(BF16)
    for n in range(cw // d):
        res = jnp.dot(up4[...], w_ref[:, wcols[1][n]:wcols[1][n] + d], preferred_element_type=F32)
        for a in range(d1):
            o2_ref[a, :, n * d:(n + 1) * d] = (res[a * q4:(a + 1) * q4, :] * out_scale).astype(o2_ref.dtype)
    for l in range(d // LANES):
        for a1 in range(d1):
            for a2 in range(d1):
                r = d1 * a2 + a1
                up16[r * q16:(r + 1) * q16, l * LANES:(l + 1) * LANES] = us4[
                    l, pl.ds(a1 * q4 + a2, q16, stride=d1), :].astype(BF16)
    for n in range(cw // d):
        res = jnp.dot(up16[...], w_ref[:, wcols[2][n]:wcols[2][n] + d], preferred_element_type=F32)
        for r in range(DILATIONS[2]):
            o3_ref[r, :, n * d:(n + 1) * d] = (res[r * q16:(r + 1) * q16, :] * out_scale).astype(o3_ref.dtype)


def _norm_proj(h, mod, g, w, wcols, out_scale, tm):
    bsz, seq, d = h.shape
    cw = len(wcols[0]) * d
    per_b = seq // tm
    d1, d2 = DILATIONS[1], DILATIONS[2]
    o1, o2, o3 = pl.pallas_call(
        functools.partial(_proj_kernel, wcols=wcols, out_scale=out_scale),
        grid=(bsz * per_b,),
        in_specs=[
            pl.BlockSpec((None, tm, d), lambda i: (i // per_b, i % per_b, 0)),
            pl.BlockSpec((None, 1, mod.shape[-1]), lambda i: (i // per_b, 0, 0)),
            pl.BlockSpec((1, d), lambda i: (0, 0)),
            _layer_spec(*w),
        ],
        out_specs=[
            pl.BlockSpec((None, tm, cw), lambda i: (i // per_b, i % per_b, 0)),
            pl.BlockSpec((None, d1, None, tm // d1, cw), lambda i: (i // per_b, 0, i % per_b, 0, 0)),
            pl.BlockSpec((None, d2, None, tm // d2, cw), lambda i: (i // per_b, 0, i % per_b, 0, 0)),
        ],
        out_shape=[
            jax.ShapeDtypeStruct((bsz, seq, cw), BF16),
            jax.ShapeDtypeStruct((bsz, d1, per_b, tm // d1, cw), BF16),
            jax.ShapeDtypeStruct((bsz, d2, per_b, tm // d2, cw), BF16),
        ],
        scratch_shapes=[pltpu.VMEM((d // LANES, tm, LANES), F32), pltpu.VMEM((d // LANES, tm, LANES), F32),
                        pltpu.VMEM((tm, d), BF16), pltpu.VMEM((tm, d), BF16)],
        compiler_params=_cparams(("parallel",)),
        name="norm_proj",
    )(h, mod, g.reshape(1, d), w[0])
    return o1, o2.reshape(bsz, seq, cw), o3.reshape(bsz, seq, cw)


def _attn_group(blocks):
    blk = ATT_BLOCK
    lane = lax.broadcasted_iota(jnp.int32, (blk, LANES), 1)
    lo = lane < HEAD_DIM
    scores = []
    for qb, kcat, _, mask, _ in blocks:
        zero = jnp.zeros_like(qb)
        q2 = jnp.concatenate([jnp.where(lo, qb, zero), jnp.where(lo, zero, qb)], axis=0)
        s = lax.dot_general(q2, kcat, NT_DIMS, preferred_element_type=F32)
        scores.append(s + mask[...])
    soft = []
    for s, (_, _, _, _, old) in zip(scores, blocks):
        ms, ps = [], []
        for hh in range(2):
            tiles = [s[hh * blk:(hh + 1) * blk, t * LANES:(t + 1) * LANES] for t in range(s.shape[1] // LANES)]
            mx = tiles[0]
            for t in tiles[1:]:
                mx = jnp.maximum(mx, t)
            m = jnp.broadcast_to(jnp.max(mx, axis=-1, keepdims=True), (blk, LANES))
            if old is not None:
                m = jnp.maximum(m, old[0][hh])
            ms.append(m)
            ps.append(jnp.concatenate([jnp.exp2(t - m).astype(BF16) for t in tiles], axis=1))
        alpha = None
        if old is not None:
            alpha = jnp.exp2(jnp.where(lo, old[0][0], old[0][1]) - jnp.where(lo, ms[0], ms[1]))
        soft.append((ms, jnp.concatenate(ps, axis=0), alpha))
    outs = []
    for (ms, p, alpha), (_, _, vcat, _, old) in zip(soft, blocks):
        pv = jnp.dot(p, vcat, preferred_element_type=F32)
        acc = jnp.where(lo, pv[:blk, :LANES], pv[blk:, :LANES])
        den = jnp.where(lo, pv[:blk, LANES:], pv[blk:, LANES:])
        if old is not None:
            acc = alpha * old[1] + acc
            den = alpha * old[0][2] + den
        outs.append((ms + [den], acc))
    return outs


def _attn_kernel(*refs):
    n_in = 3 * N_BRANCHES
    ins, o_ref, scratch = refs[:n_in], refs[n_in], refs[n_in + 1:]
    for half in range(ATT_PAIRS_PER_STEP):
        lanes = pl.ds(half * LANES, LANES)
        _attn_pair(*[r.at[:, lanes] for r in ins], o_ref.at[:, lanes], *scratch)


def _attn_pair(q1, qd2, qd3, k1, kd2, kd3, v1, v2, v3, o_ref, tmp_nat, vd1, vd2, vd3,
               acc_ref, st_ref, acc2_ref, st2_ref, mask_pc, mask_c):
    blk = ATT_BLOCK
    seq = q1.shape[0]
    nblk = seq // blk
    d1, d2 = DILATIONS[1], DILATIONS[2]
    quarter = seq // d1
    per_res = quarter // blk
    assert d2 == d1 * d1 and seq == d2 * blk and nblk % ATT_GROUP == 0 and ATT_GROUP % per_res == 0

    @pl.when(jnp.logical_and(pl.program_id(0) == 0, pl.program_id(1) == 0))
    def _():
        for vd in (vd1, vd2, vd3):
            vd[:, LANES:] = jnp.ones((seq, LANES), BF16)
        qq = lax.broadcasted_iota(jnp.int32, (2 * blk, 2 * blk), 0) & (blk - 1)
        kk = lax.broadcasted_iota(jnp.int32, (2 * blk, 2 * blk), 1)
        valid = jnp.logical_or(jnp.logical_and(kk < blk, kk >= qq), jnp.logical_and(kk >= blk, kk - blk <= qq))
        mask_pc[...] = jnp.where(valid, 0.0, NEG)
        valid_c = (lax.broadcasted_iota(jnp.int32, (2 * blk, blk), 1)
                   <= (lax.broadcasted_iota(jnp.int32, (2 * blk, blk), 0) & (blk - 1)))
        mask_c[...] = jnp.where(valid_c, 0.0, NEG)

    for v, vd in ((v1, vd1), (v2, vd2), (v3, vd3)):
        vd[:, :LANES] = v[...]

    def run_group(blocks, dests):
        for (st_dst, acc_dst, rows), (st, acc) in zip(dests, _attn_group(blocks)):
            for k in range(N_STATS):
                st_dst[k, rows, :] = st[k]
            acc_dst[rows, :] = acc

    def load_old(st_src, acc_src, rows):
        return [st_src[k, rows, :] for k in range(N_STATS)], acc_src[rows, :]

    def b0_group(ns, first):
        blocks, dests = [], []
        for n in ns:
            if first and n == 0:
                rows = pl.ds(0, blk)
                blocks.append((q1[rows, :], k1[rows, :], vd1[rows, :], mask_c, None))
            else:
                rows = pl.ds(pl.multiple_of(n * blk, blk), blk)
                krows = pl.ds(pl.multiple_of((n - 1) * blk, blk), 2 * blk)
                blocks.append((q1[rows, :], k1[krows, :], vd1[krows, :], mask_pc, None))
            dests.append((st_ref, acc_ref, rows))
        run_group(blocks, dests)

    b0_group(list(range(ATT_GROUP)), True)

    def b0_body(i, carry):
        b0_group([ATT_GROUP * i + j for j in range(ATT_GROUP)], False)
        return carry

    lax.fori_loop(1, nblk // ATT_GROUP, b0_body, 0)

    res_per_trip = ATT_GROUP // per_res

    def b1_body(i, carry):
        blocks, dests = [], []
        for jr in range(res_per_trip):
            a1 = res_per_trip * i + jr
            base = a1 * quarter
            for n in range(per_res):
                rows = pl.ds(a1 + d1 * blk * n, blk, stride=d1)
                qrows = pl.ds(pl.multiple_of(base + n * blk, blk), blk)
                if n == 0:
                    krows, mask = qrows, mask_c
                else:
                    krows, mask = pl.ds(pl.multiple_of(base + (n - 1) * blk, blk), 2 * blk), mask_pc
                blocks.append((qd2[qrows, :], kd2[krows, :], vd2[krows, :], mask,
                               load_old(st_ref, acc_ref, rows)))
                dests.append((st2_ref, acc2_ref, qrows))
        run_group(blocks, dests)
        return carry

    lax.fori_loop(0, d1 // res_per_trip, b1_body, 0)

    def b2_body(i, carry):
        blocks, dests = [], []
        for j in range(ATT_GROUP):
            a1, a2 = j % d1, (ATT_GROUP // d1) * i + j // d1
            qrows = pl.ds(pl.multiple_of((ATT_GROUP * i + j) * blk, blk), blk)
            rows = pl.ds(a1 * quarter + a2, blk, stride=d1)
            blocks.append((qd3[qrows, :], kd3[qrows, :], vd3[qrows, :], mask_c,
                           load_old(st2_ref, acc2_ref, rows)))
            dests.append((st2_ref, acc2_ref, rows))
        run_group(blocks, dests)
        return carry

    lax.fori_loop(0, d2 // ATT_GROUP, b2_body, 0)

    for a in range(d1):
        seg = slice(a * quarter, (a + 1) * quarter)
        tmp_nat[pl.ds(a, quarter, stride=d1), :] = acc2_ref[seg, :] / st2_ref[N_STATS - 1, seg, :]
    o_ref[...] = tmp_nat[...].astype(o_ref.dtype)


def _attention(qs, kvs):
    bsz, seq, _ = qs[0].shape
    steps = D_MODEL // (ATT_PAIRS_PER_STEP * LANES)

    def slab(col0):
        return pl.BlockSpec((None, seq, ATT_PAIRS_PER_STEP * LANES), lambda b, hp: (b, 0, col0 + hp))

    in_specs = [slab(0)] * N_BRANCHES + [slab(0)] * N_BRANCHES + [slab(steps)] * N_BRANCHES
    return pl.pallas_call(
        _attn_kernel,
        grid=(bsz, steps),
        in_specs=in_specs,
        out_specs=slab(0),
        out_shape=jax.ShapeDtypeStruct((bsz, seq, D_MODEL), BF16),
        scratch_shapes=([pltpu.VMEM((seq, LANES), F32)] + [pltpu.VMEM((seq, 2 * LANES), BF16)] * 3
                        + [pltpu.VMEM((seq, LANES), F32), pltpu.VMEM((N_STATS, seq, LANES), F32)] * 2
                        + [pltpu.VMEM((2 * ATT_BLOCK, 2 * ATT_BLOCK), F32), pltpu.VMEM((2 * ATT_BLOCK, ATT_BLOCK), F32)]),
        compiler_params=_cparams(("arbitrary", "arbitrary")),
        name="attention",
    )(*qs, *kvs, *kvs)


def _regroup8(tiles):
    t = list(tiles)
    lane = lax.broadcasted_iota(jnp.int32, t[0].shape, 1)
    piece = lane // SSM_GROUP
    for dist in (4, 2, 1):
        bit = (piece & dist) != 0
        shift = dist * SSM_GROUP
        for a in range(8):
            if a & dist:
                continue
            b = a | dist
            ta, tb = t[a], t[b]
            t[a] = jnp.where(bit, pltpu.roll(tb, shift, 1), ta)
            t[b] = jnp.where(bit, tb, pltpu.roll(ta, LANES - shift, 1))
    return t


def _s5_tile_perm(bsz):
    tt = S5_TILE_TOKENS
    n = bsz * tt
    perm = np.zeros((n, n), np.float32)
    for b in range(bsz):
        for c in range(tt // CHUNK):
            for t in range(CHUNK):
                perm[(t * (tt // CHUNK) + c) * bsz + b, b * tt + c * CHUNK + t] = 1.0
    return perm


def _s5_pre_kernel(h_ref, mod_ref, g_ref, perm_ref, z_ref, u_scr, up_scr):
    d = D_MODEL
    bsz = h_ref.shape[0]
    tt = S5_TILE_TOKENS
    g = g_ref[...]
    rows = z_ref.shape[0] // S5_PRE_SUBTILES
    for sub in range(S5_PRE_SUBTILES):
        for b in range(bsz):
            u_scr[b * tt:(b + 1) * tt, :] = _normmod(
                h_ref[b, sub * tt:(sub + 1) * tt, :], g, mod_ref[b, :, :d], mod_ref[b, :, d:2 * d]).astype(BF16)
        up_scr[...] = jnp.dot(perm_ref[...], u_scr[...], preferred_element_type=F32)
        for j in range(d // LANES):
            for half in range(CHUNK // 8):
                tiles = [up_scr[(8 * half + tl) * rows:(8 * half + tl + 1) * rows, j * LANES:(j + 1) * LANES]
                         for tl in range(8)]
                outs = _regroup8(tiles)
                for gl in range(8):
                    col = (8 * j + gl) * GROUP_COLS + half * LANES
                    z_ref[sub * rows:(sub + 1) * rows, col:col + LANES] = outs[gl].astype(z_ref.dtype)


def _s5_pre(h, mod, g, perm):
    bsz, seq, d = h.shape
    tt = S5_TILE_TOKENS
    rows = bsz * tt // CHUNK
    nsub = S5_PRE_SUBTILES
    return pl.pallas_call(
        _s5_pre_kernel,
        grid=(seq // (nsub * tt),),
        in_specs=[
            pl.BlockSpec((bsz, nsub * tt, d), lambda i: (0, i, 0)),
            pl.BlockSpec(mod.shape, lambda i: (0, 0, 0)),
            pl.BlockSpec((1, d), lambda i: (0, 0)),
            pl.BlockSpec(perm.shape, lambda i: (0, 0)),
        ],
        out_specs=pl.BlockSpec((nsub * rows, CHUNK * d), lambda i: (i, 0)),
        out_shape=jax.ShapeDtypeStruct((bsz * seq // CHUNK, CHUNK * d), BF16),
        scratch_shapes=[pltpu.VMEM((bsz * tt, d), BF16), pltpu.VMEM((bsz * tt, d), F32)],
        compiler_params=_cparams(("parallel",)),
        name="s5_pre",
    )(h, mod, g.reshape(1, d), perm)


def _s5_core_kernel(z_ref, win_ref, wt_ref, wx_ref, dec_ref, d_ref, o_ref, s_scr, x_scr, *, bsz):
    gc = GROUP_COLS
    nchunk = z_ref.shape[0] // bsz
    s_scr[...] = jnp.dot(z_ref[...], win_ref[...].reshape(2 * gc, gc), preferred_element_type=F32)
    ar, ai = dec_ref[:, :LANES], dec_ref[:, LANES:]

    def step(c, carry):
        xr, xi = carry
        rows = pl.ds(pl.multiple_of(c * bsz, bsz), bsz)
        x_scr[rows, :LANES] = xr
        x_scr[rows, LANES:] = xi
        return (ar * xr - ai * xi + s_scr[rows, :LANES], ar * xi + ai * xr + s_scr[rows, LANES:])

    zero = jnp.zeros((bsz, LANES), F32)
    lax.fori_loop(0, nchunk, step, (zero, zero))
    x = x_scr[...].astype(BF16)
    for gi in range(2):
        z = z_ref[:, gi * gc:(gi + 1) * gc]
        y = jnp.dot(jnp.concatenate([z, x], axis=1), jnp.concatenate([wt_ref[gi], wx_ref[gi]], axis=0),
                    preferred_element_type=F32)
        y = y + d_ref[:, gi * gc:(gi + 1) * gc] * z.astype(F32)
        o_ref[:, gi * gc:(gi + 1) * gc] = jax.nn.gelu(y).astype(o_ref.dtype)


def _s5_core(z, tables, layer, bsz):
    nrow, width = z.shape
    w_in, w_toep, w_x, dec, d_perm = tables
    wspec = pl.BlockSpec((None, 2, GROUP_COLS, GROUP_COLS), lambda k: (layer, k, 0, 0))
    return pl.pallas_call(
        functools.partial(_s5_core_kernel, bsz=bsz),
        grid=(N_PAIRS,),
        in_specs=[
            pl.BlockSpec((nrow, PAIR_COLS), lambda k: (0, k)),
            wspec, wspec, wspec,
            pl.BlockSpec((None, 1, GROUP_COLS), lambda k: (layer, 0, k)),
            pl.BlockSpec((None, 1, PAIR_COLS), lambda k: (layer, 0, k)),
        ],
        out_specs=pl.BlockSpec((nrow, PAIR_COLS), lambda k: (0, k)),
        out_shape=jax.ShapeDtypeStruct((nrow, width), BF16),
        scratch_shapes=[pltpu.VMEM((nrow, GROUP_COLS), F32), pltpu.VMEM((nrow, GROUP_COLS), F32)],
        compiler_params=_cparams(("parallel",)),
        name="s5_core",
    )(z, w_in, w_toep, w_x, dec, d_perm)


def _s5_weights(lam_re, lam_im, log_dt, b_re, b_im, c_re, c_im, d_skip):
    g, p, c16 = SSM_GROUPS, SSM_STATE, SSM_GROUP
    lam = lax.complex(lam_re.astype(F32), lam_im.astype(F32))
    dt = jnp.exp(log_dt.astype(F32))[:, None]
    steps = jnp.arange(CHUNK + 1, dtype=F32)
    apow = jnp.exp((lam * dt)[None] * steps[:, None, None])
    a = apow[1]
    bbar = ((a - 1.0) / lam)[..., None] * lax.complex(b_re.astype(F32), b_im.astype(F32))
    cmat = lax.complex(c_re.astype(F32), c_im.astype(F32))
    odd = (jnp.arange(g) % 2)[:, None, None]

    win = apow[CHUNK - 1::-1][..., None] * bbar[None]

    def lay_in(x):
        return jnp.transpose(x.astype(BF16), (1, 0, 3, 2)).reshape(g, GROUP_COLS, p)

    win_re, win_im = lay_in(win.real), lay_in(win.imag)
    zeros = jnp.zeros_like(win_re)
    w_in = jnp.concatenate([
        jnp.where(odd == 0, win_re, zeros), jnp.where(odd == 1, win_re, zeros),
        jnp.where(odd == 0, win_im, zeros), jnp.where(odd == 1, win_im, zeros)], axis=-1)

    kern = jnp.einsum('gop,kgp,gpi->kgoi', cmat, apow[:CHUNK], bbar, precision=HIGHEST).real
    idx = jnp.arange(CHUNK)
    onehot = (idx[None, None, :] - idx[None, :, None] == idx[:, None, None]).astype(BF16)
    w_toep = jnp.einsum('kst,kgoi->gsito', onehot, kern.astype(BF16),
                        preferred_element_type=BF16)
    w_toep = w_toep.reshape(g, GROUP_COLS, GROUP_COLS)

    cw = cmat[None] * apow[1:, :, None, :]

    def lay_x(x):
        return jnp.transpose(x.astype(BF16), (1, 3, 0, 2)).reshape(g, p, GROUP_COLS)

    cw_re, cw_im = lay_x(cw.real), lay_x(-cw.imag)
    zx = jnp.zeros_like(cw_re)
    w_x = jnp.concatenate([
        jnp.where(odd == 0, cw_re, zx), jnp.where(odd == 1, cw_re, zx),
        jnp.where(odd == 0, cw_im, zx), jnp.where(odd == 1, cw_im, zx)], axis=1)

    a16 = apow[CHUNK].reshape(N_PAIRS, 2 * p)
    dec = jnp.concatenate([a16.real, a16.imag], axis=-1).reshape(1, N_PAIRS * GROUP_COLS)
    d_perm = jnp.broadcast_to(d_skip.astype(F32).reshape(g, 1, c16), (g, CHUNK, c16)).reshape(1, g * GROUP_COLS)
    return w_in, w_toep, w_x, dec, d_perm


def _mlp(u, w1_ref, w2_ref):
    d = D_MODEL
    acc = None
    for k in range(D_FF // d):
        a = jnp.dot(u, w1_ref[:, k * d:(k + 1) * d], preferred_element_type=F32)
        a = jnp.square(jnp.maximum(a, 0.0)).astype(BF16)
        part = jnp.dot(a, w2_ref[k * d:(k + 1) * d, :], preferred_element_type=F32)
        acc = part if acc is None else acc + part
    return acc


def _final_norm(h, fg_ref):
    ms = jnp.mean(h * h, axis=-1, keepdims=True)
    return h * lax.rsqrt(ms + EPS) * fg_ref[...]


def _post_attn_kernel(*refs, final):
    if final:
        h_ref, y_ref, moda_ref, modm_ref, g_ref, wp_ref, w1_ref, w2_ref, fg_ref, o_ref = refs
    else:
        h_ref, y_ref, moda_ref, modm_ref, g_ref, wp_ref, w1_ref, w2_ref, o_ref = refs
    d = D_MODEL
    ymix = jnp.dot(y_ref[...], wp_ref[...], preferred_element_type=F32)
    h1 = h_ref[...] + moda_ref[:, 2 * d:3 * d] * ymix
    u = _normmod(h1, g_ref[...], modm_ref[:, :d], modm_ref[:, d:2 * d]).astype(BF16)
    h2 = h1 + modm_ref[:, 2 * d:3 * d] * _mlp(u, w1_ref, w2_ref)
    o_ref[...] = _final_norm(h2, fg_ref) if final else h2


def _post_attn(h, o, moda, modm, g2, w_o, w1, w2, final_g):
    bsz, seq, d = h.shape
    tm = MLP_ROWS
    per_b = seq // tm
    row_spec = pl.BlockSpec((None, tm, d), lambda i: (i // per_b, i % per_b, 0))
    final = final_g is not None
    in_specs = [
        row_spec, row_spec,
        pl.BlockSpec((None, 1, moda.shape[-1]), lambda i: (i // per_b, 0, 0)),
        pl.BlockSpec((None, 1, modm.shape[-1]), lambda i: (i // per_b, 0, 0)),
        pl.BlockSpec((1, d), lambda i: (0, 0)),
        _layer_spec(*w_o), _layer_spec(*w1), _layer_spec(*w2),
    ]
    args = [h, o, moda, modm, g2.reshape(1, d), w_o[0], w1[0], w2[0]]
    if final:
        in_specs.append(pl.BlockSpec((1, d), lambda i: (0, 0)))
        args.append(final_g.reshape(1, d))
    return pl.pallas_call(
        functools.partial(_post_attn_kernel, final=final),
        grid=(bsz * per_b,),
        in_specs=in_specs,
        out_specs=row_spec,
        out_shape=jax.ShapeDtypeStruct((bsz, seq, d), F32),
        compiler_params=_cparams(("parallel",)),
        name="post_attn_mlp",
    )(*args)


def _post_s5_kernel(h_ref, z_ref, moda_ref, modm_ref, g_ref, permt_ref, wp_ref, w1_ref, w2_ref, o_ref,
                    z_scr, h1_scr, u_scr):
    d = D_MODEL
    bsz, tt, _ = h_ref.shape
    rows = z_ref.shape[0]
    for j in range(d // LANES):
        for half in range(CHUNK // 8):
            tiles = []
            for gl in range(8):
                col = (8 * j + gl) * GROUP_COLS + half * LANES
                tiles.append(z_ref[:, col:col + LANES].astype(F32))
            outs = _regroup8(tiles)
            for tl in range(8):
                t = 8 * half + tl
                z_scr[t * rows:(t + 1) * rows, j * LANES:(j + 1) * LANES] = outs[tl].astype(BF16)
    zn = jnp.dot(permt_ref[...], z_scr[...], preferred_element_type=F32).astype(BF16)
    zz = jnp.dot(zn, wp_ref[...], preferred_element_type=F32)
    ymix = zz[:, :d] * jax.nn.sigmoid(zz[:, d:])
    g = g_ref[...]
    for b in range(bsz):
        sl = slice(b * tt, (b + 1) * tt)
        h1 = h_ref[b] + moda_ref[b, :, 2 * d:3 * d] * ymix[sl, :]
        h1_scr[sl, :] = h1
        u_scr[sl, :] = _normmod(h1, g, modm_ref[b, :, :d], modm_ref[b, :, d:2 * d]).astype(BF16)
    acc = _mlp(u_scr[...], w1_ref, w2_ref)
    for b in range(bsz):
        sl = slice(b * tt, (b + 1) * tt)
        o_ref[b] = h1_scr[sl, :] + modm_ref[b, :, 2 * d:3 * d] * acc[sl, :]


def _post_s5(h, zo, moda, modm, g2, permt, w_glu, w1, w2):
    bsz, seq, d = h.shape
    tt = S5_TILE_TOKENS
    rows = bsz * tt // CHUNK
    h_spec = pl.BlockSpec((bsz, tt, d), lambda i: (0, i, 0))
    return pl.pallas_call(
        _post_s5_kernel,
        grid=(seq // tt,),
        in_specs=[
            h_spec,
            pl.BlockSpec((rows, CHUNK * d), lambda i: (i, 0)),
            pl.BlockSpec(moda.shape, lambda i: (0, 0, 0)),
            pl.BlockSpec(modm.shape, lambda i: (0, 0, 0)),
            pl.BlockSpec((1, d), lambda i: (0, 0)),
            pl.BlockSpec(permt.shape, lambda i: (0, 0)),
            _layer_spec(*w_glu), _layer_spec(*w1), _layer_spec(*w2),
        ],
        out_specs=h_spec,
        out_shape=jax.ShapeDtypeStruct((bsz, seq, d), F32),
        scratch_shapes=[pltpu.VMEM((bsz * tt, d), BF16), pltpu.VMEM((bsz * tt, d), F32),
                        pltpu.VMEM((bsz * tt, d), BF16)],
        compiler_params=_cparams(("parallel",)),
        name="post_s5_mlp",
    )(h, zo, moda, modm, g2.reshape(1, d), permt, w_glu[0], w1[0], w2[0])


def kernel(x, c, ln_g, ada_w, ada_b, ssm_lam_re, ssm_lam_im, ssm_log_dt, ssm_b_re, ssm_b_im, ssm_c_re, ssm_c_im, ssm_d, ssm_w_glu, kv_g, kv_ada_w, kv_ada_b, w_kv, attn_w_q, attn_w_o, mlp_w1, mlp_w2, final_g):
    bsz, seq, d = x.shape
    depth = ln_g.shape[0]
    n_s5 = ssm_lam_re.shape[0]
    assert d == D_MODEL and seq % (DILATIONS[-1] * ATT_BLOCK) == 0

    mods = _ada_mods(c, ada_w.reshape(depth * 2, d, 3 * d), ada_b.reshape(depth * 2, 3 * d))
    mods = mods.reshape(depth, 2, bsz, 1, 3 * d)
    kv_mod = _ada_mods(c, kv_ada_w[None], kv_ada_b[None]).reshape(bsz, 1, 2 * d)

    w1 = mlp_w1.astype(BF16)
    w2 = mlp_w2.astype(BF16)
    w_glu = ssm_w_glu.astype(BF16)
    w_q = attn_w_q.astype(BF16)
    w_o = attn_w_o.astype(BF16)
    perm_np = _s5_tile_perm(bsz)
    perm = jnp.asarray(perm_np, BF16)
    permt = jnp.asarray(perm_np.T, BF16)

    tables = jax.vmap(_s5_weights)(ssm_lam_re, ssm_lam_im, ssm_log_dt, ssm_b_re, ssm_b_im,
                                   ssm_c_re, ssm_c_im, ssm_d)
    h = x
    for layer in range(n_s5):
        z = _s5_pre(h, mods[layer, 0], ln_g[layer, 0], perm)
        zo = _s5_core(z, tables, layer, bsz)
        h = _post_s5(h, zo, mods[layer, 0], mods[layer, 1], ln_g[layer, 1], permt,
                     (w_glu, layer), (w1, layer), (w2, layer))

    kv_cols = tuple((i * d, (N_BRANCHES + i) * d) for i in range(N_BRANCHES))
    q_cols = tuple((i * d,) for i in range(N_BRANCHES))
    kvs = _norm_proj(h, kv_mod, kv_g, (w_kv.astype(BF16)[None], 0), kv_cols, 1.0, KV_PROJ_TILE)
    for layer in range(n_s5, depth):
        j = layer - n_s5
        qs = _norm_proj(h, mods[layer, 0], ln_g[layer, 0], (w_q, j), q_cols, Q_SCALE, Q_PROJ_TILE)
        o = _attention(qs, kvs)
        h = _post_attn(h, o, mods[layer, 0], mods[layer, 1], ln_g[layer, 1],
                       (w_o, j), (w1, layer), (w2, layer),
                       final_g if layer == depth - 1 else None)
    return h
```

```python
import functools
import math

import numpy as np

import jax
import jax.numpy as jnp
from jax import lax
from jax.experimental import pallas as pl
from jax.experimental.pallas import tpu as pltpu

F32 = jnp.float32
BF16 = jnp.bfloat16

D_MODEL = 1024
SSM_GROUP = 16
SSM_GROUPS = D_MODEL // SSM_GROUP
SSM_STATE = 64
HEAD_DIM = 64
N_HEADS = D_MODEL // HEAD_DIM
DILATIONS = (1, 4, 16)
N_BRANCHES = len(DILATIONS)
ATT_BLOCK = 128
ATT_GROUP = 16
ATT_PAIRS_PER_STEP = 2
N_STATS = 3
Q_SCALE = HEAD_DIM ** -0.5 * math.log2(math.e)
D_FF = 4 * D_MODEL
EPS = 1e-6
NEG = -1e30

LANES = 128
CHUNK = 16
GROUP_COLS = CHUNK * SSM_GROUP
PAIR_COLS = 2 * GROUP_COLS
N_PAIRS = SSM_GROUPS // 2
S5_TILE_TOKENS = 2 * CHUNK
MLP_ROWS = 1024
KV_PROJ_TILE = 512
Q_PROJ_TILE = 1024
VMEM_LIMIT = 56 * 1024 * 1024

HIGHEST = lax.Precision.HIGHEST
NT_DIMS = (((1,), (1,)), ((), ()))


def _cparams(sem):
    return pltpu.CompilerParams(dimension_semantics=sem, vmem_limit_bytes=VMEM_LIMIT)


def _layer_spec(stack, layer):
    zeros = (0,) * (stack.ndim - 1)
    return pl.BlockSpec((None,) + stack.shape[1:], lambda i: (layer,) + zeros, pipeline_mode=pl.Buffered(1))


def _normmod(x, g, shift, scale):
    ms = jnp.mean(x * x, axis=-1, keepdims=True)
    return (x * lax.rsqrt(ms + EPS) * g) * (1.0 + scale) + shift


def _ada_kernel(c_ref, w_ref, b_ref, o_ref):
    c = c_ref[...]
    sc = (c * jax.nn.sigmoid(c)).astype(BF16)
    o_ref[...] = jnp.dot(sc, w_ref[...].astype(BF16), preferred_element_type=F32) + b_ref[...]


def _ada_mods(c, w, b):
    n, d, width = w.shape
    bsz = c.shape[0]
    return pl.pallas_call(
        _ada_kernel,
        grid=(n,),
        in_specs=[
            pl.BlockSpec((bsz, d), lambda i: (0, 0)),
            pl.BlockSpec((None, d, width), lambda i: (i, 0, 0)),
            pl.BlockSpec((None, 1, width), lambda i: (i, 0, 0)),
        ],
        out_specs=pl.BlockSpec((None, bsz, width), lambda i: (i, 0, 0)),
        out_shape=jax.ShapeDtypeStruct((n, bsz, width), F32),
        compiler_params=_cparams(("parallel",)),
        name="ada_mods",
    )(c, w, b.reshape(n, 1, width))


def _proj_kernel(h_ref, mod_ref, g_ref, w_ref, o1_ref, o2_ref, o3_ref, us, us4, up4, up16, *, wcols, out_scale):
    d = D_MODEL
    tm = h_ref.shape[0]
    d1 = DILATIONS[1]
    q4, q16 = tm // d1, tm // DILATIONS[2]
    cw = o1_ref.shape[1]
    u = _normmod(h_ref[...], g_ref[...], mod_ref[:, :d], mod_ref[:, d:2 * d])
    for l in range(d // LANES):
        us[l] = u[:, l * LANES:(l + 1) * LANES]
    ub = u.astype(BF16)
    for n in range(cw // d):
        cols = slice(n * d, (n + 1) * d)
        res = jnp.dot(ub, w_ref[:, wcols[0][n]:wcols[0][n] + d], preferred_element_type=F32)
        o1_ref[:, cols] = (res * out_scale).astype(o1_ref.dtype)
    for l in range(d // LANES):
        for a in range(d1):
            part = us[l, pl.ds(a, q4, stride=d1), :]
            us4[l, a * q4:(a + 1) * q4, :] = part
            up4[a * q4:(a + 1) * q4, l * LANES:(l + 1) * LANES] = part.astype(BF16)
    for n in range(cw // d):
        res = jnp.dot(up4[...], w_ref[:, wcols[1][n]:wcols[1][n] + d], preferred_element_type=F32)
        for a in range(d1):
            o2_ref[a, :, n * d:(n + 1) * d] = (res[a * q4:(a + 1) * q4, :] * out_scale).astype(o2_ref.dtype)
    for l in range(d // LANES):
        for a1 in range(d1):
            for a2 in range(d1):
                r = d1 * a2 + a1
                up16[r * q16:(r + 1) * q16, l * LANES:(l + 1) * LANES] = us4[
                    l, pl.ds(a1 * q4 + a2, q16, stride=d1), :].astype(BF16)
    for n in range(cw // d):
        res = jnp.dot(up16[...], w_ref[:, wcols[2][n]:wcols[2][n] + d], preferred_element_type=F32)
        for r in range(DILATIONS[2]):
            o3_ref[r, :, n * d:(n + 1) * d] = (res[r * q16:(r + 1) * q16, :] * out_scale).astype(o3_ref.dtype)


def _norm_proj(h, mod, g, w, wcols, out_scale, tm):
    bsz, seq, d = h.shape
    cw = len(wcols[0]) * d
    per_b = seq // tm
    d1, d2 = DILATIONS[1], DILATIONS[2]
    o1, o2, o3 = pl.pallas_call(
        functools.partial(_proj_kernel, wcols=wcols, out_scale=out_scale),
        grid=(bsz * per_b,),
        in_specs=[
            pl.BlockSpec((None, tm, d), lambda i: (i // per_b, i % per_b, 0)),
            pl.BlockSpec((None, 1, mod.shape[-1]), lambda i: (i // per_b, 0, 0)),
            pl.BlockSpec((1, d), lambda i: (0, 0)),
            _layer_spec(*w),
        ],
        out_specs=[
            pl.BlockSpec((None, tm, cw), lambda i: (i // per_b, i % per_b, 0)),
            pl.BlockSpec((None, d1, None, tm // d1, cw), lambda i: (i // per_b, 0, i % per_b, 0, 0)),
            pl.BlockSpec((None, d2, None, tm // d2, cw), lambda i: (i // per_b, 0, i % per_b, 0, 0)),
        ],
        out_shape=[
            jax.ShapeDtypeStruct((bsz, seq, cw), BF16),
            jax.ShapeDtypeStruct((bsz, d1, per_b, tm // d1, cw), BF16),
            jax.ShapeDtypeStruct((bsz, d2, per_b, tm // d2, cw), BF16),
        ],
        scratch_shapes=[pltpu.VMEM((d // LANES, tm, LANES), F32), pltpu.VMEM((d // LANES, tm, LANES), F32),
                        pltpu.VMEM((tm, d), BF16), pltpu.VMEM((tm, d), BF16)],
        compiler_params=_cparams(("parallel",)),
        name="norm_proj",
    )(h, mod, g.reshape(1, d), w[0])
    return o1, o2.reshape(bsz, seq, cw), o3.reshape(bsz, seq, cw)


def _attn_group(blocks):
    blk = ATT_BLOCK
    lane = lax.broadcasted_iota(jnp.int32, (blk, LANES), 1)
    lo = lane < HEAD_DIM
    scores = []
    for qb, kcat, _, mask, _ in blocks:
        zero = jnp.zeros_like(qb)
        q2 = jnp.concatenate([jnp.where(lo, qb, zero), jnp.where(lo, zero, qb)], axis=0)
        s = lax.dot_general(q2, kcat, NT_DIMS, preferred_element_type=F32)
        scores.append(s + mask[...])
    soft = []
    for s, (_, _, _, _, old) in zip(scores, blocks):
        ms, ps = [], []
        for hh in range(2):
            tiles = [s[hh * blk:(hh + 1) * blk, t * LANES:(t + 1) * LANES] for t in range(s.shape[1] // LANES)]
            mx = tiles[0]
            for t in tiles[1:]:
                mx = jnp.maximum(mx, t)
            m = jnp.broadcast_to(jnp.max(mx, axis=-1, keepdims=True), (blk, LANES))
            if old is not None:
                m = jnp.maximum(m, old[0][hh])
            ms.append(m)
            ps.append(jnp.concatenate([jnp.exp2(t - m).astype(BF16) for t in tiles], axis=1))
        alpha = None
        if old is not None:
            alpha = jnp.exp2(jnp.where(lo, old[0][0], old[0][1]) - jnp.where(lo, ms[0], ms[1]))
        soft.append((ms, jnp.concatenate(ps, axis=0), alpha))
    outs = []
    for (ms, p, alpha), (_, _, vcat, _, old) in zip(soft, blocks):
        pv = jnp.dot(p, vcat, preferred_element_type=F32)
        acc = jnp.where(lo, pv[:blk, :LANES], pv[blk:, :LANES])
        den = jnp.where(lo, pv[:blk, LANES:], pv[blk:, LANES:])
        if old is not None:
            acc = alpha * old[1] + acc
            den = alpha * old[0][2] + den
        outs.append((ms + [den], acc))
    return outs


def _attn_kernel(*refs):
    n_in = 3 * N_BRANCHES
    ins, o_ref, scratch = refs[:n_in], refs[n_in], refs[n_in + 1:]
    for half in range(ATT_PAIRS_PER_STEP):
        lanes = pl.ds(half * LANES, LANES)
        _attn_pair(*[r.at[:, lanes] for r in ins], o_ref.at[:, lanes], *scratch)


def _attn_pair(q1, qd2, qd3, k1, kd2, kd3, v1, v2, v3, o_ref, tmp_nat, vd1, vd2, vd3,
               acc_ref, st_ref, acc2_ref, st2_ref, mask_pc, mask_c):
    blk = ATT_BLOCK
    seq = q1.shape[0]
    nblk = seq // blk
    d1, d2 = DILATIONS[1], DILATIONS[2]
    quarter = seq // d1
    per_res = quarter // blk
    assert d2 == d1 * d1 and seq == d2 * blk and nblk % ATT_GROUP == 0 and ATT_GROUP % per_res == 0

    @pl.when(jnp.logical_and(pl.program_id(0) == 0, pl.program_id(1) == 0))
    def _():
        for vd in (vd1, vd2, vd3):
            vd[:, LANES:] = jnp.ones((seq, LANES), BF16)
        qq = lax.broadcasted_iota(jnp.int32, (2 * blk, 2 * blk), 0) & (blk - 1)
        kk = lax.broadcasted_iota(jnp.int32, (2 * blk, 2 * blk), 1)
        valid = jnp.logical_or(jnp.logical_and(kk < blk, kk >= qq), jnp.logical_and(kk >= blk, kk - blk <= qq))
        mask_pc[...] = jnp.where(valid, 0.0, NEG)
        valid_c = (lax.broadcasted_iota(jnp.int32, (2 * blk, blk), 1)
                   <= (lax.broadcasted_iota(jnp.int32, (2 * blk, blk), 0) & (blk - 1)))
        mask_c[...] = jnp.where(valid_c, 0.0, NEG)

    for v, vd in ((v1, vd1), (v2, vd2), (v3, vd3)):
        vd[:, :LANES] = v[...]

    def run_group(blocks, dests):
        for (st_dst, acc_dst, rows), (st, acc) in zip(dests, _attn_group(blocks)):
            for k in range(N_STATS):
                st_dst[k, rows, :] = st[k]
            acc_dst[rows, :] = acc

    def load_old(st_src, acc_src, rows):
        return [st_src[k, rows, :] for k in range(N_STATS)], acc_src[rows, :]

    def b0_group(ns, first):
        blocks, dests = [], []
        for n in ns:
            if first and n == 0:
                rows = pl.ds(0, blk)
                blocks.append((q1[rows, :], k1[rows, :], vd1[rows, :], mask_c, None))
            else:
                rows = pl.ds(pl.multiple_of(n * blk, blk), blk)
                krows = pl.ds(pl.multiple_of((n - 1) * blk, blk), 2 * blk)
                blocks.append((q1[rows, :], k1[krows, :], vd1[krows, :], mask_pc, None))
            dests.append((st_ref, acc_ref, rows))
        run_group(blocks, dests)

    b0_group(list(range(ATT_GROUP)), True)

    def b0_body(i, carry):
        b0_group([ATT_GROUP * i + j for j in range(ATT_GROUP)], False)
        return carry

    lax.fori_loop(1, nblk // ATT_GROUP, b0_body, 0)

    res_per_trip = ATT_GROUP // per_res

    def b1_body(i, carry):
        blocks, dests = [], []
        for jr in range(res_per_trip):
            a1 = res_per_trip * i + jr
            base = a1 * quarter
            for n in range(per_res):
                rows = pl.ds(a1 + d1 * blk * n, blk, stride=d1)
                qrows = pl.ds(pl.multiple_of(base + n * blk, blk), blk)
                if n == 0:
                    krows, mask = qrows, mask_c
                else:
                    krows, mask = pl.ds(pl.multiple_of(base + (n - 1) * blk, blk), 2 * blk), mask_pc
                blocks.append((qd2[qrows, :], kd2[krows, :], vd2[krows, :], mask,
                               load_old(st_ref, acc_ref, rows)))
                dests.append((st2_ref, acc2_ref, qrows))
        run_group(blocks, dests)
        return carry

    lax.fori_loop(0, d1 // res_per_trip, b1_body, 0)

    def b2_body(i, carry):
        blocks, dests = [], []
        for j in range(ATT_GROUP):
            a1, a2 = j % d1, (ATT_GROUP // d1) * i + j // d1
            qrows = pl.ds(pl.multiple_of((ATT_GROUP * i + j) * blk, blk), blk)
            rows = pl.ds(a1 * quarter + a2, blk, stride=d1)
            blocks.append((qd3[qrows, :], kd3[qrows, :], vd3[qrows, :], mask_c,
                           load_old(st2_ref, acc2_ref, rows)))
            dests.append((st2_ref, acc2_ref, rows))
        run_group(blocks, dests)
        return carry

    lax.fori_loop(0, d2 // ATT_GROUP, b2_body, 0)

    for a in range(d1):
        seg = slice(a * quarter, (a + 1) * quarter)
        tmp_nat[pl.ds(a, quarter, stride=d1), :] = acc2_ref[seg, :] / st2_ref[N_STATS - 1, seg, :]
    o_ref[...] = tmp_nat[...].astype(o_ref.dtype)


def _attention(qs, kvs):
    bsz, seq, _ = qs[0].shape
    steps = D_MODEL // (ATT_PAIRS_PER_STEP * LANES)

    def slab(col0):
        return pl.BlockSpec((None, seq, ATT_PAIRS_PER_STEP * LANES), lambda b, hp: (b, 0, col0 + hp))

    in_specs = [slab(0)] * N_BRANCHES + [slab(0)] * N_BRANCHES + [slab(steps)] * N_BRANCHES
    return pl.pallas_call(
        _attn_kernel,
        grid=(bsz, steps),
        in_specs=in_specs,
        out_specs=slab(0),
        out_shape=jax.ShapeDtypeStruct((bsz, seq, D_MODEL), BF16),
        scratch_shapes=([pltpu.VMEM((seq, LANES), F32)] + [pltpu.VMEM((seq, 2 * LANES), BF16)] * 3
                        + [pltpu.VMEM((seq, LANES), F32), pltpu.VMEM((N_STATS, seq, LANES), F32)] * 2
                        + [pltpu.VMEM((2 * ATT_BLOCK, 2 * ATT_BLOCK), F32), pltpu.VMEM((2 * ATT_BLOCK, ATT_BLOCK), F32)]),
        compiler_params=_cparams(("arbitrary", "arbitrary")),
        name="attention",
    )(*qs, *kvs, *kvs)


def _regroup8(tiles):
    t = list(tiles)
    lane = lax.broadcasted_iota(jnp.int32, t[0].shape, 1)
    piece = lane // SSM_GROUP
    for dist in (4, 2, 1):
        bit = (piece & dist) != 0
        shift = dist * SSM_GROUP
        for a in range(8):
            if a & dist:
                continue
            b = a | dist
            ta, tb = t[a], t[b]
            t[a] = jnp.where(bit, pltpu.roll(tb, shift, 1), ta)
            t[b] = jnp.where(bit, tb, pltpu.roll(ta, LANES - shift, 1))
    return t


def _s5_tile_perm(bsz):
    tt = S5_TILE_TOKENS
    n = bsz * tt
    perm = np.zeros((n, n), np.float32)
    for b in range(bsz):
        for c in range(tt // CHUNK):
            for t in range(CHUNK):
                perm[(t * (tt // CHUNK) + c) * bsz + b, b * tt + c * CHUNK + t] = 1.0
    return perm


def _s5_pre_kernel(h_ref, mod_ref, g_ref, perm_ref, z_ref, u_scr, up_scr):
    d = D_MODEL
    bsz, tt, _ = h_ref.shape
    g = g_ref[...]
    for b in range(bsz):
        u_scr[b * tt:(b + 1) * tt, :] = _normmod(
            h_ref[b], g, mod_ref[b, :, :d], mod_ref[b, :, d:2 * d]).astype(BF16)
    up_scr[...] = jnp.dot(perm_ref[...], u_scr[...], preferred_element_type=F32)
    rows = z_ref.shape[0]
    for j in range(d // LANES):
        for half in range(CHUNK // 8):
            tiles = [up_scr[(8 * half + tl) * rows:(8 * half + tl + 1) * rows, j * LANES:(j + 1) * LANES]
                     for tl in range(8)]
            outs = _regroup8(tiles)
            for gl in range(8):
                col = (8 * j + gl) * GROUP_COLS + half * LANES
                z_ref[:, col:col + LANES] = outs[gl].astype(z_ref.dtype)


def _s5_pre(h, mod, g, perm):
    bsz, seq, d = h.shape
    tt = S5_TILE_TOKENS
    rows = bsz * tt // CHUNK
    return pl.pallas_call(
        _s5_pre_kernel,
        grid=(seq // tt,),
        in_specs=[
            pl.BlockSpec((bsz, tt, d), lambda i: (0, i, 0)),
            pl.BlockSpec(mod.shape, lambda i: (0, 0, 0)),
            pl.BlockSpec((1, d), lambda i: (0, 0)),
            pl.BlockSpec(perm.shape, lambda i: (0, 0)),
        ],
        out_specs=pl.BlockSpec((rows, CHUNK * d), lambda i: (i, 0)),
        out_shape=jax.ShapeDtypeStruct((bsz * seq // CHUNK, CHUNK * d), BF16),
        scratch_shapes=[pltpu.VMEM((bsz * tt, d), BF16), pltpu.VMEM((bsz * tt, d), F32)],
        compiler_params=_cparams(("parallel",)),
        name="s5_pre",
    )(h, mod, g.reshape(1, d), perm)


def _s5_core_kernel(z_ref, win_ref, wt_ref, wx_ref, dec_ref, d_ref, o_ref, s_scr, x_scr, *, bsz):
    gc = GROUP_COLS
    nchunk = z_ref.shape[0] // bsz
    s_scr[...] = jnp.dot(z_ref[...], win_ref[...].reshape(2 * gc, gc), preferred_element_type=F32)
    ar, ai = dec_ref[:, :LANES], dec_ref[:, LANES:]

    def step(c, carry):
        xr, xi = carry
        rows = pl.ds(pl.multiple_of(c * bsz, bsz), bsz)
        x_scr[rows, :LANES] = xr
        x_scr[rows, LANES:] = xi
        return (ar * xr - ai * xi + s_scr[rows, :LANES], ar * xi + ai * xr + s_scr[rows, LANES:])

    zero = jnp.zeros((bsz, LANES), F32)
    lax.fori_loop(0, nchunk, step, (zero, zero))
    x = x_scr[...].astype(BF16)
    for gi in range(2):
        z = z_ref[:, gi * gc:(gi + 1) * gc]
        y = jnp.dot(jnp.concatenate([z, x], axis=1), jnp.concatenate([wt_ref[gi], wx_ref[gi]], axis=0),
                    preferred_element_type=F32)
        y = y + d_ref[:, gi * gc:(gi + 1) * gc] * z.astype(F32)
        o_ref[:, gi * gc:(gi + 1) * gc] = jax.nn.gelu(y).astype(o_ref.dtype)


def _s5_core(z, tables, layer, bsz):
    nrow, width = z.shape
    w_in, w_toep, w_x, dec, d_perm = tables
    wspec = pl.BlockSpec((None, 2, GROUP_COLS, GROUP_COLS), lambda k: (layer, k, 0, 0))
    return pl.pallas_call(
        functools.partial(_s5_core_kernel, bsz=bsz),
        grid=(N_PAIRS,),
        in_specs=[
            pl.BlockSpec((nrow, PAIR_COLS), lambda k: (0, k)),
            wspec, wspec, wspec,
            pl.BlockSpec((None, 1, GROUP_COLS), lambda k: (layer, 0, k)),
            pl.BlockSpec((None, 1, PAIR_COLS), lambda k: (layer, 0, k)),
        ],
        out_specs=pl.BlockSpec((nrow, PAIR_COLS), lambda k: (0, k)),
        out_shape=jax.ShapeDtypeStruct((nrow, width), BF16),
        scratch_shapes=[pltpu.VMEM((nrow, GROUP_COLS), F32), pltpu.VMEM((nrow, GROUP_COLS), F32)],
        compiler_params=_cparams(("parallel",)),
        name="s5_core",
    )(z, w_in, w_toep, w_x, dec, d_perm)


def _s5_weights(lam_re, lam_im, log_dt, b_re, b_im, c_re, c_im, d_skip):
    g, p, c16 = SSM_GROUPS, SSM_STATE, SSM_GROUP
    lam = lax.complex(lam_re.astype(F32), lam_im.astype(F32))
    dt = jnp.exp(log_dt.astype(F32))[:, None]
    steps = jnp.arange(CHUNK + 1, dtype=F32)
    apow = jnp.exp((lam * dt)[None] * steps[:, None, None])
    a = apow[1]
    bbar = ((a - 1.0) / lam)[..., None] * lax.complex(b_re.astype(F32), b_im.astype(F32))
    cmat = lax.complex(c_re.astype(F32), c_im.astype(F32))
    odd = (jnp.arange(g) % 2)[:, None, None]

    win = apow[CHUNK - 1::-1][..., None] * bbar[None]

    def lay_in(x):
        return jnp.transpose(x.astype(BF16), (1, 0, 3, 2)).reshape(g, GROUP_COLS, p)

    win_re, win_im = lay_in(win.real), lay_in(win.imag)
    zeros = jnp.zeros_like(win_re)
    w_in = jnp.concatenate([
        jnp.where(odd == 0, win_re, zeros), jnp.where(odd == 1, win_re, zeros),
        jnp.where(odd == 0, win_im, zeros), jnp.where(odd == 1, win_im, zeros)], axis=-1)

    kern = jnp.einsum('gop,kgp,gpi->kgoi', cmat, apow[:CHUNK], bbar, precision=HIGHEST).real
    idx = jnp.arange(CHUNK)
    onehot = (idx[None, None, :] - idx[None, :, None] == idx[:, None, None]).astype(BF16)
    w_toep = jnp.einsum('kst,kgoi->gsito', onehot, kern.astype(BF16),
                        preferred_element_type=BF16)
    w_toep = w_toep.reshape(g, GROUP_COLS, GROUP_COLS)

    cw = cmat[None] * apow[1:, :, None, :]

    def lay_x(x):
        return jnp.transpose(x.astype(BF16), (1, 3, 0, 2)).reshape(g, p, GROUP_COLS)

    cw_re, cw_im = lay_x(cw.real), lay_x(-cw.imag)
    zx = jnp.zeros_like(cw_re)
    w_x = jnp.concatenate([
        jnp.where(odd == 0, cw_re, zx), jnp.where(odd == 1, cw_re, zx),
        jnp.where(odd == 0, cw_im, zx), jnp.where(odd == 1, cw_im, zx)], axis=1)

    a16 = apow[CHUNK].reshape(N_PAIRS, 2 * p)
    dec = jnp.concatenate([a16.real, a16.imag], axis=-1).reshape(1, N_PAIRS * GROUP_COLS)
    d_perm = jnp.broadcast_to(d_skip.astype(F32).reshape(g, 1, c16), (g, CHUNK, c16)).reshape(1, g * GROUP_COLS)
    return w_in, w_toep, w_x, dec, d_perm


def _mlp(u, w1_ref, w2_ref):
    d = D_MODEL
    acc = None
    for k in range(D_FF // d):
        a = jnp.dot(u, w1_ref[:, k * d:(k + 1) * d], preferred_element_type=F32)
        a = jnp.square(jnp.maximum(a, 0.0)).astype(BF16)
        part = jnp.dot(a, w2_ref[k * d:(k + 1) * d, :], preferred_element_type=F32)
        acc = part if acc is None else acc + part
    return acc


def _final_norm(h, fg_ref):
    ms = jnp.mean(h * h, axis=-1, keepdims=True)
    return h * lax.rsqrt(ms + EPS) * fg_ref[...]


def _post_attn_kernel(*refs, final):
    if final:
        h_ref, y_ref, moda_ref, modm_ref, g_ref, wp_ref, w1_ref, w2_ref, fg_ref, o_ref = refs
    else:
        h_ref, y_ref, moda_ref, modm_ref, g_ref, wp_ref, w1_ref, w2_ref, o_ref = refs
    d = D_MODEL
    ymix = jnp.dot(y_ref[...], wp_ref[...], preferred_element_type=F32)
    h1 = h_ref[...] + moda_ref[:, 2 * d:3 * d] * ymix
    u = _normmod(h1, g_ref[...], modm_ref[:, :d], modm_ref[:, d:2 * d]).astype(BF16)
    h2 = h1 + modm_ref[:, 2 * d:3 * d] * _mlp(u, w1_ref, w2_ref)
    o_ref[...] = _final_norm(h2, fg_ref) if final else h2


def _post_attn(h, o, moda, modm, g2, w_o, w1, w2, final_g):
    bsz, seq, d = h.shape
    tm = MLP_ROWS
    per_b = seq // tm
    row_spec = pl.BlockSpec((None, tm, d), lambda i: (i // per_b, i % per_b, 0))
    final = final_g is not None
    in_specs = [
        row_spec, row_spec,
        pl.BlockSpec((None, 1, moda.shape[-1]), lambda i: (i // per_b, 0, 0)),
        pl.BlockSpec((None, 1, modm.shape[-1]), lambda i: (i // per_b, 0, 0)),
        pl.BlockSpec((1, d), lambda i: (0, 0)),
        _layer_spec(*w_o), _layer_spec(*w1), _layer_spec(*w2),
    ]
    args = [h, o, moda, modm, g2.reshape(1, d), w_o[0], w1[0], w2[0]]
    if final:
        in_specs.append(pl.BlockSpec((1, d), lambda i: (0, 0)))
        args.append(final_g.reshape(1, d))
    return pl.pallas_call(
        functools.partial(_post_attn_kernel, final=final),
        grid=(bsz * per_b,),
        in_specs=in_specs,
        out_specs=row_spec,
        out_shape=jax.ShapeDtypeStruct((bsz, seq, d), F32),
        compiler_params=_cparams(("parallel",)),
        name="post_attn_mlp",
    )(*args)


def _post_s5_kernel(h_ref, z_ref, moda_ref, modm_ref, g_ref, permt_ref, wp_ref, w1_ref, w2_ref, o_ref,
                    z_scr, h1_scr, u_scr):
    d = D_MODEL
    bsz, tt, _ = h_ref.shape
    rows = z_ref.shape[0]
    for j in range(d // LANES):
        for half in range(CHUNK // 8):
            tiles = []
            for gl in range(8):
                col = (8 * j + gl) * GROUP_COLS + half * LANES
                tiles.append(z_ref[:, col:col + LANES].astype(F32))
            outs = _regroup8(tiles)
            for tl in range(8):
                t = 8 * half + tl
                z_scr[t * rows:(t + 1) * rows, j * LANES:(j + 1) * LANES] = outs[tl].astype(BF16)
    zn = jnp.dot(permt_ref[...], z_scr[...], preferred_element_type=F32).astype(BF16)
    zz = jnp.dot(zn, wp_ref[...], preferred_element_type=F32)
    ymix = zz[:, :d] * jax.nn.sigmoid(zz[:, d:])
    g = g_ref[...]
    for b in range(bsz):
        sl = slice(b * tt, (b + 1) * tt)
        h1 = h_ref[b] + moda_ref[b, :, 2 * d:3 * d] * ymix[sl, :]
        h1_scr[sl, :] = h1
        u_scr[sl, :] = _normmod(h1, g, modm_ref[b, :, :d], modm_ref[b, :, d:2 * d]).astype(BF16)
    acc = _mlp(u_scr[...], w1_ref, w2_ref)
    for b in range(bsz):
        sl = slice(b * tt, (b + 1) * tt)
        o_ref[b] = h1_scr[sl, :] + modm_ref[b, :, 2 * d:3 * d] * acc[sl, :]


def _post_s5(h, zo, moda, modm, g2, permt, w_glu, w1, w2):
    bsz, seq, d = h.shape
    tt = S5_TILE_TOKENS
    rows = bsz * tt // CHUNK
    h_spec = pl.BlockSpec((bsz, tt, d), lambda i: (0, i, 0))
    return pl.pallas_call(
        _post_s5_kernel,
        grid=(seq // tt,),
        in_specs=[
            h_spec,
            pl.BlockSpec((rows, CHUNK * d), lambda i: (i, 0)),
            pl.BlockSpec(moda.shape, lambda i: (0, 0, 0)),
            pl.BlockSpec(modm.shape, lambda i: (0, 0, 0)),
            pl.BlockSpec((1, d), lambda i: (0, 0)),
            pl.BlockSpec(permt.shape, lambda i: (0, 0)),
            _layer_spec(*w_glu), _layer_spec(*w1), _layer_spec(*w2),
        ],
        out_specs=h_spec,
        out_shape=jax.ShapeDtypeStruct((bsz, seq, d), F32),
        scratch_shapes=[pltpu.VMEM((bsz * tt, d), BF16), pltpu.VMEM((bsz * tt, d), F32),
                        pltpu.VMEM((bsz * tt, d), BF16)],
        compiler_params=_cparams(("parallel",)),
        name="post_s5_mlp",
    )(h, zo, moda, modm, g2.reshape(1, d), permt, w_glu[0], w1[0], w2[0])


def kernel(x, c, ln_g, ada_w, ada_b, ssm_lam_re, ssm_lam_im, ssm_log_dt, ssm_b_re, ssm_b_im, ssm_c_re, ssm_c_im, ssm_d, ssm_w_glu, kv_g, kv_ada_w, kv_ada_b, w_kv, attn_w_q, attn_w_o, mlp_w1, mlp_w2, final_g):
    bsz, seq, d = x.shape
    depth = ln_g.shape[0]
    n_s5 = ssm_lam_re.shape[0]
    assert d == D_MODEL and seq % (DILATIONS[-1] * ATT_BLOCK) == 0

    mods = _ada_mods(c, ada_w.reshape(depth * 2, d, 3 * d), ada_b.reshape(depth * 2, 3 * d))
    mods = mods.reshape(depth, 2, bsz, 1, 3 * d)
    kv_mod = _ada_mods(c, kv_ada_w[None], kv_ada_b[None]).reshape(bsz, 1, 2 * d)

    w1 = mlp_w1.astype(BF16)
    w2 = mlp_w2.astype(BF16)
    w_glu = ssm_w_glu.astype(BF16)
    w_q = attn_w_q.astype(BF16)
    w_o = attn_w_o.astype(BF16)
    perm_np = _s5_tile_perm(bsz)
    perm = jnp.asarray(perm_np, BF16)
    permt = jnp.asarray(perm_np.T, BF16)

    tables = jax.vmap(_s5_weights)(ssm_lam_re, ssm_lam_im, ssm_log_dt, ssm_b_re, ssm_b_im,
                                   ssm_c_re, ssm_c_im, ssm_d)
    h = x
    for layer in range(n_s5):
        z = _s5_pre(h, mods[layer, 0], ln_g[layer, 0], perm)
        zo = _s5_core(z, tables, layer, bsz)
        h = _post_s5(h, zo, mods[layer, 0], mods[layer, 1], ln_g[layer, 1], permt,
                     (w_glu, layer), (w1, layer), (w2, layer))

    kv_cols = tuple((i * d, (N_BRANCHES + i) * d) for i in range(N_BRANCHES))
    q_cols = tuple((i * d,) for i in range(N_BRANCHES))
    kvs = _norm_proj(h, kv_mod, kv_g, (w_kv.astype(BF16)[None], 0), kv_cols, 1.0, KV_PROJ_TILE)
    for layer in range(n_s5, depth):
        j = layer - n_s5
        qs = _norm_proj(h, mods[layer, 0], ln_g[layer, 0], (w_q, j), q_cols, Q_SCALE, Q_PROJ_TILE)
        o = _attention(qs, kvs)
        h = _post_attn(h, o, mods[layer, 0], mods[layer, 1], ln_g[layer, 1],
                       (w_o, j), (w1, layer), (w2, layer),
                       final_g if layer == depth - 1 else None)
    return h
```

```python
import functools
import math

import numpy as np

import jax
import jax.numpy as jnp
from jax import lax
from jax.experimental import pallas as pl
from jax.experimental.pallas import tpu as pltpu

F32 = jnp.float32
BF16 = jnp.bfloat16

D_MODEL = 1024
SSM_GROUP = 16
SSM_GROUPS = D_MODEL // SSM_GROUP
SSM_STATE = 64
HEAD_DIM = 64
N_HEADS = D_MODEL // HEAD_DIM
DILATIONS = (1, 4, 16)
N_BRANCHES = len(DILATIONS)
ATT_BLOCK = 128
ATT_GROUP = 16
ATT_PAIRS_PER_STEP = 2
N_STATS = 3
Q_SCALE = HEAD_DIM ** -0.5 * math.log2(math.e)
D_FF = 4 * D_MODEL
EPS = 1e-6
NEG = -1e30

LANES = 128
CHUNK = 16
GROUP_COLS = CHUNK * SSM_GROUP
PAIR_COLS = 2 * GROUP_COLS
N_PAIRS = SSM_GROUPS // 2
S5_TILE_TOKENS = 2 * CHUNK
MLP_ROWS = 1024
KV_PROJ_TILE = 512
Q_PROJ_TILE = 1024
VMEM_LIMIT = 56 * 1024 * 1024

HIGHEST = lax.Precision.HIGHEST
NT_DIMS = (((1,), (1,)), ((), ()))


def _cparams(sem):
    return pltpu.CompilerParams(dimension_semantics=sem, vmem_limit_bytes=VMEM_LIMIT)


def _layer_spec(stack, layer):
    zeros = (0,) * (stack.ndim - 1)
    return pl.BlockSpec((None,) + stack.shape[1:], lambda i: (layer,) + zeros, pipeline_mode=pl.Buffered(1))


def _normmod(x, g, shift, scale):
    ms = jnp.mean(x * x, axis=-1, keepdims=True)
    return (x * lax.rsqrt(ms + EPS) * g) * (1.0 + scale) + shift


def _ada_kernel(c_ref, w_ref, b_ref, o_ref):
    c = c_ref[...]
    sc = (c * jax.nn.sigmoid(c)).astype(BF16)
    o_ref[...] = jnp.dot(sc, w_ref[...].astype(BF16), preferred_element_type=F32) + b_ref[...]


def _ada_mods(c, w, b):
    n, d, width = w.shape
    bsz = c.shape[0]
    return pl.pallas_call(
        _ada_kernel,
        grid=(n,),
        in_specs=[
            pl.BlockSpec((bsz, d), lambda i: (0, 0)),
            pl.BlockSpec((None, d, width), lambda i: (i, 0, 0)),
            pl.BlockSpec((None, 1, width), lambda i: (i, 0, 0)),
        ],
        out_specs=pl.BlockSpec((None, bsz, width), lambda i: (i, 0, 0)),
        out_shape=jax.ShapeDtypeStruct((n, bsz, width), F32),
        compiler_params=_cparams(("parallel",)),
        name="ada_mods",
    )(c, w, b.reshape(n, 1, width))


def _proj_kernel(h_ref, mod_ref, g_ref, w_ref, o1_ref, o2_ref, o3_ref, us, us4, up4, up16, *, wcols, out_scale):
    d = D_MODEL
    tm = h_ref.shape[0]
    d1 = DILATIONS[1]
    q4, q16 = tm // d1, tm // DILATIONS[2]
    cw = o1_ref.shape[1]
    u = _normmod(h_ref[...], g_ref[...], mod_ref[:, :d], mod_ref[:, d:2 * d])
    for l in range(d // LANES):
        us[l] = u[:, l * LANES:(l + 1) * LANES]
    ub = u.astype(BF16)
    for n in range(cw // d):
        cols = slice(n * d, (n + 1) * d)
        res = jnp.dot(ub, w_ref[:, wcols[0][n]:wcols[0][n] + d], preferred_element_type=F32)
        o1_ref[:, cols] = (res * out_scale).astype(o1_ref.dtype)
    for l in range(d // LANES):
        for a in range(d1):
            part = us[l, pl.ds(a, q4, stride=d1), :]
            us4[l, a * q4:(a + 1) * q4, :] = part
            up4[a * q4:(a + 1) * q4, l * LANES:(l + 1) * LANES] = part.astype(BF16)
    for n in range(cw // d):
        res = jnp.dot(up4[...], w_ref[:, wcols[1][n]:wcols[1][n] + d], preferred_element_type=F32)
        for a in range(d1):
            o2_ref[a, :, n * d:(n + 1) * d] = (res[a * q4:(a + 1) * q4, :] * out_scale).astype(o2_ref.dtype)
    for l in range(d // LANES):
        for a1 in range(d1):
            for a2 in range(d1):
                r = d1 * a2 + a1
                up16[r * q16:(r + 1) * q16, l * LANES:(l + 1) * LANES] = us4[
                    l, pl.ds(a1 * q4 + a2, q16, stride=d1), :].astype(BF16)
    for n in range(cw // d):
        res = jnp.dot(up16[...], w_ref[:, wcols[2][n]:wcols[2][n] + d], preferred_element_type=F32)
        for r in range(DILATIONS[2]):
            o3_ref[r, :, n * d:(n + 1) * d] = (res[r * q16:(r + 1) * q16, :] * out_scale).astype(o3_ref.dtype)


def _norm_proj(h, mod, g, w, wcols, out_scale, tm):
    bsz, seq, d = h.shape
    cw = len(wcols[0]) * d
    per_b = seq // tm
    d1, d2 = DILATIONS[1], DILATIONS[2]
    o1, o2, o3 = pl.pallas_call(
        functools.partial(_proj_kernel, wcols=wcols, out_scale=out_scale),
        grid=(bsz * per_b,),
        in_specs=[
            pl.BlockSpec((None, tm, d), lambda i: (i // per_b, i % per_b, 0)),
            pl.BlockSpec((None, 1, mod.shape[-1]), lambda i: (i // per_b, 0, 0)),
            pl.BlockSpec((1, d), lambda i: (0, 0)),
            _layer_spec(*w),
        ],
        out_specs=[
            pl.BlockSpec((None, tm, cw), lambda i: (i // per_b, i % per_b, 0)),
            pl.BlockSpec((None, d1, None, tm // d1, cw), lambda i: (i // per_b, 0, i % per_b, 0, 0)),
            pl.BlockSpec((None, d2, None, tm // d2, cw), lambda i: (i // per_b, 0, i % per_b, 0, 0)),
        ],
        out_shape=[
            jax.ShapeDtypeStruct((bsz, seq, cw), BF16),
            jax.ShapeDtypeStruct((bsz, d1, per_b, tm // d1, cw), BF16),
            jax.ShapeDtypeStruct((bsz, d2, per_b, tm // d2, cw), BF16),
        ],
        scratch_shapes=[pltpu.VMEM((d // LANES, tm, LANES), F32), pltpu.VMEM((d // LANES, tm, LANES), F32),
                        pltpu.VMEM((tm, d), BF16), pltpu.VMEM((tm, d), BF16)],
        compiler_params=_cparams(("parallel",)),
        name="norm_proj",
    )(h, mod, g.reshape(1, d), w[0])
    return o1, o2.reshape(bsz, seq, cw), o3.reshape(bsz, seq, cw)


def _attn_group(blocks):
    blk = ATT_BLOCK
    lane = lax.broadcasted_iota(jnp.int32, (blk, LANES), 1)
    lo = lane < HEAD_DIM
    scores = []
    for qb, kcat, _, mask, _ in blocks:
        zero = jnp.zeros_like(qb)
        q2 = jnp.concatenate([jnp.where(lo, qb, zero), jnp.where(lo, zero, qb)], axis=0)
        s = lax.dot_general(q2, kcat, NT_DIMS, preferred_element_type=F32)
        scores.append(s + mask[...])
    soft = []
    for s, (_, _, _, _, old) in zip(scores, blocks):
        ms, ps = [], []
        for hh in range(2):
            tiles = [s[hh * blk:(hh + 1) * blk, t * LANES:(t + 1) * LANES] for t in range(s.shape[1] // LANES)]
            mx = tiles[0]
            for t in tiles[1:]:
                mx = jnp.maximum(mx, t)
            m = jnp.broadcast_to(jnp.max(mx, axis=-1, keepdims=True), (blk, LANES))
            if old is not None:
                m = jnp.maximum(m, old[0][hh])
            ms.append(m)
            ps.append(jnp.concatenate([jnp.exp2(t - m).astype(BF16) for t in tiles], axis=1))
        alpha = None
        if old is not None:
            alpha = jnp.exp2(jnp.where(lo, old[0][0], old[0][1]) - jnp.where(lo, ms[0], ms[1]))
        soft.append((ms, jnp.concatenate(ps, axis=0), alpha))
    outs = []
    for (ms, p, alpha), (_, _, vcat, _, old) in zip(soft, blocks):
        pv = jnp.dot(p, vcat, preferred_element_type=F32)
        acc = jnp.where(lo, pv[:blk, :LANES], pv[blk:, :LANES])
        den = jnp.where(lo, pv[:blk, LANES:], pv[blk:, LANES:])
        if old is not None:
            acc = alpha * old[1] + acc
            den = alpha * old[0][2] + den
        outs.append((ms + [den], acc))
    return outs


def _attn_kernel(*refs):
    n_in = 3 * N_BRANCHES
    ins, o_ref, scratch = refs[:n_in], refs[n_in], refs[n_in + 1:]
    for half in range(ATT_PAIRS_PER_STEP):
        lanes = pl.ds(half * LANES, LANES)
        _attn_pair(*[r.at[:, lanes] for r in ins], o_ref.at[:, lanes], *scratch)


def _attn_pair(q1, qd2, qd3, k1, kd2, kd3, v1, v2, v3, o_ref, tmp_nat, vd1, vd2, vd3,
               acc_ref, st_ref, acc2_ref, st2_ref, mask_pc, mask_c):
    blk = ATT_BLOCK
    seq = q1.shape[0]
    nblk = seq // blk
    d1, d2 = DILATIONS[1], DILATIONS[2]
    quarter = seq // d1
    per_res = quarter // blk
    assert d2 == d1 * d1 and seq == d2 * blk and nblk % ATT_GROUP == 0 and ATT_GROUP % per_res == 0

    @pl.when(jnp.logical_and(pl.program_id(0) == 0, pl.program_id(1) == 0))
    def _():
        for vd in (vd1, vd2, vd3):
            vd[:, LANES:] = jnp.ones((seq, LANES), BF16)
        qq = lax.broadcasted_iota(jnp.int32, (2 * blk, 2 * blk), 0) & (blk - 1)
        kk = lax.broadcasted_iota(jnp.int32, (2 * blk, 2 * blk), 1)
        valid = jnp.logical_or(jnp.logical_and(kk < blk, kk >= qq), jnp.logical_and(kk >= blk, kk - blk <= qq))
        mask_pc[...] = jnp.where(valid, 0.0, NEG)
        valid_c = (lax.broadcasted_iota(jnp.int32, (2 * blk, blk), 1)
                   <= (lax.broadcasted_iota(jnp.int32, (2 * blk, blk), 0) & (blk - 1)))
        mask_c[...] = jnp.where(valid_c, 0.0, NEG)

    for v, vd in ((v1, vd1), (v2, vd2), (v3, vd3)):
        vd[:, :LANES] = v[...]

    def run_group(blocks, dests):
        for (st_dst, acc_dst, rows), (st, acc) in zip(dests, _attn_group(blocks)):
            for k in range(N_STATS):
                st_dst[k, rows, :] = st[k]
            acc_dst[rows, :] = acc

    def load_old(st_src, acc_src, rows):
        return [st_src[k, rows, :] for k in range(N_STATS)], acc_src[rows, :]

    def b0_group(ns, first):
        blocks, dests = [], []
        for n in ns:
            if first and n == 0:
                rows = pl.ds(0, blk)
                blocks.append((q1[rows, :], k1[rows, :], vd1[rows, :], mask_c, None))
            else:
                rows = pl.ds(pl.multiple_of(n * blk, blk), blk)
                krows = pl.ds(pl.multiple_of((n - 1) * blk, blk), 2 * blk)
                blocks.append((q1[rows, :], k1[krows, :], vd1[krows, :], mask_pc, None))
            dests.append((st_ref, acc_ref, rows))
        run_group(blocks, dests)

    b0_group(list(range(ATT_GROUP)), True)

    def b0_body(i, carry):
        b0_group([ATT_GROUP * i + j for j in range(ATT_GROUP)], False)
        return carry

    lax.fori_loop(1, nblk // ATT_GROUP, b0_body, 0)

    res_per_trip = ATT_GROUP // per_res

    def b1_body(i, carry):
        blocks, dests = [], []
        for jr in range(res_per_trip):
            a1 = res_per_trip * i + jr
            base = a1 * quarter
            for n in range(per_res):
                rows = pl.ds(a1 + d1 * blk * n, blk, stride=d1)
                qrows = pl.ds(pl.multiple_of(base + n * blk, blk), blk)
                if n == 0:
                    krows, mask = qrows, mask_c
                else:
                    krows, mask = pl.ds(pl.multiple_of(base + (n - 1) * blk, blk), 2 * blk), mask_pc
                blocks.append((qd2[qrows, :], kd2[krows, :], vd2[krows, :], mask,
                               load_old(st_ref, acc_ref, rows)))
                dests.append((st2_ref, acc2_ref, qrows))
        run_group(blocks, dests)
        return carry

    lax.fori_loop(0, d1 // res_per_trip, b1_body, 0)

    def b2_body(i, carry):
        blocks, dests = [], []
        for j in range(ATT_GROUP):
            a1, a2 = j % d1, (ATT_GROUP // d1) * i + j // d1
            qrows = pl.ds(pl.multiple_of((ATT_GROUP * i + j) * blk, blk), blk)
            rows = pl.ds(a1 * quarter + a2, blk, stride=d1)
            blocks.append((qd3[qrows, :], kd3[qrows, :], vd3[qrows, :], mask_c,
                           load_old(st2_ref, acc2_ref, rows)))
            dests.append((st2_ref, acc2_ref, rows))
        run_group(blocks, dests)
        return carry

    lax.fori_loop(0, d2 // ATT_GROUP, b2_body, 0)

    for a in range(d1):
        seg = slice(a * quarter, (a + 1) * quarter)
        tmp_nat[pl.ds(a, quarter, stride=d1), :] = acc2_ref[seg, :] / st2_ref[N_STATS - 1, seg, :]
    o_ref[...] = tmp_nat[...].astype(o_ref.dtype)


def _attention(qs, kvs):
    bsz, seq, _ = qs[0].shape
    steps = D_MODEL // (ATT_PAIRS_PER_STEP * LANES)

    def slab(col0):
        return pl.BlockSpec((None, seq, ATT_PAIRS_PER_STEP * LANES), lambda b, hp: (b, 0, col0 + hp))

    in_specs = [slab(0)] * N_BRANCHES + [slab(0)] * N_BRANCHES + [slab(steps)] * N_BRANCHES
    return pl.pallas_call(
        _attn_kernel,
        grid=(bsz, steps),
        in_specs=in_specs,
        out_specs=slab(0),
        out_shape=jax.ShapeDtypeStruct((bsz, seq, D_MODEL), BF16),
        scratch_shapes=([pltpu.VMEM((seq, LANES), F32)] + [pltpu.VMEM((seq, 2 * LANES), BF16)] * 3
                        + [pltpu.VMEM((seq, LANES), F32), pltpu.VMEM((N_STATS, seq, LANES), F32)] * 2
                        + [pltpu.VMEM((2 * ATT_BLOCK, 2 * ATT_BLOCK), F32), pltpu.VMEM((2 * ATT_BLOCK, ATT_BLOCK), F32)]),
        compiler_params=_cparams(("arbitrary", "arbitrary")),
        name="attention",
    )(*qs, *kvs, *kvs)


def _regroup8(tiles):
    t = list(tiles)
    lane = lax.broadcasted_iota(jnp.int32, t[0].shape, 1)
    piece = lane // SSM_GROUP
    for dist in (4, 2, 1):
        bit = (piece & dist) != 0
        shift = dist * SSM_GROUP
        for a in range(8):
            if a & dist:
                continue
            b = a | dist
            ta, tb = t[a], t[b]
            t[a] = jnp.where(bit, pltpu.roll(tb, shift, 1), ta)
            t[b] = jnp.where(bit, tb, pltpu.roll(ta, LANES - shift, 1))
    return t


def _s5_tile_perm(bsz):
    tt = S5_TILE_TOKENS
    n = bsz * tt
    perm = np.zeros((n, n), np.float32)
    for b in range(bsz):
        for c in range(tt // CHUNK):
            for t in range(CHUNK):
                perm[(t * (tt // CHUNK) + c) * bsz + b, b * tt + c * CHUNK + t] = 1.0
    return perm


def _s5_pre_kernel(h_ref, mod_ref, g_ref, perm_ref, z_ref, u_scr, up_scr):
    d = D_MODEL
    bsz, tt, _ = h_ref.shape
    g = g_ref[...]
    for b in range(bsz):
        u_scr[b * tt:(b + 1) * tt, :] = _normmod(
            h_ref[b], g, mod_ref[b, :, :d], mod_ref[b, :, d:2 * d]).astype(BF16)
    up_scr[...] = jnp.dot(perm_ref[...], u_scr[...], preferred_element_type=F32)
    rows = z_ref.shape[0]
    for j in range(d // LANES):
        for half in range(CHUNK // 8):
            tiles = [up_scr[(8 * half + tl) * rows:(8 * half + tl + 1) * rows, j * LANES:(j + 1) * LANES]
                     for tl in range(8)]
            outs = _regroup8(tiles)
            for gl in range(8):
                col = (8 * j + gl) * GROUP_COLS + half * LANES
                z_ref[:, col:col + LANES] = outs[gl].astype(z_ref.dtype)


def _s5_pre(h, mod, g, perm):
    bsz, seq, d = h.shape
    tt = S5_TILE_TOKENS
    rows = bsz * tt // CHUNK
    return pl.pallas_call(
        _s5_pre_kernel,
        grid=(seq // tt,),
        in_specs=[
            pl.BlockSpec((bsz, tt, d), lambda i: (0, i, 0)),
            pl.BlockSpec(mod.shape, lambda i: (0, 0, 0)),
            pl.BlockSpec((1, d), lambda i: (0, 0)),
            pl.BlockSpec(perm.shape, lambda i: (0, 0)),
        ],
        out_specs=pl.BlockSpec((rows, CHUNK * d), lambda i: (i, 0)),
        out_shape=jax.ShapeDtypeStruct((bsz * seq // CHUNK, CHUNK * d), BF16),
        scratch_shapes=[pltpu.VMEM((bsz * tt, d), BF16), pltpu.VMEM((bsz * tt, d), F32)],
        compiler_params=_cparams(("parallel",)),
        name="s5_pre",
    )(h, mod, g.reshape(1, d), perm)


def _s5_core_kernel(z_ref, win_ref, wt_ref, wx_ref, dec_ref, d_ref, o_ref, s_scr, x_scr, *, bsz):
    gc = GROUP_COLS
    nchunk = z_ref.shape[0] // bsz
    s_scr[...] = jnp.dot(z_ref[...], win_ref[...].reshape(2 * gc, gc), preferred_element_type=F32)
    ar, ai = dec_ref[:, :LANES], dec_ref[:, LANES:]

    def step(c, carry):
        xr, xi = carry
        rows = pl.ds(pl.multiple_of(c * bsz, bsz), bsz)
        x_scr[rows, :LANES] = xr
        x_scr[rows, LANES:] = xi
        return (ar * xr - ai * xi + s_scr[rows, :LANES], ar * xi + ai * xr + s_scr[rows, LANES:])

    zero = jnp.zeros((bsz, LANES), F32)
    lax.fori_loop(0, nchunk, step, (zero, zero))
    x = x_scr[...].astype(BF16)
    for gi in range(2):
        z = z_ref[:, gi * gc:(gi + 1) * gc]
        y = jnp.dot(jnp.concatenate([z, x], axis=1), jnp.concatenate([wt_ref[gi], wx_ref[gi]], axis=0),
                    preferred_element_type=F32)
        y = y + d_ref[:, gi * gc:(gi + 1) * gc] * z.astype(F32)
        o_ref[:, gi * gc:(gi + 1) * gc] = jax.nn.gelu(y).astype(o_ref.dtype)


def _s5_core(z, tables, layer, bsz):
    nrow, width = z.shape
    w_in, w_toep, w_x, dec, d_perm = tables
    wspec = pl.BlockSpec((None, 2, GROUP_COLS, GROUP_COLS), lambda k: (layer, k, 0, 0))
    return pl.pallas_call(
        functools.partial(_s5_core_kernel, bsz=bsz),
        grid=(N_PAIRS,),
        in_specs=[
            pl.BlockSpec((nrow, PAIR_COLS), lambda k: (0, k)),
            wspec, wspec, wspec,
            pl.BlockSpec((None, 1, GROUP_COLS), lambda k: (layer, 0, k)),
            pl.BlockSpec((None, 1, PAIR_COLS), lambda k: (layer, 0, k)),
        ],
        out_specs=pl.BlockSpec((nrow, PAIR_COLS), lambda k: (0, k)),
        out_shape=jax.ShapeDtypeStruct((nrow, width), BF16),
        scratch_shapes=[pltpu.VMEM((nrow, GROUP_COLS), F32), pltpu.VMEM((nrow, GROUP_COLS), F32)],
        compiler_params=_cparams(("parallel",)),
        name="s5_core",
    )(z, w_in, w_toep, w_x, dec, d_perm)


def _s5_weights(lam_re, lam_im, log_dt, b_re, b_im, c_re, c_im, d_skip):
    g, p, c16 = SSM_GROUPS, SSM_STATE, SSM_GROUP
    lam = lax.complex(lam_re.astype(F32), lam_im.astype(F32))
    dt = jnp.exp(log_dt.astype(F32))[:, None]
    steps = jnp.arange(CHUNK + 1, dtype=F32)
    apow = jnp.exp((lam * dt)[None] * steps[:, None, None])
    a = apow[1]
    bbar = ((a - 1.0) / lam)[..., None] * lax.complex(b_re.astype(F32), b_im.astype(F32))
    cmat = lax.complex(c_re.astype(F32), c_im.astype(F32))
    odd = (jnp.arange(g) % 2)[:, None, None]

    win = apow[CHUNK - 1::-1][..., None] * bbar[None]

    def lay_in(x):
        return jnp.transpose(x.astype(BF16), (1, 0, 3, 2)).reshape(g, GROUP_COLS, p)

    win_re, win_im = lay_in(win.real), lay_in(win.imag)
    zeros = jnp.zeros_like(win_re)
    w_in = jnp.concatenate([
        jnp.where(odd == 0, win_re, zeros), jnp.where(odd == 1, win_re, zeros),
        jnp.where(odd == 0, win_im, zeros), jnp.where(odd == 1, win_im, zeros)], axis=-1)

    kern = jnp.einsum('gop,kgp,gpi->kgoi', cmat, apow[:CHUNK], bbar, precision=HIGHEST).real
    idx = jnp.arange(CHUNK)
    onehot = (idx[None, None, :] - idx[None, :, None] == idx[:, None, None]).astype(BF16)
    w_toep = jnp.einsum('kst,kgoi->gsito', onehot, kern.astype(BF16),
                        preferred_element_type=BF16)
    w_toep = w_toep.reshape(g, GROUP_COLS, GROUP_COLS)

    cw = cmat[None] * apow[1:, :, None, :]

    def lay_x(x):
        return jnp.transpose(x.astype(BF16), (1, 3, 0, 2)).reshape(g, p, GROUP_COLS)

    cw_re, cw_im = lay_x(cw.real), lay_x(-cw.imag)
    zx = jnp.zeros_like(cw_re)
    w_x = jnp.concatenate([
        jnp.where(odd == 0, cw_re, zx), jnp.where(odd == 1, cw_re, zx),
        jnp.where(odd == 0, cw_im, zx), jnp.where(odd == 1, cw_im, zx)], axis=1)

    a16 = apow[CHUNK].reshape(N_PAIRS, 2 * p)
    dec = jnp.concatenate([a16.real, a16.imag], axis=-1).reshape(1, N_PAIRS * GROUP_COLS)
    d_perm = jnp.broadcast_to(d_skip.astype(F32).reshape(g, 1, c16), (g, CHUNK, c16)).reshape(1, g * GROUP_COLS)
    return w_in, w_toep, w_x, dec, d_perm


def _mlp(u, w1_ref, w2_ref):
    d = D_MODEL
    acc = None
    for k in range(D_FF // d):
        a = jnp.dot(u, w1_ref[:, k * d:(k + 1) * d], preferred_element_type=F32)
        a = jnp.square(jnp.maximum(a, 0.0)).astype(BF16)
        part = jnp.dot(a, w2_ref[k * d:(k + 1) * d, :], preferred_element_type=F32)
        acc = part if acc is None else acc + part
    return acc


def _final_norm(h, fg_ref):
    ms = jnp.mean(h * h, axis=-1, keepdims=True)
    return h * lax.rsqrt(ms + EPS) * fg_ref[...]


def _ride_specs(ride, nsteps):
    stacks, layer = ride
    ins, outs, shapes = [], [], []
    for st in stacks:
        _, r, c = st.shape
        ins.append(pl.BlockSpec((None, r // nsteps, c), lambda i: (layer, i, 0)))
        outs.append(pl.BlockSpec((r // nsteps, c), lambda i: (i, 0)))
        shapes.append(jax.ShapeDtypeStruct((r, c), BF16))
    return ins, outs, shapes


def _ride_cast(ride_in, ride_out):
    for src, dst in zip(ride_in, ride_out):
        dst[...] = src[...].astype(dst.dtype)


def _post_attn_kernel(*refs, final, n_ride):
    h_ref, y_ref, moda_ref, modm_ref, g_ref, wp_ref, w1_ref, w2_ref = refs[:8]
    pos = 8 + int(final)
    fg_ref = refs[8] if final else None
    o_ref = refs[pos + n_ride]
    _ride_cast(refs[pos:pos + n_ride], refs[pos + n_ride + 1:])
    d = D_MODEL
    ymix = jnp.dot(y_ref[...], wp_ref[...], preferred_element_type=F32)
    h1 = h_ref[...] + moda_ref[:, 2 * d:3 * d] * ymix
    u = _normmod(h1, g_ref[...], modm_ref[:, :d], modm_ref[:, d:2 * d]).astype(BF16)
    h2 = h1 + modm_ref[:, 2 * d:3 * d] * _mlp(u, w1_ref, w2_ref)
    o_ref[...] = _final_norm(h2, fg_ref) if final else h2


def _post_attn(h, o, moda, modm, g2, w_o, w1, w2, final_g, ride):
    bsz, seq, d = h.shape
    tm = MLP_ROWS
    per_b = seq // tm
    row_spec = pl.BlockSpec((None, tm, d), lambda i: (i // per_b, i % per_b, 0))
    final = final_g is not None
    ride_in, ride_out, ride_shapes = _ride_specs(ride, bsz * per_b) if ride else ([], [], [])
    in_specs = [
        row_spec, row_spec,
        pl.BlockSpec((None, 1, moda.shape[-1]), lambda i: (i // per_b, 0, 0)),
        pl.BlockSpec((None, 1, modm.shape[-1]), lambda i: (i // per_b, 0, 0)),
        pl.BlockSpec((1, d), lambda i: (0, 0)),
        _layer_spec(*w_o), _layer_spec(*w1), _layer_spec(*w2),
    ]
    args = [h, o, moda, modm, g2.reshape(1, d), w_o[0], w1[0], w2[0]]
    if final:
        in_specs.append(pl.BlockSpec((1, d), lambda i: (0, 0)))
        args.append(final_g.reshape(1, d))
    if ride:
        args += ride[0]
    outs = pl.pallas_call(
        functools.partial(_post_attn_kernel, final=final, n_ride=len(ride_in)),
        grid=(bsz * per_b,),
        in_specs=in_specs + ride_in,
        out_specs=[row_spec] + ride_out,
        out_shape=[jax.ShapeDtypeStruct((bsz, seq, d), F32)] + ride_shapes,
        compiler_params=_cparams(("parallel",)),
        name="post_attn_mlp",
    )(*args)
    return outs[0], outs[1:]


def _post_s5_kernel(*refs, n_ride):
    h_ref, z_ref, moda_ref, modm_ref, g_ref, permt_ref, wp_ref, w1_ref, w2_ref = refs[:9]
    o_ref = refs[9 + n_ride]
    z_scr, h1_scr, u_scr = refs[10 + 2 * n_ride:]
    _ride_cast(refs[9:9 + n_ride], refs[10 + n_ride:10 + 2 * n_ride])
    d = D_MODEL
    bsz, tt, _ = h_ref.shape
    rows = z_ref.shape[0]
    for j in range(d // LANES):
        for half in range(CHUNK // 8):
            tiles = []
            for gl in range(8):
                col = (8 * j + gl) * GROUP_COLS + half * LANES
                tiles.append(z_ref[:, col:col + LANES].astype(F32))
            outs = _regroup8(tiles)
            for tl in range(8):
                t = 8 * half + tl
                z_scr[t * rows:(t + 1) * rows, j * LANES:(j + 1) * LANES] = outs[tl].astype(BF16)
    zn = jnp.dot(permt_ref[...], z_scr[...], preferred_element_type=F32).astype(BF16)
    zz = jnp.dot(zn, wp_ref[...], preferred_element_type=F32)
    ymix = zz[:, :d] * jax.nn.sigmoid(zz[:, d:])
    g = g_ref[...]
    for b in range(bsz):
        sl = slice(b * tt, (b + 1) * tt)
        h1 = h_ref[b] + moda_ref[b, :, 2 * d:3 * d] * ymix[sl, :]
        h1_scr[sl, :] = h1
        u_scr[sl, :] = _normmod(h1, g, modm_ref[b, :, :d], modm_ref[b, :, d:2 * d]).astype(BF16)
    acc = _mlp(u_scr[...], w1_ref, w2_ref)
    for b in range(bsz):
        sl = slice(b * tt, (b + 1) * tt)
        o_ref[b] = h1_scr[sl, :] + modm_ref[b, :, 2 * d:3 * d] * acc[sl, :]


def _post_s5(h, zo, moda, modm, g2, permt, w_glu, w1, w2, ride):
    bsz, seq, d = h.shape
    tt = S5_TILE_TOKENS
    rows = bsz * tt // CHUNK
    h_spec = pl.BlockSpec((bsz, tt, d), lambda i: (0, i, 0))
    ride_in, ride_out, ride_shapes = _ride_specs(ride, seq // tt) if ride else ([], [], [])
    outs = pl.pallas_call(
        functools.partial(_post_s5_kernel, n_ride=len(ride_in)),
        grid=(seq // tt,),
        in_specs=[
            h_spec,
            pl.BlockSpec((rows, CHUNK * d), lambda i: (i, 0)),
            pl.BlockSpec(moda.shape, lambda i: (0, 0, 0)),
            pl.BlockSpec(modm.shape, lambda i: (0, 0, 0)),
            pl.BlockSpec((1, d), lambda i: (0, 0)),
            pl.BlockSpec(permt.shape, lambda i: (0, 0)),
            _layer_spec(*w_glu), _layer_spec(*w1), _layer_spec(*w2),
        ] + ride_in,
        out_specs=[h_spec] + ride_out,
        out_shape=[jax.ShapeDtypeStruct((bsz, seq, d), F32)] + ride_shapes,
        scratch_shapes=[pltpu.VMEM((bsz * tt, d), BF16), pltpu.VMEM((bsz * tt, d), F32),
                        pltpu.VMEM((bsz * tt, d), BF16)],
        compiler_params=_cparams(("parallel",)),
        name="post_s5_mlp",
    )(h, zo, moda, modm, g2.reshape(1, d), permt, w_glu[0], w1[0], w2[0], *(ride[0] if ride else []))
    return outs[0], outs[1:]


def kernel(x, c, ln_g, ada_w, ada_b, ssm_lam_re, ssm_lam_im, ssm_log_dt, ssm_b_re, ssm_b_im, ssm_c_re, ssm_c_im, ssm_d, ssm_w_glu, kv_g, kv_ada_w, kv_ada_b, w_kv, attn_w_q, attn_w_o, mlp_w1, mlp_w2, final_g):
    bsz, seq, d = x.shape
    depth = ln_g.shape[0]
    n_s5 = ssm_lam_re.shape[0]
    assert d == D_MODEL and seq % (DILATIONS[-1] * ATT_BLOCK) == 0

    mods = _ada_mods(c, ada_w.reshape(depth * 2, d, 3 * d), ada_b.reshape(depth * 2, 3 * d))
    mods = mods.reshape(depth, 2, bsz, 1, 3 * d)
    kv_mod = _ada_mods(c, kv_ada_w[None], kv_ada_b[None]).reshape(bsz, 1, 2 * d)

    w1 = (mlp_w1[0].astype(BF16)[None], 0)
    w2 = (mlp_w2[0].astype(BF16)[None], 0)

    def next_mlp_weights(layer):
        return ([mlp_w1, mlp_w2], layer + 1) if layer + 1 < depth else None

    w_glu = ssm_w_glu.astype(BF16)
    w_q = attn_w_q.astype(BF16)
    w_o = attn_w_o.astype(BF16)
    perm_np = _s5_tile_perm(bsz)
    perm = jnp.asarray(perm_np, BF16)
    permt = jnp.asarray(perm_np.T, BF16)

    tables = jax.vmap(_s5_weights)(ssm_lam_re, ssm_lam_im, ssm_log_dt, ssm_b_re, ssm_b_im,
                                   ssm_c_re, ssm_c_im, ssm_d)
    h = x
    for layer in range(n_s5):
        z = _s5_pre(h, mods[layer, 0], ln_g[layer, 0], perm)
        zo = _s5_core(z, tables, layer, bsz)
        h, cast = _post_s5(h, zo, mods[layer, 0], mods[layer, 1], ln_g[layer, 1], permt,
                           (w_glu, layer), w1, w2, next_mlp_weights(layer))
        if cast:
            w1, w2 = (cast[0][None], 0), (cast[1][None], 0)

    kv_cols = tuple((i * d, (N_BRANCHES + i) * d) for i in range(N_BRANCHES))
    q_cols = tuple((i * d,) for i in range(N_BRANCHES))
    kvs = _norm_proj(h, kv_mod, kv_g, (w_kv.astype(BF16)[None], 0), kv_cols, 1.0, KV_PROJ_TILE)
    for layer in range(n_s5, depth):
        j = layer - n_s5
        qs = _norm_proj(h, mods[layer, 0], ln_g[layer, 0], (w_q, j), q_cols, Q_SCALE, Q_PROJ_TILE)
        o = _attention(qs, kvs)
        h, cast = _post_attn(h, o, mods[layer, 0], mods[layer, 1], ln_g[layer, 1],
                             (w_o, j), w1, w2,
                             final_g if layer == depth - 1 else None, next_mlp_weights(layer))
        if cast:
            w1, w2 = (cast[0][None], 0), (cast[1][None], 0)
    return h
```

```python
import functools
import math

import numpy as np

import jax
import jax.numpy as jnp
from jax import lax
from jax.experimental import pallas as pl
from jax.experimental.pallas import tpu as pltpu

F32 = jnp.float32
BF16 = jnp.bfloat16

D_MODEL = 1024
SSM_GROUP = 16
SSM_GROUPS = D_MODEL // SSM_GROUP
SSM_STATE = 64
HEAD_DIM = 64
N_HEADS = D_MODEL // HEAD_DIM
DILATIONS = (1, 4, 16)
N_BRANCHES = len(DILATIONS)
ATT_BLOCK = 128
ATT_GROUP = 16
ATT_PAIRS_PER_STEP = 2
N_STATS = 3
Q_SCALE = HEAD_DIM ** -0.5 * math.log2(math.e)
D_FF = 4 * D_MODEL
EPS = 1e-6
NEG = -1e30

LANES = 128
CHUNK = 16
GROUP_COLS = CHUNK * SSM_GROUP
PAIR_COLS = 2 * GROUP_COLS
N_PAIRS = SSM_GROUPS // 2
S5_TILE_TOKENS = 2 * CHUNK
MLP_ROWS = 1024
KV_PROJ_TILE = 512
Q_PROJ_TILE = 1024
VMEM_LIMIT = 56 * 1024 * 1024

HIGHEST = lax.Precision.HIGHEST
NT_DIMS = (((1,), (1,)), ((), ()))


def _cparams(sem):
    return pltpu.CompilerParams(dimension_semantics=sem, vmem_limit_bytes=VMEM_LIMIT)


def _layer_spec(stack, layer):
    zeros = (0,) * (stack.ndim - 1)
    return pl.BlockSpec((None,) + stack.shape[1:], lambda i: (layer,) + zeros, pipeline_mode=pl.Buffered(1))


def _normmod(x, g, shift, scale):
    ms = jnp.mean(x * x, axis=-1, keepdims=True)
    return (x * lax.rsqrt(ms + EPS) * g) * (1.0 + scale) + shift


def _ada_kernel(c_ref, w_ref, b_ref, o_ref):
    c = c_ref[...]
    sc = (c * jax.nn.sigmoid(c)).astype(BF16)
    o_ref[...] = jnp.dot(sc, w_ref[...].astype(BF16), preferred_element_type=F32) + b_ref[...]


def _ada_mods(c, w, b):
    n, d, width = w.shape
    bsz = c.shape[0]
    return pl.pallas_call(
        _ada_kernel,
        grid=(n,),
        in_specs=[
            pl.BlockSpec((bsz, d), lambda i: (0, 0)),
            pl.BlockSpec((None, d, width), lambda i: (i, 0, 0)),
            pl.BlockSpec((None, 1, width), lambda i: (i, 0, 0)),
        ],
        out_specs=pl.BlockSpec((None, bsz, width), lambda i: (i, 0, 0)),
        out_shape=jax.ShapeDtypeStruct((n, bsz, width), F32),
        compiler_params=_cparams(("parallel",)),
        name="ada_mods",
    )(c, w, b.reshape(n, 1, width))


def _proj_kernel(h_ref, mod_ref, g_ref, w_ref, o1_ref, o2_ref, o3_ref, us, us4, up4, up16, *, wcols, out_scale):
    d = D_MODEL
    tm = h_ref.shape[0]
    d1 = DILATIONS[1]
    q4, q16 = tm // d1, tm // DILATIONS[2]
    cw = o1_ref.shape[1]
    u = _normmod(h_ref[...], g_ref[...], mod_ref[:, :d], mod_ref[:, d:2 * d])
    for l in range(d // LANES):
        us[l] = u[:, l * LANES:(l + 1) * LANES]
    ub = u.astype(BF16)
    for n in range(cw // d):
        cols = slice(n * d, (n + 1) * d)
        res = jnp.dot(ub, w_ref[:, wcols[0][n]:wcols[0][n] + d], preferred_element_type=F32)
        o1_ref[:, cols] = (res * out_scale).astype(o1_ref.dtype)
    for l in range(d // LANES):
        for a in range(d1):
            part = us[l, pl.ds(a, q4, stride=d1), :]
            us4[l, a * q4:(a + 1) * q4, :] = part
            up4[a * q4:(a + 1) * q4, l * LANES:(l + 1) * LANES] = part.astype(BF16)
    for n in range(cw // d):
        res = jnp.dot(up4[...], w_ref[:, wcols[1][n]:wcols[1][n] + d], preferred_element_type=F32)
        for a in range(d1):
            o2_ref[a, :, n * d:(n + 1) * d] = (res[a * q4:(a + 1) * q4, :] * out_scale).astype(o2_ref.dtype)
    for l in range(d // LANES):
        for a1 in range(d1):
            for a2 in range(d1):
                r = d1 * a2 + a1
                up16[r * q16:(r + 1) * q16, l * LANES:(l + 1) * LANES] = us4[
                    l, pl.ds(a1 * q4 + a2, q16, stride=d1), :].astype(BF16)
    for n in range(cw // d):
        res = jnp.dot(up16[...], w_ref[:, wcols[2][n]:wcols[2][n] + d], preferred_element_type=F32)
        for r in range(DILATIONS[2]):
            o3_ref[r, :, n * d:(n + 1) * d] = (res[r * q16:(r + 1) * q16, :] * out_scale).astype(o3_ref.dtype)


def _norm_proj(h, mod, g, w, wcols, out_scale, tm):
    bsz, seq, d = h.shape
    cw = len(wcols[0]) * d
    per_b = seq // tm
    d1, d2 = DILATIONS[1], DILATIONS[2]
    o1, o2, o3 = pl.pallas_call(
        functools.partial(_proj_kernel, wcols=wcols, out_scale=out_scale),
        grid=(bsz * per_b,),
        in_specs=[
            pl.BlockSpec((None, tm, d), lambda i: (i // per_b, i % per_b, 0)),
            pl.BlockSpec((None, 1, mod.shape[-1]), lambda i: (i // per_b, 0, 0)),
            pl.BlockSpec((1, d), lambda i: (0, 0)),
            _layer_spec(*w),
        ],
        out_specs=[
            pl.BlockSpec((None, tm, cw), lambda i: (i // per_b, i % per_b, 0)),
            pl.BlockSpec((None, d1, None, tm // d1, cw), lambda i: (i // per_b, 0, i % per_b, 0, 0)),
            pl.BlockSpec((None, d2, None, tm // d2, cw), lambda i: (i // per_b, 0, i % per_b, 0, 0)),
        ],
        out_shape=[
            jax.ShapeDtypeStruct((bsz, seq, cw), BF16),
            jax.ShapeDtypeStruct((bsz, d1, per_b, tm // d1, cw), BF16),
            jax.ShapeDtypeStruct((bsz, d2, per_b, tm // d2, cw), BF16),
        ],
        scratch_shapes=[pltpu.VMEM((d // LANES, tm, LANES), F32), pltpu.VMEM((d // LANES, tm, LANES), F32),
                        pltpu.VMEM((tm, d), BF16), pltpu.VMEM((tm, d), BF16)],
        compiler_params=_cparams(("parallel",)),
        name="norm_proj",
    )(h, mod, g.reshape(1, d), w[0])
    return o1, o2.reshape(bsz, seq, cw), o3.reshape(bsz, seq, cw)


def _attn_group(blocks):
    blk = ATT_BLOCK
    lane = lax.broadcasted_iota(jnp.int32, (blk, LANES), 1)
    lo = lane < HEAD_DIM
    scores = []
    for qb, kcat, _, mask, _ in blocks:
        zero = jnp.zeros_like(qb)
        q2 = jnp.concatenate([jnp.where(lo, qb, zero), jnp.where(lo, zero, qb)], axis=0)
        s = lax.dot_general(q2, kcat, NT_DIMS, preferred_element_type=F32)
        scores.append(s + mask[...])
    soft = []
    for s, (_, _, _, _, old) in zip(scores, blocks):
        ms, ps = [], []
        for hh in range(2):
            tiles = [s[hh * blk:(hh + 1) * blk, t * LANES:(t + 1) * LANES] for t in range(s.shape[1] // LANES)]
            mx = tiles[0]
            for t in tiles[1:]:
                mx = jnp.maximum(mx, t)
            m = jnp.broadcast_to(jnp.max(mx, axis=-1, keepdims=True), (blk, LANES))
            if old is not None:
                m = jnp.maximum(m, old[0][hh])
            ms.append(m)
            ps.append(jnp.concatenate([jnp.exp2(t - m).astype(BF16) for t in tiles], axis=1))
        alpha = None
        if old is not None:
            alpha = jnp.exp2(jnp.where(lo, old[0][0], old[0][1]) - jnp.where(lo, ms[0], ms[1]))
        soft.append((ms, jnp.concatenate(ps, axis=0), alpha))
    outs = []
    for (ms, p, alpha), (_, _, vcat, _, old) in zip(soft, blocks):
        pv = jnp.dot(p, vcat, preferred_element_type=F32)
        acc = jnp.where(lo, pv[:blk, :LANES], pv[blk:, :LANES])
        den = jnp.where(lo, pv[:blk, LANES:], pv[blk:, LANES:])
        if old is not None:
            acc = alpha * old[1] + acc
            den = alpha * old[0][2] + den
        outs.append((ms + [den], acc))
    return outs


def _attn_kernel(*refs):
    n_in = 3 * N_BRANCHES
    ins, o_ref, scratch = refs[:n_in], refs[n_in], refs[n_in + 1:]
    for half in range(ATT_PAIRS_PER_STEP):
        lanes = pl.ds(half * LANES, LANES)
        _attn_pair(*[r.at[:, lanes] for r in ins], o_ref.at[:, lanes], *scratch)


def _attn_pair(q1, qd2, qd3, k1, kd2, kd3, v1, v2, v3, o_ref, tmp_nat, vd1, vd2, vd3,
               acc_ref, st_ref, acc2_ref, st2_ref, mask_pc, mask_c):
    blk = ATT_BLOCK
    seq = q1.shape[0]
    nblk = seq // blk
    d1, d2 = DILATIONS[1], DILATIONS[2]
    quarter = seq // d1
    per_res = quarter // blk
    assert d2 == d1 * d1 and seq == d2 * blk and nblk % ATT_GROUP == 0 and ATT_GROUP % per_res == 0

    @pl.when(jnp.logical_and(pl.program_id(0) == 0, pl.program_id(1) == 0))
    def _():
        for vd in (vd1, vd2, vd3):
            vd[:, LANES:] = jnp.ones((seq, LANES), BF16)
        qq = lax.broadcasted_iota(jnp.int32, (2 * blk, 2 * blk), 0) & (blk - 1)
        kk = lax.broadcasted_iota(jnp.int32, (2 * blk, 2 * blk), 1)
        valid = jnp.logical_or(jnp.logical_and(kk < blk, kk >= qq), jnp.logical_and(kk >= blk, kk - blk <= qq))
        mask_pc[...] = jnp.where(valid, 0.0, NEG)
        valid_c = (lax.broadcasted_iota(jnp.int32, (2 * blk, blk), 1)
                   <= (lax.broadcasted_iota(jnp.int32, (2 * blk, blk), 0) & (blk - 1)))
        mask_c[...] = jnp.where(valid_c, 0.0, NEG)

    for v, vd in ((v1, vd1), (v2, vd2), (v3, vd3)):
        vd[:, :LANES] = v[...]

    def run_group(blocks, dests):
        for (st_dst, acc_dst, rows), (st, acc) in zip(dests, _attn_group(blocks)):
            for k in range(N_STATS):
                st_dst[k, rows, :] = st[k]
            acc_dst[rows, :] = acc

    def load_old(st_src, acc_src, rows):
        return [st_src[k, rows, :] for k in range(N_STATS)], acc_src[rows, :]

    def b0_group(ns, first):
        blocks, dests = [], []
        for n in ns:
            if first and n == 0:
                rows = pl.ds(0, blk)
                blocks.append((q1[rows, :], k1[rows, :], vd1[rows, :], mask_c, None))
            else:
                rows = pl.ds(pl.multiple_of(n * blk, blk), blk)
                krows = pl.ds(pl.multiple_of((n - 1) * blk, blk), 2 * blk)
                blocks.append((q1[rows, :], k1[krows, :], vd1[krows, :], mask_pc, None))
            dests.append((st_ref, acc_ref, rows))
        run_group(blocks, dests)

    b0_group(list(range(ATT_GROUP)), True)

    def b0_body(i, carry):
        b0_group([ATT_GROUP * i + j for j in range(ATT_GROUP)], False)
        return carry

    lax.fori_loop(1, nblk // ATT_GROUP, b0_body, 0)

    res_per_trip = ATT_GROUP // per_res

    def b1_body(i, carry):
        blocks, dests = [], []
        for jr in range(res_per_trip):
            a1 = res_per_trip * i + jr
            base = a1 * quarter
            for n in range(per_res):
                rows = pl.ds(a1 + d1 * blk * n, blk, stride=d1)
                qrows = pl.ds(pl.multiple_of(base + n * blk, blk), blk)
                if n == 0:
                    krows, mask = qrows, mask_c
                else:
                    krows, mask = pl.ds(pl.multiple_of(base + (n - 1) * blk, blk), 2 * blk), mask_pc
                blocks.append((qd2[qrows, :], kd2[krows, :], vd2[krows, :], mask,
                               load_old(st_ref, acc_ref, rows)))
                dests.append((st2_ref, acc2_ref, qrows))
        run_group(blocks, dests)
        return carry

    lax.fori_loop(0, d1 // res_per_trip, b1_body, 0)

    def b2_body(i, carry):
        blocks, dests = [], []
        for j in range(ATT_GROUP):
            a1, a2 = j % d1, (ATT_GROUP // d1) * i + j // d1
            qrows = pl.ds(pl.multiple_of((ATT_GROUP * i + j) * blk, blk), blk)
            rows = pl.ds(a1 * quarter + a2, blk, stride=d1)
            blocks.append((qd3[qrows, :], kd3[qrows, :], vd3[qrows, :], mask_c,
                           load_old(st2_ref, acc2_ref, rows)))
            dests.append((st2_ref, acc2_ref, rows))
        run_group(blocks, dests)
        return carry

    lax.fori_loop(0, d2 // ATT_GROUP, b2_body, 0)

    for a in range(d1):
        seg = slice(a * quarter, (a + 1) * quarter)
        tmp_nat[pl.ds(a, quarter, stride=d1), :] = acc2_ref[seg, :] / st2_ref[N_STATS - 1, seg, :]
    o_ref[...] = tmp_nat[...].astype(o_ref.dtype)


def _attention(qs, kvs):
    bsz, seq, _ = qs[0].shape
    steps = D_MODEL // (ATT_PAIRS_PER_STEP * LANES)

    def slab(col0):
        return pl.BlockSpec((None, seq, ATT_PAIRS_PER_STEP * LANES), lambda b, hp: (b, 0, col0 + hp))

    in_specs = [slab(0)] * N_BRANCHES + [slab(0)] * N_BRANCHES + [slab(steps)] * N_BRANCHES
    return pl.pallas_call(
        _attn_kernel,
        grid=(bsz, steps),
        in_specs=in_specs,
        out_specs=slab(0),
        out_shape=jax.ShapeDtypeStruct((bsz, seq, D_MODEL), BF16),
        scratch_shapes=([pltpu.VMEM((seq, LANES), F32)] + [pltpu.VMEM((seq, 2 * LANES), BF16)] * 3
                        + [pltpu.VMEM((seq, LANES), F32), pltpu.VMEM((N_STATS, seq, LANES), F32)] * 2
                        + [pltpu.VMEM((2 * ATT_BLOCK, 2 * ATT_BLOCK), F32), pltpu.VMEM((2 * ATT_BLOCK, ATT_BLOCK), F32)]),
        compiler_params=_cparams(("arbitrary", "arbitrary")),
        name="attention",
    )(*qs, *kvs, *kvs)


def _regroup8(tiles):
    t = list(tiles)
    lane = lax.broadcasted_iota(jnp.int32, t[0].shape, 1)
    piece = lane // SSM_GROUP
    for dist in (4, 2, 1):
        bit = (piece & dist) != 0
        shift = dist * SSM_GROUP
        for a in range(8):
            if a & dist:
                continue
            b = a | dist
            ta, tb = t[a], t[b]
            t[a] = jnp.where(bit, pltpu.roll(tb, shift, 1), ta)
            t[b] = jnp.where(bit, tb, pltpu.roll(ta, LANES - shift, 1))
    return t


def _s5_tile_perm(bsz):
    tt = S5_TILE_TOKENS
    n = bsz * tt
    perm = np.zeros((n, n), np.float32)
    for b in range(bsz):
        for c in range(tt // CHUNK):
            for t in range(CHUNK):
                perm[(t * (tt // CHUNK) + c) * bsz + b, b * tt + c * CHUNK + t] = 1.0
    return perm


def _s5_pre_kernel(*refs, n_ride):
    h_ref, mod_ref, g_ref, perm_ref = refs[:4]
    z_ref = refs[4 + n_ride]
    u_scr, up_scr = refs[5 + 2 * n_ride:]
    _ride_cast(refs[4:4 + n_ride], refs[5 + n_ride:5 + 2 * n_ride])
    d = D_MODEL
    bsz, tt, _ = h_ref.shape
    g = g_ref[...]
    for b in range(bsz):
        u_scr[b * tt:(b + 1) * tt, :] = _normmod(
            h_ref[b], g, mod_ref[b, :, :d], mod_ref[b, :, d:2 * d]).astype(BF16)
    up_scr[...] = jnp.dot(perm_ref[...], u_scr[...], preferred_element_type=F32)
    rows = z_ref.shape[0]
    for j in range(d // LANES):
        for half in range(CHUNK // 8):
            tiles = [up_scr[(8 * half + tl) * rows:(8 * half + tl + 1) * rows, j * LANES:(j + 1) * LANES]
                     for tl in range(8)]
            outs = _regroup8(tiles)
            for gl in range(8):
                col = (8 * j + gl) * GROUP_COLS + half * LANES
                z_ref[:, col:col + LANES] = outs[gl].astype(z_ref.dtype)


def _s5_pre(h, mod, g, perm, ride):
    bsz, seq, d = h.shape
    tt = S5_TILE_TOKENS
    rows = bsz * tt // CHUNK
    ride_in, ride_out, ride_shapes, ride_args = _ride_specs(ride, seq // tt)
    outs = pl.pallas_call(
        functools.partial(_s5_pre_kernel, n_ride=len(ride_in)),
        grid=(seq // tt,),
        in_specs=[
            pl.BlockSpec((bsz, tt, d), lambda i: (0, i, 0)),
            pl.BlockSpec(mod.shape, lambda i: (0, 0, 0)),
            pl.BlockSpec((1, d), lambda i: (0, 0)),
            pl.BlockSpec(perm.shape, lambda i: (0, 0)),
        ] + ride_in,
        out_specs=[pl.BlockSpec((rows, CHUNK * d), lambda i: (i, 0))] + ride_out,
        out_shape=[jax.ShapeDtypeStruct((bsz * seq // CHUNK, CHUNK * d), BF16)] + ride_shapes,
        scratch_shapes=[pltpu.VMEM((bsz * tt, d), BF16), pltpu.VMEM((bsz * tt, d), F32)],
        compiler_params=_cparams(("parallel",)),
        name="s5_pre",
    )(h, mod, g.reshape(1, d), perm, *ride_args)
    return outs[0], outs[1:]


def _s5_core_kernel(z_ref, win_ref, wt_ref, wx_ref, dec_ref, d_ref, o_ref, s_scr, x_scr, *, bsz):
    gc = GROUP_COLS
    nchunk = z_ref.shape[0] // bsz
    s_scr[...] = jnp.dot(z_ref[...], win_ref[...].reshape(2 * gc, gc), preferred_element_type=F32)
    ar, ai = dec_ref[:, :LANES], dec_ref[:, LANES:]

    def step(c, carry):
        xr, xi = carry
        rows = pl.ds(pl.multiple_of(c * bsz, bsz), bsz)
        x_scr[rows, :LANES] = xr
        x_scr[rows, LANES:] = xi
        return (ar * xr - ai * xi + s_scr[rows, :LANES], ar * xi + ai * xr + s_scr[rows, LANES:])

    zero = jnp.zeros((bsz, LANES), F32)
    lax.fori_loop(0, nchunk, step, (zero, zero))
    x = x_scr[...].astype(BF16)
    for gi in range(2):
        z = z_ref[:, gi * gc:(gi + 1) * gc]
        y = jnp.dot(jnp.concatenate([z, x], axis=1), jnp.concatenate([wt_ref[gi], wx_ref[gi]], axis=0),
                    preferred_element_type=F32)
        y = y + d_ref[:, gi * gc:(gi + 1) * gc] * z.astype(F32)
        o_ref[:, gi * gc:(gi + 1) * gc] = jax.nn.gelu(y).astype(o_ref.dtype)


def _s5_core(z, tables, layer, bsz):
    nrow, width = z.shape
    w_in, w_toep, w_x, dec, d_perm = tables
    wspec = pl.BlockSpec((None, 2, GROUP_COLS, GROUP_COLS), lambda k: (layer, k, 0, 0))
    return pl.pallas_call(
        functools.partial(_s5_core_kernel, bsz=bsz),
        grid=(N_PAIRS,),
        in_specs=[
            pl.BlockSpec((nrow, PAIR_COLS), lambda k: (0, k)),
            wspec, wspec, wspec,
            pl.BlockSpec((None, 1, GROUP_COLS), lambda k: (layer, 0, k)),
            pl.BlockSpec((None, 1, PAIR_COLS), lambda k: (layer, 0, k)),
        ],
        out_specs=pl.BlockSpec((nrow, PAIR_COLS), lambda k: (0, k)),
        out_shape=jax.ShapeDtypeStruct((nrow, width), BF16),
        scratch_shapes=[pltpu.VMEM((nrow, GROUP_COLS), F32), pltpu.VMEM((nrow, GROUP_COLS), F32)],
        compiler_params=_cparams(("parallel",)),
        name="s5_core",
    )(z, w_in, w_toep, w_x, dec, d_perm)


def _s5_weights(lam_re, lam_im, log_dt, b_re, b_im, c_re, c_im, d_skip):
    g, p, c16 = SSM_GROUPS, SSM_STATE, SSM_GROUP
    lam = lax.complex(lam_re.astype(F32), lam_im.astype(F32))
    dt = jnp.exp(log_dt.astype(F32))[:, None]
    steps = jnp.arange(CHUNK + 1, dtype=F32)
    apow = jnp.exp((lam * dt)[None] * steps[:, None, None])
    a = apow[1]
    bbar = ((a - 1.0) / lam)[..., None] * lax.complex(b_re.astype(F32), b_im.astype(F32))
    cmat = lax.complex(c_re.astype(F32), c_im.astype(F32))
    odd = (jnp.arange(g) % 2)[:, None, None]

    win = apow[CHUNK - 1::-1][..., None] * bbar[None]

    def lay_in(x):
        return jnp.transpose(x.astype(BF16), (1, 0, 3, 2)).reshape(g, GROUP_COLS, p)

    win_re, win_im = lay_in(win.real), lay_in(win.imag)
    zeros = jnp.zeros_like(win_re)
    w_in = jnp.concatenate([
        jnp.where(odd == 0, win_re, zeros), jnp.where(odd == 1, win_re, zeros),
        jnp.where(odd == 0, win_im, zeros), jnp.where(odd == 1, win_im, zeros)], axis=-1)

    kern = jnp.einsum('gop,kgp,gpi->kgoi', cmat, apow[:CHUNK], bbar, precision=HIGHEST).real
    idx = jnp.arange(CHUNK)
    onehot = (idx[None, None, :] - idx[None, :, None] == idx[:, None, None]).astype(BF16)
    w_toep = jnp.einsum('kst,kgoi->gsito', onehot, kern.astype(BF16),
                        preferred_element_type=BF16)
    w_toep = w_toep.reshape(g, GROUP_COLS, GROUP_COLS)

    cw = cmat[None] * apow[1:, :, None, :]

    def lay_x(x):
        return jnp.transpose(x.astype(BF16), (1, 3, 0, 2)).reshape(g, p, GROUP_COLS)

    cw_re, cw_im = lay_x(cw.real), lay_x(-cw.imag)
    zx = jnp.zeros_like(cw_re)
    w_x = jnp.concatenate([
        jnp.where(odd == 0, cw_re, zx), jnp.where(odd == 1, cw_re, zx),
        jnp.where(odd == 0, cw_im, zx), jnp.where(odd == 1, cw_im, zx)], axis=1)

    a16 = apow[CHUNK].reshape(N_PAIRS, 2 * p)
    dec = jnp.concatenate([a16.real, a16.imag], axis=-1).reshape(1, N_PAIRS * GROUP_COLS)
    d_perm = jnp.broadcast_to(d_skip.astype(F32).reshape(g, 1, c16), (g, CHUNK, c16)).reshape(1, g * GROUP_COLS)
    return w_in, w_toep, w_x, dec, d_perm


def _mlp(u, w1_ref, w2_ref):
    d = D_MODEL
    acc = None
    for k in range(D_FF // d):
        a = jnp.dot(u, w1_ref[:, k * d:(k + 1) * d], preferred_element_type=F32)
        a = jnp.square(jnp.maximum(a, 0.0)).astype(BF16)
        part = jnp.dot(a, w2_ref[k * d:(k + 1) * d, :], preferred_element_type=F32)
        acc = part if acc is None else acc + part
    return acc


def _final_norm(h, fg_ref):
    ms = jnp.mean(h * h, axis=-1, keepdims=True)
    return h * lax.rsqrt(ms + EPS) * fg_ref[...]


def _ride_specs(ride, nsteps):
    ins, outs, shapes = [], [], []
    for st, layer in ride:
        _, r, c = st.shape
        ins.append(pl.BlockSpec((None, r // nsteps, c), lambda i, layer=layer: (layer, i, 0)))
        outs.append(pl.BlockSpec((r // nsteps, c), lambda i: (i, 0)))
        shapes.append(jax.ShapeDtypeStruct((r, c), BF16))
    return ins, outs, shapes, [st for st, _ in ride]


def _ride_cast(ride_in, ride_out):
    for src, dst in zip(ride_in, ride_out):
        dst[...] = src[...].astype(dst.dtype)


def _post_attn_kernel(*refs, final, n_ride):
    h_ref, y_ref, moda_ref, modm_ref, g_ref, wp_ref, w1_ref, w2_ref = refs[:8]
    pos = 8 + int(final)
    fg_ref = refs[8] if final else None
    o_ref = refs[pos + n_ride]
    _ride_cast(refs[pos:pos + n_ride], refs[pos + n_ride + 1:])
    d = D_MODEL
    ymix = jnp.dot(y_ref[...], wp_ref[...], preferred_element_type=F32)
    h1 = h_ref[...] + moda_ref[:, 2 * d:3 * d] * ymix
    u = _normmod(h1, g_ref[...], modm_ref[:, :d], modm_ref[:, d:2 * d]).astype(BF16)
    h2 = h1 + modm_ref[:, 2 * d:3 * d] * _mlp(u, w1_ref, w2_ref)
    o_ref[...] = _final_norm(h2, fg_ref) if final else h2


def _post_attn(h, o, moda, modm, g2, w_o, w1, w2, final_g, ride):
    bsz, seq, d = h.shape
    tm = MLP_ROWS
    per_b = seq // tm
    row_spec = pl.BlockSpec((None, tm, d), lambda i: (i // per_b, i % per_b, 0))
    final = final_g is not None
    ride_in, ride_out, ride_shapes, ride_args = _ride_specs(ride, bsz * per_b)
    in_specs = [
        row_spec, row_spec,
        pl.BlockSpec((None, 1, moda.shape[-1]), lambda i: (i // per_b, 0, 0)),
        pl.BlockSpec((None, 1, modm.shape[-1]), lambda i: (i // per_b, 0, 0)),
        pl.BlockSpec((1, d), lambda i: (0, 0)),
        _layer_spec(*w_o), _layer_spec(*w1), _layer_spec(*w2),
    ]
    args = [h, o, moda, modm, g2.reshape(1, d), w_o[0], w1[0], w2[0]]
    if final:
        in_specs.append(pl.BlockSpec((1, d), lambda i: (0, 0)))
        args.append(final_g.reshape(1, d))
    args += ride_args
    outs = pl.pallas_call(
        functools.partial(_post_attn_kernel, final=final, n_ride=len(ride_in)),
        grid=(bsz * per_b,),
        in_specs=in_specs + ride_in,
        out_specs=[row_spec] + ride_out,
        out_shape=[jax.ShapeDtypeStruct((bsz, seq, d), F32)] + ride_shapes,
        compiler_params=_cparams(("parallel",)),
        name="post_attn_mlp",
    )(*args)
    return outs[0], outs[1:]


def _post_s5_kernel(*refs, n_ride):
    h_ref, z_ref, moda_ref, modm_ref, g_ref, permt_ref, wp_ref, w1_ref, w2_ref = refs[:9]
    o_ref = refs[9 + n_ride]
    z_scr, h1_scr, u_scr = refs[10 + 2 * n_ride:]
    _ride_cast(refs[9:9 + n_ride], refs[10 + n_ride:10 + 2 * n_ride])
    d = D_MODEL
    bsz, tt, _ = h_ref.shape
    rows = z_ref.shape[0]
    for j in range(d // LANES):
        for half in range(CHUNK // 8):
            tiles = []
            for gl in range(8):
                col = (8 * j + gl) * GROUP_COLS + half * LANES
                tiles.append(z_ref[:, col:col + LANES].astype(F32))
            outs = _regroup8(tiles)
            for tl in range(8):
                t = 8 * half + tl
                z_scr[t * rows:(t + 1) * rows, j * LANES:(j + 1) * LANES] = outs[tl].astype(BF16)
    zn = jnp.dot(permt_ref[...], z_scr[...], preferred_element_type=F32).astype(BF16)
    zz = jnp.dot(zn, wp_ref[...], preferred_element_type=F32)
    ymix = zz[:, :d] * jax.nn.sigmoid(zz[:, d:])
    g = g_ref[...]
    for b in range(bsz):
        sl = slice(b * tt, (b + 1) * tt)
        h1 = h_ref[b] + moda_ref[b, :, 2 * d:3 * d] * ymix[sl, :]
        h1_scr[sl, :] = h1
        u_scr[sl, :] = _normmod(h1, g, modm_ref[b, :, :d], modm_ref[b, :, d:2 * d]).astype(BF16)
    acc = _mlp(u_scr[...], w1_ref, w2_ref)
    for b in range(bsz):
        sl = slice(b * tt, (b + 1) * tt)
        o_ref[b] = h1_scr[sl, :] + modm_ref[b, :, 2 * d:3 * d] * acc[sl, :]


def _post_s5(h, zo, moda, modm, g2, permt, w_glu, w1, w2, ride):
    bsz, seq, d = h.shape
    tt = S5_TILE_TOKENS
    rows = bsz * tt // CHUNK
    h_spec = pl.BlockSpec((bsz, tt, d), lambda i: (0, i, 0))
    ride_in, ride_out, ride_shapes, ride_args = _ride_specs(ride, seq // tt)
    outs = pl.pallas_call(
        functools.partial(_post_s5_kernel, n_ride=len(ride_in)),
        grid=(seq // tt,),
        in_specs=[
            h_spec,
            pl.BlockSpec((rows, CHUNK * d), lambda i: (i, 0)),
            pl.BlockSpec(moda.shape, lambda i: (0, 0, 0)),
            pl.BlockSpec(modm.shape, lambda i: (0, 0, 0)),
            pl.BlockSpec((1, d), lambda i: (0, 0)),
            pl.BlockSpec(permt.shape, lambda i: (0, 0)),
            _layer_spec(*w_glu), _layer_spec(*w1), _layer_spec(*w2),
        ] + ride_in,
        out_specs=[h_spec] + ride_out,
        out_shape=[jax.ShapeDtypeStruct((bsz, seq, d), F32)] + ride_shapes,
        scratch_shapes=[pltpu.VMEM((bsz * tt, d), BF16), pltpu.VMEM((bsz * tt, d), F32),
                        pltpu.VMEM((bsz * tt, d), BF16)],
        compiler_params=_cparams(("parallel",)),
        name="post_s5_mlp",
    )(h, zo, moda, modm, g2.reshape(1, d), permt, w_glu[0], w1[0], w2[0], *ride_args)
    return outs[0], outs[1:]


def kernel(x, c, ln_g, ada_w, ada_b, ssm_lam_re, ssm_lam_im, ssm_log_dt, ssm_b_re, ssm_b_im, ssm_c_re, ssm_c_im, ssm_d, ssm_w_glu, kv_g, kv_ada_w, kv_ada_b, w_kv, attn_w_q, attn_w_o, mlp_w1, mlp_w2, final_g):
    bsz, seq, d = x.shape
    depth = ln_g.shape[0]
    n_s5 = ssm_lam_re.shape[0]
    assert d == D_MODEL and seq % (DILATIONS[-1] * ATT_BLOCK) == 0 and 0 < n_s5 < depth

    mods = _ada_mods(c, ada_w.reshape(depth * 2, d, 3 * d), ada_b.reshape(depth * 2, 3 * d))
    mods = mods.reshape(depth, 2, bsz, 1, 3 * d)
    kv_mod = _ada_mods(c, kv_ada_w[None], kv_ada_b[None]).reshape(bsz, 1, 2 * d)

    perm_np = _s5_tile_perm(bsz)
    perm = jnp.asarray(perm_np, BF16)
    permt = jnp.asarray(perm_np.T, BF16)
    tables = jax.vmap(_s5_weights)(ssm_lam_re, ssm_lam_im, ssm_log_dt, ssm_b_re, ssm_b_im,
                                   ssm_c_re, ssm_c_im, ssm_d)

    def as_layer(cast):
        return (cast[None], 0)

    h = x
    for layer in range(n_s5):
        ride = [(mlp_w1, 0), (mlp_w2, 0), (ssm_w_glu, 0)] if layer == 0 else []
        z, cast = _s5_pre(h, mods[layer, 0], ln_g[layer, 0], perm, ride)
        if layer == 0:
            w1, w2, w_glu = map(as_layer, cast)
        zo = _s5_core(z, tables, layer, bsz)
        ride = [(mlp_w1, layer + 1), (mlp_w2, layer + 1)]
        if layer + 1 < n_s5:
            ride += [(ssm_w_glu, layer + 1)]
        else:
            ride += [(w_kv[None], 0), (attn_w_q, 0), (attn_w_o, 0)]
        h, cast = _post_s5(h, zo, mods[layer, 0], mods[layer, 1], ln_g[layer, 1], permt, w_glu, w1, w2, ride)
        w1, w2 = as_layer(cast[0]), as_layer(cast[1])
        if layer + 1 < n_s5:
            w_glu = as_layer(cast[2])
        else:
            w_kvb, w_q, w_o = map(as_layer, cast[2:])

    kv_cols = tuple((i * d, (N_BRANCHES + i) * d) for i in range(N_BRANCHES))
    q_cols = tuple((i * d,) for i in range(N_BRANCHES))
    kvs = _norm_proj(h, kv_mod, kv_g, w_kvb, kv_cols, 1.0, KV_PROJ_TILE)
    for layer in range(n_s5, depth):
        j = layer - n_s5
        last = layer == depth - 1
        qs = _norm_proj(h, mods[layer, 0], ln_g[layer, 0], w_q, q_cols, Q_SCALE, Q_PROJ_TILE)
        o = _attention(qs, kvs)
        ride = [] if last else [(mlp_w1, layer + 1), (mlp_w2, layer + 1), (attn_w_q, j + 1), (attn_w_o, j + 1)]
        h, cast = _post_attn(h, o, mods[layer, 0], mods[layer, 1], ln_g[layer, 1], w_o, w1, w2,
                             final_g if last else None, ride)
        if not last:
            w1, w2, w_q, w_o = map(as_layer, cast)
    return h
```

```python
import functools
import math

import numpy as np

import jax
import jax.numpy as jnp
from jax import lax
from jax.experimental import pallas as pl
from jax.experimental.pallas import tpu as pltpu

F32 = jnp.float32
BF16 = jnp.bfloat16

D_MODEL = 1024
SSM_GROUP = 16
SSM_GROUPS = D_MODEL // SSM_GROUP
SSM_STATE = 64
HEAD_DIM = 64
N_HEADS = D_MODEL // HEAD_DIM
DILATIONS = (1, 4, 16)
N_BRANCHES = len(DILATIONS)
ATT_BLOCK = 128
ATT_GROUP = 16
ATT_PAIRS_PER_STEP = 2
N_STATS = 3
Q_SCALE = HEAD_DIM ** -0.5 * math.log2(math.e)
D_FF = 4 * D_MODEL
EPS = 1e-6
NEG = -1e30

LANES = 128
CHUNK = 16
GROUP_COLS = CHUNK * SSM_GROUP
PAIR_COLS = 2 * GROUP_COLS
N_PAIRS = SSM_GROUPS // 2
S5_TILE_TOKENS = 2 * CHUNK
MLP_ROWS = 1024
KV_PROJ_TILE = 512
Q_PROJ_TILE = 1024
VMEM_LIMIT = 56 * 1024 * 1024

HIGHEST = lax.Precision.HIGHEST
NT_DIMS = (((1,), (1,)), ((), ()))


def _cparams(sem):
    return pltpu.CompilerParams(dimension_semantics=sem, vmem_limit_bytes=VMEM_LIMIT)


def _layer_spec(stack, layer):
    zeros = (0,) * (stack.ndim - 1)
    return pl.BlockSpec((None,) + stack.shape[1:], lambda i: (layer,) + zeros, pipeline_mode=pl.Buffered(1))


def _normmod(x, g, shift, scale):
    ms = jnp.mean(x * x, axis=-1, keepdims=True)
    return (x * lax.rsqrt(ms + EPS) * g) * (1.0 + scale) + shift


def _ada_kernel(c_ref, w_ref, b_ref, o_ref):
    c = c_ref[...]
    sc = (c * jax.nn.sigmoid(c)).astype(BF16)
    o_ref[...] = jnp.dot(sc, w_ref[...].astype(BF16), preferred_element_type=F32) + b_ref[...]


def _ada_mods(c, w, b):
    n, d, width = w.shape
    bsz = c.shape[0]
    return pl.pallas_call(
        _ada_kernel,
        grid=(n,),
        in_specs=[
            pl.BlockSpec((bsz, d), lambda i: (0, 0)),
            pl.BlockSpec((None, d, width), lambda i: (i, 0, 0)),
            pl.BlockSpec((None, 1, width), lambda i: (i, 0, 0)),
        ],
        out_specs=pl.BlockSpec((None, bsz, width), lambda i: (i, 0, 0)),
        out_shape=jax.ShapeDtypeStruct((n, bsz, width), F32),
        compiler_params=_cparams(("parallel",)),
        name="ada_mods",
    )(c, w, b.reshape(n, 1, width))


def _proj_kernel(h_ref, mod_ref, g_ref, w_ref, o1_ref, o2_ref, o3_ref, us, us4, up4, up16, *, wcols, out_scale):
    d = D_MODEL
    tm = h_ref.shape[0]
    d1 = DILATIONS[1]
    q4, q16 = tm // d1, tm // DILATIONS[2]
    cw = o1_ref.shape[1]
    u = _normmod(h_ref[...], g_ref[...], mod_ref[:, :d], mod_ref[:, d:2 * d])
    for l in range(d // LANES):
        us[l] = u[:, l * LANES:(l + 1) * LANES]
    ub = u.astype(BF16)
    for n in range(cw // d):
        cols = slice(n * d, (n + 1) * d)
        res = jnp.dot(ub, w_ref[:, wcols[0][n]:wcols[0][n] + d], preferred_element_type=F32)
        o1_ref[:, cols] = (res * out_scale).astype(o1_ref.dtype)
    for l in range(d // LANES):
        for a in range(d1):
            part = us[l, pl.ds(a, q4, stride=d1), :]
            us4[l, a * q4:(a + 1) * q4, :] = part
            up4[a * q4:(a + 1) * q4, l * LANES:(l + 1) * LANES] = part.astype(BF16)
    for n in range(cw // d):
        res = jnp.dot(up4[...], w_ref[:, wcols[1][n]:wcols[1][n] + d], preferred_element_type=F32)
        for a in range(d1):
            o2_ref[a, :, n * d:(n + 1) * d] = (res[a * q4:(a + 1) * q4, :] * out_scale).astype(o2_ref.dtype)
    for l in range(d // LANES):
        for a1 in range(d1):
            for a2 in range(d1):
                r = d1 * a2 + a1
                up16[r * q16:(r + 1) * q16, l * LANES:(l + 1) * LANES] = us4[
                    l, pl.ds(a1 * q4 + a2, q16, stride=d1), :].astype(BF16)
    for n in range(cw // d):
        res = jnp.dot(up16[...], w_ref[:, wcols[2][n]:wcols[2][n] + d], preferred_element_type=F32)
        for r in range(DILATIONS[2]):
            o3_ref[r, :, n * d:(n + 1) * d] = (res[r * q16:(r + 1) * q16, :] * out_scale).astype(o3_ref.dtype)


def _norm_proj(h, mod, g, w, wcols, out_scale, tm):
    bsz, seq, d = h.shape
    cw = len(wcols[0]) * d
    per_b = seq // tm
    d1, d2 = DILATIONS[1], DILATIONS[2]
    o1, o2, o3 = pl.pallas_call(
        functools.partial(_proj_kernel, wcols=wcols, out_scale=out_scale),
        grid=(bsz * per_b,),
        in_specs=[
            pl.BlockSpec((None, tm, d), lambda i: (i // per_b, i % per_b, 0)),
            pl.BlockSpec((None, 1, mod.shape[-1]), lambda i: (i // per_b, 0, 0)),
            pl.BlockSpec((1, d), lambda i: (0, 0)),
            _layer_spec(*w),
        ],
        out_specs=[
            pl.BlockSpec((None, tm, cw), lambda i: (i // per_b, i % per_b, 0)),
            pl.BlockSpec((None, d1, None, tm // d1, cw), lambda i: (i // per_b, 0, i % per_b, 0, 0)),
            pl.BlockSpec((None, d2, None, tm // d2, cw), lambda i: (i // per_b, 0, i % per_b, 0, 0)),
        ],
        out_shape=[
            jax.ShapeDtypeStruct((bsz, seq, cw), BF16),
            jax.ShapeDtypeStruct((bsz, d1, per_b, tm // d1, cw), BF16),
            jax.ShapeDtypeStruct((bsz, d2, per_b, tm // d2, cw), BF16),
        ],
        scratch_shapes=[pltpu.VMEM((d // LANES, tm, LANES), F32), pltpu.VMEM((d // LANES, tm, LANES), F32),
                        pltpu.VMEM((tm, d), BF16), pltpu.VMEM((tm, d), BF16)],
        compiler_params=_cparams(("parallel",)),
        name="norm_proj",
    )(h, mod, g.reshape(1, d), w[0])
    return o1, o2.reshape(bsz, seq, cw), o3.reshape(bsz, seq, cw)


def _attn_group(blocks):
    blk = ATT_BLOCK
    lane = lax.broadcasted_iota(jnp.int32, (blk, LANES), 1)
    lo = lane < HEAD_DIM
    scores = []
    for qb, kcat, _, mask, _ in blocks:
        zero = jnp.zeros_like(qb)
        q2 = jnp.concatenate([jnp.where(lo, qb, zero), jnp.where(lo, zero, qb)], axis=0)
        s = lax.dot_general(q2, kcat, NT_DIMS, preferred_element_type=F32)
        scores.append(s + mask[...])
    soft = []
    for s, (_, _, _, _, old) in zip(scores, blocks):
        ms, ps = [], []
        for hh in range(2):
            tiles = [s[hh * blk:(hh + 1) * blk, t * LANES:(t + 1) * LANES] for t in range(s.shape[1] // LANES)]
            mx = tiles[0]
            for t in tiles[1:]:
                mx = jnp.maximum(mx, t)
            m = jnp.broadcast_to(jnp.max(mx, axis=-1, keepdims=True), (blk, LANES))
            if old is not None:
                m = jnp.maximum(m, old[0][hh])
            ms.append(m)
            ps.append(jnp.concatenate([jnp.exp2(t - m).astype(BF16) for t in tiles], axis=1))
        alpha = None
        if old is not None:
            alpha = jnp.exp2(jnp.where(lo, old[0][0], old[0][1]) - jnp.where(lo, ms[0], ms[1]))
        soft.append((ms, jnp.concatenate(ps, axis=0), alpha))
    outs = []
    for (ms, p, alpha), (_, _, vcat, _, old) in zip(soft, blocks):
        pv = jnp.dot(p, vcat, preferred_element_type=F32)
        acc = jnp.where(lo, pv[:blk, :LANES], pv[blk:, :LANES])
        den = jnp.where(lo, pv[:blk, LANES:], pv[blk:, LANES:])
        if old is not None:
            acc = alpha * old[1] + acc
            den = alpha * old[0][2] + den
        outs.append((ms + [den], acc))
    return outs


def _attn_kernel(*refs):
    n_in = 3 * N_BRANCHES
    ins, o_ref, scratch = refs[:n_in], refs[n_in], refs[n_in + 1:]
    for half in range(ATT_PAIRS_PER_STEP):
        lanes = pl.ds(half * LANES, LANES)
        _attn_pair(*[r.at[:, lanes] for r in ins], o_ref.at[:, lanes], *scratch)


def _attn_pair(q1, qd2, qd3, k1, kd2, kd3, v1, v2, v3, o_ref, tmp_nat, vd1, vd2, vd3,
               acc_ref, st_ref, acc2_ref, st2_ref, mask_pc, mask_c):
    blk = ATT_BLOCK
    seq = q1.shape[0]
    nblk = seq // blk
    d1, d2 = DILATIONS[1], DILATIONS[2]
    quarter = seq // d1
    per_res = quarter // blk
    assert d2 == d1 * d1 and seq == d2 * blk and nblk % ATT_GROUP == 0 and ATT_GROUP % per_res == 0

    @pl.when(jnp.logical_and(pl.program_id(0) == 0, pl.program_id(1) == 0))
    def _():
        for vd in (vd1, vd2, vd3):
            vd[:, LANES:] = jnp.ones((seq, LANES), BF16)
        qq = lax.broadcasted_iota(jnp.int32, (2 * blk, 2 * blk), 0) & (blk - 1)
        kk = lax.broadcasted_iota(jnp.int32, (2 * blk, 2 * blk), 1)
        valid = jnp.logical_or(jnp.logical_and(kk < blk, kk >= qq), jnp.logical_and(kk >= blk, kk - blk <= qq))
        mask_pc[...] = jnp.where(valid, 0.0, NEG)
        valid_c = (lax.broadcasted_iota(jnp.int32, (2 * blk, blk), 1)
                   <= (lax.broadcasted_iota(jnp.int32, (2 * blk, blk), 0) & (blk - 1)))
        mask_c[...] = jnp.where(valid_c, 0.0, NEG)

    for v, vd in ((v1, vd1), (v2, vd2), (v3, vd3)):
        vd[:, :LANES] = v[...]

    def run_group(blocks, dests):
        for (st_dst, acc_dst, rows), (st, acc) in zip(dests, _attn_group(blocks)):
            for k in range(N_STATS):
                st_dst[k, rows, :] = st[k]
            acc_dst[rows, :] = acc

    def load_old(st_src, acc_src, rows):
        return [st_src[k, rows, :] for k in range(N_STATS)], acc_src[rows, :]

    def b0_group(ns, first):
        blocks, dests = [], []
        for n in ns:
            if first and n == 0:
                rows = pl.ds(0, blk)
                blocks.append((q1[rows, :], k1[rows, :], vd1[rows, :], mask_c, None))
            else:
                rows = pl.ds(pl.multiple_of(n * blk, blk), blk)
                krows = pl.ds(pl.multiple_of((n - 1) * blk, blk), 2 * blk)
                blocks.append((q1[rows, :], k1[krows, :], vd1[krows, :], mask_pc, None))
            dests.append((st_ref, acc_ref, rows))
        run_group(blocks, dests)

    b0_group(list(range(ATT_GROUP)), True)

    def b0_body(i, carry):
        b0_group([ATT_GROUP * i + j for j in range(ATT_GROUP)], False)
        return carry

    lax.fori_loop(1, nblk // ATT_GROUP, b0_body, 0)

    res_per_trip = ATT_GROUP // per_res

    def b1_body(i, carry):
        blocks, dests = [], []
        for jr in range(res_per_trip):
            a1 = res_per_trip * i + jr
            base = a1 * quarter
            for n in range(per_res):
                rows = pl.ds(a1 + d1 * blk * n, blk, stride=d1)
                qrows = pl.ds(pl.multiple_of(base + n * blk, blk), blk)
                if n == 0:
                    krows, mask = qrows, mask_c
                else:
                    krows, mask = pl.ds(pl.multiple_of(base + (n - 1) * blk, blk), 2 * blk), mask_pc
                blocks.append((qd2[qrows, :], kd2[krows, :], vd2[krows, :], mask,
                               load_old(st_ref, acc_ref, rows)))
                dests.append((st2_ref, acc2_ref, qrows))
        run_group(blocks, dests)
        return carry

    lax.fori_loop(0, d1 // res_per_trip, b1_body, 0)

    def b2_body(i, carry):
        blocks, dests = [], []
        for j in range(ATT_GROUP):
            a1, a2 = j % d1, (ATT_GROUP // d1) * i + j // d1
            qrows = pl.ds(pl.multiple_of((ATT_GROUP * i + j) * blk, blk), blk)
            rows = pl.ds(a1 * quarter + a2, blk, stride=d1)
            blocks.append((qd3[qrows, :], kd3[qrows, :], vd3[qrows, :], mask_c,
                           load_old(st2_ref, acc2_ref, rows)))
            dests.append((st2_ref, acc2_ref, rows))
        run_group(blocks, dests)
        return carry

    lax.fori_loop(0, d2 // ATT_GROUP, b2_body, 0)

    for a in range(d1):
        seg = slice(a * quarter, (a + 1) * quarter)
        tmp_nat[pl.ds(a, quarter, stride=d1), :] = acc2_ref[seg, :] / st2_ref[N_STATS - 1, seg, :]
    o_ref[...] = tmp_nat[...].astype(o_ref.dtype)


def _attention(qs, kvs):
    bsz, seq, _ = qs[0].shape
    steps = D_MODEL // (ATT_PAIRS_PER_STEP * LANES)

    def slab(col0):
        return pl.BlockSpec((None, seq, ATT_PAIRS_PER_STEP * LANES), lambda b, hp: (b, 0, col0 + hp))

    in_specs = [slab(0)] * N_BRANCHES + [slab(0)] * N_BRANCHES + [slab(steps)] * N_BRANCHES
    return pl.pallas_call(
        _attn_kernel,
        grid=(bsz, steps),
        in_specs=in_specs,
        out_specs=slab(0),
        out_shape=jax.ShapeDtypeStruct((bsz, seq, D_MODEL), BF16),
        scratch_shapes=([pltpu.VMEM((seq, LANES), F32)] + [pltpu.VMEM((seq, 2 * LANES), BF16)] * 3
                        + [pltpu.VMEM((seq, LANES), F32), pltpu.VMEM((N_STATS, seq, LANES), F32)] * 2
                        + [pltpu.VMEM((2 * ATT_BLOCK, 2 * ATT_BLOCK), F32), pltpu.VMEM((2 * ATT_BLOCK, ATT_BLOCK), F32)]),
        compiler_params=_cparams(("arbitrary", "arbitrary")),
        name="attention",
    )(*qs, *kvs, *kvs)


def _regroup8(tiles):
    t = list(tiles)
    lane = lax.broadcasted_iota(jnp.int32, t[0].shape, 1)
    piece = lane // SSM_GROUP
    for dist in (4, 2, 1):
        bit = (piece & dist) != 0
        shift = dist * SSM_GROUP
        for a in range(8):
            if a & dist:
                continue
            b = a | dist
            ta, tb = t[a], t[b]
            t[a] = jnp.where(bit, pltpu.roll(tb, shift, 1), ta)
            t[b] = jnp.where(bit, tb, pltpu.roll(ta, LANES - shift, 1))
    return t


def _s5_tile_perm(bsz):
    tt = S5_TILE_TOKENS
    n = bsz * tt
    perm = np.zeros((n, n), np.float32)
    for b in range(bsz):
        for c in range(tt // CHUNK):
            for t in range(CHUNK):
                perm[(t * (tt // CHUNK) + c) * bsz + b, b * tt + c * CHUNK + t] = 1.0
    return perm


def _s5_pre_kernel(*refs, n_ride):
    h_ref, mod_ref, g_ref, perm_ref = refs[:4]
    z_ref = refs[4 + n_ride]
    u_scr, up_scr = refs[5 + 2 * n_ride:]
    _ride_cast(refs[4:4 + n_ride], refs[5 + n_ride:5 + 2 * n_ride])
    d = D_MODEL
    bsz, tt, _ = h_ref.shape
    g = g_ref[...]
    for b in range(bsz):
        u_scr[b * tt:(b + 1) * tt, :] = _normmod(
            h_ref[b], g, mod_ref[b, :, :d], mod_ref[b, :, d:2 * d]).astype(BF16)
    up_scr[...] = jnp.dot(perm_ref[...], u_scr[...], preferred_element_type=F32)
    rows = z_ref.shape[0]
    for j in range(d // LANES):
        for half in range(CHUNK // 8):
            tiles = [up_scr[(8 * half + tl) * rows:(8 * half + tl + 1) * rows, j * LANES:(j + 1) * LANES]
                     for tl in range(8)]
            outs = _regroup8(tiles)
            for gl in range(8):
                col = (8 * j + gl) * GROUP_COLS + half * LANES
                z_ref[:, col:col + LANES] = outs[gl].astype(z_ref.dtype)


def _s5_pre(h, mod, g, perm, ride):
    bsz, seq, d = h.shape
    tt = S5_TILE_TOKENS
    rows = bsz * tt // CHUNK
    ride_in, ride_out, ride_shapes, ride_args = _ride_specs(ride, seq // tt)
    outs = pl.pallas_call(
        functools.partial(_s5_pre_kernel, n_ride=len(ride_in)),
        grid=(seq // tt,),
        in_specs=[
            pl.BlockSpec((bsz, tt, d), lambda i: (0, i, 0)),
            pl.BlockSpec(mod.shape, lambda i: (0, 0, 0)),
            pl.BlockSpec((1, d), lambda i: (0, 0)),
            pl.BlockSpec(perm.shape, lambda i: (0, 0)),
        ] + ride_in,
        out_specs=[pl.BlockSpec((rows, CHUNK * d), lambda i: (i, 0))] + ride_out,
        out_shape=[jax.ShapeDtypeStruct((bsz * seq // CHUNK, CHUNK * d), BF16)] + ride_shapes,
        scratch_shapes=[pltpu.VMEM((bsz * tt, d), BF16), pltpu.VMEM((bsz * tt, d), F32)],
        compiler_params=_cparams(("parallel",)),
        name="s5_pre",
    )(h, mod, g.reshape(1, d), perm, *ride_args)
    return outs[0], outs[1:]


def _s5_core_kernel(z_ref, win_ref, wt_ref, wx_ref, dec_ref, d_ref, o_ref, s_scr, x_scr, *, bsz):
    gc = GROUP_COLS
    nchunk = z_ref.shape[0] // bsz
    s_scr[...] = jnp.dot(z_ref[...], win_ref[...].reshape(2 * gc, gc), preferred_element_type=F32)
    ar, ai = dec_ref[:, :LANES], dec_ref[:, LANES:]

    def step(c, carry):
        xr, xi = carry
        rows = pl.ds(pl.multiple_of(c * bsz, bsz), bsz)
        x_scr[rows, :LANES] = xr
        x_scr[rows, LANES:] = xi
        return (ar * xr - ai * xi + s_scr[rows, :LANES], ar * xi + ai * xr + s_scr[rows, LANES:])

    zero = jnp.zeros((bsz, LANES), F32)
    lax.fori_loop(0, nchunk, step, (zero, zero))
    x = x_scr[...].astype(BF16)
    for gi in range(2):
        z = z_ref[:, gi * gc:(gi + 1) * gc]
        y = jnp.dot(jnp.concatenate([z, x], axis=1), jnp.concatenate([wt_ref[gi], wx_ref[gi]], axis=0),
                    preferred_element_type=F32)
        y = y + d_ref[:, gi * gc:(gi + 1) * gc] * z.astype(F32)
        o_ref[:, gi * gc:(gi + 1) * gc] = jax.nn.gelu(y).astype(o_ref.dtype)


def _s5_core(z, tables, layer, bsz):
    nrow, width = z.shape
    w_in, w_toep, w_x, dec, d_perm = tables
    wspec = pl.BlockSpec((None, 2, GROUP_COLS, GROUP_COLS), lambda k: (layer, k, 0, 0))
    return pl.pallas_call(
        functools.partial(_s5_core_kernel, bsz=bsz),
        grid=(N_PAIRS,),
        in_specs=[
            pl.BlockSpec((nrow, PAIR_COLS), lambda k: (0, k)),
            wspec, wspec, wspec,
            pl.BlockSpec((None, 1, GROUP_COLS), lambda k: (layer, 0, k)),
            pl.BlockSpec((None, 1, PAIR_COLS), lambda k: (layer, 0, k)),
        ],
        out_specs=pl.BlockSpec((nrow, PAIR_COLS), lambda k: (0, k)),
        out_shape=jax.ShapeDtypeStruct((nrow, width), BF16),
        scratch_shapes=[pltpu.VMEM((nrow, GROUP_COLS), F32), pltpu.VMEM((nrow, GROUP_COLS), F32)],
        compiler_params=_cparams(("parallel",)),
        name="s5_core",
    )(z, w_in, w_toep, w_x, dec, d_perm)


def _s5_weights(lam_re, lam_im, log_dt, b_re, b_im, c_re, c_im, d_skip):
    g, p, c16 = SSM_GROUPS, SSM_STATE, SSM_GROUP
    lam = lax.complex(lam_re.astype(F32), lam_im.astype(F32))
    dt = jnp.exp(log_dt.astype(F32))[:, None]
    steps = jnp.arange(CHUNK + 1, dtype=F32)
    apow = jnp.exp((lam * dt)[None] * steps[:, None, None])
    a = apow[1]
    bbar = ((a - 1.0) / lam)[..., None] * lax.complex(b_re.astype(F32), b_im.astype(F32))
    cmat = lax.complex(c_re.astype(F32), c_im.astype(F32))
    odd = (jnp.arange(g) % 2)[:, None, None]

    a_rev = jnp.transpose(apow[CHUNK - 1::-1], (1, 0, 2))[:, :, None, :]
    win = a_rev * jnp.transpose(bbar, (0, 2, 1))[:, None, :, :]
    win_re = win.real.astype(BF16).reshape(g, GROUP_COLS, p)
    win_im = win.imag.astype(BF16).reshape(g, GROUP_COLS, p)
    zeros = jnp.zeros_like(win_re)
    w_in = jnp.concatenate([
        jnp.where(odd == 0, win_re, zeros), jnp.where(odd == 1, win_re, zeros),
        jnp.where(odd == 0, win_im, zeros), jnp.where(odd == 1, win_im, zeros)], axis=-1)

    kern = jnp.einsum('gop,kgp,gpi->kgoi', cmat, apow[:CHUNK], bbar, precision=HIGHEST).real
    kcat = jnp.transpose(kern.astype(BF16), (1, 3, 0, 2)).reshape(g, c16, GROUP_COLS)
    w_toep = jnp.stack([jnp.pad(kcat[:, :, :GROUP_COLS - c16 * s], ((0, 0), (0, 0), (c16 * s, 0)))
                        for s in range(CHUNK)], axis=1)
    w_toep = w_toep.reshape(g, GROUP_COLS, GROUP_COLS)

    c_exp = jnp.tile(jnp.transpose(cmat, (0, 2, 1)), (1, 1, CHUNK))
    a_exp = jnp.repeat(jnp.transpose(apow[1:], (1, 2, 0)), c16, axis=-1)
    cw = c_exp * a_exp
    cw_re, cw_im = cw.real.astype(BF16), (-cw.imag).astype(BF16)
    zx = jnp.zeros_like(cw_re)
    w_x = jnp.concatenate([
        jnp.where(odd == 0, cw_re, zx), jnp.where(odd == 1, cw_re, zx),
        jnp.where(odd == 0, cw_im, zx), jnp.where(odd == 1, cw_im, zx)], axis=1)

    a16 = apow[CHUNK].reshape(N_PAIRS, 2 * p)
    dec = jnp.concatenate([a16.real, a16.imag], axis=-1).reshape(1, N_PAIRS * GROUP_COLS)
    d_perm = jnp.broadcast_to(d_skip.astype(F32).reshape(g, 1, c16), (g, CHUNK, c16)).reshape(1, g * GROUP_COLS)
    return w_in, w_toep, w_x, dec, d_perm


def _mlp(u, w1_ref, w2_ref):
    d = D_MODEL
    acc = None
    for k in range(D_FF // d):
        a = jnp.dot(u, w1_ref[:, k * d:(k + 1) * d], preferred_element_type=F32)
        a = jnp.square(jnp.maximum(a, 0.0)).astype(BF16)
        part = jnp.dot(a, w2_ref[k * d:(k + 1) * d, :], preferred_element_type=F32)
        acc = part if acc is None else acc + part
    return acc


def _final_norm(h, fg_ref):
    ms = jnp.mean(h * h, axis=-1, keepdims=True)
    return h * lax.rsqrt(ms + EPS) * fg_ref[...]


def _ride_specs(ride, nsteps):
    ins, outs, shapes = [], [], []
    for st, layer in ride:
        _, r, c = st.shape
        ins.append(pl.BlockSpec((None, r // nsteps, c), lambda i, layer=layer: (layer, i, 0)))
        outs.append(pl.BlockSpec((r // nsteps, c), lambda i: (i, 0)))
        shapes.append(jax.ShapeDtypeStruct((r, c), BF16))
    return ins, outs, shapes, [st for st, _ in ride]


def _ride_cast(ride_in, ride_out):
    for src, dst in zip(ride_in, ride_out):
        dst[...] = src[...].astype(dst.dtype)


def _post_attn_kernel(*refs, final, n_ride):
    h_ref, y_ref, moda_ref, modm_ref, g_ref, wp_ref, w1_ref, w2_ref = refs[:8]
    pos = 8 + int(final)
    fg_ref = refs[8] if final else None
    o_ref = refs[pos + n_ride]
    _ride_cast(refs[pos:pos + n_ride], refs[pos + n_ride + 1:])
    d = D_MODEL
    ymix = jnp.dot(y_ref[...], wp_ref[...], preferred_element_type=F32)
    h1 = h_ref[...] + moda_ref[:, 2 * d:3 * d] * ymix
    u = _normmod(h1, g_ref[...], modm_ref[:, :d], modm_ref[:, d:2 * d]).astype(BF16)
    h2 = h1 + modm_ref[:, 2 * d:3 * d] * _mlp(u, w1_ref, w2_ref)
    o_ref[...] = _final_norm(h2, fg_ref) if final else h2


def _post_attn(h, o, moda, modm, g2, w_o, w1, w2, final_g, ride):
    bsz, seq, d = h.shape
    tm = MLP_ROWS
    per_b = seq // tm
    row_spec = pl.BlockSpec((None, tm, d), lambda i: (i // per_b, i % per_b, 0))
    final = final_g is not None
    ride_in, ride_out, ride_shapes, ride_args = _ride_specs(ride, bsz * per_b)
    in_specs = [
        row_spec, row_spec,
        pl.BlockSpec((None, 1, moda.shape[-1]), lambda i: (i // per_b, 0, 0)),
        pl.BlockSpec((None, 1, modm.shape[-1]), lambda i: (i // per_b, 0, 0)),
        pl.BlockSpec((1, d), lambda i: (0, 0)),
        _layer_spec(*w_o), _layer_spec(*w1), _layer_spec(*w2),
    ]
    args = [h, o, moda, modm, g2.reshape(1, d), w_o[0], w1[0], w2[0]]
    if final:
        in_specs.append(pl.BlockSpec((1, d), lambda i: (0, 0)))
        args.append(final_g.reshape(1, d))
    args += ride_args
    outs = pl.pallas_call(
        functools.partial(_post_attn_kernel, final=final, n_ride=len(ride_in)),
        grid=(bsz * per_b,),
        in_specs=in_specs + ride_in,
        out_specs=[row_spec] + ride_out,
        out_shape=[jax.ShapeDtypeStruct((bsz, seq, d), F32)] + ride_shapes,
        compiler_params=_cparams(("parallel",)),
        name="post_attn_mlp",
    )(*args)
    return outs[0], outs[1:]


def _post_s5_kernel(*refs, n_ride):
    h_ref, z_ref, moda_ref, modm_ref, g_ref, permt_ref, wp_ref, w1_ref, w2_ref = refs[:9]
    o_ref = refs[9 + n_ride]
    z_scr, h1_scr, u_scr = refs[10 + 2 * n_ride:]
    _ride_cast(refs[9:9 + n_ride], refs[10 + n_ride:10 + 2 * n_ride])
    d = D_MODEL
    bsz, tt, _ = h_ref.shape
    rows = z_ref.shape[0]
    for j in range(d // LANES):
        for half in range(CHUNK // 8):
            tiles = []
            for gl in range(8):
                col = (8 * j + gl) * GROUP_COLS + half * LANES
                tiles.append(z_ref[:, col:col + LANES].astype(F32))
            outs = _regroup8(tiles)
            for tl in range(8):
                t = 8 * half + tl
                z_scr[t * rows:(t + 1) * rows, j * LANES:(j + 1) * LANES] = outs[tl].astype(BF16)
    zn = jnp.dot(permt_ref[...], z_scr[...], preferred_element_type=F32).astype(BF16)
    zz = jnp.dot(zn, wp_ref[...], preferred_element_type=F32)
    ymix = zz[:, :d] * jax.nn.sigmoid(zz[:, d:])
    g = g_ref[...]
    for b in range(bsz):
        sl = slice(b * tt, (b + 1) * tt)
        h1 = h_ref[b] + moda_ref[b, :, 2 * d:3 * d] * ymix[sl, :]
        h1_scr[sl, :] = h1
        u_scr[sl, :] = _normmod(h1, g, modm_ref[b, :, :d], modm_ref[b, :, d:2 * d]).astype(BF16)
    acc = _mlp(u_scr[...], w1_ref, w2_ref)
    for b in range(bsz):
        sl = slice(b * tt, (b + 1) * tt)
        o_ref[b] = h1_scr[sl, :] + modm_ref[b, :, 2 * d:3 * d] * acc[sl, :]


def _post_s5(h, zo, moda, modm, g2, permt, w_glu, w1, w2, ride):
    bsz, seq, d = h.shape
    tt = S5_TILE_TOKENS
    rows = bsz * tt // CHUNK
    h_spec = pl.BlockSpec((bsz, tt, d), lambda i: (0, i, 0))
    ride_in, ride_out, ride_shapes, ride_args = _ride_specs(ride, seq // tt)
    outs = pl.pallas_call(
        functools.partial(_post_s5_kernel, n_ride=len(ride_in)),
        grid=(seq // tt,),
        in_specs=[
            h_spec,
            pl.BlockSpec((rows, CHUNK * d), lambda i: (i, 0)),
            pl.BlockSpec(moda.shape, lambda i: (0, 0, 0)),
            pl.BlockSpec(modm.shape, lambda i: (0, 0, 0)),
            pl.BlockSpec((1, d), lambda i: (0, 0)),
            pl.BlockSpec(permt.shape, lambda i: (0, 0)),
            _layer_spec(*w_glu), _layer_spec(*w1), _layer_spec(*w2),
        ] + ride_in,
        out_specs=[h_spec] + ride_out,
        out_shape=[jax.ShapeDtypeStruct((bsz, seq, d), F32)] + ride_shapes,
        scratch_shapes=[pltpu.VMEM((bsz * tt, d), BF16), pltpu.VMEM((bsz * tt, d), F32),
                        pltpu.VMEM((bsz * tt, d), BF16)],
        compiler_params=_cparams(("parallel",)),
        name="post_s5_mlp",
    )(h, zo, moda, modm, g2.reshape(1, d), permt, w_glu[0], w1[0], w2[0], *ride_args)
    return outs[0], outs[1:]


def kernel(x, c, ln_g, ada_w, ada_b, ssm_lam_re, ssm_lam_im, ssm_log_dt, ssm_b_re, ssm_b_im, ssm_c_re, ssm_c_im, ssm_d, ssm_w_glu, kv_g, kv_ada_w, kv_ada_b, w_kv, attn_w_q, attn_w_o, mlp_w1, mlp_w2, final_g):
    bsz, seq, d = x.shape
    depth = ln_g.shape[0]
    n_s5 = ssm_lam_re.shape[0]
    assert d == D_MODEL and seq % (DILATIONS[-1] * ATT_BLOCK) == 0 and 0 < n_s5 < depth

    mods = _ada_mods(c, ada_w.reshape(depth * 2, d, 3 * d), ada_b.reshape(depth * 2, 3 * d))
    mods = mods.reshape(depth, 2, bsz, 1, 3 * d)
    kv_mod = _ada_mods(c, kv_ada_w[None], kv_ada_b[None]).reshape(bsz, 1, 2 * d)

    perm_np = _s5_tile_perm(bsz)
    perm = jnp.asarray(perm_np, BF16)
    permt = jnp.asarray(perm_np.T, BF16)
    tables = jax.vmap(_s5_weights)(ssm_lam_re, ssm_lam_im, ssm_log_dt, ssm_b_re, ssm_b_im,
                                   ssm_c_re, ssm_c_im, ssm_d)

    def as_layer(cast):
        return (cast[None], 0)

    h = x
    for layer in range(n_s5):
        ride = [(mlp_w1, 0), (mlp_w2, 0), (ssm_w_glu, 0)] if layer == 0 else []
        z, cast = _s5_pre(h, mods[layer, 0], ln_g[layer, 0], perm, ride)
        if layer == 0:
            w1, w2, w_glu = map(as_layer, cast)
        zo = _s5_core(z, tables, layer, bsz)
        ride = [(mlp_w1, layer + 1), (mlp_w2, layer + 1)]
        if layer + 1 < n_s5:
            ride += [(ssm_w_glu, layer + 1)]
        else:
            ride += [(w_kv[None], 0), (attn_w_q, 0), (attn_w_o, 0)]
        h, cast = _post_s5(h, zo, mods[layer, 0], mods[layer, 1], ln_g[layer, 1], permt, w_glu, w1, w2, ride)
        w1, w2 = as_layer(cast[0]), as_layer(cast[1])
        if layer + 1 < n_s5:
            w_glu = as_layer(cast[2])
        else:
            w_kvb, w_q, w_o = map(as_layer, cast[2:])

    kv_cols = tuple((i * d, (N_BRANCHES + i) * d) for i in range(N_BRANCHES))
    q_cols = tuple((i * d,) for i in range(N_BRANCHES))
    kvs = _norm_proj(h, kv_mod, kv_g, w_kvb, kv_cols, 1.0, KV_PROJ_TILE)
    for layer in range(n_s5, depth):
        j = layer - n_s5
        last = layer == depth - 1
        qs = _norm_proj(h, mods[layer, 0], ln_g[layer, 0], w_q, q_cols, Q_SCALE, Q_PROJ_TILE)
        o = _attention(qs, kvs)
        ride = [] if last else [(mlp_w1, layer + 1), (mlp_w2, layer + 1), (attn_w_q, j + 1), (attn_w_o, j + 1)]
        h, cast = _post_attn(h, o, mods[layer, 0], mods[layer, 1], ln_g[layer, 1], w_o, w1, w2,
                             final_g if last else None, ride)
        if not last:
            w1, w2, w_q, w_o = map(as_layer, cast)
    return h
```

```python
import functools
import math

import numpy as np

import jax
import jax.numpy as jnp
from jax import lax
from jax.experimental import pallas as pl
from jax.experimental.pallas import tpu as pltpu

F32 = jnp.float32
BF16 = jnp.bfloat16

D_MODEL = 1024
SSM_GROUP = 16
SSM_GROUPS = D_MODEL // SSM_GROUP
SSM_STATE = 64
HEAD_DIM = 64
N_HEADS = D_MODEL // HEAD_DIM
DILATIONS = (1, 4, 16)
N_BRANCHES = len(DILATIONS)
ATT_BLOCK = 128
ATT_GROUP = 16
ATT_PAIRS_PER_STEP = 2
N_STATS = 3
Q_SCALE = HEAD_DIM ** -0.5 * math.log2(math.e)
D_FF = 4 * D_MODEL
EPS = 1e-6
NEG = -1e30

LANES = 128
CHUNK = 16
GROUP_COLS = CHUNK * SSM_GROUP
PAIR_COLS = 2 * GROUP_COLS
N_PAIRS = SSM_GROUPS // 2
S5_TILE_TOKENS = 2 * CHUNK
MLP_ROWS = 1024
KV_PROJ_TILE = 512
Q_PROJ_TILE = 1024
VMEM_LIMIT = 56 * 1024 * 1024

HIGHEST = lax.Precision.HIGHEST
NT_DIMS = (((1,), (1,)), ((), ()))


def _cparams(sem):
    return pltpu.CompilerParams(dimension_semantics=sem, vmem_limit_bytes=VMEM_LIMIT)


def _layer_spec(stack, layer):
    zeros = (0,) * (stack.ndim - 1)
    return pl.BlockSpec((None,) + stack.shape[1:], lambda i: (layer,) + zeros, pipeline_mode=pl.Buffered(1))


def _normmod(x, g, shift, scale):
    ms = jnp.mean(x * x, axis=-1, keepdims=True)
    return (x * lax.rsqrt(ms + EPS) * g) * (1.0 + scale) + shift


def _ada_kernel(c_ref, w_ref, b_ref, o_ref):
    c = c_ref[...]
    sc = (c * jax.nn.sigmoid(c)).astype(BF16)
    o_ref[...] = jnp.dot(sc, w_ref[...].astype(BF16), preferred_element_type=F32) + b_ref[...]


def _ada_mods(c, w, b):
    n, d, width = w.shape
    bsz = c.shape[0]
    return pl.pallas_call(
        _ada_kernel,
        grid=(n,),
        in_specs=[
            pl.BlockSpec((bsz, d), lambda i: (0, 0)),
            pl.BlockSpec((None, d, width), lambda i: (i, 0, 0)),
            pl.BlockSpec((None, 1, width), lambda i: (i, 0, 0)),
        ],
        out_specs=pl.BlockSpec((None, bsz, width), lambda i: (i, 0, 0)),
        out_shape=jax.ShapeDtypeStruct((n, bsz, width), F32),
        compiler_params=_cparams(("parallel",)),
        name="ada_mods",
    )(c, w, b.reshape(n, 1, width))


def _proj_kernel(h_ref, mod_ref, g_ref, w_ref, o1_ref, o2_ref, o3_ref, us, us4, up4, up16, *, wcols, out_scale):
    d = D_MODEL
    tm = h_ref.shape[0]
    d1 = DILATIONS[1]
    q4, q16 = tm // d1, tm // DILATIONS[2]
    cw = o1_ref.shape[1]
    u = _normmod(h_ref[...], g_ref[...], mod_ref[:, :d], mod_ref[:, d:2 * d])
    for l in range(d // LANES):
        us[l] = u[:, l * LANES:(l + 1) * LANES]
    ub = u.astype(BF16)
    for n in range(cw // d):
        cols = slice(n * d, (n + 1) * d)
        res = jnp.dot(ub, w_ref[:, wcols[0][n]:wcols[0][n] + d], preferred_element_type=F32)
        o1_ref[:, cols] = (res * out_scale).astype(o1_ref.dtype)
    for l in range(d // LANES):
        for a in range(d1):
            part = us[l, pl.ds(a, q4, stride=d1), :]
            us4[l, a * q4:(a + 1) * q4, :] = part
            up4[a * q4:(a + 1) * q4, l * LANES:(l + 1) * LANES] = part.astype(BF16)
    for n in range(cw // d):
        res = jnp.dot(up4[...], w_ref[:, wcols[1][n]:wcols[1][n] + d], preferred_element_type=F32)
        for a in range(d1):
            o2_ref[a, :, n * d:(n + 1) * d] = (res[a * q4:(a + 1) * q4, :] * out_scale).astype(o2_ref.dtype)
    for l in range(d // LANES):
        for a1 in range(d1):
            for a2 in range(d1):
                r = d1 * a2 + a1
                up16[r * q16:(r + 1) * q16, l * LANES:(l + 1) * LANES] = us4[
                    l, pl.ds(a1 * q4 + a2, q16, stride=d1), :].astype(BF16)
    for n in range(cw // d):
        res = jnp.dot(up16[...], w_ref[:, wcols[2][n]:wcols[2][n] + d], preferred_element_type=F32)
        for r in range(DILATIONS[2]):
            o3_ref[r, :, n * d:(n + 1) * d] = (res[r * q16:(r + 1) * q16, :] * out_scale).astype(o3_ref.dtype)


def _norm_proj(h, mod, g, w, wcols, out_scale, tm):
    bsz, seq, d = h.shape
    cw = len(wcols[0]) * d
    per_b = seq // tm
    d1, d2 = DILATIONS[1], DILATIONS[2]
    o1, o2, o3 = pl.pallas_call(
        functools.partial(_proj_kernel, wcols=wcols, out_scale=out_scale),
        grid=(bsz * per_b,),
        in_specs=[
            pl.BlockSpec((None, tm, d), lambda i: (i // per_b, i % per_b, 0)),
            pl.BlockSpec((None, 1, mod.shape[-1]), lambda i: (i // per_b, 0, 0)),
            pl.BlockSpec((1, d), lambda i: (0, 0)),
            _layer_spec(*w),
        ],
        out_specs=[
            pl.BlockSpec((None, tm, cw), lambda i: (i // per_b, i % per_b, 0)),
            pl.BlockSpec((None, d1, None, tm // d1, cw), lambda i: (i // per_b, 0, i % per_b, 0, 0)),
            pl.BlockSpec((None, d2, None, tm // d2, cw), lambda i: (i // per_b, 0, i % per_b, 0, 0)),
        ],
        out_shape=[
            jax.ShapeDtypeStruct((bsz, seq, cw), BF16),
            jax.ShapeDtypeStruct((bsz, d1, per_b, tm // d1, cw), BF16),
            jax.ShapeDtypeStruct((bsz, d2, per_b, tm // d2, cw), BF16),
        ],
        scratch_shapes=[pltpu.VMEM((d // LANES, tm, LANES), F32), pltpu.VMEM((d // LANES, tm, LANES), F32),
                        pltpu.VMEM((tm, d), BF16), pltpu.VMEM((tm, d), BF16)],
        compiler_params=_cparams(("parallel",)),
        name="norm_proj",
    )(h, mod, g.reshape(1, d), w[0])
    return o1, o2.reshape(bsz, seq, cw), o3.reshape(bsz, seq, cw)


def _attn_group(blocks):
    blk = ATT_BLOCK
    lane = lax.broadcasted_iota(jnp.int32, (blk, LANES), 1)
    lo = lane < HEAD_DIM
    scores = []
    for qb, kcat, _, mask, _ in blocks:
        zero = jnp.zeros_like(qb)
        q2 = jnp.concatenate([jnp.where(lo, qb, zero), jnp.where(lo, zero, qb)], axis=0)
        s = lax.dot_general(q2, kcat, NT_DIMS, preferred_element_type=F32)
        scores.append(s + mask[...])
    soft = []
    for s, (_, _, _, _, old) in zip(scores, blocks):
        ms, ps = [], []
        for hh in range(2):
            tiles = [s[hh * blk:(hh + 1) * blk, t * LANES:(t + 1) * LANES] for t in range(s.shape[1] // LANES)]
            mx = tiles[0]
            for t in tiles[1:]:
                mx = jnp.maximum(mx, t)
            m = jnp.broadcast_to(jnp.max(mx, axis=-1, keepdims=True), (blk, LANES))
            if old is not None:
                m = jnp.maximum(m, old[0][hh])
            ms.append(m)
            ps.append(jnp.concatenate([jnp.exp2(t - m).astype(BF16) for t in tiles], axis=1))
        alpha = None
        if old is not None:
            alpha = jnp.exp2(jnp.where(lo, old[0][0], old[0][1]) - jnp.where(lo, ms[0], ms[1]))
        soft.append((ms, jnp.concatenate(ps, axis=0), alpha))
    outs = []
    for (ms, p, alpha), (_, _, vcat, _, old) in zip(soft, blocks):
        pv = jnp.dot(p, vcat, preferred_element_type=F32)
        acc = jnp.where(lo, pv[:blk, :LANES], pv[blk:, :LANES])
        den = jnp.where(lo, pv[:blk, LANES:], pv[blk:, LANES:])
        if old is not None:
            acc = alpha * old[1] + acc
            den = alpha * old[0][2] + den
        outs.append((ms + [den], acc))
    return outs


def _attn_kernel(*refs):
    n_in = 3 * N_BRANCHES
    ins, o_ref, scratch = refs[:n_in], refs[n_in], refs[n_in + 1:]
    for half in range(ATT_PAIRS_PER_STEP):
        lanes = pl.ds(half * LANES, LANES)
        _attn_pair(*[r.at[:, lanes] for r in ins], o_ref.at[:, lanes], *scratch)


def _attn_pair(q1, qd2, qd3, k1, kd2, kd3, v1, v2, v3, o_ref, tmp_nat, vd1, vd2, vd3,
               acc_ref, st_ref, acc2_ref, st2_ref, mask_pc, mask_c):
    blk = ATT_BLOCK
    seq = q1.shape[0]
    nblk = seq // blk
    d1, d2 = DILATIONS[1], DILATIONS[2]
    quarter = seq // d1
    per_res = quarter // blk
    assert d2 == d1 * d1 and seq == d2 * blk and nblk % ATT_GROUP == 0 and ATT_GROUP % per_res == 0

    @pl.when(jnp.logical_and(pl.program_id(0) == 0, pl.program_id(1) == 0))
    def _():
        for vd in (vd1, vd2, vd3):
            vd[:, LANES:] = jnp.ones((seq, LANES), BF16)
        qq = lax.broadcasted_iota(jnp.int32, (2 * blk, 2 * blk), 0) & (blk - 1)
        kk = lax.broadcasted_iota(jnp.int32, (2 * blk, 2 * blk), 1)
        valid = jnp.logical_or(jnp.logical_and(kk < blk, kk >= qq), jnp.logical_and(kk >= blk, kk - blk <= qq))
        mask_pc[...] = jnp.where(valid, 0.0, NEG)
        valid_c = (lax.broadcasted_iota(jnp.int32, (2 * blk, blk), 1)
                   <= (lax.broadcasted_iota(jnp.int32, (2 * blk, blk), 0) & (blk - 1)))
        mask_c[...] = jnp.where(valid_c, 0.0, NEG)

    for v, vd in ((v1, vd1), (v2, vd2), (v3, vd3)):
        vd[:, :LANES] = v[...]

    def run_group(blocks, dests):
        for (st_dst, acc_dst, rows), (st, acc) in zip(dests, _attn_group(blocks)):
            for k in range(N_STATS):
                st_dst[k, rows, :] = st[k]
            acc_dst[rows, :] = acc

    def load_old(st_src, acc_src, rows):
        return [st_src[k, rows, :] for k in range(N_STATS)], acc_src[rows, :]

    def b0_group(ns, first):
        blocks, dests = [], []
        for n in ns:
            if first and n == 0:
                rows = pl.ds(0, blk)
                blocks.append((q1[rows, :], k1[rows, :], vd1[rows, :], mask_c, None))
            else:
                rows = pl.ds(pl.multiple_of(n * blk, blk), blk)
                krows = pl.ds(pl.multiple_of((n - 1) * blk, blk), 2 * blk)
                blocks.append((q1[rows, :], k1[krows, :], vd1[krows, :], mask_pc, None))
            dests.append((st_ref, acc_ref, rows))
        run_group(blocks, dests)

    b0_group(list(range(ATT_GROUP)), True)

    def b0_body(i, carry):
        b0_group([ATT_GROUP * i + j for j in range(ATT_GROUP)], False)
        return carry

    lax.fori_loop(1, nblk // ATT_GROUP, b0_body, 0)

    res_per_trip = ATT_GROUP // per_res

    def b1_body(i, carry):
        blocks, dests = [], []
        for jr in range(res_per_trip):
            a1 = res_per_trip * i + jr
            base = a1 * quarter
            for n in range(per_res):
                rows = pl.ds(a1 + d1 * blk * n, blk, stride=d1)
                qrows = pl.ds(pl.multiple_of(base + n * blk, blk), blk)
                if n == 0:
                    krows, mask = qrows, mask_c
                else:
                    krows, mask = pl.ds(pl.multiple_of(base + (n - 1) * blk, blk), 2 * blk), mask_pc
                blocks.append((qd2[qrows, :], kd2[krows, :], vd2[krows, :], mask,
                               load_old(st_ref, acc_ref, rows)))
                dests.append((st2_ref, acc2_ref, qrows))
        run_group(blocks, dests)
        return carry

    lax.fori_loop(0, d1 // res_per_trip, b1_body, 0)

    def b2_body(i, carry):
        blocks, dests = [], []
        for j in range(ATT_GROUP):
            a1, a2 = j % d1, (ATT_GROUP // d1) * i + j // d1
            qrows = pl.ds(pl.multiple_of((ATT_GROUP * i + j) * blk, blk), blk)
            rows = pl.ds(a1 * quarter + a2, blk, stride=d1)
            blocks.append((qd3[qrows, :], kd3[qrows, :], vd3[qrows, :], mask_c,
                           load_old(st2_ref, acc2_ref, rows)))
            dests.append((st2_ref, acc2_ref, rows))
        run_group(blocks, dests)
        return carry

    lax.fori_loop(0, d2 // ATT_GROUP, b2_body, 0)

    for a in range(d1):
        seg = slice(a * quarter, (a + 1) * quarter)
        tmp_nat[pl.ds(a, quarter, stride=d1), :] = acc2_ref[seg, :] / st2_ref[N_STATS - 1, seg, :]
    o_ref[...] = tmp_nat[...].astype(o_ref.dtype)


def _attention(qs, kvs):
    bsz, seq, _ = qs[0].shape
    steps = D_MODEL // (ATT_PAIRS_PER_STEP * LANES)

    def slab(col0):
        return pl.BlockSpec((None, seq, ATT_PAIRS_PER_STEP * LANES), lambda b, hp: (b, 0, col0 + hp))

    in_specs = [slab(0)] * N_BRANCHES + [slab(0)] * N_BRANCHES + [slab(steps)] * N_BRANCHES
    return pl.pallas_call(
        _attn_kernel,
        grid=(bsz, steps),
        in_specs=in_specs,
        out_specs=slab(0),
        out_shape=jax.ShapeDtypeStruct((bsz, seq, D_MODEL), BF16),
        scratch_shapes=([pltpu.VMEM((seq, LANES), F32)] + [pltpu.VMEM((seq, 2 * LANES), BF16)] * 3
                        + [pltpu.VMEM((seq, LANES), F32), pltpu.VMEM((N_STATS, seq, LANES), F32)] * 2
                        + [pltpu.VMEM((2 * ATT_BLOCK, 2 * ATT_BLOCK), F32), pltpu.VMEM((2 * ATT_BLOCK, ATT_BLOCK), F32)]),
        compiler_params=_cparams(("arbitrary", "arbitrary")),
        name="attention",
    )(*qs, *kvs, *kvs)


def _regroup8(tiles):
    t = list(tiles)
    lane = lax.broadcasted_iota(jnp.int32, t[0].shape, 1)
    piece = lane // SSM_GROUP
    for dist in (4, 2, 1):
        bit = (piece & dist) != 0
        shift = dist * SSM_GROUP
        for a in range(8):
            if a & dist:
                continue
            b = a | dist
            ta, tb = t[a], t[b]
            t[a] = jnp.where(bit, pltpu.roll(tb, shift, 1), ta)
            t[b] = jnp.where(bit, tb, pltpu.roll(ta, LANES - shift, 1))
    return t


def _s5_tile_perm(bsz):
    tt = S5_TILE_TOKENS
    n = bsz * tt
    perm = np.zeros((n, n), np.float32)
    for b in range(bsz):
        for c in range(tt // CHUNK):
            for t in range(CHUNK):
                perm[(t * (tt // CHUNK) + c) * bsz + b, b * tt + c * CHUNK + t] = 1.0
    return perm


def _s5_pre_kernel(*refs, n_ride):
    h_ref, mod_ref, g_ref, perm_ref = refs[:4]
    z_ref = refs[4 + n_ride]
    u_scr, up_scr = refs[5 + 2 * n_ride:]
    _ride_cast(refs[4:4 + n_ride], refs[5 + n_ride:5 + 2 * n_ride])
    d = D_MODEL
    bsz, tt, _ = h_ref.shape
    g = g_ref[...]
    for b in range(bsz):
        u_scr[b * tt:(b + 1) * tt, :] = _normmod(
            h_ref[b], g, mod_ref[b, :, :d], mod_ref[b, :, d:2 * d]).astype(BF16)
    up_scr[...] = jnp.dot(perm_ref[...], u_scr[...], preferred_element_type=F32)
    rows = z_ref.shape[0]
    for j in range(d // LANES):
        for half in range(CHUNK // 8):
            tiles = [up_scr[(8 * half + tl) * rows:(8 * half + tl + 1) * rows, j * LANES:(j + 1) * LANES]
                     for tl in range(8)]
            outs = _regroup8(tiles)
            for gl in range(8):
                col = (8 * j + gl) * GROUP_COLS + half * LANES
                z_ref[:, col:col + LANES] = outs[gl].astype(z_ref.dtype)


def _s5_pre(h, mod, g, perm, ride):
    bsz, seq, d = h.shape
    tt = S5_TILE_TOKENS
    rows = bsz * tt // CHUNK
    ride_in, ride_out, ride_shapes, ride_args = _ride_specs(ride, seq // tt)
    outs = pl.pallas_call(
        functools.partial(_s5_pre_kernel, n_ride=len(ride_in)),
        grid=(seq // tt,),
        in_specs=[
            pl.BlockSpec((bsz, tt, d), lambda i: (0, i, 0)),
            pl.BlockSpec(mod.shape, lambda i: (0, 0, 0)),
            pl.BlockSpec((1, d), lambda i: (0, 0)),
            pl.BlockSpec(perm.shape, lambda i: (0, 0)),
        ] + ride_in,
        out_specs=[pl.BlockSpec((rows, CHUNK * d), lambda i: (i, 0))] + ride_out,
        out_shape=[jax.ShapeDtypeStruct((bsz * seq // CHUNK, CHUNK * d), BF16)] + ride_shapes,
        scratch_shapes=[pltpu.VMEM((bsz * tt, d), BF16), pltpu.VMEM((bsz * tt, d), F32)],
        compiler_params=_cparams(("parallel",)),
        name="s5_pre",
    )(h, mod, g.reshape(1, d), perm, *ride_args)
    return outs[0], outs[1:]


def _s5_core_kernel(z_ref, win_ref, wt_ref, wx_ref, dec_ref, d_ref, o_ref, s_scr, x_scr, *, bsz):
    gc = GROUP_COLS
    nchunk = z_ref.shape[0] // bsz
    s_scr[...] = jnp.dot(z_ref[...], win_ref[...].reshape(2 * gc, gc), preferred_element_type=F32)
    ar, ai = dec_ref[:, :LANES], dec_ref[:, LANES:]

    def step(c, carry):
        xr, xi = carry
        rows = pl.ds(pl.multiple_of(c * bsz, bsz), bsz)
        x_scr[rows, :LANES] = xr
        x_scr[rows, LANES:] = xi
        return (ar * xr - ai * xi + s_scr[rows, :LANES], ar * xi + ai * xr + s_scr[rows, LANES:])

    zero = jnp.zeros((bsz, LANES), F32)
    lax.fori_loop(0, nchunk, step, (zero, zero))
    x = x_scr[...].astype(BF16)
    for gi in range(2):
        z = z_ref[:, gi * gc:(gi + 1) * gc]
        y = jnp.dot(jnp.concatenate([z, x], axis=1), jnp.concatenate([wt_ref[gi], wx_ref[gi]], axis=0),
                    preferred_element_type=F32)
        y = y + d_ref[:, gi * gc:(gi + 1) * gc] * z.astype(F32)
        o_ref[:, gi * gc:(gi + 1) * gc] = jax.nn.gelu(y).astype(o_ref.dtype)


def _s5_core(z, tables, layer, bsz):
    nrow, width = z.shape
    w_in, w_toep, w_x, dec, d_perm = tables
    wspec = pl.BlockSpec((None, 2, GROUP_COLS, GROUP_COLS), lambda k: (layer, k, 0, 0))
    return pl.pallas_call(
        functools.partial(_s5_core_kernel, bsz=bsz),
        grid=(N_PAIRS,),
        in_specs=[
            pl.BlockSpec((nrow, PAIR_COLS), lambda k: (0, k)),
            wspec, wspec, wspec,
            pl.BlockSpec((None, 1, GROUP_COLS), lambda k: (layer, 0, k)),
            pl.BlockSpec((None, 1, PAIR_COLS), lambda k: (layer, 0, k)),
        ],
        out_specs=pl.BlockSpec((nrow, PAIR_COLS), lambda k: (0, k)),
        out_shape=jax.ShapeDtypeStruct((nrow, width), BF16),
        scratch_shapes=[pltpu.VMEM((nrow, GROUP_COLS), F32), pltpu.VMEM((nrow, GROUP_COLS), F32)],
        compiler_params=_cparams(("parallel",)),
        name="s5_core",
    )(z, w_in, w_toep, w_x, dec, d_perm)


def _s5_weights(lam_re, lam_im, log_dt, b_re, b_im, c_re, c_im, d_skip):
    g, p, c16 = SSM_GROUPS, SSM_STATE, SSM_GROUP
    lam = lax.complex(lam_re.astype(F32), lam_im.astype(F32))
    dt = jnp.exp(log_dt.astype(F32))[:, None]
    steps = jnp.arange(CHUNK + 1, dtype=F32)
    apow = jnp.exp((lam * dt)[None] * steps[:, None, None])
    a = apow[1]
    bbar = ((a - 1.0) / lam)[..., None] * lax.complex(b_re.astype(F32), b_im.astype(F32))
    cmat = lax.complex(c_re.astype(F32), c_im.astype(F32))
    odd = (jnp.arange(g) % 2)[:, None, None]

    win = apow[CHUNK - 1::-1][..., None] * bbar[None]

    def lay_in(x):
        return jnp.transpose(x.astype(BF16), (1, 0, 3, 2)).reshape(g, GROUP_COLS, p)

    win_re, win_im = lay_in(win.real), lay_in(win.imag)
    zeros = jnp.zeros_like(win_re)
    w_in = jnp.concatenate([
        jnp.where(odd == 0, win_re, zeros), jnp.where(odd == 1, win_re, zeros),
        jnp.where(odd == 0, win_im, zeros), jnp.where(odd == 1, win_im, zeros)], axis=-1)

    kern = jnp.einsum('gop,kgp,gpi->kgoi', cmat, apow[:CHUNK], bbar, precision=HIGHEST).real
    kcat = jnp.transpose(kern.astype(BF16), (1, 3, 0, 2)).reshape(g, c16, GROUP_COLS)
    w_toep = jnp.stack([jnp.pad(kcat[:, :, :GROUP_COLS - c16 * s], ((0, 0), (0, 0), (c16 * s, 0)))
                        for s in range(CHUNK)], axis=1)
    w_toep = w_toep.reshape(g, GROUP_COLS, GROUP_COLS)

    cw = cmat[None] * apow[1:, :, None, :]

    def lay_x(x):
        return jnp.transpose(x.astype(BF16), (1, 3, 0, 2)).reshape(g, p, GROUP_COLS)

    cw_re, cw_im = lay_x(cw.real), lay_x(-cw.imag)
    zx = jnp.zeros_like(cw_re)
    w_x = jnp.concatenate([
        jnp.where(odd == 0, cw_re, zx), jnp.where(odd == 1, cw_re, zx),
        jnp.where(odd == 0, cw_im, zx), jnp.where(odd == 1, cw_im, zx)], axis=1)

    a16 = apow[CHUNK].reshape(N_PAIRS, 2 * p)
    dec = jnp.concatenate([a16.real, a16.imag], axis=-1).reshape(1, N_PAIRS * GROUP_COLS)
    d_perm = jnp.broadcast_to(d_skip.astype(F32).reshape(g, 1, c16), (g, CHUNK, c16)).reshape(1, g * GROUP_COLS)
    return w_in, w_toep, w_x, dec, d_perm


def _mlp(u, w1_ref, w2_ref):
    d = D_MODEL
    acc = None
    for k in range(D_FF // d):
        a = jnp.dot(u, w1_ref[:, k * d:(k + 1) * d], preferred_element_type=F32)
        a = jnp.square(jnp.maximum(a, 0.0)).astype(BF16)
        part = jnp.dot(a, w2_ref[k * d:(k + 1) * d, :], preferred_element_type=F32)
        acc = part if acc is None else acc + part
    return acc


def _final_norm(h, fg_ref):
    ms = jnp.mean(h * h, axis=-1, keepdims=True)
    return h * lax.rsqrt(ms + EPS) * fg_ref[...]


def _ride_specs(ride, nsteps):
    ins, outs, shapes = [], [], []
    for st, layer in ride:
        _, r, c = st.shape
        ins.append(pl.BlockSpec((None, r // nsteps, c), lambda i, layer=layer: (layer, i, 0)))
        outs.append(pl.BlockSpec((r // nsteps, c), lambda i: (i, 0)))
        shapes.append(jax.ShapeDtypeStruct((r, c), BF16))
    return ins, outs, shapes, [st for st, _ in ride]


def _ride_cast(ride_in, ride_out):
    for src, dst in zip(ride_in, ride_out):
        dst[...] = src[...].astype(dst.dtype)


def _post_attn_kernel(*refs, final, n_ride):
    h_ref, y_ref, moda_ref, modm_ref, g_ref, wp_ref, w1_ref, w2_ref = refs[:8]
    pos = 8 + int(final)
    fg_ref = refs[8] if final else None
    o_ref = refs[pos + n_ride]
    _ride_cast(refs[pos:pos + n_ride], refs[pos + n_ride + 1:])
    d = D_MODEL
    ymix = jnp.dot(y_ref[...], wp_ref[...], preferred_element_type=F32)
    h1 = h_ref[...] + moda_ref[:, 2 * d:3 * d] * ymix
    u = _normmod(h1, g_ref[...], modm_ref[:, :d], modm_ref[:, d:2 * d]).astype(BF16)
    h2 = h1 + modm_ref[:, 2 * d:3 * d] * _mlp(u, w1_ref, w2_ref)
    o_ref[...] = _final_norm(h2, fg_ref) if final else h2


def _post_attn(h, o, moda, modm, g2, w_o, w1, w2, final_g, ride):
    bsz, seq, d = h.shape
    tm = MLP_ROWS
    per_b = seq // tm
    row_spec = pl.BlockSpec((None, tm, d), lambda i: (i // per_b, i % per_b, 0))
    final = final_g is not None
    ride_in, ride_out, ride_shapes, ride_args = _ride_specs(ride, bsz * per_b)
    in_specs = [
        row_spec, row_spec,
        pl.BlockSpec((None, 1, moda.shape[-1]), lambda i: (i // per_b, 0, 0)),
        pl.BlockSpec((None, 1, modm.shape[-1]), lambda i: (i // per_b, 0, 0)),
        pl.BlockSpec((1, d), lambda i: (0, 0)),
        _layer_spec(*w_o), _layer_spec(*w1), _layer_spec(*w2),
    ]
    args = [h, o, moda, modm, g2.reshape(1, d), w_o[0], w1[0], w2[0]]
    if final:
        in_specs.append(pl.BlockSpec((1, d), lambda i: (0, 0)))
        args.append(final_g.reshape(1, d))
    args += ride_args
    outs = pl.pallas_call(
        functools.partial(_post_attn_kernel, final=final, n_ride=len(ride_in)),
        grid=(bsz * per_b,),
        in_specs=in_specs + ride_in,
        out_specs=[row_spec] + ride_out,
        out_shape=[jax.ShapeDtypeStruct((bsz, seq, d), F32)] + ride_shapes,
        compiler_params=_cparams(("parallel",)),
        name="post_attn_mlp",
    )(*args)
    return outs[0], outs[1:]


def _post_s5_kernel(*refs, n_ride):
    h_ref, z_ref, moda_ref, modm_ref, g_ref, permt_ref, wp_ref, w1_ref, w2_ref = refs[:9]
    o_ref = refs[9 + n_ride]
    z_scr, h1_scr, u_scr = refs[10 + 2 * n_ride:]
    _ride_cast(refs[9:9 + n_ride], refs[10 + n_ride:10 + 2 * n_ride])
    d = D_MODEL
    bsz, tt, _ = h_ref.shape
    rows = z_ref.shape[0]
    for j in range(d // LANES):
        for half in range(CHUNK // 8):
            tiles = []
            for gl in range(8):
                col = (8 * j + gl) * GROUP_COLS + half * LANES
                tiles.append(z_ref[:, col:col + LANES].astype(F32))
            outs = _regroup8(tiles)
            for tl in range(8):
                t = 8 * half + tl
                z_scr[t * rows:(t + 1) * rows, j * LANES:(j + 1) * LANES] = outs[tl].astype(BF16)
    zn = jnp.dot(permt_ref[...], z_scr[...], preferred_element_type=F32).astype(BF16)
    zz = jnp.dot(zn, wp_ref[...], preferred_element_type=F32)
    ymix = zz[:, :d] * jax.nn.sigmoid(zz[:, d:])
    g = g_ref[...]
    for b in range(bsz):
        sl = slice(b * tt, (b + 1) * tt)
        h1 = h_ref[b] + moda_ref[b, :, 2 * d:3 * d] * ymix[sl, :]
        h1_scr[sl, :] = h1
        u_scr[sl, :] = _normmod(h1, g, modm_ref[b, :, :d], modm_ref[b, :, d:2 * d]).astype(BF16)
    acc = _mlp(u_scr[...], w1_ref, w2_ref)
    for b in range(bsz):
        sl = slice(b * tt, (b + 1) * tt)
        o_ref[b] = h1_scr[sl, :] + modm_ref[b, :, 2 * d:3 * d] * acc[sl, :]


def _post_s5(h, zo, moda, modm, g2, permt, w_glu, w1, w2, ride):
    bsz, seq, d = h.shape
    tt = S5_TILE_TOKENS
    rows = bsz * tt // CHUNK
    h_spec = pl.BlockSpec((bsz, tt, d), lambda i: (0, i, 0))
    ride_in, ride_out, ride_shapes, ride_args = _ride_specs(ride, seq // tt)
    outs = pl.pallas_call(
        functools.partial(_post_s5_kernel, n_ride=len(ride_in)),
        grid=(seq // tt,),
        in_specs=[
            h_spec,
            pl.BlockSpec((rows, CHUNK * d), lambda i: (i, 0)),
            pl.BlockSpec(moda.shape, lambda i: (0, 0, 0)),
            pl.BlockSpec(modm.shape, lambda i: (0, 0, 0)),
            pl.BlockSpec((1, d), lambda i: (0, 0)),
            pl.BlockSpec(permt.shape, lambda i: (0, 0)),
            _layer_spec(*w_glu), _layer_spec(*w1), _layer_spec(*w2),
        ] + ride_in,
        out_specs=[h_spec] + ride_out,
        out_shape=[jax.ShapeDtypeStruct((bsz, seq, d), F32)] + ride_shapes,
        scratch_shapes=[pltpu.VMEM((bsz * tt, d), BF16), pltpu.VMEM((bsz * tt, d), F32),
                        pltpu.VMEM((bsz * tt, d), BF16)],
        compiler_params=_cparams(("parallel",)),
        name="post_s5_mlp",
    )(h, zo, moda, modm, g2.reshape(1, d), permt, w_glu[0], w1[0], w2[0], *ride_args)
    return outs[0], outs[1:]


def kernel(x, c, ln_g, ada_w, ada_b, ssm_lam_re, ssm_lam_im, ssm_log_dt, ssm_b_re, ssm_b_im, ssm_c_re, ssm_c_im, ssm_d, ssm_w_glu, kv_g, kv_ada_w, kv_ada_b, w_kv, attn_w_q, attn_w_o, mlp_w1, mlp_w2, final_g):
    bsz, seq, d = x.shape
    depth = ln_g.shape[0]
    n_s5 = ssm_lam_re.shape[0]
    assert d == D_MODEL and seq % (DILATIONS[-1] * ATT_BLOCK) == 0 and 0 < n_s5 < depth

    mods = _ada_mods(c, ada_w.reshape(depth * 2, d, 3 * d), ada_b.reshape(depth * 2, 3 * d))
    mods = mods.reshape(depth, 2, bsz, 1, 3 * d)
    kv_mod = _ada_mods(c, kv_ada_w[None], kv_ada_b[None]).reshape(bsz, 1, 2 * d)

    perm_np = _s5_tile_perm(bsz)
    perm = jnp.asarray(perm_np, BF16)
    permt = jnp.asarray(perm_np.T, BF16)
    tables = jax.vmap(_s5_weights)(ssm_lam_re, ssm_lam_im, ssm_log_dt, ssm_b_re, ssm_b_im,
                                   ssm_c_re, ssm_c_im, ssm_d)

    def as_layer(cast):
        return (cast[None], 0)

    h = x
    for layer in range(n_s5):
        ride = [(mlp_w1, 0), (mlp_w2, 0), (ssm_w_glu, 0)] if layer == 0 else []
        z, cast = _s5_pre(h, mods[layer, 0], ln_g[layer, 0], perm, ride)
        if layer == 0:
            w1, w2, w_glu = map(as_layer, cast)
        zo = _s5_core(z, tables, layer, bsz)
        ride = [(mlp_w1, layer + 1), (mlp_w2, layer + 1)]
        if layer + 1 < n_s5:
            ride += [(ssm_w_glu, layer + 1)]
        else:
            ride += [(w_kv[None], 0), (attn_w_q, 0), (attn_w_o, 0)]
        h, cast = _post_s5(h, zo, mods[layer, 0], mods[layer, 1], ln_g[layer, 1], permt, w_glu, w1, w2, ride)
        w1, w2 = as_layer(cast[0]), as_layer(cast[1])
        if layer + 1 < n_s5:
            w_glu = as_layer(cast[2])
        else:
            w_kvb, w_q, w_o = map(as_layer, cast[2:])

    kv_cols = tuple((i * d, (N_BRANCHES + i) * d) for i in range(N_BRANCHES))
    q_cols = tuple((i * d,) for i in range(N_BRANCHES))
    kvs = _norm_proj(h, kv_mod, kv_g, w_kvb, kv_cols, 1.0, KV_PROJ_TILE)
    for layer in range(n_s5, depth):
        j = layer - n_s5
        last = layer == depth - 1
        qs = _norm_proj(h, mods[layer, 0], ln_g[layer, 0], w_q, q_cols, Q_SCALE, Q_PROJ_TILE)
        o = _attention(qs, kvs)
        ride = [] if last else [(mlp_w1, layer + 1), (mlp_w2, layer + 1), (attn_w_q, j + 1), (attn_w_o, j + 1)]
        h, cast = _post_attn(h, o, mods[layer, 0], mods[layer, 1], ln_g[layer, 1], w_o, w1, w2,
                             final_g if last else None, ride)
        if not last:
            w1, w2, w_q, w_o = map(as_layer, cast)
    return h
```

```python
import functools
import math

import numpy as np

import jax
import jax.numpy as jnp
from jax import lax
from jax.experimental import pallas as pl
from jax.experimental.pallas import tpu as pltpu

F32 = jnp.float32
BF16 = jnp.bfloat16

D_MODEL = 1024
SSM_GROUP = 16
SSM_GROUPS = D_MODEL // SSM_GROUP
SSM_STATE = 64
HEAD_DIM = 64
DILATIONS = (1, 4, 16)
N_BRANCHES = len(DILATIONS)
ATT_BLOCK = 128
ATT_GROUP = 16
ATT_PAIRS_PER_STEP = 2
N_STATS = 3
Q_SCALE = HEAD_DIM ** -0.5 * math.log2(math.e)
D_FF = 4 * D_MODEL
EPS = 1e-6
NEG = -1e30

LANES = 128
CHUNK = 16
GROUP_COLS = CHUNK * SSM_GROUP
PAIR_COLS = 2 * GROUP_COLS
N_PAIRS = SSM_GROUPS // 2
S5_TILE_TOKENS = 2 * CHUNK
MLP_ROWS = 1024
KV_PROJ_TILE = 512
Q_PROJ_TILE = 1024
VMEM_LIMIT = 56 * 1024 * 1024

HIGHEST = lax.Precision.HIGHEST
NT_DIMS = (((1,), (1,)), ((), ()))


def _cparams(sem):
    return pltpu.CompilerParams(dimension_semantics=sem, vmem_limit_bytes=VMEM_LIMIT)


def _layer_spec(stack, layer):
    zeros = (0,) * (stack.ndim - 1)
    return pl.BlockSpec((None,) + stack.shape[1:], lambda i: (layer,) + zeros, pipeline_mode=pl.Buffered(1))


def _normmod(x, g, shift, scale):
    ms = jnp.mean(x * x, axis=-1, keepdims=True)
    return (x * lax.rsqrt(ms + EPS) * g) * (1.0 + scale) + shift


def _ada_kernel(c_ref, w_ref, b_ref, o_ref):
    c = c_ref[...]
    sc = (c * jax.nn.sigmoid(c)).astype(BF16)
    o_ref[...] = jnp.dot(sc, w_ref[...].astype(BF16), preferred_element_type=F32) + b_ref[...]


def _ada_mods(c, w, b):
    n, d, width = w.shape
    bsz = c.shape[0]
    return pl.pallas_call(
        _ada_kernel,
        grid=(n,),
        in_specs=[
            pl.BlockSpec((bsz, d), lambda i: (0, 0)),
            pl.BlockSpec((None, d, width), lambda i: (i, 0, 0)),
            pl.BlockSpec((None, 1, width), lambda i: (i, 0, 0)),
        ],
        out_specs=pl.BlockSpec((None, bsz, width), lambda i: (i, 0, 0)),
        out_shape=jax.ShapeDtypeStruct((n, bsz, width), F32),
        compiler_params=_cparams(("parallel",)),
        name="ada_mods",
    )(c, w, b.reshape(n, 1, width))


def _proj_kernel(h_ref, mod_ref, g_ref, w_ref, o1_ref, o2_ref, o3_ref, us, us4, up4, up16, *, wcols, out_scale):
    d = D_MODEL
    tm = h_ref.shape[0]
    d1 = DILATIONS[1]
    q4, q16 = tm // d1, tm // DILATIONS[2]
    cw = o1_ref.shape[1]
    u = _normmod(h_ref[...], g_ref[...], mod_ref[:, :d], mod_ref[:, d:2 * d])
    for l in range(d // LANES):
        us[l] = u[:, l * LANES:(l + 1) * LANES]
    ub = u.astype(BF16)
    for n in range(cw // d):
        cols = slice(n * d, (n + 1) * d)
        res = jnp.dot(ub, w_ref[:, wcols[0][n]:wcols[0][n] + d], preferred_element_type=F32)
        o1_ref[:, cols] = (res * out_scale).astype(o1_ref.dtype)
    for l in range(d // LANES):
        for a in range(d1):
            part = us[l, pl.ds(a, q4, stride=d1), :]
            us4[l, a * q4:(a + 1) * q4, :] = part
            up4[a * q4:(a + 1) * q4, l * LANES:(l + 1) * LANES] = part.astype(BF16)
    for n in range(cw // d):
        res = jnp.dot(up4[...], w_ref[:, wcols[1][n]:wcols[1][n] + d], preferred_element_type=F32)
        for a in range(d1):
            o2_ref[a, :, n * d:(n + 1) * d] = (res[a * q4:(a + 1) * q4, :] * out_scale).astype(o2_ref.dtype)
    for l in range(d // LANES):
        for a1 in range(d1):
            for a2 in range(d1):
                r = d1 * a2 + a1
                up16[r * q16:(r + 1) * q16, l * LANES:(l + 1) * LANES] = us4[
                    l, pl.ds(a1 * q4 + a2, q16, stride=d1), :].astype(BF16)
    for n in range(cw // d):
        res = jnp.dot(up16[...], w_ref[:, wcols[2][n]:wcols[2][n] + d], preferred_element_type=F32)
        for r in range(DILATIONS[2]):
            o3_ref[r, :, n * d:(n + 1) * d] = (res[r * q16:(r + 1) * q16, :] * out_scale).astype(o3_ref.dtype)


def _norm_proj(h, mod, g, w, wcols, out_scale, tm):
    bsz, seq, d = h.shape
    cw = len(wcols[0]) * d
    per_b = seq // tm
    d1, d2 = DILATIONS[1], DILATIONS[2]
    o1, o2, o3 = pl.pallas_call(
        functools.partial(_proj_kernel, wcols=wcols, out_scale=out_scale),
        grid=(bsz * per_b,),
        in_specs=[
            pl.BlockSpec((None, tm, d), lambda i: (i // per_b, i % per_b, 0)),
            pl.BlockSpec((None, 1, mod.shape[-1]), lambda i: (i // per_b, 0, 0)),
            pl.BlockSpec((1, d), lambda i: (0, 0)),
            _layer_spec(*w),
        ],
        out_specs=[
            pl.BlockSpec((None, tm, cw), lambda i: (i // per_b, i % per_b, 0)),
            pl.BlockSpec((None, d1, None, tm // d1, cw), lambda i: (i // per_b, 0, i % per_b, 0, 0)),
            pl.BlockSpec((None, d2, None, tm // d2, cw), lambda i: (i // per_b, 0, i % per_b, 0, 0)),
        ],
        out_shape=[
            jax.ShapeDtypeStruct((bsz, seq, cw), BF16),
            jax.ShapeDtypeStruct((bsz, d1, per_b, tm // d1, cw), BF16),
            jax.ShapeDtypeStruct((bsz, d2, per_b, tm // d2, cw), BF16),
        ],
        scratch_shapes=[pltpu.VMEM((d // LANES, tm, LANES), F32), pltpu.VMEM((d // LANES, tm, LANES), F32),
                        pltpu.VMEM((tm, d), BF16), pltpu.VMEM((tm, d), BF16)],
        compiler_params=_cparams(("parallel",)),
        name="norm_proj",
    )(h, mod, g.reshape(1, d), w[0])
    return o1, o2.reshape(bsz, seq, cw), o3.reshape(bsz, seq, cw)


def _attn_group(blocks):
    blk = ATT_BLOCK
    lane = lax.broadcasted_iota(jnp.int32, (blk, LANES), 1)
    lo = lane < HEAD_DIM
    scores = []
    for qb, kcat, _, mask, _ in blocks:
        zero = jnp.zeros_like(qb)
        q2 = jnp.concatenate([jnp.where(lo, qb, zero), jnp.where(lo, zero, qb)], axis=0)
        s = lax.dot_general(q2, kcat, NT_DIMS, preferred_element_type=F32)
        scores.append(s + mask[...])
    soft = []
    for s, (_, _, _, _, old) in zip(scores, blocks):
        ms, ps = [], []
        for hh in range(2):
            tiles = [s[hh * blk:(hh + 1) * blk, t * LANES:(t + 1) * LANES] for t in range(s.shape[1] // LANES)]
            mx = tiles[0]
            for t in tiles[1:]:
                mx = jnp.maximum(mx, t)
            m = jnp.broadcast_to(jnp.max(mx, axis=-1, keepdims=True), (blk, LANES))
            if old is not None:
                m = jnp.maximum(m, old[0][hh])
            ms.append(m)
            ps.append(jnp.concatenate([jnp.exp2(t - m).astype(BF16) for t in tiles], axis=1))
        alpha = None
        if old is not None:
            alpha = jnp.exp2(jnp.where(lo, old[0][0], old[0][1]) - jnp.where(lo, ms[0], ms[1]))
        soft.append((ms, jnp.concatenate(ps, axis=0), alpha))
    outs = []
    for (ms, p, alpha), (_, _, vcat, _, old) in zip(soft, blocks):
        pv = jnp.dot(p, vcat, preferred_element_type=F32)
        acc = jnp.where(lo, pv[:blk, :LANES], pv[blk:, :LANES])
        den = jnp.where(lo, pv[:blk, LANES:], pv[blk:, LANES:])
        if old is not None:
            acc = alpha * old[1] + acc
            den = alpha * old[0][2] + den
        outs.append((ms + [den], acc))
    return outs


def _attn_kernel(*refs):
    n_in = 3 * N_BRANCHES
    ins, o_ref, scratch = refs[:n_in], refs[n_in], refs[n_in + 1:]
    for half in range(ATT_PAIRS_PER_STEP):
        lanes = pl.ds(half * LANES, LANES)
        _attn_pair(*[r.at[:, lanes] for r in ins], o_ref.at[:, lanes], *scratch)


def _attn_pair(q1, qd2, qd3, k1, kd2, kd3, v1, v2, v3, o_ref, tmp_nat, vd1, vd2, vd3,
               acc_ref, st_ref, acc2_ref, st2_ref, mask_pc, mask_c):
    blk = ATT_BLOCK
    seq = q1.shape[0]
    nblk = seq // blk
    d1, d2 = DILATIONS[1], DILATIONS[2]
    quarter = seq // d1
    per_res = quarter // blk
    assert d2 == d1 * d1 and seq == d2 * blk and nblk % ATT_GROUP == 0 and ATT_GROUP % per_res == 0

    @pl.when(jnp.logical_and(pl.program_id(0) == 0, pl.program_id(1) == 0))
    def _():
        for vd in (vd1, vd2, vd3):
            vd[:, LANES:] = jnp.ones((seq, LANES), BF16)
        qq = lax.broadcasted_iota(jnp.int32, (2 * blk, 2 * blk), 0) & (blk - 1)
        kk = lax.broadcasted_iota(jnp.int32, (2 * blk, 2 * blk), 1)
        valid = jnp.logical_or(jnp.logical_and(kk < blk, kk >= qq), jnp.logical_and(kk >= blk, kk - blk <= qq))
        mask_pc[...] = jnp.where(valid, 0.0, NEG)
        valid_c = (lax.broadcasted_iota(jnp.int32, (2 * blk, blk), 1)
                   <= (lax.broadcasted_iota(jnp.int32, (2 * blk, blk), 0) & (blk - 1)))
        mask_c[...] = jnp.where(valid_c, 0.0, NEG)

    for v, vd in ((v1, vd1), (v2, vd2), (v3, vd3)):
        vd[:, :LANES] = v[...]

    def run_group(blocks, dests):
        for (st_dst, acc_dst, rows), (st, acc) in zip(dests, _attn_group(blocks)):
            for k in range(N_STATS):
                st_dst[k, rows, :] = st[k]
            acc_dst[rows, :] = acc

    def load_old(st_src, acc_src, rows):
        return [st_src[k, rows, :] for k in range(N_STATS)], acc_src[rows, :]

    def b0_group(ns, first):
        blocks, dests = [], []
        for n in ns:
            if first and n == 0:
                rows = pl.ds(0, blk)
                blocks.append((q1[rows, :], k1[rows, :], vd1[rows, :], mask_c, None))
            else:
                rows = pl.ds(pl.multiple_of(n * blk, blk), blk)
                krows = pl.ds(pl.multiple_of((n - 1) * blk, blk), 2 * blk)
                blocks.append((q1[rows, :], k1[krows, :], vd1[krows, :], mask_pc, None))
            dests.append((st_ref, acc_ref, rows))
        run_group(blocks, dests)

    b0_group(list(range(ATT_GROUP)), True)

    def b0_body(i, carry):
        b0_group([ATT_GROUP * i + j for j in range(ATT_GROUP)], False)
        return carry

    lax.fori_loop(1, nblk // ATT_GROUP, b0_body, 0)

    res_per_trip = ATT_GROUP // per_res

    def b1_body(i, carry):
        blocks, dests = [], []
        for jr in range(res_per_trip):
            a1 = res_per_trip * i + jr
            base = a1 * quarter
            for n in range(per_res):
                rows = pl.ds(a1 + d1 * blk * n, blk, stride=d1)
                qrows = pl.ds(pl.multiple_of(base + n * blk, blk), blk)
                if n == 0:
                    krows, mask = qrows, mask_c
                else:
                    krows, mask = pl.ds(pl.multiple_of(base + (n - 1) * blk, blk), 2 * blk), mask_pc
                blocks.append((qd2[qrows, :], kd2[krows, :], vd2[krows, :], mask,
                               load_old(st_ref, acc_ref, rows)))
                dests.append((st2_ref, acc2_ref, qrows))
        run_group(blocks, dests)
        return carry

    lax.fori_loop(0, d1 // res_per_trip, b1_body, 0)

    def b2_body(i, carry):
        blocks, dests = [], []
        for j in range(ATT_GROUP):
            a1, a2 = j % d1, (ATT_GROUP // d1) * i + j // d1
            qrows = pl.ds(pl.multiple_of((ATT_GROUP * i + j) * blk, blk), blk)
            rows = pl.ds(a1 * quarter + a2, blk, stride=d1)
            blocks.append((qd3[qrows, :], kd3[qrows, :], vd3[qrows, :], mask_c,
                           load_old(st2_ref, acc2_ref, rows)))
            dests.append((st2_ref, acc2_ref, rows))
        run_group(blocks, dests)
        return carry

    lax.fori_loop(0, d2 // ATT_GROUP, b2_body, 0)

    for a in range(d1):
        seg = slice(a * quarter, (a + 1) * quarter)
        tmp_nat[pl.ds(a, quarter, stride=d1), :] = acc2_ref[seg, :] / st2_ref[N_STATS - 1, seg, :]
    o_ref[...] = tmp_nat[...].astype(o_ref.dtype)


def _attention(qs, kvs):
    bsz, seq, _ = qs[0].shape
    steps = D_MODEL // (ATT_PAIRS_PER_STEP * LANES)

    def slab(col0):
        return pl.BlockSpec((None, seq, ATT_PAIRS_PER_STEP * LANES), lambda b, hp: (b, 0, col0 + hp))

    in_specs = [slab(0)] * N_BRANCHES + [slab(0)] * N_BRANCHES + [slab(steps)] * N_BRANCHES
    return pl.pallas_call(
        _attn_kernel,
        grid=(bsz, steps),
        in_specs=in_specs,
        out_specs=slab(0),
        out_shape=jax.ShapeDtypeStruct((bsz, seq, D_MODEL), BF16),
        scratch_shapes=([pltpu.VMEM((seq, LANES), F32)] + [pltpu.VMEM((seq, 2 * LANES), BF16)] * 3
                        + [pltpu.VMEM((seq, LANES), F32), pltpu.VMEM((N_STATS, seq, LANES), F32)] * 2
                        + [pltpu.VMEM((2 * ATT_BLOCK, 2 * ATT_BLOCK), F32), pltpu.VMEM((2 * ATT_BLOCK, ATT_BLOCK), F32)]),
        compiler_params=_cparams(("arbitrary", "arbitrary")),
        name="attention",
    )(*qs, *kvs, *kvs)


def _regroup8(tiles):
    t = list(tiles)
    lane = lax.broadcasted_iota(jnp.int32, t[0].shape, 1)
    piece = lane // SSM_GROUP
    for dist in (4, 2, 1):
        bit = (piece & dist) != 0
        shift = dist * SSM_GROUP
        for a in range(8):
            if a & dist:
                continue
            b = a | dist
            ta, tb = t[a], t[b]
            t[a] = jnp.where(bit, pltpu.roll(tb, shift, 1), ta)
            t[b] = jnp.where(bit, tb, pltpu.roll(ta, LANES - shift, 1))
    return t


def _s5_tile_perm(bsz):
    tt = S5_TILE_TOKENS
    n = bsz * tt
    perm = np.zeros((n, n), np.float32)
    for b in range(bsz):
        for c in range(tt // CHUNK):
            for t in range(CHUNK):
                perm[(t * (tt // CHUNK) + c) * bsz + b, b * tt + c * CHUNK + t] = 1.0
    return perm


def _s5_pre_kernel(*refs, n_ride):
    h_ref, mod_ref, g_ref, perm_ref = refs[:4]
    z_ref = refs[4 + n_ride]
    u_scr, up_scr = refs[5 + 2 * n_ride:]
    _ride_cast(refs[4:4 + n_ride], refs[5 + n_ride:5 + 2 * n_ride])
    d = D_MODEL
    bsz, tt, _ = h_ref.shape
    g = g_ref[...]
    for b in range(bsz):
        u_scr[b * tt:(b + 1) * tt, :] = _normmod(
            h_ref[b], g, mod_ref[b, :, :d], mod_ref[b, :, d:2 * d]).astype(BF16)
    up_scr[...] = jnp.dot(perm_ref[...], u_scr[...], preferred_element_type=F32)
    rows = z_ref.shape[0]
    for j in range(d // LANES):
        for half in range(CHUNK // 8):
            tiles = [up_scr[(8 * half + tl) * rows:(8 * half + tl + 1) * rows, j * LANES:(j + 1) * LANES]
                     for tl in range(8)]
            outs = _regroup8(tiles)
            for gl in range(8):
                col = (8 * j + gl) * GROUP_COLS + half * LANES
                z_ref[:, col:col + LANES] = outs[gl].astype(z_ref.dtype)


def _s5_pre(h, mod, g, perm, ride):
    bsz, seq, d = h.shape
    tt = S5_TILE_TOKENS
    rows = bsz * tt // CHUNK
    ride_in, ride_out, ride_shapes, ride_args = _ride_specs(ride, seq // tt)
    outs = pl.pallas_call(
        functools.partial(_s5_pre_kernel, n_ride=len(ride_in)),
        grid=(seq // tt,),
        in_specs=[
            pl.BlockSpec((bsz, tt, d), lambda i: (0, i, 0)),
            pl.BlockSpec(mod.shape, lambda i: (0, 0, 0)),
            pl.BlockSpec((1, d), lambda i: (0, 0)),
            pl.BlockSpec(perm.shape, lambda i: (0, 0)),
        ] + ride_in,
        out_specs=[pl.BlockSpec((rows, CHUNK * d), lambda i: (i, 0))] + ride_out,
        out_shape=[jax.ShapeDtypeStruct((bsz * seq // CHUNK, CHUNK * d), BF16)] + ride_shapes,
        scratch_shapes=[pltpu.VMEM((bsz * tt, d), BF16), pltpu.VMEM((bsz * tt, d), F32)],
        compiler_params=_cparams(("parallel",)),
        name="s5_pre",
    )(h, mod, g.reshape(1, d), perm, *ride_args)
    return outs[0], outs[1:]


def _s5_core_kernel(z_ref, win_ref, wt_ref, wx_ref, dec_ref, d_ref, o_ref, s_scr, x_scr, *, bsz):
    gc = GROUP_COLS
    nchunk = z_ref.shape[0] // bsz
    s_scr[...] = jnp.dot(z_ref[...], win_ref[...].reshape(2 * gc, gc), preferred_element_type=F32)
    ar, ai = dec_ref[:, :LANES], dec_ref[:, LANES:]

    def step(c, carry):
        xr, xi = carry
        rows = pl.ds(pl.multiple_of(c * bsz, bsz), bsz)
        x_scr[rows, :LANES] = xr
        x_scr[rows, LANES:] = xi
        return (ar * xr - ai * xi + s_scr[rows, :LANES], ar * xi + ai * xr + s_scr[rows, LANES:])

    zero = jnp.zeros((bsz, LANES), F32)
    lax.fori_loop(0, nchunk, step, (zero, zero))
    x = x_scr[...].astype(BF16)
    for gi in range(2):
        z = z_ref[:, gi * gc:(gi + 1) * gc]
        y = jnp.dot(jnp.concatenate([z, x], axis=1), jnp.concatenate([wt_ref[gi], wx_ref[gi]], axis=0),
                    preferred_element_type=F32)
        y = y + d_ref[:, gi * gc:(gi + 1) * gc] * z.astype(F32)
        o_ref[:, gi * gc:(gi + 1) * gc] = jax.nn.gelu(y).astype(o_ref.dtype)


def _s5_core(z, tables, layer, bsz):
    nrow, width = z.shape
    w_in, w_toep, w_x, dec, d_perm = tables
    wspec = pl.BlockSpec((None, 2, GROUP_COLS, GROUP_COLS), lambda k: (layer, k, 0, 0))
    return pl.pallas_call(
        functools.partial(_s5_core_kernel, bsz=bsz),
        grid=(N_PAIRS,),
        in_specs=[
            pl.BlockSpec((nrow, PAIR_COLS), lambda k: (0, k)),
            wspec, wspec, wspec,
            pl.BlockSpec((None, 1, GROUP_COLS), lambda k: (layer, 0, k)),
            pl.BlockSpec((None, 1, PAIR_COLS), lambda k: (layer, 0, k)),
        ],
        out_specs=pl.BlockSpec((nrow, PAIR_COLS), lambda k: (0, k)),
        out_shape=jax.ShapeDtypeStruct((nrow, width), BF16),
        scratch_shapes=[pltpu.VMEM((nrow, GROUP_COLS), F32), pltpu.VMEM((nrow, GROUP_COLS), F32)],
        compiler_params=_cparams(("parallel",)),
        name="s5_core",
    )(z, w_in, w_toep, w_x, dec, d_perm)


def _s5_weights(lam_re, lam_im, log_dt, b_re, b_im, c_re, c_im, d_skip):
    g, p, c16 = SSM_GROUPS, SSM_STATE, SSM_GROUP
    lam = lax.complex(lam_re.astype(F32), lam_im.astype(F32))
    dt = jnp.exp(log_dt.astype(F32))[:, None]
    steps = jnp.arange(CHUNK + 1, dtype=F32)
    apow = jnp.exp((lam * dt)[None] * steps[:, None, None])
    a = apow[1]
    bbar = ((a - 1.0) / lam)[..., None] * lax.complex(b_re.astype(F32), b_im.astype(F32))
    cmat = lax.complex(c_re.astype(F32), c_im.astype(F32))
    odd = (jnp.arange(g) % 2)[:, None, None]

    win = apow[CHUNK - 1::-1][..., None] * bbar[None]

    def lay_in(x):
        return jnp.transpose(x.astype(BF16), (1, 0, 3, 2)).reshape(g, GROUP_COLS, p)

    win_re, win_im = lay_in(win.real), lay_in(win.imag)
    zeros = jnp.zeros_like(win_re)
    w_in = jnp.concatenate([
        jnp.where(odd == 0, win_re, zeros), jnp.where(odd == 1, win_re, zeros),
        jnp.where(odd == 0, win_im, zeros), jnp.where(odd == 1, win_im, zeros)], axis=-1)

    kern = jnp.einsum('gop,kgp,gpi->kgoi', cmat, apow[:CHUNK], bbar, precision=HIGHEST).real
    idx = jnp.arange(CHUNK)
    onehot = (idx[None, None, :] - idx[None, :, None] == idx[:, None, None]).astype(BF16)
    w_toep = jnp.einsum('kst,kgoi->gsito', onehot, kern.astype(BF16),
                        preferred_element_type=BF16)
    w_toep = w_toep.reshape(g, GROUP_COLS, GROUP_COLS)

    cw = cmat[None] * apow[1:, :, None, :]

    def lay_x(x):
        return jnp.transpose(x.astype(BF16), (1, 3, 0, 2)).reshape(g, p, GROUP_COLS)

    cw_re, cw_im = lay_x(cw.real), lay_x(-cw.imag)
    zx = jnp.zeros_like(cw_re)
    w_x = jnp.concatenate([
        jnp.where(odd == 0, cw_re, zx), jnp.where(odd == 1, cw_re, zx),
        jnp.where(odd == 0, cw_im, zx), jnp.where(odd == 1, cw_im, zx)], axis=1)

    a16 = apow[CHUNK].reshape(N_PAIRS, 2 * p)
    dec = jnp.concatenate([a16.real, a16.imag], axis=-1).reshape(1, N_PAIRS * GROUP_COLS)
    d_perm = jnp.broadcast_to(d_skip.astype(F32).reshape(g, 1, c16), (g, CHUNK, c16)).reshape(1, g * GROUP_COLS)
    return w_in, w_toep, w_x, dec, d_perm


def _mlp(u, w1_ref, w2_ref):
    d = D_MODEL
    acc = None
    for k in range(D_FF // d):
        a = jnp.dot(u, w1_ref[:, k * d:(k + 1) * d], preferred_element_type=F32)
        a = jnp.square(jnp.maximum(a, 0.0)).astype(BF16)
        part = jnp.dot(a, w2_ref[k * d:(k + 1) * d, :], preferred_element_type=F32)
        acc = part if acc is None else acc + part
    return acc


def _final_norm(h, fg_ref):
    ms = jnp.mean(h * h, axis=-1, keepdims=True)
    return h * lax.rsqrt(ms + EPS) * fg_ref[...]


def _ride_specs(ride, nsteps):
    ins, outs, shapes = [], [], []
    for st, layer in ride:
        _, r, c = st.shape
        assert r % nsteps == 0
        ins.append(pl.BlockSpec((None, r // nsteps, c), lambda i, layer=layer: (layer, i, 0)))
        outs.append(pl.BlockSpec((r // nsteps, c), lambda i: (i, 0)))
        shapes.append(jax.ShapeDtypeStruct((r, c), BF16))
    return ins, outs, shapes, [st for st, _ in ride]


def _ride_cast(ride_in, ride_out):
    for src, dst in zip(ride_in, ride_out):
        dst[...] = src[...].astype(dst.dtype)


def _post_attn_kernel(*refs, final, n_ride):
    h_ref, y_ref, moda_ref, modm_ref, g_ref, wp_ref, w1_ref, w2_ref = refs[:8]
    pos = 8 + int(final)
    fg_ref = refs[8] if final else None
    o_ref = refs[pos + n_ride]
    _ride_cast(refs[pos:pos + n_ride], refs[pos + n_ride + 1:])
    d = D_MODEL
    ymix = jnp.dot(y_ref[...], wp_ref[...], preferred_element_type=F32)
    h1 = h_ref[...] + moda_ref[:, 2 * d:3 * d] * ymix
    u = _normmod(h1, g_ref[...], modm_ref[:, :d], modm_ref[:, d:2 * d]).astype(BF16)
    h2 = h1 + modm_ref[:, 2 * d:3 * d] * _mlp(u, w1_ref, w2_ref)
    o_ref[...] = _final_norm(h2, fg_ref) if final else h2


def _post_attn(h, o, moda, modm, g2, w_o, w1, w2, final_g, ride):
    bsz, seq, d = h.shape
    tm = MLP_ROWS
    per_b = seq // tm
    row_spec = pl.BlockSpec((None, tm, d), lambda i: (i // per_b, i % per_b, 0))
    final = final_g is not None
    ride_in, ride_out, ride_shapes, ride_args = _ride_specs(ride, bsz * per_b)
    in_specs = [
        row_spec, row_spec,
        pl.BlockSpec((None, 1, moda.shape[-1]), lambda i: (i // per_b, 0, 0)),
        pl.BlockSpec((None, 1, modm.shape[-1]), lambda i: (i // per_b, 0, 0)),
        pl.BlockSpec((1, d), lambda i: (0, 0)),
        _layer_spec(*w_o), _layer_spec(*w1), _layer_spec(*w2),
    ]
    args = [h, o, moda, modm, g2.reshape(1, d), w_o[0], w1[0], w2[0]]
    if final:
        in_specs.append(pl.BlockSpec((1, d), lambda i: (0, 0)))
        args.append(final_g.reshape(1, d))
    args += ride_args
    outs = pl.pallas_call(
        functools.partial(_post_attn_kernel, final=final, n_ride=len(ride_in)),
        grid=(bsz * per_b,),
        in_specs=in_specs + ride_in,
        out_specs=[row_spec] + ride_out,
        out_shape=[jax.ShapeDtypeStruct((bsz, seq, d), F32)] + ride_shapes,
        compiler_params=_cparams(("parallel",)),
        name="post_attn_mlp",
    )(*args)
    return outs[0], outs[1:]


def _post_s5_kernel(*refs, n_ride):
    h_ref, z_ref, moda_ref, modm_ref, g_ref, permt_ref, wp_ref, w1_ref, w2_ref = refs[:9]
    o_ref = refs[9 + n_ride]
    z_scr, h1_scr, u_scr = refs[10 + 2 * n_ride:]
    _ride_cast(refs[9:9 + n_ride], refs[10 + n_ride:10 + 2 * n_ride])
    d = D_MODEL
    bsz, tt, _ = h_ref.shape
    rows = z_ref.shape[0]
    for j in range(d // LANES):
        for half in range(CHUNK // 8):
            tiles = []
            for gl in range(8):
                col = (8 * j + gl) * GROUP_COLS + half * LANES
                tiles.append(z_ref[:, col:col + LANES].astype(F32))
            outs = _regroup8(tiles)
            for tl in range(8):
                t = 8 * half + tl
                z_scr[t * rows:(t + 1) * rows, j * LANES:(j + 1) * LANES] = outs[tl].astype(BF16)
    zn = jnp.dot(permt_ref[...], z_scr[...], preferred_element_type=F32).astype(BF16)
    zz = jnp.dot(zn, wp_ref[...], preferred_element_type=F32)
    ymix = zz[:, :d] * jax.nn.sigmoid(zz[:, d:])
    g = g_ref[...]
    for b in range(bsz):
        sl = slice(b * tt, (b + 1) * tt)
        h1 = h_ref[b] + moda_ref[b, :, 2 * d:3 * d] * ymix[sl, :]
        h1_scr[sl, :] = h1
        u_scr[sl, :] = _normmod(h1, g, modm_ref[b, :, :d], modm_ref[b, :, d:2 * d]).astype(BF16)
    acc = _mlp(u_scr[...], w1_ref, w2_ref)
    for b in range(bsz):
        sl = slice(b * tt, (b + 1) * tt)
        o_ref[b] = h1_scr[sl, :] + modm_ref[b, :, 2 * d:3 * d] * acc[sl, :]


def _post_s5(h, zo, moda, modm, g2, permt, w_glu, w1, w2, ride):
    bsz, seq, d = h.shape
    tt = S5_TILE_TOKENS
    rows = bsz * tt // CHUNK
    h_spec = pl.BlockSpec((bsz, tt, d), lambda i: (0, i, 0))
    ride_in, ride_out, ride_shapes, ride_args = _ride_specs(ride, seq // tt)
    outs = pl.pallas_call(
        functools.partial(_post_s5_kernel, n_ride=len(ride_in)),
        grid=(seq // tt,),
        in_specs=[
            h_spec,
            pl.BlockSpec((rows, CHUNK * d), lambda i: (i, 0)),
            pl.BlockSpec(moda.shape, lambda i: (0, 0, 0)),
            pl.BlockSpec(modm.shape, lambda i: (0, 0, 0)),
            pl.BlockSpec((1, d), lambda i: (0, 0)),
            pl.BlockSpec(permt.shape, lambda i: (0, 0)),
            _layer_spec(*w_glu), _layer_spec(*w1), _layer_spec(*w2),
        ] + ride_in,
        out_specs=[h_spec] + ride_out,
        out_shape=[jax.ShapeDtypeStruct((bsz, seq, d), F32)] + ride_shapes,
        scratch_shapes=[pltpu.VMEM((bsz * tt, d), BF16), pltpu.VMEM((bsz * tt, d), F32),
                        pltpu.VMEM((bsz * tt, d), BF16)],
        compiler_params=_cparams(("parallel",)),
        name="post_s5_mlp",
    )(h, zo, moda, modm, g2.reshape(1, d), permt, w_glu[0], w1[0], w2[0], *ride_args)
    return outs[0], outs[1:]


def kernel(x, c, ln_g, ada_w, ada_b, ssm_lam_re, ssm_lam_im, ssm_log_dt, ssm_b_re, ssm_b_im, ssm_c_re, ssm_c_im, ssm_d, ssm_w_glu, kv_g, kv_ada_w, kv_ada_b, w_kv, attn_w_q, attn_w_o, mlp_w1, mlp_w2, final_g):
    bsz, seq, d = x.shape
    depth = ln_g.shape[0]
    n_s5 = ssm_lam_re.shape[0]
    assert d == D_MODEL and seq % (DILATIONS[-1] * ATT_BLOCK) == 0 and 0 < n_s5 < depth

    mods = _ada_mods(c, ada_w.reshape(depth * 2, d, 3 * d), ada_b.reshape(depth * 2, 3 * d))
    mods = mods.reshape(depth, 2, bsz, 1, 3 * d)
    kv_mod = _ada_mods(c, kv_ada_w[None], kv_ada_b[None]).reshape(bsz, 1, 2 * d)

    perm_np = _s5_tile_perm(bsz)
    perm = jnp.asarray(perm_np, BF16)
    permt = jnp.asarray(perm_np.T, BF16)
    tables = jax.vmap(_s5_weights)(ssm_lam_re, ssm_lam_im, ssm_log_dt, ssm_b_re, ssm_b_im,
                                   ssm_c_re, ssm_c_im, ssm_d)

    def as_layer(cast):
        return (cast[None], 0)

    h = x
    for layer in range(n_s5):
        ride = [(mlp_w1, 0), (mlp_w2, 0), (ssm_w_glu, 0)] if layer == 0 else []
        z, cast = _s5_pre(h, mods[layer, 0], ln_g[layer, 0], perm, ride)
        if layer == 0:
            w1, w2, w_glu = map(as_layer, cast)
        zo = _s5_core(z, tables, layer, bsz)
        ride = [(mlp_w1, layer + 1), (mlp_w2, layer + 1)]
        if layer + 1 < n_s5:
            ride += [(ssm_w_glu, layer + 1)]
        else:
            ride += [(w_kv[None], 0), (attn_w_q, 0), (attn_w_o, 0)]
        h, cast = _post_s5(h, zo, mods[layer, 0], mods[layer, 1], ln_g[layer, 1], permt, w_glu, w1, w2, ride)
        w1, w2 = as_layer(cast[0]), as_layer(cast[1])
        if layer + 1 < n_s5:
            w_glu = as_layer(cast[2])
        else:
            w_kvb, w_q, w_o = map(as_layer, cast[2:])

    kv_cols = tuple((i * d, (N_BRANCHES + i) * d) for i in range(N_BRANCHES))
    q_cols = tuple((i * d,) for i in range(N_BRANCHES))
    kvs = _norm_proj(h, kv_mod, kv_g, w_kvb, kv_cols, 1.0, KV_PROJ_TILE)
    for layer in range(n_s5, depth):
        j = layer - n_s5
        last = layer == depth - 1
        qs = _norm_proj(h, mods[layer, 0], ln_g[layer, 0], w_q, q_cols, Q_SCALE, Q_PROJ_TILE)
        o = _attention(qs, kvs)
        ride = [] if last else [(mlp_w1, layer + 1), (mlp_w2, layer + 1), (attn_w_q, j + 1), (attn_w_o, j + 1)]
        h, cast = _post_attn(h, o, mods[layer, 0], mods[layer, 1], ln_g[layer, 1], w_o, w1, w2,
                             final_g if last else None, ride)
        if not last:
            w1, w2, w_q, w_o = map(as_layer, cast)
    return h
```

```python
import functools
import math

import numpy as np

import jax
import jax.numpy as jnp
from jax import lax
from jax.experimental import pallas as pl
from jax.experimental.pallas import tpu as pltpu

F32 = jnp.float32
BF16 = jnp.bfloat16

D_MODEL = 1024
SSM_GROUP = 16
SSM_GROUPS = D_MODEL // SSM_GROUP
SSM_STATE = 64
HEAD_DIM = 64
DILATIONS = (1, 4, 16)
N_BRANCHES = len(DILATIONS)
ATT_BLOCK = 128
ATT_GROUP = 16
ATT_PAIRS_PER_STEP = 2
N_STATS = 3
Q_SCALE = HEAD_DIM ** -0.5 * math.log2(math.e)
D_FF = 4 * D_MODEL
EPS = 1e-6
NEG = -1e30

LANES = 128
CHUNK = 16
GROUP_COLS = CHUNK * SSM_GROUP
PAIR_COLS = 2 * GROUP_COLS
N_PAIRS = SSM_GROUPS // 2
S5_PAIRS_PER_STEP = 2
S5_TILE_TOKENS = 2 * CHUNK
MLP_ROWS = 1024
KV_PROJ_TILE = 512
Q_PROJ_TILE = 1024
VMEM_LIMIT = 56 * 1024 * 1024

HIGHEST = lax.Precision.HIGHEST
NT_DIMS = (((1,), (1,)), ((), ()))


def _cparams(sem):
    return pltpu.CompilerParams(dimension_semantics=sem, vmem_limit_bytes=VMEM_LIMIT)


def _layer_spec(stack, layer):
    zeros = (0,) * (stack.ndim - 1)
    return pl.BlockSpec((None,) + stack.shape[1:], lambda i: (layer,) + zeros, pipeline_mode=pl.Buffered(1))


def _normmod(x, g, shift, scale):
    ms = jnp.mean(x * x, axis=-1, keepdims=True)
    return (x * lax.rsqrt(ms + EPS) * g) * (1.0 + scale) + shift


def _ada_kernel(c_ref, w_ref, b_ref, o_ref):
    c = c_ref[...]
    sc = (c * jax.nn.sigmoid(c)).astype(BF16)
    o_ref[...] = jnp.dot(sc, w_ref[...].astype(BF16), preferred_element_type=F32) + b_ref[...]


def _ada_mods(c, w, b):
    n, d, width = w.shape
    bsz = c.shape[0]
    return pl.pallas_call(
        _ada_kernel,
        grid=(n,),
        in_specs=[
            pl.BlockSpec((bsz, d), lambda i: (0, 0)),
            pl.BlockSpec((None, d, width), lambda i: (i, 0, 0)),
            pl.BlockSpec((None, 1, width), lambda i: (i, 0, 0)),
        ],
        out_specs=pl.BlockSpec((None, bsz, width), lambda i: (i, 0, 0)),
        out_shape=jax.ShapeDtypeStruct((n, bsz, width), F32),
        compiler_params=_cparams(("parallel",)),
        name="ada_mods",
    )(c, w, b.reshape(n, 1, width))


def _proj_kernel(h_ref, mod_ref, g_ref, w_ref, o1_ref, o2_ref, o3_ref, us, us4, up4, up16, *, wcols, out_scale):
    d = D_MODEL
    tm = h_ref.shape[0]
    d1 = DILATIONS[1]
    q4, q16 = tm // d1, tm // DILATIONS[2]
    cw = o1_ref.shape[1]
    u = _normmod(h_ref[...], g_ref[...], mod_ref[:, :d], mod_ref[:, d:2 * d])
    for l in range(d // LANES):
        us[l] = u[:, l * LANES:(l + 1) * LANES]
    ub = u.astype(BF16)
    for n in range(cw // d):
        cols = slice(n * d, (n + 1) * d)
        res = jnp.dot(ub, w_ref[:, wcols[0][n]:wcols[0][n] + d], preferred_element_type=F32)
        o1_ref[:, cols] = (res * out_scale).astype(o1_ref.dtype)
    for l in range(d // LANES):
        for a in range(d1):
            part = us[l, pl.ds(a, q4, stride=d1), :]
            us4[l, a * q4:(a + 1) * q4, :] = part
            up4[a * q4:(a + 1) * q4, l * LANES:(l + 1) * LANES] = part.astype(BF16)
    for n in range(cw // d):
        res = jnp.dot(up4[...], w_ref[:, wcols[1][n]:wcols[1][n] + d], preferred_element_type=F32)
        for a in range(d1):
            o2_ref[a, :, n * d:(n + 1) * d] = (res[a * q4:(a + 1) * q4, :] * out_scale).astype(o2_ref.dtype)
    for l in range(d // LANES):
        for a1 in range(d1):
            for a2 in range(d1):
                r = d1 * a2 + a1
                up16[r * q16:(r + 1) * q16, l * LANES:(l + 1) * LANES] = us4[
                    l, pl.ds(a1 * q4 + a2, q16, stride=d1), :].astype(BF16)
    for n in range(cw // d):
        res = jnp.dot(up16[...], w_ref[:, wcols[2][n]:wcols[2][n] + d], preferred_element_type=F32)
        for r in range(DILATIONS[2]):
            o3_ref[r, :, n * d:(n + 1) * d] = (res[r * q16:(r + 1) * q16, :] * out_scale).astype(o3_ref.dtype)


def _norm_proj(h, mod, g, w, wcols, out_scale, tm):
    bsz, seq, d = h.shape
    cw = len(wcols[0]) * d
    per_b = seq // tm
    d1, d2 = DILATIONS[1], DILATIONS[2]
    o1, o2, o3 = pl.pallas_call(
        functools.partial(_proj_kernel, wcols=wcols, out_scale=out_scale),
        grid=(bsz * per_b,),
        in_specs=[
            pl.BlockSpec((None, tm, d), lambda i: (i // per_b, i % per_b, 0)),
            pl.BlockSpec((None, 1, mod.shape[-1]), lambda i: (i // per_b, 0, 0)),
            pl.BlockSpec((1, d), lambda i: (0, 0)),
            _layer_spec(*w),
        ],
        out_specs=[
            pl.BlockSpec((None, tm, cw), lambda i: (i // per_b, i % per_b, 0)),
            pl.BlockSpec((None, d1, None, tm // d1, cw), lambda i: (i // per_b, 0, i % per_b, 0, 0)),
            pl.BlockSpec((None, d2, None, tm // d2, cw), lambda i: (i // per_b, 0, i % per_b, 0, 0)),
        ],
        out_shape=[
            jax.ShapeDtypeStruct((bsz, seq, cw), BF16),
            jax.ShapeDtypeStruct((bsz, d1, per_b, tm // d1, cw), BF16),
            jax.ShapeDtypeStruct((bsz, d2, per_b, tm // d2, cw), BF16),
        ],
        scratch_shapes=[pltpu.VMEM((d // LANES, tm, LANES), F32), pltpu.VMEM((d // LANES, tm, LANES), F32),
                        pltpu.VMEM((tm, d), BF16), pltpu.VMEM((tm, d), BF16)],
        compiler_params=_cparams(("parallel",)),
        name="norm_proj",
    )(h, mod, g.reshape(1, d), w[0])
    return o1, o2.reshape(bsz, seq, cw), o3.reshape(bsz, seq, cw)


def _attn_group(blocks):
    blk = ATT_BLOCK
    lane = lax.broadcasted_iota(jnp.int32, (blk, LANES), 1)
    lo = lane < HEAD_DIM
    scores = []
    for qb, kcat, _, mask, _ in blocks:
        zero = jnp.zeros_like(qb)
        q2 = jnp.concatenate([jnp.where(lo, qb, zero), jnp.where(lo, zero, qb)], axis=0)
        s = lax.dot_general(q2, kcat, NT_DIMS, preferred_element_type=F32)
        scores.append(s + mask[...])
    soft = []
    for s, (_, _, _, _, old) in zip(scores, blocks):
        ms, ps = [], []
        for hh in range(2):
            tiles = [s[hh * blk:(hh + 1) * blk, t * LANES:(t + 1) * LANES] for t in range(s.shape[1] // LANES)]
            mx = tiles[0]
            for t in tiles[1:]:
                mx = jnp.maximum(mx, t)
            m = jnp.broadcast_to(jnp.max(mx, axis=-1, keepdims=True), (blk, LANES))
            if old is not None:
                m = jnp.maximum(m, old[0][hh])
            ms.append(m)
            ps.append(jnp.concatenate([jnp.exp2(t - m).astype(BF16) for t in tiles], axis=1))
        alpha = None
        if old is not None:
            alpha = jnp.exp2(jnp.where(lo, old[0][0], old[0][1]) - jnp.where(lo, ms[0], ms[1]))
        soft.append((ms, jnp.concatenate(ps, axis=0), alpha))
    outs = []
    for (ms, p, alpha), (_, _, vcat, _, old) in zip(soft, blocks):
        pv = jnp.dot(p, vcat, preferred_element_type=F32)
        acc = jnp.where(lo, pv[:blk, :LANES], pv[blk:, :LANES])
        den = jnp.where(lo, pv[:blk, LANES:], pv[blk:, LANES:])
        if old is not None:
            acc = alpha * old[1] + acc
            den = alpha * old[0][2] + den
        outs.append((ms + [den], acc))
    return outs


def _attn_kernel(*refs):
    n_in = 3 * N_BRANCHES
    ins, o_ref, scratch = refs[:n_in], refs[n_in], refs[n_in + 1:]
    for half in range(ATT_PAIRS_PER_STEP):
        lanes = pl.ds(half * LANES, LANES)
        _attn_pair(*[r.at[:, lanes] for r in ins], o_ref.at[:, lanes], *scratch)


def _attn_pair(q1, qd2, qd3, k1, kd2, kd3, v1, v2, v3, o_ref, tmp_nat, vd1, vd2, vd3,
               acc_ref, st_ref, acc2_ref, st2_ref, mask_pc, mask_c):
    blk = ATT_BLOCK
    seq = q1.shape[0]
    nblk = seq // blk
    d1, d2 = DILATIONS[1], DILATIONS[2]
    quarter = seq // d1
    per_res = quarter // blk
    assert d2 == d1 * d1 and seq == d2 * blk and nblk % ATT_GROUP == 0 and ATT_GROUP % per_res == 0

    @pl.when(jnp.logical_and(pl.program_id(0) == 0, pl.program_id(1) == 0))
    def _():
        for vd in (vd1, vd2, vd3):
            vd[:, LANES:] = jnp.ones((seq, LANES), BF16)
        qq = lax.broadcasted_iota(jnp.int32, (2 * blk, 2 * blk), 0) & (blk - 1)
        kk = lax.broadcasted_iota(jnp.int32, (2 * blk, 2 * blk), 1)
        valid = jnp.logical_or(jnp.logical_and(kk < blk, kk >= qq), jnp.logical_and(kk >= blk, kk - blk <= qq))
        mask_pc[...] = jnp.where(valid, 0.0, NEG)
        valid_c = (lax.broadcasted_iota(jnp.int32, (2 * blk, blk), 1)
                   <= (lax.broadcasted_iota(jnp.int32, (2 * blk, blk), 0) & (blk - 1)))
        mask_c[...] = jnp.where(valid_c, 0.0, NEG)

    for v, vd in ((v1, vd1), (v2, vd2), (v3, vd3)):
        vd[:, :LANES] = v[...]

    def run_group(blocks, dests):
        for (st_dst, acc_dst, rows), (st, acc) in zip(dests, _attn_group(blocks)):
            for k in range(N_STATS):
                st_dst[k, rows, :] = st[k]
            acc_dst[rows, :] = acc

    def load_old(st_src, acc_src, rows):
        return [st_src[k, rows, :] for k in range(N_STATS)], acc_src[rows, :]

    def b0_group(ns, first):
        blocks, dests = [], []
        for n in ns:
            if first and n == 0:
                rows = pl.ds(0, blk)
                blocks.append((q1[rows, :], k1[rows, :], vd1[rows, :], mask_c, None))
            else:
                rows = pl.ds(pl.multiple_of(n * blk, blk), blk)
                krows = pl.ds(pl.multiple_of((n - 1) * blk, blk), 2 * blk)
                blocks.append((q1[rows, :], k1[krows, :], vd1[krows, :], mask_pc, None))
            dests.append((st_ref, acc_ref, rows))
        run_group(blocks, dests)

    b0_group(list(range(ATT_GROUP)), True)

    def b0_body(i, carry):
        b0_group([ATT_GROUP * i + j for j in range(ATT_GROUP)], False)
        return carry

    lax.fori_loop(1, nblk // ATT_GROUP, b0_body, 0)

    res_per_trip = ATT_GROUP // per_res

    def b1_body(i, carry):
        blocks, dests = [], []
        for jr in range(res_per_trip):
            a1 = res_per_trip * i + jr
            base = a1 * quarter
            for n in range(per_res):
                rows = pl.ds(a1 + d1 * blk * n, blk, stride=d1)
                qrows = pl.ds(pl.multiple_of(base + n * blk, blk), blk)
                if n == 0:
                    krows, mask = qrows, mask_c
                else:
                    krows, mask = pl.ds(pl.multiple_of(base + (n - 1) * blk, blk), 2 * blk), mask_pc
                blocks.append((qd2[qrows, :], kd2[krows, :], vd2[krows, :], mask,
                               load_old(st_ref, acc_ref, rows)))
                dests.append((st2_ref, acc2_ref, qrows))
        run_group(blocks, dests)
        return carry

    lax.fori_loop(0, d1 // res_per_trip, b1_body, 0)

    def b2_body(i, carry):
        blocks, dests = [], []
        for j in range(ATT_GROUP):
            a1, a2 = j % d1, (ATT_GROUP // d1) * i + j // d1
            qrows = pl.ds(pl.multiple_of((ATT_GROUP * i + j) * blk, blk), blk)
            rows = pl.ds(a1 * quarter + a2, blk, stride=d1)
            blocks.append((qd3[qrows, :], kd3[qrows, :], vd3[qrows, :], mask_c,
                           load_old(st2_ref, acc2_ref, rows)))
            dests.append((st2_ref, acc2_ref, rows))
        run_group(blocks, dests)
        return carry

    lax.fori_loop(0, d2 // ATT_GROUP, b2_body, 0)

    for a in range(d1):
        seg = slice(a * quarter, (a + 1) * quarter)
        tmp_nat[pl.ds(a, quarter, stride=d1), :] = acc2_ref[seg, :] / st2_ref[N_STATS - 1, seg, :]
    o_ref[...] = tmp_nat[...].astype(o_ref.dtype)


def _attention(qs, kvs):
    bsz, seq, _ = qs[0].shape
    steps = D_MODEL // (ATT_PAIRS_PER_STEP * LANES)

    def slab(col0):
        return pl.BlockSpec((None, seq, ATT_PAIRS_PER_STEP * LANES), lambda b, hp: (b, 0, col0 + hp))

    in_specs = [slab(0)] * N_BRANCHES + [slab(0)] * N_BRANCHES + [slab(steps)] * N_BRANCHES
    return pl.pallas_call(
        _attn_kernel,
        grid=(bsz, steps),
        in_specs=in_specs,
        out_specs=slab(0),
        out_shape=jax.ShapeDtypeStruct((bsz, seq, D_MODEL), BF16),
        scratch_shapes=([pltpu.VMEM((seq, LANES), F32)] + [pltpu.VMEM((seq, 2 * LANES), BF16)] * 3
                        + [pltpu.VMEM((seq, LANES), F32), pltpu.VMEM((N_STATS, seq, LANES), F32)] * 2
                        + [pltpu.VMEM((2 * ATT_BLOCK, 2 * ATT_BLOCK), F32), pltpu.VMEM((2 * ATT_BLOCK, ATT_BLOCK), F32)]),
        compiler_params=_cparams(("arbitrary", "arbitrary")),
        name="attention",
    )(*qs, *kvs, *kvs)


def _regroup8(tiles):
    t = list(tiles)
    lane = lax.broadcasted_iota(jnp.int32, t[0].shape, 1)
    piece = lane // SSM_GROUP
    for dist in (4, 2, 1):
        bit = (piece & dist) != 0
        shift = dist * SSM_GROUP
        for a in range(8):
            if a & dist:
                continue
            b = a | dist
            ta, tb = t[a], t[b]
            t[a] = jnp.where(bit, pltpu.roll(tb, shift, 1), ta)
            t[b] = jnp.where(bit, tb, pltpu.roll(ta, LANES - shift, 1))
    return t


def _s5_tile_perm(bsz):
    tt = S5_TILE_TOKENS
    n = bsz * tt
    perm = np.zeros((n, n), np.float32)
    for b in range(bsz):
        for c in range(tt // CHUNK):
            for t in range(CHUNK):
                perm[(t * (tt // CHUNK) + c) * bsz + b, b * tt + c * CHUNK + t] = 1.0
    return perm


def _s5_pre_kernel(*refs, n_ride):
    h_ref, mod_ref, g_ref, perm_ref = refs[:4]
    z_ref = refs[4 + n_ride]
    u_scr, up_scr = refs[5 + 2 * n_ride:]
    _ride_cast(refs[4:4 + n_ride], refs[5 + n_ride:5 + 2 * n_ride])
    d = D_MODEL
    bsz, tt, _ = h_ref.shape
    g = g_ref[...]
    for b in range(bsz):
        u_scr[b * tt:(b + 1) * tt, :] = _normmod(
            h_ref[b], g, mod_ref[b, :, :d], mod_ref[b, :, d:2 * d]).astype(BF16)
    up_scr[...] = jnp.dot(perm_ref[...], u_scr[...], preferred_element_type=F32)
    rows = z_ref.shape[0]
    for j in range(d // LANES):
        for half in range(CHUNK // 8):
            tiles = [up_scr[(8 * half + tl) * rows:(8 * half + tl + 1) * rows, j * LANES:(j + 1) * LANES]
                     for tl in range(8)]
            outs = _regroup8(tiles)
            for gl in range(8):
                col = (8 * j + gl) * GROUP_COLS + half * LANES
                z_ref[:, col:col + LANES] = outs[gl].astype(z_ref.dtype)


def _s5_pre(h, mod, g, perm, ride):
    bsz, seq, d = h.shape
    tt = S5_TILE_TOKENS
    rows = bsz * tt // CHUNK
    ride_in, ride_out, ride_shapes, ride_args = _ride_specs(ride, seq // tt)
    outs = pl.pallas_call(
        functools.partial(_s5_pre_kernel, n_ride=len(ride_in)),
        grid=(seq // tt,),
        in_specs=[
            pl.BlockSpec((bsz, tt, d), lambda i: (0, i, 0)),
            pl.BlockSpec(mod.shape, lambda i: (0, 0, 0)),
            pl.BlockSpec((1, d), lambda i: (0, 0)),
            pl.BlockSpec(perm.shape, lambda i: (0, 0)),
        ] + ride_in,
        out_specs=[pl.BlockSpec((rows, CHUNK * d), lambda i: (i, 0))] + ride_out,
        out_shape=[jax.ShapeDtypeStruct((bsz * seq // CHUNK, CHUNK * d), BF16)] + ride_shapes,
        scratch_shapes=[pltpu.VMEM((bsz * tt, d), BF16), pltpu.VMEM((bsz * tt, d), F32)],
        compiler_params=_cparams(("parallel",)),
        name="s5_pre",
    )(h, mod, g.reshape(1, d), perm, *ride_args)
    return outs[0], outs[1:]


def _s5_core_kernel(z_ref, win_ref, wt_ref, wx_ref, dec_ref, d_ref, o_ref, s_scr, x_scr, *, bsz):
    gc = GROUP_COLS
    npair = S5_PAIRS_PER_STEP
    nchunk = z_ref.shape[0] // bsz
    for q in range(npair):
        s_scr[:, q * gc:(q + 1) * gc] = jnp.dot(z_ref[:, q * PAIR_COLS:(q + 1) * PAIR_COLS],
                                                win_ref[2 * q:2 * q + 2].reshape(2 * gc, gc),
                                                preferred_element_type=F32)
    re_cols = [slice(q * gc, q * gc + LANES) for q in range(npair)]
    im_cols = [slice(q * gc + LANES, (q + 1) * gc) for q in range(npair)]
    ar = [dec_ref[:, cols] for cols in re_cols]
    ai = [dec_ref[:, cols] for cols in im_cols]

    def step(c, carry):
        rows = pl.ds(pl.multiple_of(c * bsz, bsz), bsz)
        new = []
        for q in range(npair):
            xr, xi = carry[2 * q], carry[2 * q + 1]
            x_scr[rows, re_cols[q]] = xr
            x_scr[rows, im_cols[q]] = xi
            new += [ar[q] * xr - ai[q] * xi + s_scr[rows, re_cols[q]],
                    ar[q] * xi + ai[q] * xr + s_scr[rows, im_cols[q]]]
        return tuple(new)

    zero = jnp.zeros((bsz, LANES), F32)
    lax.fori_loop(0, nchunk, step, (zero,) * (2 * npair))
    for g in range(2 * npair):
        z = z_ref[:, g * gc:(g + 1) * gc]
        x = x_scr[:, (g // 2) * gc:(g // 2 + 1) * gc].astype(BF16)
        y = jnp.dot(jnp.concatenate([z, x], axis=1), jnp.concatenate([wt_ref[g], wx_ref[g]], axis=0),
                    preferred_element_type=F32)
        y = y + d_ref[:, g * gc:(g + 1) * gc] * z.astype(F32)
        o_ref[:, g * gc:(g + 1) * gc] = jax.nn.gelu(y).astype(o_ref.dtype)


def _s5_core(z, tables, layer, bsz):
    nrow, width = z.shape
    w_in, w_toep, w_x, dec, d_perm = tables
    npair = S5_PAIRS_PER_STEP
    wspec = pl.BlockSpec((None, 2 * npair, GROUP_COLS, GROUP_COLS), lambda k: (layer, k, 0, 0))
    return pl.pallas_call(
        functools.partial(_s5_core_kernel, bsz=bsz),
        grid=(N_PAIRS // npair,),
        in_specs=[
            pl.BlockSpec((nrow, npair * PAIR_COLS), lambda k: (0, k)),
            wspec, wspec, wspec,
            pl.BlockSpec((None, 1, npair * GROUP_COLS), lambda k: (layer, 0, k)),
            pl.BlockSpec((None, 1, npair * PAIR_COLS), lambda k: (layer, 0, k)),
        ],
        out_specs=pl.BlockSpec((nrow, npair * PAIR_COLS), lambda k: (0, k)),
        out_shape=jax.ShapeDtypeStruct((nrow, width), BF16),
        scratch_shapes=[pltpu.VMEM((nrow, npair * GROUP_COLS), F32), pltpu.VMEM((nrow, npair * GROUP_COLS), F32)],
        compiler_params=_cparams(("parallel",)),
        name="s5_core",
    )(z, w_in, w_toep, w_x, dec, d_perm)


def _s5_weights(lam_re, lam_im, log_dt, b_re, b_im, c_re, c_im, d_skip):
    g, p, c16 = SSM_GROUPS, SSM_STATE, SSM_GROUP
    lam = lax.complex(lam_re.astype(F32), lam_im.astype(F32))
    dt = jnp.exp(log_dt.astype(F32))[:, None]
    steps = jnp.arange(CHUNK + 1, dtype=F32)
    apow = jnp.exp((lam * dt)[None] * steps[:, None, None])
    a = apow[1]
    bbar = ((a - 1.0) / lam)[..., None] * lax.complex(b_re.astype(F32), b_im.astype(F32))
    cmat = lax.complex(c_re.astype(F32), c_im.astype(F32))
    odd = (jnp.arange(g) % 2)[:, None, None]

    win = apow[CHUNK - 1::-1][..., None] * bbar[None]

    def lay_in(x):
        return jnp.transpose(x.astype(BF16), (1, 0, 3, 2)).reshape(g, GROUP_COLS, p)

    win_re, win_im = lay_in(win.real), lay_in(win.imag)
    zeros = jnp.zeros_like(win_re)
    w_in = jnp.concatenate([
        jnp.where(odd == 0, win_re, zeros), jnp.where(odd == 1, win_re, zeros),
        jnp.where(odd == 0, win_im, zeros), jnp.where(odd == 1, win_im, zeros)], axis=-1)

    kern = jnp.einsum('gop,kgp,gpi->kgoi', cmat, apow[:CHUNK], bbar, precision=HIGHEST).real
    idx = jnp.arange(CHUNK)
    onehot = (idx[None, None, :] - idx[None, :, None] == idx[:, None, None]).astype(BF16)
    w_toep = jnp.einsum('kst,kgoi->gsito', onehot, kern.astype(BF16),
                        preferred_element_type=BF16)
    w_toep = w_toep.reshape(g, GROUP_COLS, GROUP_COLS)

    cw = cmat[None] * apow[1:, :, None, :]

    def lay_x(x):
        return jnp.transpose(x.astype(BF16), (1, 3, 0, 2)).reshape(g, p, GROUP_COLS)

    cw_re, cw_im = lay_x(cw.real), lay_x(-cw.imag)
    zx = jnp.zeros_like(cw_re)
    w_x = jnp.concatenate([
        jnp.where(odd == 0, cw_re, zx), jnp.where(odd == 1, cw_re, zx),
        jnp.where(odd == 0, cw_im, zx), jnp.where(odd == 1, cw_im, zx)], axis=1)

    a16 = apow[CHUNK].reshape(N_PAIRS, 2 * p)
    dec = jnp.concatenate([a16.real, a16.imag], axis=-1).reshape(1, N_PAIRS * GROUP_COLS)
    d_perm = jnp.broadcast_to(d_skip.astype(F32).reshape(g, 1, c16), (g, CHUNK, c16)).reshape(1, g * GROUP_COLS)
    return w_in, w_toep, w_x, dec, d_perm


def _mlp(u, w1_ref, w2_ref):
    d = D_MODEL
    acc = None
    for k in range(D_FF // d):
        a = jnp.dot(u, w1_ref[:, k * d:(k + 1) * d], preferred_element_type=F32)
        a = jnp.square(jnp.maximum(a, 0.0)).astype(BF16)
        part = jnp.dot(a, w2_ref[k * d:(k + 1) * d, :], preferred_element_type=F32)
        acc = part if acc is None else acc + part
    return acc


def _final_norm(h, fg_ref):
    ms = jnp.mean(h * h, axis=-1, keepdims=True)
    return h * lax.rsqrt(ms + EPS) * fg_ref[...]


def _ride_specs(ride, nsteps):
    ins, outs, shapes = [], [], []
    for st, layer in ride:
        _, r, c = st.shape
        assert r % nsteps == 0
        ins.append(pl.BlockSpec((None, r // nsteps, c), lambda i, layer=layer: (layer, i, 0)))
        outs.append(pl.BlockSpec((r // nsteps, c), lambda i: (i, 0)))
        shapes.append(jax.ShapeDtypeStruct((r, c), BF16))
    return ins, outs, shapes, [st for st, _ in ride]


def _ride_cast(ride_in, ride_out):
    for src, dst in zip(ride_in, ride_out):
        dst[...] = src[...].astype(dst.dtype)


def _post_attn_kernel(*refs, final, n_ride):
    h_ref, y_ref, moda_ref, modm_ref, g_ref, wp_ref, w1_ref, w2_ref = refs[:8]
    pos = 8 + int(final)
    fg_ref = refs[8] if final else None
    o_ref = refs[pos + n_ride]
    _ride_cast(refs[pos:pos + n_ride], refs[pos + n_ride + 1:])
    d = D_MODEL
    ymix = jnp.dot(y_ref[...], wp_ref[...], preferred_element_type=F32)
    h1 = h_ref[...] + moda_ref[:, 2 * d:3 * d] * ymix
    u = _normmod(h1, g_ref[...], modm_ref[:, :d], modm_ref[:, d:2 * d]).astype(BF16)
    h2 = h1 + modm_ref[:, 2 * d:3 * d] * _mlp(u, w1_ref, w2_ref)
    o_ref[...] = _final_norm(h2, fg_ref) if final else h2


def _post_attn(h, o, moda, modm, g2, w_o, w1, w2, final_g, ride):
    bsz, seq, d = h.shape
    tm = MLP_ROWS
    per_b = seq // tm
    row_spec = pl.BlockSpec((None, tm, d), lambda i: (i // per_b, i % per_b, 0))
    final = final_g is not None
    ride_in, ride_out, ride_shapes, ride_args = _ride_specs(ride, bsz * per_b)
    in_specs = [
        row_spec, row_spec,
        pl.BlockSpec((None, 1, moda.shape[-1]), lambda i: (i // per_b, 0, 0)),
        pl.BlockSpec((None, 1, modm.shape[-1]), lambda i: (i // per_b, 0, 0)),
        pl.BlockSpec((1, d), lambda i: (0, 0)),
        _layer_spec(*w_o), _layer_spec(*w1), _layer_spec(*w2),
    ]
    args = [h, o, moda, modm, g2.reshape(1, d), w_o[0], w1[0], w2[0]]
    if final:
        in_specs.append(pl.BlockSpec((1, d), lambda i: (0, 0)))
        args.append(final_g.reshape(1, d))
    args += ride_args
    outs = pl.pallas_call(
        functools.partial(_post_attn_kernel, final=final, n_ride=len(ride_in)),
        grid=(bsz * per_b,),
        in_specs=in_specs + ride_in,
        out_specs=[row_spec] + ride_out,
        out_shape=[jax.ShapeDtypeStruct((bsz, seq, d), F32)] + ride_shapes,
        compiler_params=_cparams(("parallel",)),
        name="post_attn_mlp",
    )(*args)
    return outs[0], outs[1:]


def _post_s5_kernel(*refs, n_ride):
    h_ref, z_ref, moda_ref, modm_ref, g_ref, permt_ref, wp_ref, w1_ref, w2_ref = refs[:9]
    o_ref = refs[9 + n_ride]
    z_scr, h1_scr, u_scr = refs[10 + 2 * n_ride:]
    _ride_cast(refs[9:9 + n_ride], refs[10 + n_ride:10 + 2 * n_ride])
    d = D_MODEL
    bsz, tt, _ = h_ref.shape
    rows = z_ref.shape[0]
    for j in range(d // LANES):
        for half in range(CHUNK // 8):
            tiles = []
            for gl in range(8):
                col = (8 * j + gl) * GROUP_COLS + half * LANES
                tiles.append(z_ref[:, col:col + LANES].astype(F32))
            outs = _regroup8(tiles)
            for tl in range(8):
                t = 8 * half + tl
                z_scr[t * rows:(t + 1) * rows, j * LANES:(j + 1) * LANES] = outs[tl].astype(BF16)
    zn = jnp.dot(permt_ref[...], z_scr[...], preferred_element_type=F32).astype(BF16)
    zz = jnp.dot(zn, wp_ref[...], preferred_element_type=F32)
    ymix = zz[:, :d] * jax.nn.sigmoid(zz[:, d:])
    g = g_ref[...]
    for b in range(bsz):
        sl = slice(b * tt, (b + 1) * tt)
        h1 = h_ref[b] + moda_ref[b, :, 2 * d:3 * d] * ymix[sl, :]
        h1_scr[sl, :] = h1
        u_scr[sl, :] = _normmod(h1, g, modm_ref[b, :, :d], modm_ref[b, :, d:2 * d]).astype(BF16)
    acc = _mlp(u_scr[...], w1_ref, w2_ref)
    for b in range(bsz):
        sl = slice(b * tt, (b + 1) * tt)
        o_ref[b] = h1_scr[sl, :] + modm_ref[b, :, 2 * d:3 * d] * acc[sl, :]


def _post_s5(h, zo, moda, modm, g2, permt, w_glu, w1, w2, ride):
    bsz, seq, d = h.shape
    tt = S5_TILE_TOKENS
    rows = bsz * tt // CHUNK
    h_spec = pl.BlockSpec((bsz, tt, d), lambda i: (0, i, 0))
    ride_in, ride_out, ride_shapes, ride_args = _ride_specs(ride, seq // tt)
    outs = pl.pallas_call(
        functools.partial(_post_s5_kernel, n_ride=len(ride_in)),
        grid=(seq // tt,),
        in_specs=[
            h_spec,
            pl.BlockSpec((rows, CHUNK * d), lambda i: (i, 0)),
            pl.BlockSpec(moda.shape, lambda i: (0, 0, 0)),
            pl.BlockSpec(modm.shape, lambda i: (0, 0, 0)),
            pl.BlockSpec((1, d), lambda i: (0, 0)),
            pl.BlockSpec(permt.shape, lambda i: (0, 0)),
            _layer_spec(*w_glu), _layer_spec(*w1), _layer_spec(*w2),
        ] + ride_in,
        out_specs=[h_spec] + ride_out,
        out_shape=[jax.ShapeDtypeStruct((bsz, seq, d), F32)] + ride_shapes,
        scratch_shapes=[pltpu.VMEM((bsz * tt, d), BF16), pltpu.VMEM((bsz * tt, d), F32),
                        pltpu.VMEM((bsz * tt, d), BF16)],
        compiler_params=_cparams(("parallel",)),
        name="post_s5_mlp",
    )(h, zo, moda, modm, g2.reshape(1, d), permt, w_glu[0], w1[0], w2[0], *ride_args)
    return outs[0], outs[1:]


def kernel(x, c, ln_g, ada_w, ada_b, ssm_lam_re, ssm_lam_im, ssm_log_dt, ssm_b_re, ssm_b_im, ssm_c_re, ssm_c_im, ssm_d, ssm_w_glu, kv_g, kv_ada_w, kv_ada_b, w_kv, attn_w_q, attn_w_o, mlp_w1, mlp_w2, final_g):
    bsz, seq, d = x.shape
    depth = ln_g.shape[0]
    n_s5 = ssm_lam_re.shape[0]
    assert d == D_MODEL and seq % (DILATIONS[-1] * ATT_BLOCK) == 0 and 0 < n_s5 < depth

    mods = _ada_mods(c, ada_w.reshape(depth * 2, d, 3 * d), ada_b.reshape(depth * 2, 3 * d))
    mods = mods.reshape(depth, 2, bsz, 1, 3 * d)
    kv_mod = _ada_mods(c, kv_ada_w[None], kv_ada_b[None]).reshape(bsz, 1, 2 * d)

    perm_np = _s5_tile_perm(bsz)
    perm = jnp.asarray(perm_np, BF16)
    permt = jnp.asarray(perm_np.T, BF16)
    tables = jax.vmap(_s5_weights)(ssm_lam_re, ssm_lam_im, ssm_log_dt, ssm_b_re, ssm_b_im,
                                   ssm_c_re, ssm_c_im, ssm_d)

    def as_layer(cast):
        return (cast[None], 0)

    h = x
    for layer in range(n_s5):
        ride = [(mlp_w1, 0), (mlp_w2, 0), (ssm_w_glu, 0)] if layer == 0 else []
        z, cast = _s5_pre(h, mods[layer, 0], ln_g[layer, 0], perm, ride)
        if layer == 0:
            w1, w2, w_glu = map(as_layer, cast)
        zo = _s5_core(z, tables, layer, bsz)
        ride = [(mlp_w1, layer + 1), (mlp_w2, layer + 1)]
        if layer + 1 < n_s5:
            ride += [(ssm_w_glu, layer + 1)]
        else:
            ride += [(w_kv[None], 0), (attn_w_q, 0), (attn_w_o, 0)]
        h, cast = _post_s5(h, zo, mods[layer, 0], mods[layer, 1], ln_g[layer, 1], permt, w_glu, w1, w2, ride)
        w1, w2 = as_layer(cast[0]), as_layer(cast[1])
        if layer + 1 < n_s5:
            w_glu = as_layer(cast[2])
        else:
            w_kvb, w_q, w_o = map(as_layer, cast[2:])

    kv_cols = tuple((i * d, (N_BRANCHES + i) * d) for i in range(N_BRANCHES))
    q_cols = tuple((i * d,) for i in range(N_BRANCHES))
    kvs = _norm_proj(h, kv_mod, kv_g, w_kvb, kv_cols, 1.0, KV_PROJ_TILE)
    for layer in range(n_s5, depth):
        j = layer - n_s5
        last = layer == depth - 1
        qs = _norm_proj(h, mods[layer, 0], ln_g[layer, 0], w_q, q_cols, Q_SCALE, Q_PROJ_TILE)
        o = _attention(qs, kvs)
        ride = [] if last else [(mlp_w1, layer + 1), (mlp_w2, layer + 1), (attn_w_q, j + 1), (attn_w_o, j + 1)]
        h, cast = _post_attn(h, o, mods[layer, 0], mods[layer, 1], ln_g[layer, 1], w_o, w1, w2,
                             final_g if last else None, ride)
        if not last:
            w1, w2, w_q, w_o = map(as_layer, cast)
    return h
```

```python
import functools
import math

import numpy as np

import jax
import jax.numpy as jnp
from jax import lax
from jax.experimental import pallas as pl
from jax.experimental.pallas import tpu as pltpu

F32 = jnp.float32
BF16 = jnp.bfloat16

D_MODEL = 1024
SSM_GROUP = 16
SSM_GROUPS = D_MODEL // SSM_GROUP
SSM_STATE = 64
HEAD_DIM = 64
DILATIONS = (1, 4, 16)
N_BRANCHES = len(DILATIONS)
ATT_BLOCK = 128
ATT_GROUP = 16
ATT_PAIRS_PER_STEP = 4
N_STATS = 3
Q_SCALE = HEAD_DIM ** -0.5 * math.log2(math.e)
D_FF = 4 * D_MODEL
EPS = 1e-6
NEG = -1e30

LANES = 128
CHUNK = 16
GROUP_COLS = CHUNK * SSM_GROUP
PAIR_COLS = 2 * GROUP_COLS
N_PAIRS = SSM_GROUPS // 2
S5_PAIRS_PER_STEP = 2
S5_TILE_TOKENS = 2 * CHUNK
MLP_ROWS = 1024
KV_PROJ_TILE = 512
Q_PROJ_TILE = 1024
VMEM_LIMIT = 56 * 1024 * 1024

HIGHEST = lax.Precision.HIGHEST
NT_DIMS = (((1,), (1,)), ((), ()))


def _cparams(sem):
    return pltpu.CompilerParams(dimension_semantics=sem, vmem_limit_bytes=VMEM_LIMIT)


def _layer_spec(stack, layer):
    zeros = (0,) * (stack.ndim - 1)
    return pl.BlockSpec((None,) + stack.shape[1:], lambda i: (layer,) + zeros, pipeline_mode=pl.Buffered(1))


def _normmod(x, g, shift, scale):
    ms = jnp.mean(x * x, axis=-1, keepdims=True)
    return (x * lax.rsqrt(ms + EPS) * g) * (1.0 + scale) + shift


def _ada_kernel(c_ref, w_ref, b_ref, o_ref):
    c = c_ref[...]
    sc = (c * jax.nn.sigmoid(c)).astype(BF16)
    o_ref[...] = jnp.dot(sc, w_ref[...].astype(BF16), preferred_element_type=F32) + b_ref[...]


def _ada_mods(c, w, b):
    n, d, width = w.shape
    bsz = c.shape[0]
    return pl.pallas_call(
        _ada_kernel,
        grid=(n,),
        in_specs=[
            pl.BlockSpec((bsz, d), lambda i: (0, 0)),
            pl.BlockSpec((None, d, width), lambda i: (i, 0, 0)),
            pl.BlockSpec((None, 1, width), lambda i: (i, 0, 0)),
        ],
        out_specs=pl.BlockSpec((None, bsz, width), lambda i: (i, 0, 0)),
        out_shape=jax.ShapeDtypeStruct((n, bsz, width), F32),
        compiler_params=_cparams(("parallel",)),
        name="ada_mods",
    )(c, w, b.reshape(n, 1, width))


def _proj_kernel(h_ref, mod_ref, g_ref, w_ref, o1_ref, o2_ref, o3_ref, us, us4, up4, up16, *, wcols, out_scale):
    d = D_MODEL
    tm = h_ref.shape[0]
    d1 = DILATIONS[1]
    q4, q16 = tm // d1, tm // DILATIONS[2]
    cw = o1_ref.shape[1]
    u = _normmod(h_ref[...], g_ref[...], mod_ref[:, :d], mod_ref[:, d:2 * d])
    for l in range(d // LANES):
        us[l] = u[:, l * LANES:(l + 1) * LANES]
    ub = u.astype(BF16)
    for n in range(cw // d):
        cols = slice(n * d, (n + 1) * d)
        res = jnp.dot(ub, w_ref[:, wcols[0][n]:wcols[0][n] + d], preferred_element_type=F32)
        o1_ref[:, cols] = (res * out_scale).astype(o1_ref.dtype)
    for l in range(d // LANES):
        for a in range(d1):
            part = us[l, pl.ds(a, q4, stride=d1), :]
            us4[l, a * q4:(a + 1) * q4, :] = part
            up4[a * q4:(a + 1) * q4, l * LANES:(l + 1) * LANES] = part.astype(BF16)
    for n in range(cw // d):
        res = jnp.dot(up4[...], w_ref[:, wcols[1][n]:wcols[1][n] + d], preferred_element_type=F32)
        for a in range(d1):
            o2_ref[a, :, n * d:(n + 1) * d] = (res[a * q4:(a + 1) * q4, :] * out_scale).astype(o2_ref.dtype)
    for l in range(d // LANES):
        for a1 in range(d1):
            for a2 in range(d1):
                r = d1 * a2 + a1
                up16[r * q16:(r + 1) * q16, l * LANES:(l + 1) * LANES] = us4[
                    l, pl.ds(a1 * q4 + a2, q16, stride=d1), :].astype(BF16)
    for n in range(cw // d):
        res = jnp.dot(up16[...], w_ref[:, wcols[2][n]:wcols[2][n] + d], preferred_element_type=F32)
        for r in range(DILATIONS[2]):
            o3_ref[r, :, n * d:(n + 1) * d] = (res[r * q16:(r + 1) * q16, :] * out_scale).astype(o3_ref.dtype)


def _norm_proj(h, mod, g, w, wcols, out_scale, tm):
    bsz, seq, d = h.shape
    cw = len(wcols[0]) * d
    per_b = seq // tm
    d1, d2 = DILATIONS[1], DILATIONS[2]
    o1, o2, o3 = pl.pallas_call(
        functools.partial(_proj_kernel, wcols=wcols, out_scale=out_scale),
        grid=(bsz * per_b,),
        in_specs=[
            pl.BlockSpec((None, tm, d), lambda i: (i // per_b, i % per_b, 0)),
            pl.BlockSpec((None, 1, mod.shape[-1]), lambda i: (i // per_b, 0, 0)),
            pl.BlockSpec((1, d), lambda i: (0, 0)),
            _layer_spec(*w),
        ],
        out_specs=[
            pl.BlockSpec((None, tm, cw), lambda i: (i // per_b, i % per_b, 0)),
            pl.BlockSpec((None, d1, None, tm // d1, cw), lambda i: (i // per_b, 0, i % per_b, 0, 0)),
            pl.BlockSpec((None, d2, None, tm // d2, cw), lambda i: (i // per_b, 0, i % per_b, 0, 0)),
        ],
        out_shape=[
            jax.ShapeDtypeStruct((bsz, seq, cw), BF16),
            jax.ShapeDtypeStruct((bsz, d1, per_b, tm // d1, cw), BF16),
            jax.ShapeDtypeStruct((bsz, d2, per_b, tm // d2, cw), BF16),
        ],
        scratch_shapes=[pltpu.VMEM((d // LANES, tm, LANES), F32), pltpu.VMEM((d // LANES, tm, LANES), F32),
                        pltpu.VMEM((tm, d), BF16), pltpu.VMEM((tm, d), BF16)],
        compiler_params=_cparams(("parallel",)),
        name="norm_proj",
    )(h, mod, g.reshape(1, d), w[0])
    return o1, o2.reshape(bsz, seq, cw), o3.reshape(bsz, seq, cw)


def _attn_group(blocks):
    blk = ATT_BLOCK
    lane = lax.broadcasted_iota(jnp.int32, (blk, LANES), 1)
    lo = lane < HEAD_DIM
    scores = []
    for qb, kcat, _, mask, _ in blocks:
        zero = jnp.zeros_like(qb)
        q2 = jnp.concatenate([jnp.where(lo, qb, zero), jnp.where(lo, zero, qb)], axis=0)
        s = lax.dot_general(q2, kcat, NT_DIMS, preferred_element_type=F32)
        scores.append(s + mask[...])
    soft = []
    for s, (_, _, _, _, old) in zip(scores, blocks):
        ms, ps = [], []
        for hh in range(2):
            tiles = [s[hh * blk:(hh + 1) * blk, t * LANES:(t + 1) * LANES] for t in range(s.shape[1] // LANES)]
            mx = tiles[0]
            for t in tiles[1:]:
                mx = jnp.maximum(mx, t)
            m = jnp.broadcast_to(jnp.max(mx, axis=-1, keepdims=True), (blk, LANES))
            if old is not None:
                m = jnp.maximum(m, old[0][hh])
            ms.append(m)
            ps.append(jnp.concatenate([jnp.exp2(t - m).astype(BF16) for t in tiles], axis=1))
        alpha = None
        if old is not None:
            alpha = jnp.exp2(jnp.where(lo, old[0][0], old[0][1]) - jnp.where(lo, ms[0], ms[1]))
        soft.append((ms, jnp.concatenate(ps, axis=0), alpha))
    outs = []
    for (ms, p, alpha), (_, _, vcat, _, old) in zip(soft, blocks):
        pv = jnp.dot(p, vcat, preferred_element_type=F32)
        acc = jnp.where(lo, pv[:blk, :LANES], pv[blk:, :LANES])
        den = jnp.where(lo, pv[:blk, LANES:], pv[blk:, LANES:])
        if old is not None:
            acc = alpha * old[1] + acc
            den = alpha * old[0][2] + den
        outs.append((ms + [den], acc))
    return outs


def _attn_kernel(*refs):
    n_in = 3 * N_BRANCHES
    ins, o_ref, scratch = refs[:n_in], refs[n_in], refs[n_in + 1:]
    for half in range(ATT_PAIRS_PER_STEP):
        lanes = pl.ds(half * LANES, LANES)
        _attn_pair(*[r.at[:, lanes] for r in ins], o_ref.at[:, lanes], *scratch)


def _attn_pair(q1, qd2, qd3, k1, kd2, kd3, v1, v2, v3, o_ref, tmp_nat, vd1, vd2, vd3,
               acc_ref, st_ref, acc2_ref, st2_ref, mask_pc, mask_c):
    blk = ATT_BLOCK
    seq = q1.shape[0]
    nblk = seq // blk
    d1, d2 = DILATIONS[1], DILATIONS[2]
    quarter = seq // d1
    per_res = quarter // blk
    assert d2 == d1 * d1 and seq == d2 * blk and nblk % ATT_GROUP == 0 and ATT_GROUP % per_res == 0

    @pl.when(jnp.logical_and(pl.program_id(0) == 0, pl.program_id(1) == 0))
    def _():
        for vd in (vd1, vd2, vd3):
            vd[:, LANES:] = jnp.ones((seq, LANES), BF16)
        qq = lax.broadcasted_iota(jnp.int32, (2 * blk, 2 * blk), 0) & (blk - 1)
        kk = lax.broadcasted_iota(jnp.int32, (2 * blk, 2 * blk), 1)
        valid = jnp.logical_or(jnp.logical_and(kk < blk, kk >= qq), jnp.logical_and(kk >= blk, kk - blk <= qq))
        mask_pc[...] = jnp.where(valid, 0.0, NEG)
        valid_c = (lax.broadcasted_iota(jnp.int32, (2 * blk, blk), 1)
                   <= (lax.broadcasted_iota(jnp.int32, (2 * blk, blk), 0) & (blk - 1)))
        mask_c[...] = jnp.where(valid_c, 0.0, NEG)

    for v, vd in ((v1, vd1), (v2, vd2), (v3, vd3)):
        vd[:, :LANES] = v[...]

    def run_group(blocks, dests):
        for (st_dst, acc_dst, rows), (st, acc) in zip(dests, _attn_group(blocks)):
            for k in range(N_STATS):
                st_dst[k, rows, :] = st[k]
            acc_dst[rows, :] = acc

    def load_old(st_src, acc_src, rows):
        return [st_src[k, rows, :] for k in range(N_STATS)], acc_src[rows, :]

    def b0_group(ns, first):
        blocks, dests = [], []
        for n in ns:
            if first and n == 0:
                rows = pl.ds(0, blk)
                blocks.append((q1[rows, :], k1[rows, :], vd1[rows, :], mask_c, None))
            else:
                rows = pl.ds(pl.multiple_of(n * blk, blk), blk)
                krows = pl.ds(pl.multiple_of((n - 1) * blk, blk), 2 * blk)
                blocks.append((q1[rows, :], k1[krows, :], vd1[krows, :], mask_pc, None))
            dests.append((st_ref, acc_ref, rows))
        run_group(blocks, dests)

    b0_group(list(range(ATT_GROUP)), True)

    def b0_body(i, carry):
        b0_group([ATT_GROUP * i + j for j in range(ATT_GROUP)], False)
        return carry

    lax.fori_loop(1, nblk // ATT_GROUP, b0_body, 0)

    res_per_trip = ATT_GROUP // per_res

    def b1_body(i, carry):
        blocks, dests = [], []
        for jr in range(res_per_trip):
            a1 = res_per_trip * i + jr
            base = a1 * quarter
            for n in range(per_res):
                rows = pl.ds(a1 + d1 * blk * n, blk, stride=d1)
                qrows = pl.ds(pl.multiple_of(base + n * blk, blk), blk)
                if n == 0:
                    krows, mask = qrows, mask_c
                else:
                    krows, mask = pl.ds(pl.multiple_of(base + (n - 1) * blk, blk), 2 * blk), mask_pc
                blocks.append((qd2[qrows, :], kd2[krows, :], vd2[krows, :], mask,
                               load_old(st_ref, acc_ref, rows)))
                dests.append((st2_ref, acc2_ref, qrows))
        run_group(blocks, dests)
        return carry

    lax.fori_loop(0, d1 // res_per_trip, b1_body, 0)

    def b2_body(i, carry):
        blocks, dests = [], []
        for j in range(ATT_GROUP):
            a1, a2 = j % d1, (ATT_GROUP // d1) * i + j // d1
            qrows = pl.ds(pl.multiple_of((ATT_GROUP * i + j) * blk, blk), blk)
            rows = pl.ds(a1 * quarter + a2, blk, stride=d1)
            blocks.append((qd3[qrows, :], kd3[qrows, :], vd3[qrows, :], mask_c,
                           load_old(st2_ref, acc2_ref, rows)))
            dests.append((st2_ref, acc2_ref, rows))
        run_group(blocks, dests)
        return carry

    lax.fori_loop(0, d2 // ATT_GROUP, b2_body, 0)

    for a in range(d1):
        seg = slice(a * quarter, (a + 1) * quarter)
        tmp_nat[pl.ds(a, quarter, stride=d1), :] = acc2_ref[seg, :] / st2_ref[N_STATS - 1, seg, :]
    o_ref[...] = tmp_nat[...].astype(o_ref.dtype)


def _attention(qs, kvs):
    bsz, seq, _ = qs[0].shape
    steps = D_MODEL // (ATT_PAIRS_PER_STEP * LANES)

    def slab(col0):
        return pl.BlockSpec((None, seq, ATT_PAIRS_PER_STEP * LANES), lambda b, hp: (b, 0, col0 + hp))

    in_specs = [slab(0)] * N_BRANCHES + [slab(0)] * N_BRANCHES + [slab(steps)] * N_BRANCHES
    return pl.pallas_call(
        _attn_kernel,
        grid=(bsz, steps),
        in_specs=in_specs,
        out_specs=slab(0),
        out_shape=jax.ShapeDtypeStruct((bsz, seq, D_MODEL), BF16),
        scratch_shapes=([pltpu.VMEM((seq, LANES), F32)] + [pltpu.VMEM((seq, 2 * LANES), BF16)] * 3
                        + [pltpu.VMEM((seq, LANES), F32), pltpu.VMEM((N_STATS, seq, LANES), F32)] * 2
                        + [pltpu.VMEM((2 * ATT_BLOCK, 2 * ATT_BLOCK), F32), pltpu.VMEM((2 * ATT_BLOCK, ATT_BLOCK), F32)]),
        compiler_params=_cparams(("arbitrary", "arbitrary")),
        name="attention",
    )(*qs, *kvs, *kvs)


def _regroup8(tiles):
    t = list(tiles)
    lane = lax.broadcasted_iota(jnp.int32, t[0].shape, 1)
    piece = lane // SSM_GROUP
    for dist in (4, 2, 1):
        bit = (piece & dist) != 0
        shift = dist * SSM_GROUP
        for a in range(8):
            if a & dist:
                continue
            b = a | dist
            ta, tb = t[a], t[b]
            t[a] = jnp.where(bit, pltpu.roll(tb, shift, 1), ta)
            t[b] = jnp.where(bit, tb, pltpu.roll(ta, LANES - shift, 1))
    return t


def _s5_tile_perm(bsz):
    tt = S5_TILE_TOKENS
    n = bsz * tt
    perm = np.zeros((n, n), np.float32)
    for b in range(bsz):
        for c in range(tt // CHUNK):
            for t in range(CHUNK):
                perm[(t * (tt // CHUNK) + c) * bsz + b, b * tt + c * CHUNK + t] = 1.0
    return perm


def _s5_pre_kernel(*refs, n_ride):
    h_ref, mod_ref, g_ref, perm_ref = refs[:4]
    z_ref = refs[4 + n_ride]
    u_scr, up_scr = refs[5 + 2 * n_ride:]
    _ride_cast(refs[4:4 + n_ride], refs[5 + n_ride:5 + 2 * n_ride])
    d = D_MODEL
    bsz, tt, _ = h_ref.shape
    g = g_ref[...]
    for b in range(bsz):
        u_scr[b * tt:(b + 1) * tt, :] = _normmod(
            h_ref[b], g, mod_ref[b, :, :d], mod_ref[b, :, d:2 * d]).astype(BF16)
    up_scr[...] = jnp.dot(perm_ref[...], u_scr[...], preferred_element_type=F32)
    rows = z_ref.shape[0]
    for j in range(d // LANES):
        for half in range(CHUNK // 8):
            tiles = [up_scr[(8 * half + tl) * rows:(8 * half + tl + 1) * rows, j * LANES:(j + 1) * LANES]
                     for tl in range(8)]
            outs = _regroup8(tiles)
            for gl in range(8):
                col = (8 * j + gl) * GROUP_COLS + half * LANES
                z_ref[:, col:col + LANES] = outs[gl].astype(z_ref.dtype)


def _s5_pre(h, mod, g, perm, ride):
    bsz, seq, d = h.shape
    tt = S5_TILE_TOKENS
    rows = bsz * tt // CHUNK
    ride_in, ride_out, ride_shapes, ride_args = _ride_specs(ride, seq // tt)
    outs = pl.pallas_call(
        functools.partial(_s5_pre_kernel, n_ride=len(ride_in)),
        grid=(seq // tt,),
        in_specs=[
            pl.BlockSpec((bsz, tt, d), lambda i: (0, i, 0)),
            pl.BlockSpec(mod.shape, lambda i: (0, 0, 0)),
            pl.BlockSpec((1, d), lambda i: (0, 0)),
            pl.BlockSpec(perm.shape, lambda i: (0, 0)),
        ] + ride_in,
        out_specs=[pl.BlockSpec((rows, CHUNK * d), lambda i: (i, 0))] + ride_out,
        out_shape=[jax.ShapeDtypeStruct((bsz * seq // CHUNK, CHUNK * d), BF16)] + ride_shapes,
        scratch_shapes=[pltpu.VMEM((bsz * tt, d), BF16), pltpu.VMEM((bsz * tt, d), F32)],
        compiler_params=_cparams(("parallel",)),
        name="s5_pre",
    )(h, mod, g.reshape(1, d), perm, *ride_args)
    return outs[0], outs[1:]


def _s5_core_kernel(z_ref, win_ref, wt_ref, wx_ref, dec_ref, d_ref, o_ref, s_scr, x_scr, *, bsz):
    gc = GROUP_COLS
    npair = S5_PAIRS_PER_STEP
    nchunk = z_ref.shape[0] // bsz
    for q in range(npair):
        s_scr[:, q * gc:(q + 1) * gc] = jnp.dot(z_ref[:, q * PAIR_COLS:(q + 1) * PAIR_COLS],
                                                win_ref[2 * q:2 * q + 2].reshape(2 * gc, gc),
                                                preferred_element_type=F32)
    re_cols = [slice(q * gc, q * gc + LANES) for q in range(npair)]
    im_cols = [slice(q * gc + LANES, (q + 1) * gc) for q in range(npair)]
    ar = [dec_ref[:, cols] for cols in re_cols]
    ai = [dec_ref[:, cols] for cols in im_cols]

    def step(c, carry):
        rows = pl.ds(pl.multiple_of(c * bsz, bsz), bsz)
        new = []
        for q in range(npair):
            xr, xi = carry[2 * q], carry[2 * q + 1]
            x_scr[rows, re_cols[q]] = xr
            x_scr[rows, im_cols[q]] = xi
            new += [ar[q] * xr - ai[q] * xi + s_scr[rows, re_cols[q]],
                    ar[q] * xi + ai[q] * xr + s_scr[rows, im_cols[q]]]
        return tuple(new)

    zero = jnp.zeros((bsz, LANES), F32)
    lax.fori_loop(0, nchunk, step, (zero,) * (2 * npair))
    for g in range(2 * npair):
        z = z_ref[:, g * gc:(g + 1) * gc]
        x = x_scr[:, (g // 2) * gc:(g // 2 + 1) * gc].astype(BF16)
        y = jnp.dot(jnp.concatenate([z, x], axis=1), jnp.concatenate([wt_ref[g], wx_ref[g]], axis=0),
                    preferred_element_type=F32)
        y = y + d_ref[:, g * gc:(g + 1) * gc] * z.astype(F32)
        o_ref[:, g * gc:(g + 1) * gc] = jax.nn.gelu(y).astype(o_ref.dtype)


def _s5_core(z, tables, layer, bsz):
    nrow, width = z.shape
    w_in, w_toep, w_x, dec, d_perm = tables
    npair = S5_PAIRS_PER_STEP
    wspec = pl.BlockSpec((None, 2 * npair, GROUP_COLS, GROUP_COLS), lambda k: (layer, k, 0, 0))
    return pl.pallas_call(
        functools.partial(_s5_core_kernel, bsz=bsz),
        grid=(N_PAIRS // npair,),
        in_specs=[
            pl.BlockSpec((nrow, npair * PAIR_COLS), lambda k: (0, k)),
            wspec, wspec, wspec,
            pl.BlockSpec((None, 1, npair * GROUP_COLS), lambda k: (layer, 0, k)),
            pl.BlockSpec((None, 1, npair * PAIR_COLS), lambda k: (layer, 0, k)),
        ],
        out_specs=pl.BlockSpec((nrow, npair * PAIR_COLS), lambda k: (0, k)),
        out_shape=jax.ShapeDtypeStruct((nrow, width), BF16),
        scratch_shapes=[pltpu.VMEM((nrow, npair * GROUP_COLS), F32), pltpu.VMEM((nrow, npair * GROUP_COLS), F32)],
        compiler_params=_cparams(("parallel",)),
        name="s5_core",
    )(z, w_in, w_toep, w_x, dec, d_perm)


def _s5_weights(lam_re, lam_im, log_dt, b_re, b_im, c_re, c_im, d_skip):
    g, p, c16 = SSM_GROUPS, SSM_STATE, SSM_GROUP
    lam = lax.complex(lam_re.astype(F32), lam_im.astype(F32))
    dt = jnp.exp(log_dt.astype(F32))[:, None]
    steps = jnp.arange(CHUNK + 1, dtype=F32)
    apow = jnp.exp((lam * dt)[None] * steps[:, None, None])
    a = apow[1]
    bbar = ((a - 1.0) / lam)[..., None] * lax.complex(b_re.astype(F32), b_im.astype(F32))
    cmat = lax.complex(c_re.astype(F32), c_im.astype(F32))
    odd = (jnp.arange(g) % 2)[:, None, None]

    win = apow[CHUNK - 1::-1][..., None] * bbar[None]

    def lay_in(x):
        return jnp.transpose(x.astype(BF16), (1, 0, 3, 2)).reshape(g, GROUP_COLS, p)

    win_re, win_im = lay_in(win.real), lay_in(win.imag)
    zeros = jnp.zeros_like(win_re)
    w_in = jnp.concatenate([
        jnp.where(odd == 0, win_re, zeros), jnp.where(odd == 1, win_re, zeros),
        jnp.where(odd == 0, win_im, zeros), jnp.where(odd == 1, win_im, zeros)], axis=-1)

    kern = jnp.einsum('gop,kgp,gpi->kgoi', cmat, apow[:CHUNK], bbar, precision=HIGHEST).real
    idx = jnp.arange(CHUNK)
    onehot = (idx[None, None, :] - idx[None, :, None] == idx[:, None, None]).astype(BF16)
    w_toep = jnp.einsum('kst,kgoi->gsito', onehot, kern.astype(BF16),
                        preferred_element_type=BF16)
    w_toep = w_toep.reshape(g, GROUP_COLS, GROUP_COLS)

    cw = cmat[None] * apow[1:, :, None, :]

    def lay_x(x):
        return jnp.transpose(x.astype(BF16), (1, 3, 0, 2)).reshape(g, p, GROUP_COLS)

    cw_re, cw_im = lay_x(cw.real), lay_x(-cw.imag)
    zx = jnp.zeros_like(cw_re)
    w_x = jnp.concatenate([
        jnp.where(odd == 0, cw_re, zx), jnp.where(odd == 1, cw_re, zx),
        jnp.where(odd == 0, cw_im, zx), jnp.where(odd == 1, cw_im, zx)], axis=1)

    a16 = apow[CHUNK].reshape(N_PAIRS, 2 * p)
    dec = jnp.concatenate([a16.real, a16.imag], axis=-1).reshape(1, N_PAIRS * GROUP_COLS)
    d_perm = jnp.broadcast_to(d_skip.astype(F32).reshape(g, 1, c16), (g, CHUNK, c16)).reshape(1, g * GROUP_COLS)
    return w_in, w_toep, w_x, dec, d_perm


def _mlp(u, w1_ref, w2_ref):
    d = D_MODEL
    acc = None
    for k in range(D_FF // d):
        a = jnp.dot(u, w1_ref[:, k * d:(k + 1) * d], preferred_element_type=F32)
        a = jnp.square(jnp.maximum(a, 0.0)).astype(BF16)
        part = jnp.dot(a, w2_ref[k * d:(k + 1) * d, :], preferred_element_type=F32)
        acc = part if acc is None else acc + part
    return acc


def _final_norm(h, fg_ref):
    ms = jnp.mean(h * h, axis=-1, keepdims=True)
    return h * lax.rsqrt(ms + EPS) * fg_ref[...]


def _ride_specs(ride, nsteps):
    ins, outs, shapes = [], [], []
    for st, layer in ride:
        _, r, c = st.shape
        assert r % nsteps == 0
        ins.append(pl.BlockSpec((None, r // nsteps, c), lambda i, layer=layer: (layer, i, 0)))
        outs.append(pl.BlockSpec((r // nsteps, c), lambda i: (i, 0)))
        shapes.append(jax.ShapeDtypeStruct((r, c), BF16))
    return ins, outs, shapes, [st for st, _ in ride]


def _ride_cast(ride_in, ride_out):
    for src, dst in zip(ride_in, ride_out):
        dst[...] = src[...].astype(dst.dtype)


def _post_attn_kernel(*refs, final, n_ride):
    h_ref, y_ref, moda_ref, modm_ref, g_ref, wp_ref, w1_ref, w2_ref = refs[:8]
    pos = 8 + int(final)
    fg_ref = refs[8] if final else None
    o_ref = refs[pos + n_ride]
    _ride_cast(refs[pos:pos + n_ride], refs[pos + n_ride + 1:])
    d = D_MODEL
    ymix = jnp.dot(y_ref[...], wp_ref[...], preferred_element_type=F32)
    h1 = h_ref[...] + moda_ref[:, 2 * d:3 * d] * ymix
    u = _normmod(h1, g_ref[...], modm_ref[:, :d], modm_ref[:, d:2 * d]).astype(BF16)
    h2 = h1 + modm_ref[:, 2 * d:3 * d] * _mlp(u, w1_ref, w2_ref)
    o_ref[...] = _final_norm(h2, fg_ref) if final else h2


def _post_attn(h, o, moda, modm, g2, w_o, w1, w2, final_g, ride):
    bsz, seq, d = h.shape
    tm = MLP_ROWS
    per_b = seq // tm
    row_spec = pl.BlockSpec((None, tm, d), lambda i: (i // per_b, i % per_b, 0))
    final = final_g is not None
    ride_in, ride_out, ride_shapes, ride_args = _ride_specs(ride, bsz * per_b)
    in_specs = [
        row_spec, row_spec,
        pl.BlockSpec((None, 1, moda.shape[-1]), lambda i: (i // per_b, 0, 0)),
        pl.BlockSpec((None, 1, modm.shape[-1]), lambda i: (i // per_b, 0, 0)),
        pl.BlockSpec((1, d), lambda i: (0, 0)),
        _layer_spec(*w_o), _layer_spec(*w1), _layer_spec(*w2),
    ]
    args = [h, o, moda, modm, g2.reshape(1, d), w_o[0], w1[0], w2[0]]
    if final:
        in_specs.append(pl.BlockSpec((1, d), lambda i: (0, 0)))
        args.append(final_g.reshape(1, d))
    args += ride_args
    outs = pl.pallas_call(
        functools.partial(_post_attn_kernel, final=final, n_ride=len(ride_in)),
        grid=(bsz * per_b,),
        in_specs=in_specs + ride_in,
        out_specs=[row_spec] + ride_out,
        out_shape=[jax.ShapeDtypeStruct((bsz, seq, d), F32)] + ride_shapes,
        compiler_params=_cparams(("parallel",)),
        name="post_attn_mlp",
    )(*args)
    return outs[0], outs[1:]


def _post_s5_kernel(*refs, n_ride):
    h_ref, z_ref, moda_ref, modm_ref, g_ref, permt_ref, wp_ref, w1_ref, w2_ref = refs[:9]
    o_ref = refs[9 + n_ride]
    z_scr, h1_scr, u_scr = refs[10 + 2 * n_ride:]
    _ride_cast(refs[9:9 + n_ride], refs[10 + n_ride:10 + 2 * n_ride])
    d = D_MODEL
    bsz, tt, _ = h_ref.shape
    rows = z_ref.shape[0]
    for j in range(d // LANES):
        for half in range(CHUNK // 8):
            tiles = []
            for gl in range(8):
                col = (8 * j + gl) * GROUP_COLS + half * LANES
                tiles.append(z_ref[:, col:col + LANES].astype(F32))
            outs = _regroup8(tiles)
            for tl in range(8):
                t = 8 * half + tl
                z_scr[t * rows:(t + 1) * rows, j * LANES:(j + 1) * LANES] = outs[tl].astype(BF16)
    zn = jnp.dot(permt_ref[...], z_scr[...], preferred_element_type=F32).astype(BF16)
    zz = jnp.dot(zn, wp_ref[...], preferred_element_type=F32)
    ymix = zz[:, :d] * jax.nn.sigmoid(zz[:, d:])
    g = g_ref[...]
    for b in range(bsz):
        sl = slice(b * tt, (b + 1) * tt)
        h1 = h_ref[b] + moda_ref[b, :, 2 * d:3 * d] * ymix[sl, :]
        h1_scr[sl, :] = h1
        u_scr[sl, :] = _normmod(h1, g, modm_ref[b, :, :d], modm_ref[b, :, d:2 * d]).astype(BF16)
    acc = _mlp(u_scr[...], w1_ref, w2_ref)
    for b in range(bsz):
        sl = slice(b * tt, (b + 1) * tt)
        o_ref[b] = h1_scr[sl, :] + modm_ref[b, :, 2 * d:3 * d] * acc[sl, :]


def _post_s5(h, zo, moda, modm, g2, permt, w_glu, w1, w2, ride):
    bsz, seq, d = h.shape
    tt = S5_TILE_TOKENS
    rows = bsz * tt // CHUNK
    h_spec = pl.BlockSpec((bsz, tt, d), lambda i: (0, i, 0))
    ride_in, ride_out, ride_shapes, ride_args = _ride_specs(ride, seq // tt)
    outs = pl.pallas_call(
        functools.partial(_post_s5_kernel, n_ride=len(ride_in)),
        grid=(seq // tt,),
        in_specs=[
            h_spec,
            pl.BlockSpec((rows, CHUNK * d), lambda i: (i, 0)),
            pl.BlockSpec(moda.shape, lambda i: (0, 0, 0)),
            pl.BlockSpec(modm.shape, lambda i: (0, 0, 0)),
            pl.BlockSpec((1, d), lambda i: (0, 0)),
            pl.BlockSpec(permt.shape, lambda i: (0, 0)),
            _layer_spec(*w_glu), _layer_spec(*w1), _layer_spec(*w2),
        ] + ride_in,
        out_specs=[h_spec] + ride_out,
        out_shape=[jax.ShapeDtypeStruct((bsz, seq, d), F32)] + ride_shapes,
        scratch_shapes=[pltpu.VMEM((bsz * tt, d), BF16), pltpu.VMEM((bsz * tt, d), F32),
                        pltpu.VMEM((bsz * tt, d), BF16)],
        compiler_params=_cparams(("parallel",)),
        name="post_s5_mlp",
    )(h, zo, moda, modm, g2.reshape(1, d), permt, w_glu[0], w1[0], w2[0], *ride_args)
    return outs[0], outs[1:]


def kernel(x, c, ln_g, ada_w, ada_b, ssm_lam_re, ssm_lam_im, ssm_log_dt, ssm_b_re, ssm_b_im, ssm_c_re, ssm_c_im, ssm_d, ssm_w_glu, kv_g, kv_ada_w, kv_ada_b, w_kv, attn_w_q, attn_w_o, mlp_w1, mlp_w2, final_g):
    bsz, seq, d = x.shape
    depth = ln_g.shape[0]
    n_s5 = ssm_lam_re.shape[0]
    assert d == D_MODEL and seq % (DILATIONS[-1] * ATT_BLOCK) == 0 and 0 < n_s5 < depth

    mods = _ada_mods(c, ada_w.reshape(depth * 2, d, 3 * d), ada_b.reshape(depth * 2, 3 * d))
    mods = mods.reshape(depth, 2, bsz, 1, 3 * d)
    kv_mod = _ada_mods(c, kv_ada_w[None], kv_ada_b[None]).reshape(bsz, 1, 2 * d)

    perm_np = _s5_tile_perm(bsz)
    perm = jnp.asarray(perm_np, BF16)
    permt = jnp.asarray(perm_np.T, BF16)
    tables = jax.vmap(_s5_weights)(ssm_lam_re, ssm_lam_im, ssm_log_dt, ssm_b_re, ssm_b_im,
                                   ssm_c_re, ssm_c_im, ssm_d)

    def as_layer(cast):
        return (cast[None], 0)

    h = x
    for layer in range(n_s5):
        ride = [(mlp_w1, 0), (mlp_w2, 0), (ssm_w_glu, 0)] if layer == 0 else []
        z, cast = _s5_pre(h, mods[layer, 0], ln_g[layer, 0], perm, ride)
        if layer == 0:
            w1, w2, w_glu = map(as_layer, cast)
        zo = _s5_core(z, tables, layer, bsz)
        ride = [(mlp_w1, layer + 1), (mlp_w2, layer + 1)]
        if layer + 1 < n_s5:
            ride += [(ssm_w_glu, layer + 1)]
        else:
            ride += [(w_kv[None], 0), (attn_w_q, 0), (attn_w_o, 0)]
        h, cast = _post_s5(h, zo, mods[layer, 0], mods[layer, 1], ln_g[layer, 1], permt, w_glu, w1, w2, ride)
        w1, w2 = as_layer(cast[0]), as_layer(cast[1])
        if layer + 1 < n_s5:
            w_glu = as_layer(cast[2])
        else:
            w_kvb, w_q, w_o = map(as_layer, cast[2:])

    kv_cols = tuple((i * d, (N_BRANCHES + i) * d) for i in range(N_BRANCHES))
    q_cols = tuple((i * d,) for i in range(N_BRANCHES))
    kvs = _norm_proj(h, kv_mod, kv_g, w_kvb, kv_cols, 1.0, KV_PROJ_TILE)
    for layer in range(n_s5, depth):
        j = layer - n_s5
        last = layer == depth - 1
        qs = _norm_proj(h, mods[layer, 0], ln_g[layer, 0], w_q, q_cols, Q_SCALE, Q_PROJ_TILE)
        o = _attention(qs, kvs)
        ride = [] if last else [(mlp_w1, layer + 1), (mlp_w2, layer + 1), (attn_w_q, j + 1), (attn_w_o, j + 1)]
        h, cast = _post_attn(h, o, mods[layer, 0], mods[layer, 1], ln_g[layer, 1], w_o, w1, w2,
                             final_g if last else None, ride)
        if not last:
            w1, w2, w_q, w_o = map(as_layer, cast)
    return h
```

```python
import functools
import math

import numpy as np

import jax
import jax.numpy as jnp
from jax import lax
from jax.experimental import pallas as pl
from jax.experimental.pallas import tpu as pltpu

F32 = jnp.float32
BF16 = jnp.bfloat16

D_MODEL = 1024
SSM_GROUP = 16
SSM_GROUPS = D_MODEL // SSM_GROUP
SSM_STATE = 64
HEAD_DIM = 64
DILATIONS = (1, 4, 16)
N_BRANCHES = len(DILATIONS)
ATT_BLOCK = 128
ATT_GROUP = 16
ATT_PAIRS_PER_STEP = 2
N_STATS = 3
Q_SCALE = HEAD_DIM ** -0.5 * math.log2(math.e)
D_FF = 4 * D_MODEL
EPS = 1e-6
NEG = -1e30

LANES = 128
CHUNK = 16
GROUP_COLS = CHUNK * SSM_GROUP
PAIR_COLS = 2 * GROUP_COLS
N_PAIRS = SSM_GROUPS // 2
S5_PAIRS_PER_STEP = 2
S5_TILE_TOKENS = 2 * CHUNK
MLP_ROWS = 1024
KV_PROJ_TILE = 512
Q_PROJ_TILE = 1024
ADA_CHUNK = 1024
ADA_SLOTS = 3
VMEM_LIMIT = 56 * 1024 * 1024

HIGHEST = lax.Precision.HIGHEST
NT_DIMS = (((1,), (1,)), ((), ()))


def _cparams(sem):
    return pltpu.CompilerParams(dimension_semantics=sem, vmem_limit_bytes=VMEM_LIMIT)


def _layer_spec(stack, layer):
    zeros = (0,) * (stack.ndim - 1)
    return pl.BlockSpec((None,) + stack.shape[1:], lambda i: (layer,) + zeros, pipeline_mode=pl.Buffered(1))


def _normmod(x, g, shift, scale):
    ms = jnp.mean(x * x, axis=-1, keepdims=True)
    return (x * lax.rsqrt(ms + EPS) * g) * (1.0 + scale) + shift


def _ada_kernel(c_ref, w_hbm, b_ref, o_ref, buf, sem):
    n, _, width = w_hbm.shape
    tn = buf.shape[-1]
    per = width // tn
    total = n * per

    def chunk_copy(i):
        cols = pl.ds((i % per) * tn, tn)
        return pltpu.make_async_copy(w_hbm.at[i // per, :, cols], buf.at[i % ADA_SLOTS], sem.at[i % ADA_SLOTS])

    for i in range(min(ADA_SLOTS - 1, total)):
        chunk_copy(i).start()
    c = c_ref[...]
    sc = (c * jax.nn.sigmoid(c)).astype(BF16)
    for i in range(total):
        if i + ADA_SLOTS - 1 < total:
            chunk_copy(i + ADA_SLOTS - 1).start()
        chunk_copy(i).wait()
        cols = slice((i % per) * tn, (i % per + 1) * tn)
        res = jnp.dot(sc, buf[i % ADA_SLOTS].astype(BF16), preferred_element_type=F32)
        o_ref[i // per, :, cols] = res + b_ref[i // per, :, cols]


def _ada_mods(c, w, b):
    n, d, width = w.shape
    bsz = c.shape[0]
    assert width % ADA_CHUNK == 0
    return pl.pallas_call(
        _ada_kernel,
        in_specs=[
            pl.BlockSpec(memory_space=pltpu.VMEM),
            pl.BlockSpec(memory_space=pl.ANY),
            pl.BlockSpec(memory_space=pltpu.VMEM),
        ],
        out_specs=pl.BlockSpec(memory_space=pltpu.VMEM),
        out_shape=jax.ShapeDtypeStruct((n, bsz, width), F32),
        scratch_shapes=[pltpu.VMEM((ADA_SLOTS, d, ADA_CHUNK), F32), pltpu.SemaphoreType.DMA((ADA_SLOTS,))],
        compiler_params=pltpu.CompilerParams(vmem_limit_bytes=VMEM_LIMIT),
        name="ada_mods",
    )(c, w, b.reshape(n, 1, width))


def _proj_kernel(h_ref, mod_ref, g_ref, w_ref, o1_ref, o2_ref, o3_ref, us, us4, up4, up16, *, wcols, out_scale):
    d = D_MODEL
    tm = h_ref.shape[0]
    d1 = DILATIONS[1]
    q4, q16 = tm // d1, tm // DILATIONS[2]
    cw = o1_ref.shape[1]
    u = _normmod(h_ref[...], g_ref[...], mod_ref[:, :d], mod_ref[:, d:2 * d])
    for l in range(d // LANES):
        us[l] = u[:, l * LANES:(l + 1) * LANES]
    ub = u.astype(BF16)
    for n in range(cw // d):
        cols = slice(n * d, (n + 1) * d)
        res = jnp.dot(ub, w_ref[:, wcols[0][n]:wcols[0][n] + d], preferred_element_type=F32)
        o1_ref[:, cols] = (res * out_scale).astype(o1_ref.dtype)
    for l in range(d // LANES):
        for a in range(d1):
            part = us[l, pl.ds(a, q4, stride=d1), :]
            us4[l, a * q4:(a + 1) * q4, :] = part
            up4[a * q4:(a + 1) * q4, l * LANES:(l + 1) * LANES] = part.astype(BF16)
    for n in range(cw // d):
        res = jnp.dot(up4[...], w_ref[:, wcols[1][n]:wcols[1][n] + d], preferred_element_type=F32)
        for a in range(d1):
            o2_ref[a, :, n * d:(n + 1) * d] = (res[a * q4:(a + 1) * q4, :] * out_scale).astype(o2_ref.dtype)
    for l in range(d // LANES):
        for a1 in range(d1):
            for a2 in range(d1):
                r = d1 * a2 + a1
                up16[r * q16:(r + 1) * q16, l * LANES:(l + 1) * LANES] = us4[
                    l, pl.ds(a1 * q4 + a2, q16, stride=d1), :].astype(BF16)
    for n in range(cw // d):
        res = jnp.dot(up16[...], w_ref[:, wcols[2][n]:wcols[2][n] + d], preferred_element_type=F32)
        for r in range(DILATIONS[2]):
            o3_ref[r, :, n * d:(n + 1) * d] = (res[r * q16:(r + 1) * q16, :] * out_scale).astype(o3_ref.dtype)


def _norm_proj(h, mod, g, w, wcols, out_scale, tm):
    bsz, seq, d = h.shape
    cw = len(wcols[0]) * d
    per_b = seq // tm
    d1, d2 = DILATIONS[1], DILATIONS[2]
    o1, o2, o3 = pl.pallas_call(
        functools.partial(_proj_kernel, wcols=wcols, out_scale=out_scale),
        grid=(bsz * per_b,),
        in_specs=[
            pl.BlockSpec((None, tm, d), lambda i: (i // per_b, i % per_b, 0)),
            pl.BlockSpec((None, 1, mod.shape[-1]), lambda i: (i // per_b, 0, 0)),
            pl.BlockSpec((1, d), lambda i: (0, 0)),
            _layer_spec(*w),
        ],
        out_specs=[
            pl.BlockSpec((None, tm, cw), lambda i: (i // per_b, i % per_b, 0)),
            pl.BlockSpec((None, d1, None, tm // d1, cw), lambda i: (i // per_b, 0, i % per_b, 0, 0)),
            pl.BlockSpec((None, d2, None, tm // d2, cw), lambda i: (i // per_b, 0, i % per_b, 0, 0)),
        ],
        out_shape=[
            jax.ShapeDtypeStruct((bsz, seq, cw), BF16),
            jax.ShapeDtypeStruct((bsz, d1, per_b, tm // d1, cw), BF16),
            jax.ShapeDtypeStruct((bsz, d2, per_b, tm // d2, cw), BF16),
        ],
        scratch_shapes=[pltpu.VMEM((d // LANES, tm, LANES), F32), pltpu.VMEM((d // LANES, tm, LANES), F32),
                        pltpu.VMEM((tm, d), BF16), pltpu.VMEM((tm, d), BF16)],
        compiler_params=_cparams(("parallel",)),
        name="norm_proj",
    )(h, mod, g.reshape(1, d), w[0])
    return o1, o2.reshape(bsz, seq, cw), o3.reshape(bsz, seq, cw)


def _attn_group(blocks):
    blk = ATT_BLOCK
    lane = lax.broadcasted_iota(jnp.int32, (blk, LANES), 1)
    lo = lane < HEAD_DIM
    scores = []
    for qb, kcat, _, mask, _ in blocks:
        zero = jnp.zeros_like(qb)
        q2 = jnp.concatenate([jnp.where(lo, qb, zero), jnp.where(lo, zero, qb)], axis=0)
        s = lax.dot_general(q2, kcat, NT_DIMS, preferred_element_type=F32)
        scores.append(s + mask[...])
    soft = []
    for s, (_, _, _, _, old) in zip(scores, blocks):
        ms, ps = [], []
        for hh in range(2):
            tiles = [s[hh * blk:(hh + 1) * blk, t * LANES:(t + 1) * LANES] for t in range(s.shape[1] // LANES)]
            mx = tiles[0]
            for t in tiles[1:]:
                mx = jnp.maximum(mx, t)
            m = jnp.broadcast_to(jnp.max(mx, axis=-1, keepdims=True), (blk, LANES))
            if old is not None:
                m = jnp.maximum(m, old[0][hh])
            ms.append(m)
            ps.append(jnp.concatenate([jnp.exp2(t - m).astype(BF16) for t in tiles], axis=1))
        alpha = None
        if old is not None:
            alpha = jnp.exp2(jnp.where(lo, old[0][0], old[0][1]) - jnp.where(lo, ms[0], ms[1]))
        soft.append((ms, jnp.concatenate(ps, axis=0), alpha))
    outs = []
    for (ms, p, alpha), (_, _, vcat, _, old) in zip(soft, blocks):
        pv = jnp.dot(p, vcat, preferred_element_type=F32)
        acc = jnp.where(lo, pv[:blk, :LANES], pv[blk:, :LANES])
        den = jnp.where(lo, pv[:blk, LANES:], pv[blk:, LANES:])
        if old is not None:
            acc = alpha * old[1] + acc
            den = alpha * old[0][2] + den
        outs.append((ms + [den], acc))
    return outs


def _attn_kernel(*refs):
    n_in = 3 * N_BRANCHES
    ins, o_ref, scratch = refs[:n_in], refs[n_in], refs[n_in + 1:]
    for half in range(ATT_PAIRS_PER_STEP):
        lanes = pl.ds(half * LANES, LANES)
        _attn_pair(*[r.at[:, lanes] for r in ins], o_ref.at[:, lanes], *scratch)


def _attn_pair(q1, qd2, qd3, k1, kd2, kd3, v1, v2, v3, o_ref, tmp_nat, vd1, vd2, vd3,
               acc_ref, st_ref, acc2_ref, st2_ref, mask_pc, mask_c):
    blk = ATT_BLOCK
    seq = q1.shape[0]
    nblk = seq // blk
    d1, d2 = DILATIONS[1], DILATIONS[2]
    quarter = seq // d1
    per_res = quarter // blk
    assert d2 == d1 * d1 and seq == d2 * blk and nblk % ATT_GROUP == 0 and ATT_GROUP % per_res == 0

    @pl.when(jnp.logical_and(pl.program_id(0) == 0, pl.program_id(1) == 0))
    def _():
        for vd in (vd1, vd2, vd3):
            vd[:, LANES:] = jnp.ones((seq, LANES), BF16)
        qq = lax.broadcasted_iota(jnp.int32, (2 * blk, 2 * blk), 0) & (blk - 1)
        kk = lax.broadcasted_iota(jnp.int32, (2 * blk, 2 * blk), 1)
        valid = jnp.logical_or(jnp.logical_and(kk < blk, kk >= qq), jnp.logical_and(kk >= blk, kk - blk <= qq))
        mask_pc[...] = jnp.where(valid, 0.0, NEG)
        valid_c = (lax.broadcasted_iota(jnp.int32, (2 * blk, blk), 1)
                   <= (lax.broadcasted_iota(jnp.int32, (2 * blk, blk), 0) & (blk - 1)))
        mask_c[...] = jnp.where(valid_c, 0.0, NEG)

    for v, vd in ((v1, vd1), (v2, vd2), (v3, vd3)):
        vd[:, :LANES] = v[...]

    def run_group(blocks, dests):
        for (st_dst, acc_dst, rows), (st, acc) in zip(dests, _attn_group(blocks)):
            for k in range(N_STATS):
                st_dst[k, rows, :] = st[k]
            acc_dst[rows, :] = acc

    def load_old(st_src, acc_src, rows):
        return [st_src[k, rows, :] for k in range(N_STATS)], acc_src[rows, :]

    def b0_group(ns, first):
        blocks, dests = [], []
        for n in ns:
            if first and n == 0:
                rows = pl.ds(0, blk)
                blocks.append((q1[rows, :], k1[rows, :], vd1[rows, :], mask_c, None))
            else:
                rows = pl.ds(pl.multiple_of(n * blk, blk), blk)
                krows = pl.ds(pl.multiple_of((n - 1) * blk, blk), 2 * blk)
                blocks.append((q1[rows, :], k1[krows, :], vd1[krows, :], mask_pc, None))
            dests.append((st_ref, acc_ref, rows))
        run_group(blocks, dests)

    b0_group(list(range(ATT_GROUP)), True)

    def b0_body(i, carry):
        b0_group([ATT_GROUP * i + j for j in range(ATT_GROUP)], False)
        return carry

    lax.fori_loop(1, nblk // ATT_GROUP, b0_body, 0)

    res_per_trip = ATT_GROUP // per_res

    def b1_body(i, carry):
        blocks, dests = [], []
        for jr in range(res_per_trip):
            a1 = res_per_trip * i + jr
            base = a1 * quarter
            for n in range(per_res):
                rows = pl.ds(a1 + d1 * blk * n, blk, stride=d1)
                qrows = pl.ds(pl.multiple_of(base + n * blk, blk), blk)
                if n == 0:
                    krows, mask = qrows, mask_c
                else:
                    krows, mask = pl.ds(pl.multiple_of(base + (n - 1) * blk, blk), 2 * blk), mask_pc
                blocks.append((qd2[qrows, :], kd2[krows, :], vd2[krows, :], mask,
                               load_old(st_ref, acc_ref, rows)))
                dests.append((st2_ref, acc2_ref, qrows))
        run_group(blocks, dests)
        return carry

    lax.fori_loop(0, d1 // res_per_trip, b1_body, 0)

    def b2_body(i, carry):
        blocks, dests = [], []
        for j in range(ATT_GROUP):
            a1, a2 = j % d1, (ATT_GROUP // d1) * i + j // d1
            qrows = pl.ds(pl.multiple_of((ATT_GROUP * i + j) * blk, blk), blk)
            rows = pl.ds(a1 * quarter + a2, blk, stride=d1)
            blocks.append((qd3[qrows, :], kd3[qrows, :], vd3[qrows, :], mask_c,
                           load_old(st2_ref, acc2_ref, rows)))
            dests.append((st2_ref, acc2_ref, rows))
        run_group(blocks, dests)
        return carry

    lax.fori_loop(0, d2 // ATT_GROUP, b2_body, 0)

    for a in range(d1):
        seg = slice(a * quarter, (a + 1) * quarter)
        tmp_nat[pl.ds(a, quarter, stride=d1), :] = acc2_ref[seg, :] / st2_ref[N_STATS - 1, seg, :]
    o_ref[...] = tmp_nat[...].astype(o_ref.dtype)


def _attention(qs, kvs):
    bsz, seq, _ = qs[0].shape
    steps = D_MODEL // (ATT_PAIRS_PER_STEP * LANES)

    def slab(col0):
        return pl.BlockSpec((None, seq, ATT_PAIRS_PER_STEP * LANES), lambda b, hp: (b, 0, col0 + hp))

    in_specs = [slab(0)] * N_BRANCHES + [slab(0)] * N_BRANCHES + [slab(steps)] * N_BRANCHES
    return pl.pallas_call(
        _attn_kernel,
        grid=(bsz, steps),
        in_specs=in_specs,
        out_specs=slab(0),
        out_shape=jax.ShapeDtypeStruct((bsz, seq, D_MODEL), BF16),
        scratch_shapes=([pltpu.VMEM((seq, LANES), F32)] + [pltpu.VMEM((seq, 2 * LANES), BF16)] * 3
                        + [pltpu.VMEM((seq, LANES), F32), pltpu.VMEM((N_STATS, seq, LANES), F32)] * 2
                        + [pltpu.VMEM((2 * ATT_BLOCK, 2 * ATT_BLOCK), F32), pltpu.VMEM((2 * ATT_BLOCK, ATT_BLOCK), F32)]),
        compiler_params=_cparams(("arbitrary", "arbitrary")),
        name="attention",
    )(*qs, *kvs, *kvs)


def _regroup8(tiles):
    t = list(tiles)
    lane = lax.broadcasted_iota(jnp.int32, t[0].shape, 1)
    piece = lane // SSM_GROUP
    for dist in (4, 2, 1):
        bit = (piece & dist) != 0
        shift = dist * SSM_GROUP
        for a in range(8):
            if a & dist:
                continue
            b = a | dist
            ta, tb = t[a], t[b]
            t[a] = jnp.where(bit, pltpu.roll(tb, shift, 1), ta)
            t[b] = jnp.where(bit, tb, pltpu.roll(ta, LANES - shift, 1))
    return t


def _s5_tile_perm(bsz):
    tt = S5_TILE_TOKENS
    n = bsz * tt
    perm = np.zeros((n, n), np.float32)
    for b in range(bsz):
        for c in range(tt // CHUNK):
            for t in range(CHUNK):
                perm[(t * (tt // CHUNK) + c) * bsz + b, b * tt + c * CHUNK + t] = 1.0
    return perm


def _s5_pre_kernel(*refs, n_ride):
    h_ref, mod_ref, g_ref, perm_ref = refs[:4]
    z_ref = refs[4 + n_ride]
    u_scr, up_scr = refs[5 + 2 * n_ride:]
    _ride_cast(refs[4:4 + n_ride], refs[5 + n_ride:5 + 2 * n_ride])
    d = D_MODEL
    bsz, tt, _ = h_ref.shape
    g = g_ref[...]
    for b in range(bsz):
        u_scr[b * tt:(b + 1) * tt, :] = _normmod(
            h_ref[b], g, mod_ref[b, :, :d], mod_ref[b, :, d:2 * d]).astype(BF16)
    up_scr[...] = jnp.dot(perm_ref[...], u_scr[...], preferred_element_type=F32)
    rows = z_ref.shape[0]
    for j in range(d // LANES):
        for half in range(CHUNK // 8):
            tiles = [up_scr[(8 * half + tl) * rows:(8 * half + tl + 1) * rows, j * LANES:(j + 1) * LANES]
                     for tl in range(8)]
            outs = _regroup8(tiles)
            for gl in range(8):
                col = (8 * j + gl) * GROUP_COLS + half * LANES
                z_ref[:, col:col + LANES] = outs[gl].astype(z_ref.dtype)


def _s5_pre(h, mod, g, perm, ride):
    bsz, seq, d = h.shape
    tt = S5_TILE_TOKENS
    rows = bsz * tt // CHUNK
    ride_in, ride_out, ride_shapes, ride_args = _ride_specs(ride, seq // tt)
    outs = pl.pallas_call(
        functools.partial(_s5_pre_kernel, n_ride=len(ride_in)),
        grid=(seq // tt,),
        in_specs=[
            pl.BlockSpec((bsz, tt, d), lambda i: (0, i, 0)),
            pl.BlockSpec(mod.shape, lambda i: (0, 0, 0)),
            pl.BlockSpec((1, d), lambda i: (0, 0)),
            pl.BlockSpec(perm.shape, lambda i: (0, 0)),
        ] + ride_in,
        out_specs=[pl.BlockSpec((rows, CHUNK * d), lambda i: (i, 0))] + ride_out,
        out_shape=[jax.ShapeDtypeStruct((bsz * seq // CHUNK, CHUNK * d), BF16)] + ride_shapes,
        scratch_shapes=[pltpu.VMEM((bsz * tt, d), BF16), pltpu.VMEM((bsz * tt, d), F32)],
        compiler_params=_cparams(("parallel",)),
        name="s5_pre",
    )(h, mod, g.reshape(1, d), perm, *ride_args)
    return outs[0], outs[1:]


def _s5_core_kernel(z_ref, win_ref, wt_ref, wx_ref, dec_ref, d_ref, o_ref, s_scr, x_scr, *, bsz):
    gc = GROUP_COLS
    npair = S5_PAIRS_PER_STEP
    nchunk = z_ref.shape[0] // bsz
    for q in range(npair):
        s_scr[:, q * gc:(q + 1) * gc] = jnp.dot(z_ref[:, q * PAIR_COLS:(q + 1) * PAIR_COLS],
                                                win_ref[2 * q:2 * q + 2].reshape(2 * gc, gc),
                                                preferred_element_type=F32)
    re_cols = [slice(q * gc, q * gc + LANES) for q in range(npair)]
    im_cols = [slice(q * gc + LANES, (q + 1) * gc) for q in range(npair)]
    ar = [dec_ref[:, cols] for cols in re_cols]
    ai = [dec_ref[:, cols] for cols in im_cols]

    def step(c, carry):
        rows = pl.ds(pl.multiple_of(c * bsz, bsz), bsz)
        new = []
        for q in range(npair):
            xr, xi = carry[2 * q], carry[2 * q + 1]
            x_scr[rows, re_cols[q]] = xr
            x_scr[rows, im_cols[q]] = xi
            new += [ar[q] * xr - ai[q] * xi + s_scr[rows, re_cols[q]],
                    ar[q] * xi + ai[q] * xr + s_scr[rows, im_cols[q]]]
        return tuple(new)

    zero = jnp.zeros((bsz, LANES), F32)
    lax.fori_loop(0, nchunk, step, (zero,) * (2 * npair))
    for g in range(2 * npair):
        z = z_ref[:, g * gc:(g + 1) * gc]
        x = x_scr[:, (g // 2) * gc:(g // 2 + 1) * gc].astype(BF16)
        y = jnp.dot(jnp.concatenate([z, x], axis=1), jnp.concatenate([wt_ref[g], wx_ref[g]], axis=0),
                    preferred_element_type=F32)
        y = y + d_ref[:, g * gc:(g + 1) * gc] * z.astype(F32)
        o_ref[:, g * gc:(g + 1) * gc] = jax.nn.gelu(y).astype(o_ref.dtype)


def _s5_core(z, tables, layer, bsz):
    nrow, width = z.shape
    w_in, w_toep, w_x, dec, d_perm = tables
    npair = S5_PAIRS_PER_STEP
    wspec = pl.BlockSpec((None, 2 * npair, GROUP_COLS, GROUP_COLS), lambda k: (layer, k, 0, 0))
    return pl.pallas_call(
        functools.partial(_s5_core_kernel, bsz=bsz),
        grid=(N_PAIRS // npair,),
        in_specs=[
            pl.BlockSpec((nrow, npair * PAIR_COLS), lambda k: (0, k)),
            wspec, wspec, wspec,
            pl.BlockSpec((None, 1, npair * GROUP_COLS), lambda k: (layer, 0, k)),
            pl.BlockSpec((None, 1, npair * PAIR_COLS), lambda k: (layer, 0, k)),
        ],
        out_specs=pl.BlockSpec((nrow, npair * PAIR_COLS), lambda k: (0, k)),
        out_shape=jax.ShapeDtypeStruct((nrow, width), BF16),
        scratch_shapes=[pltpu.VMEM((nrow, npair * GROUP_COLS), F32), pltpu.VMEM((nrow, npair * GROUP_COLS), F32)],
        compiler_params=_cparams(("parallel",)),
        name="s5_core",
    )(z, w_in, w_toep, w_x, dec, d_perm)


def _s5_weights(lam_re, lam_im, log_dt, b_re, b_im, c_re, c_im, d_skip):
    g, p, c16 = SSM_GROUPS, SSM_STATE, SSM_GROUP
    lam = lax.complex(lam_re.astype(F32), lam_im.astype(F32))
    dt = jnp.exp(log_dt.astype(F32))[:, None]
    steps = jnp.arange(CHUNK + 1, dtype=F32)
    apow = jnp.exp((lam * dt)[None] * steps[:, None, None])
    a = apow[1]
    bbar = ((a - 1.0) / lam)[..., None] * lax.complex(b_re.astype(F32), b_im.astype(F32))
    cmat = lax.complex(c_re.astype(F32), c_im.astype(F32))
    odd = (jnp.arange(g) % 2)[:, None, None]

    win = apow[CHUNK - 1::-1][..., None] * bbar[None]

    def lay_in(x):
        return jnp.transpose(x.astype(BF16), (1, 0, 3, 2)).reshape(g, GROUP_COLS, p)

    win_re, win_im = lay_in(win.real), lay_in(win.imag)
    zeros = jnp.zeros_like(win_re)
    w_in = jnp.concatenate([
        jnp.where(odd == 0, win_re, zeros), jnp.where(odd == 1, win_re, zeros),
        jnp.where(odd == 0, win_im, zeros), jnp.where(odd == 1, win_im, zeros)], axis=-1)

    kern = jnp.einsum('gop,kgp,gpi->kgoi', cmat, apow[:CHUNK], bbar, precision=HIGHEST).real
    idx = jnp.arange(CHUNK)
    onehot = (idx[None, None, :] - idx[None, :, None] == idx[:, None, None]).astype(BF16)
    w_toep = jnp.einsum('kst,kgoi->gsito', onehot, kern.astype(BF16),
                        preferred_element_type=BF16)
    w_toep = w_toep.reshape(g, GROUP_COLS, GROUP_COLS)

    cw = cmat[None] * apow[1:, :, None, :]

    def lay_x(x):
        return jnp.transpose(x.astype(BF16), (1, 3, 0, 2)).reshape(g, p, GROUP_COLS)

    cw_re, cw_im = lay_x(cw.real), lay_x(-cw.imag)
    zx = jnp.zeros_like(cw_re)
    w_x = jnp.concatenate([
        jnp.where(odd == 0, cw_re, zx), jnp.where(odd == 1, cw_re, zx),
        jnp.where(odd == 0, cw_im, zx), jnp.where(odd == 1, cw_im, zx)], axis=1)

    a16 = apow[CHUNK].reshape(N_PAIRS, 2 * p)
    dec = jnp.concatenate([a16.real, a16.imag], axis=-1).reshape(1, N_PAIRS * GROUP_COLS)
    d_perm = jnp.broadcast_to(d_skip.astype(F32).reshape(g, 1, c16), (g, CHUNK, c16)).reshape(1, g * GROUP_COLS)
    return w_in, w_toep, w_x, dec, d_perm


def _mlp(u, w1_ref, w2_ref):
    d = D_MODEL
    acc = None
    for k in range(D_FF // d):
        a = jnp.dot(u, w1_ref[:, k * d:(k + 1) * d], preferred_element_type=F32)
        a = jnp.square(jnp.maximum(a, 0.0)).astype(BF16)
        part = jnp.dot(a, w2_ref[k * d:(k + 1) * d, :], preferred_element_type=F32)
        acc = part if acc is None else acc + part
    return acc


def _final_norm(h, fg_ref):
    ms = jnp.mean(h * h, axis=-1, keepdims=True)
    return h * lax.rsqrt(ms + EPS) * fg_ref[...]


def _ride_specs(ride, nsteps):
    ins, outs, shapes = [], [], []
    for st, layer in ride:
        _, r, c = st.shape
        assert r % nsteps == 0
        ins.append(pl.BlockSpec((None, r // nsteps, c), lambda i, layer=layer: (layer, i, 0)))
        outs.append(pl.BlockSpec((r // nsteps, c), lambda i: (i, 0)))
        shapes.append(jax.ShapeDtypeStruct((r, c), BF16))
    return ins, outs, shapes, [st for st, _ in ride]


def _ride_cast(ride_in, ride_out):
    for src, dst in zip(ride_in, ride_out):
        dst[...] = src[...].astype(dst.dtype)


def _post_attn_kernel(*refs, final, n_ride):
    h_ref, y_ref, moda_ref, modm_ref, g_ref, wp_ref, w1_ref, w2_ref = refs[:8]
    pos = 8 + int(final)
    fg_ref = refs[8] if final else None
    o_ref = refs[pos + n_ride]
    _ride_cast(refs[pos:pos + n_ride], refs[pos + n_ride + 1:])
    d = D_MODEL
    ymix = jnp.dot(y_ref[...], wp_ref[...], preferred_element_type=F32)
    h1 = h_ref[...] + moda_ref[:, 2 * d:3 * d] * ymix
    u = _normmod(h1, g_ref[...], modm_ref[:, :d], modm_ref[:, d:2 * d]).astype(BF16)
    h2 = h1 + modm_ref[:, 2 * d:3 * d] * _mlp(u, w1_ref, w2_ref)
    o_ref[...] = _final_norm(h2, fg_ref) if final else h2


def _post_attn(h, o, moda, modm, g2, w_o, w1, w2, final_g, ride):
    bsz, seq, d = h.shape
    tm = MLP_ROWS
    per_b = seq // tm
    row_spec = pl.BlockSpec((None, tm, d), lambda i: (i // per_b, i % per_b, 0))
    final = final_g is not None
    ride_in, ride_out, ride_shapes, ride_args = _ride_specs(ride, bsz * per_b)
    in_specs = [
        row_spec, row_spec,
        pl.BlockSpec((None, 1, moda.shape[-1]), lambda i: (i // per_b, 0, 0)),
        pl.BlockSpec((None, 1, modm.shape[-1]), lambda i: (i // per_b, 0, 0)),
        pl.BlockSpec((1, d), lambda i: (0, 0)),
        _layer_spec(*w_o), _layer_spec(*w1), _layer_spec(*w2),
    ]
    args = [h, o, moda, modm, g2.reshape(1, d), w_o[0], w1[0], w2[0]]
    if final:
        in_specs.append(pl.BlockSpec((1, d), lambda i: (0, 0)))
        args.append(final_g.reshape(1, d))
    args += ride_args
    outs = pl.pallas_call(
        functools.partial(_post_attn_kernel, final=final, n_ride=len(ride_in)),
        grid=(bsz * per_b,),
        in_specs=in_specs + ride_in,
        out_specs=[row_spec] + ride_out,
        out_shape=[jax.ShapeDtypeStruct((bsz, seq, d), F32)] + ride_shapes,
        compiler_params=_cparams(("parallel",)),
        name="post_attn_mlp",
    )(*args)
    return outs[0], outs[1:]


def _post_s5_kernel(*refs, n_ride):
    h_ref, z_ref, moda_ref, modm_ref, g_ref, permt_ref, wp_ref, w1_ref, w2_ref = refs[:9]
    o_ref = refs[9 + n_ride]
    z_scr, h1_scr, u_scr = refs[10 + 2 * n_ride:]
    _ride_cast(refs[9:9 + n_ride], refs[10 + n_ride:10 + 2 * n_ride])
    d = D_MODEL
    bsz, tt, _ = h_ref.shape
    rows = z_ref.shape[0]
    for j in range(d // LANES):
        for half in range(CHUNK // 8):
            tiles = []
            for gl in range(8):
                col = (8 * j + gl) * GROUP_COLS + half * LANES
                tiles.append(z_ref[:, col:col + LANES].astype(F32))
            outs = _regroup8(tiles)
            for tl in range(8):
                t = 8 * half + tl
                z_scr[t * rows:(t + 1) * rows, j * LANES:(j + 1) * LANES] = outs[tl].astype(BF16)
    zn = jnp.dot(permt_ref[...], z_scr[...], preferred_element_type=F32).astype(BF16)
    zz = jnp.dot(zn, wp_ref[...], preferred_element_type=F32)
    ymix = zz[:, :d] * jax.nn.sigmoid(zz[:, d:])
    g = g_ref[...]
    for b in range(bsz):
        sl = slice(b * tt, (b + 1) * tt)
        h1 = h_ref[b] + moda_ref[b, :, 2 * d:3 * d] * ymix[sl, :]
        h1_scr[sl, :] = h1
        u_scr[sl, :] = _normmod(h1, g, modm_ref[b, :, :d], modm_ref[b, :, d:2 * d]).astype(BF16)
    acc = _mlp(u_scr[...], w1_ref, w2_ref)
    for b in range(bsz):
        sl = slice(b * tt, (b + 1) * tt)
        o_ref[b] = h1_scr[sl, :] + modm_ref[b, :, 2 * d:3 * d] * acc[sl, :]


def _post_s5(h, zo, moda, modm, g2, permt, w_glu, w1, w2, ride):
    bsz, seq, d = h.shape
    tt = S5_TILE_TOKENS
    rows = bsz * tt // CHUNK
    h_spec = pl.BlockSpec((bsz, tt, d), lambda i: (0, i, 0))
    ride_in, ride_out, ride_shapes, ride_args = _ride_specs(ride, seq // tt)
    outs = pl.pallas_call(
        functools.partial(_post_s5_kernel, n_ride=len(ride_in)),
        grid=(seq // tt,),
        in_specs=[
            h_spec,
            pl.BlockSpec((rows, CHUNK * d), lambda i: (i, 0)),
            pl.BlockSpec(moda.shape, lambda i: (0, 0, 0)),
            pl.BlockSpec(modm.shape, lambda i: (0, 0, 0)),
            pl.BlockSpec((1, d), lambda i: (0, 0)),
            pl.BlockSpec(permt.shape, lambda i: (0, 0)),
            _layer_spec(*w_glu), _layer_spec(*w1), _layer_spec(*w2),
        ] + ride_in,
        out_specs=[h_spec] + ride_out,
        out_shape=[jax.ShapeDtypeStruct((bsz, seq, d), F32)] + ride_shapes,
        scratch_shapes=[pltpu.VMEM((bsz * tt, d), BF16), pltpu.VMEM((bsz * tt, d), F32),
                        pltpu.VMEM((bsz * tt, d), BF16)],
        compiler_params=_cparams(("parallel",)),
        name="post_s5_mlp",
    )(h, zo, moda, modm, g2.reshape(1, d), permt, w_glu[0], w1[0], w2[0], *ride_args)
    return outs[0], outs[1:]


def kernel(x, c, ln_g, ada_w, ada_b, ssm_lam_re, ssm_lam_im, ssm_log_dt, ssm_b_re, ssm_b_im, ssm_c_re, ssm_c_im, ssm_d, ssm_w_glu, kv_g, kv_ada_w, kv_ada_b, w_kv, attn_w_q, attn_w_o, mlp_w1, mlp_w2, final_g):
    bsz, seq, d = x.shape
    depth = ln_g.shape[0]
    n_s5 = ssm_lam_re.shape[0]
    assert d == D_MODEL and seq % (DILATIONS[-1] * ATT_BLOCK) == 0 and 0 < n_s5 < depth

    mods = _ada_mods(c, ada_w.reshape(depth * 2, d, 3 * d), ada_b.reshape(depth * 2, 3 * d))
    mods = mods.reshape(depth, 2, bsz, 1, 3 * d)
    kv_mod = _ada_mods(c, kv_ada_w[None], kv_ada_b[None]).reshape(bsz, 1, 2 * d)

    perm_np = _s5_tile_perm(bsz)
    perm = jnp.asarray(perm_np, BF16)
    permt = jnp.asarray(perm_np.T, BF16)
    tables = jax.vmap(_s5_weights)(ssm_lam_re, ssm_lam_im, ssm_log_dt, ssm_b_re, ssm_b_im,
                                   ssm_c_re, ssm_c_im, ssm_d)

    def as_layer(cast):
        return (cast[None], 0)

    h = x
    for layer in range(n_s5):
        ride = [(mlp_w1, 0), (mlp_w2, 0), (ssm_w_glu, 0)] if layer == 0 else []
        z, cast = _s5_pre(h, mods[layer, 0], ln_g[layer, 0], perm, ride)
        if layer == 0:
            w1, w2, w_glu = map(as_layer, cast)
        zo = _s5_core(z, tables, layer, bsz)
        ride = [(mlp_w1, layer + 1), (mlp_w2, layer + 1)]
        if layer + 1 < n_s5:
            ride += [(ssm_w_glu, layer + 1)]
        else:
            ride += [(w_kv[None], 0), (attn_w_q, 0), (attn_w_o, 0)]
        h, cast = _post_s5(h, zo, mods[layer, 0], mods[layer, 1], ln_g[layer, 1], permt, w_glu, w1, w2, ride)
        w1, w2 = as_layer(cast[0]), as_layer(cast[1])
        if layer + 1 < n_s5:
            w_glu = as_layer(cast[2])
        else:
            w_kvb, w_q, w_o = map(as_layer, cast[2:])

    kv_cols = tuple((i * d, (N_BRANCHES + i) * d) for i in range(N_BRANCHES))
    q_cols = tuple((i * d,) for i in range(N_BRANCHES))
    kvs = _norm_proj(h, kv_mod, kv_g, w_kvb, kv_cols, 1.0, KV_PROJ_TILE)
    for layer in range(n_s5, depth):
        j = layer - n_s5
        last = layer == depth - 1
        qs = _norm_proj(h, mods[layer, 0], ln_g[layer, 0], w_q, q_cols, Q_SCALE, Q_PROJ_TILE)
        o = _attention(qs, kvs)
        ride = [] if last else [(mlp_w1, layer + 1), (mlp_w2, layer + 1), (attn_w_q, j + 1), (attn_w_o, j + 1)]
        h, cast = _post_attn(h, o, mods[layer, 0], mods[layer, 1], ln_g[layer, 1], w_o, w1, w2,
                             final_g if last else None, ride)
        if not last:
            w1, w2, w_q, w_o = map(as_layer, cast)
    return h
```

```python
import functools
import math

import numpy as np

import jax
import jax.numpy as jnp
from jax import lax
from jax.experimental import pallas as pl
from jax.experimental.pallas import tpu as pltpu

F32 = jnp.float32
BF16 = jnp.bfloat16

D_MODEL = 1024
SSM_GROUP = 16
SSM_GROUPS = D_MODEL // SSM_GROUP
SSM_STATE = 64
HEAD_DIM = 64
DILATIONS = (1, 4, 16)
N_BRANCHES = len(DILATIONS)
ATT_BLOCK = 128
ATT_GROUP = 16
ATT_PAIRS_PER_STEP = 2
N_STATS = 3
Q_SCALE = HEAD_DIM ** -0.5 * math.log2(math.e)
D_FF = 4 * D_MODEL
EPS = 1e-6
NEG = -1e30

LANES = 128
CHUNK = 16
GROUP_COLS = CHUNK * SSM_GROUP
PAIR_COLS = 2 * GROUP_COLS
N_PAIRS = SSM_GROUPS // 2
S5_PAIRS_PER_STEP = 2
S5_TILE_TOKENS = 2 * CHUNK
S5_PRE_SLOTS = 3
MLP_ROWS = 1024
KV_PROJ_TILE = 512
Q_PROJ_TILE = 1024
VMEM_LIMIT = 56 * 1024 * 1024

HIGHEST = lax.Precision.HIGHEST
NT_DIMS = (((1,), (1,)), ((), ()))


def _cparams(sem):
    return pltpu.CompilerParams(dimension_semantics=sem, vmem_limit_bytes=VMEM_LIMIT)


def _layer_spec(stack, layer):
    zeros = (0,) * (stack.ndim - 1)
    return pl.BlockSpec((None,) + stack.shape[1:], lambda i: (layer,) + zeros, pipeline_mode=pl.Buffered(1))


def _normmod(x, g, shift, scale):
    ms = jnp.mean(x * x, axis=-1, keepdims=True)
    return (x * lax.rsqrt(ms + EPS) * g) * (1.0 + scale) + shift


def _ada_kernel(c_ref, w_ref, b_ref, o_ref):
    c = c_ref[...]
    sc = (c * jax.nn.sigmoid(c)).astype(BF16)
    o_ref[...] = jnp.dot(sc, w_ref[...].astype(BF16), preferred_element_type=F32) + b_ref[...]


def _ada_mods(c, w, b):
    n, d, width = w.shape
    bsz = c.shape[0]
    return pl.pallas_call(
        _ada_kernel,
        grid=(n,),
        in_specs=[
            pl.BlockSpec((bsz, d), lambda i: (0, 0)),
            pl.BlockSpec((None, d, width), lambda i: (i, 0, 0)),
            pl.BlockSpec((None, 1, width), lambda i: (i, 0, 0)),
        ],
        out_specs=pl.BlockSpec((None, bsz, width), lambda i: (i, 0, 0)),
        out_shape=jax.ShapeDtypeStruct((n, bsz, width), F32),
        compiler_params=_cparams(("parallel",)),
        name="ada_mods",
    )(c, w, b.reshape(n, 1, width))


def _proj_kernel(h_ref, mod_ref, g_ref, w_ref, o1_ref, o2_ref, o3_ref, us, us4, up4, up16, *, wcols, out_scale):
    d = D_MODEL
    tm = h_ref.shape[0]
    d1 = DILATIONS[1]
    q4, q16 = tm // d1, tm // DILATIONS[2]
    cw = o1_ref.shape[1]
    u = _normmod(h_ref[...], g_ref[...], mod_ref[:, :d], mod_ref[:, d:2 * d])
    for l in range(d // LANES):
        us[l] = u[:, l * LANES:(l + 1) * LANES]
    ub = u.astype(BF16)
    for n in range(cw // d):
        cols = slice(n * d, (n + 1) * d)
        res = jnp.dot(ub, w_ref[:, wcols[0][n]:wcols[0][n] + d], preferred_element_type=F32)
        o1_ref[:, cols] = (res * out_scale).astype(o1_ref.dtype)
    for l in range(d // LANES):
        for a in range(d1):
            part = us[l, pl.ds(a, q4, stride=d1), :]
            us4[l, a * q4:(a + 1) * q4, :] = part
            up4[a * q4:(a + 1) * q4, l * LANES:(l + 1) * LANES] = part.astype(BF16)
    for n in range(cw // d):
        res = jnp.dot(up4[...], w_ref[:, wcols[1][n]:wcols[1][n] + d], preferred_element_type=F32)
        for a in range(d1):
            o2_ref[a, :, n * d:(n + 1) * d] = (res[a * q4:(a + 1) * q4, :] * out_scale).astype(o2_ref.dtype)
    for l in range(d // LANES):
        for a1 in range(d1):
            for a2 in range(d1):
                r = d1 * a2 + a1
                up16[r * q16:(r + 1) * q16, l * LANES:(l + 1) * LANES] = us4[
                    l, pl.ds(a1 * q4 + a2, q16, stride=d1), :].astype(BF16)
    for n in range(cw // d):
        res = jnp.dot(up16[...], w_ref[:, wcols[2][n]:wcols[2][n] + d], preferred_element_type=F32)
        for r in range(DILATIONS[2]):
            o3_ref[r, :, n * d:(n + 1) * d] = (res[r * q16:(r + 1) * q16, :] * out_scale).astype(o3_ref.dtype)


def _norm_proj(h, mod, g, w, wcols, out_scale, tm):
    bsz, seq, d = h.shape
    cw = len(wcols[0]) * d
    per_b = seq // tm
    d1, d2 = DILATIONS[1], DILATIONS[2]
    o1, o2, o3 = pl.pallas_call(
        functools.partial(_proj_kernel, wcols=wcols, out_scale=out_scale),
        grid=(bsz * per_b,),
        in_specs=[
            pl.BlockSpec((None, tm, d), lambda i: (i // per_b, i % per_b, 0)),
            pl.BlockSpec((None, 1, mod.shape[-1]), lambda i: (i // per_b, 0, 0)),
            pl.BlockSpec((1, d), lambda i: (0, 0)),
            _layer_spec(*w),
        ],
        out_specs=[
            pl.BlockSpec((None, tm, cw), lambda i: (i // per_b, i % per_b, 0)),
            pl.BlockSpec((None, d1, None, tm // d1, cw), lambda i: (i // per_b, 0, i % per_b, 0, 0)),
            pl.BlockSpec((None, d2, None, tm // d2, cw), lambda i: (i // per_b, 0, i % per_b, 0, 0)),
        ],
        out_shape=[
            jax.ShapeDtypeStruct((bsz, seq, cw), BF16),
            jax.ShapeDtypeStruct((bsz, d1, per_b, tm // d1, cw), BF16),
            jax.ShapeDtypeStruct((bsz, d2, per_b, tm // d2, cw), BF16),
        ],
        scratch_shapes=[pltpu.VMEM((d // LANES, tm, LANES), F32), pltpu.VMEM((d // LANES, tm, LANES), F32),
                        pltpu.VMEM((tm, d), BF16), pltpu.VMEM((tm, d), BF16)],
        compiler_params=_cparams(("parallel",)),
        name="norm_proj",
    )(h, mod, g.reshape(1, d), w[0])
    return o1, o2.reshape(bsz, seq, cw), o3.reshape(bsz, seq, cw)


def _attn_group(blocks):
    blk = ATT_BLOCK
    lane = lax.broadcasted_iota(jnp.int32, (blk, LANES), 1)
    lo = lane < HEAD_DIM
    scores = []
    for qb, kcat, _, mask, _ in blocks:
        zero = jnp.zeros_like(qb)
        q2 = jnp.concatenate([jnp.where(lo, qb, zero), jnp.where(lo, zero, qb)], axis=0)
        s = lax.dot_general(q2, kcat, NT_DIMS, preferred_element_type=F32)
        scores.append(s + mask[...])
    soft = []
    for s, (_, _, _, _, old) in zip(scores, blocks):
        ms, ps = [], []
        for hh in range(2):
            tiles = [s[hh * blk:(hh + 1) * blk, t * LANES:(t + 1) * LANES] for t in range(s.shape[1] // LANES)]
            mx = tiles[0]
            for t in tiles[1:]:
                mx = jnp.maximum(mx, t)
            m = jnp.broadcast_to(jnp.max(mx, axis=-1, keepdims=True), (blk, LANES))
            if old is not None:
                m = jnp.maximum(m, old[0][hh])
            ms.append(m)
            ps.append(jnp.concatenate([jnp.exp2(t - m).astype(BF16) for t in tiles], axis=1))
        alpha = None
        if old is not None:
            alpha = jnp.exp2(jnp.where(lo, old[0][0], old[0][1]) - jnp.where(lo, ms[0], ms[1]))
        soft.append((ms, jnp.concatenate(ps, axis=0), alpha))
    outs = []
    for (ms, p, alpha), (_, _, vcat, _, old) in zip(soft, blocks):
        pv = jnp.dot(p, vcat, preferred_element_type=F32)
        acc = jnp.where(lo, pv[:blk, :LANES], pv[blk:, :LANES])
        den = jnp.where(lo, pv[:blk, LANES:], pv[blk:, LANES:])
        if old is not None:
            acc = alpha * old[1] + acc
            den = alpha * old[0][2] + den
        outs.append((ms + [den], acc))
    return outs


def _attn_kernel(*refs):
    n_in = 3 * N_BRANCHES
    ins, o_ref, scratch = refs[:n_in], refs[n_in], refs[n_in + 1:]
    for half in range(ATT_PAIRS_PER_STEP):
        lanes = pl.ds(half * LANES, LANES)
        _attn_pair(*[r.at[:, lanes] for r in ins], o_ref.at[:, lanes], *scratch)


def _attn_pair(q1, qd2, qd3, k1, kd2, kd3, v1, v2, v3, o_ref, tmp_nat, vd1, vd2, vd3,
               acc_ref, st_ref, acc2_ref, st2_ref, mask_pc, mask_c):
    blk = ATT_BLOCK
    seq = q1.shape[0]
    nblk = seq // blk
    d1, d2 = DILATIONS[1], DILATIONS[2]
    quarter = seq // d1
    per_res = quarter // blk
    assert d2 == d1 * d1 and seq == d2 * blk and nblk % ATT_GROUP == 0 and ATT_GROUP % per_res == 0

    @pl.when(jnp.logical_and(pl.program_id(0) == 0, pl.program_id(1) == 0))
    def _():
        for vd in (vd1, vd2, vd3):
            vd[:, LANES:] = jnp.ones((seq, LANES), BF16)
        qq = lax.broadcasted_iota(jnp.int32, (2 * blk, 2 * blk), 0) & (blk - 1)
        kk = lax.broadcasted_iota(jnp.int32, (2 * blk, 2 * blk), 1)
        valid = jnp.logical_or(jnp.logical_and(kk < blk, kk >= qq), jnp.logical_and(kk >= blk, kk - blk <= qq))
        mask_pc[...] = jnp.where(valid, 0.0, NEG)
        valid_c = (lax.broadcasted_iota(jnp.int32, (2 * blk, blk), 1)
                   <= (lax.broadcasted_iota(jnp.int32, (2 * blk, blk), 0) & (blk - 1)))
        mask_c[...] = jnp.where(valid_c, 0.0, NEG)

    for v, vd in ((v1, vd1), (v2, vd2), (v3, vd3)):
        vd[:, :LANES] = v[...]

    def run_group(blocks, dests):
        for (st_dst, acc_dst, rows), (st, acc) in zip(dests, _attn_group(blocks)):
            for k in range(N_STATS):
                st_dst[k, rows, :] = st[k]
            acc_dst[rows, :] = acc

    def load_old(st_src, acc_src, rows):
        return [st_src[k, rows, :] for k in range(N_STATS)], acc_src[rows, :]

    def b0_group(ns, first):
        blocks, dests = [], []
        for n in ns:
            if first and n == 0:
                rows = pl.ds(0, blk)
                blocks.append((q1[rows, :], k1[rows, :], vd1[rows, :], mask_c, None))
            else:
                rows = pl.ds(pl.multiple_of(n * blk, blk), blk)
                krows = pl.ds(pl.multiple_of((n - 1) * blk, blk), 2 * blk)
                blocks.append((q1[rows, :], k1[krows, :], vd1[krows, :], mask_pc, None))
            dests.append((st_ref, acc_ref, rows))
        run_group(blocks, dests)

    b0_group(list(range(ATT_GROUP)), True)

    def b0_body(i, carry):
        b0_group([ATT_GROUP * i + j for j in range(ATT_GROUP)], False)
        return carry

    lax.fori_loop(1, nblk // ATT_GROUP, b0_body, 0)

    res_per_trip = ATT_GROUP // per_res

    def b1_body(i, carry):
        blocks, dests = [], []
        for jr in range(res_per_trip):
            a1 = res_per_trip * i + jr
            base = a1 * quarter
            for n in range(per_res):
                rows = pl.ds(a1 + d1 * blk * n, blk, stride=d1)
                qrows = pl.ds(pl.multiple_of(base + n * blk, blk), blk)
                if n == 0:
                    krows, mask = qrows, mask_c
                else:
                    krows, mask = pl.ds(pl.multiple_of(base + (n - 1) * blk, blk), 2 * blk), mask_pc
                blocks.append((qd2[qrows, :], kd2[krows, :], vd2[krows, :], mask,
                               load_old(st_ref, acc_ref, rows)))
                dests.append((st2_ref, acc2_ref, qrows))
        run_group(blocks, dests)
        return carry

    lax.fori_loop(0, d1 // res_per_trip, b1_body, 0)

    def b2_body(i, carry):
        blocks, dests = [], []
        for j in range(ATT_GROUP):
            a1, a2 = j % d1, (ATT_GROUP // d1) * i + j // d1
            qrows = pl.ds(pl.multiple_of((ATT_GROUP * i + j) * blk, blk), blk)
            rows = pl.ds(a1 * quarter + a2, blk, stride=d1)
            blocks.append((qd3[qrows, :], kd3[qrows, :], vd3[qrows, :], mask_c,
                           load_old(st2_ref, acc2_ref, rows)))
            dests.append((st2_ref, acc2_ref, rows))
        run_group(blocks, dests)
        return carry

    lax.fori_loop(0, d2 // ATT_GROUP, b2_body, 0)

    for a in range(d1):
        seg = slice(a * quarter, (a + 1) * quarter)
        tmp_nat[pl.ds(a, quarter, stride=d1), :] = acc2_ref[seg, :] / st2_ref[N_STATS - 1, seg, :]
    o_ref[...] = tmp_nat[...].astype(o_ref.dtype)


def _attention(qs, kvs):
    bsz, seq, _ = qs[0].shape
    steps = D_MODEL // (ATT_PAIRS_PER_STEP * LANES)

    def slab(col0):
        return pl.BlockSpec((None, seq, ATT_PAIRS_PER_STEP * LANES), lambda b, hp: (b, 0, col0 + hp))

    in_specs = [slab(0)] * N_BRANCHES + [slab(0)] * N_BRANCHES + [slab(steps)] * N_BRANCHES
    return pl.pallas_call(
        _attn_kernel,
        grid=(bsz, steps),
        in_specs=in_specs,
        out_specs=slab(0),
        out_shape=jax.ShapeDtypeStruct((bsz, seq, D_MODEL), BF16),
        scratch_shapes=([pltpu.VMEM((seq, LANES), F32)] + [pltpu.VMEM((seq, 2 * LANES), BF16)] * 3
                        + [pltpu.VMEM((seq, LANES), F32), pltpu.VMEM((N_STATS, seq, LANES), F32)] * 2
                        + [pltpu.VMEM((2 * ATT_BLOCK, 2 * ATT_BLOCK), F32), pltpu.VMEM((2 * ATT_BLOCK, ATT_BLOCK), F32)]),
        compiler_params=_cparams(("arbitrary", "arbitrary")),
        name="attention",
    )(*qs, *kvs, *kvs)


def _regroup8(tiles):
    t = list(tiles)
    lane = lax.broadcasted_iota(jnp.int32, t[0].shape, 1)
    piece = lane // SSM_GROUP
    for dist in (4, 2, 1):
        bit = (piece & dist) != 0
        shift = dist * SSM_GROUP
        for a in range(8):
            if a & dist:
                continue
            b = a | dist
            ta, tb = t[a], t[b]
            t[a] = jnp.where(bit, pltpu.roll(tb, shift, 1), ta)
            t[b] = jnp.where(bit, tb, pltpu.roll(ta, LANES - shift, 1))
    return t


def _s5_tile_perm(bsz):
    tt = S5_TILE_TOKENS
    n = bsz * tt
    perm = np.zeros((n, n), np.float32)
    for b in range(bsz):
        for c in range(tt // CHUNK):
            for t in range(CHUNK):
                perm[(t * (tt // CHUNK) + c) * bsz + b, b * tt + c * CHUNK + t] = 1.0
    return perm


def _s5_pre_kernel(*refs, n_ride):
    h_hbm, mod_ref, g_ref, perm_ref = refs[:4]
    z_ref = refs[4 + n_ride]
    u_scr, up_scr, h_buf, h_sem = refs[5 + 2 * n_ride:]
    _ride_cast(refs[4:4 + n_ride], refs[5 + n_ride:5 + 2 * n_ride])
    d = D_MODEL
    _, bsz, tt, _ = h_buf.shape
    step, nstep = pl.program_id(0), pl.num_programs(0)

    def tile_copy(s):
        slot = s % S5_PRE_SLOTS
        return pltpu.make_async_copy(h_hbm.at[:, pl.ds(pl.multiple_of(s * tt, tt), tt), :], h_buf.at[slot],
                                     h_sem.at[slot])

    @pl.when(step == 0)
    def _():
        for s in range(S5_PRE_SLOTS - 1):
            tile_copy(s).start()

    @pl.when(step + S5_PRE_SLOTS - 1 < nstep)
    def _():
        tile_copy(step + S5_PRE_SLOTS - 1).start()

    tile_copy(step).wait()
    h_ref = h_buf.at[step % S5_PRE_SLOTS]
    g = g_ref[...]
    for b in range(bsz):
        u_scr[b * tt:(b + 1) * tt, :] = _normmod(
            h_ref[b], g, mod_ref[b, :, :d], mod_ref[b, :, d:2 * d]).astype(BF16)
    up_scr[...] = jnp.dot(perm_ref[...], u_scr[...], preferred_element_type=F32)
    rows = z_ref.shape[0]
    for j in range(d // LANES):
        for half in range(CHUNK // 8):
            tiles = [up_scr[(8 * half + tl) * rows:(8 * half + tl + 1) * rows, j * LANES:(j + 1) * LANES]
                     for tl in range(8)]
            outs = _regroup8(tiles)
            for gl in range(8):
                col = (8 * j + gl) * GROUP_COLS + half * LANES
                z_ref[:, col:col + LANES] = outs[gl].astype(z_ref.dtype)


def _s5_pre(h, mod, g, perm, ride):
    bsz, seq, d = h.shape
    tt = S5_TILE_TOKENS
    rows = bsz * tt // CHUNK
    ride_in, ride_out, ride_shapes, ride_args = _ride_specs(ride, seq // tt)
    outs = pl.pallas_call(
        functools.partial(_s5_pre_kernel, n_ride=len(ride_in)),
        grid=(seq // tt,),
        in_specs=[
            pl.BlockSpec(memory_space=pl.ANY),
            pl.BlockSpec(mod.shape, lambda i: (0, 0, 0)),
            pl.BlockSpec((1, d), lambda i: (0, 0)),
            pl.BlockSpec(perm.shape, lambda i: (0, 0)),
        ] + ride_in,
        out_specs=[pl.BlockSpec((rows, CHUNK * d), lambda i: (i, 0))] + ride_out,
        out_shape=[jax.ShapeDtypeStruct((bsz * seq // CHUNK, CHUNK * d), BF16)] + ride_shapes,
        scratch_shapes=[pltpu.VMEM((bsz * tt, d), BF16), pltpu.VMEM((bsz * tt, d), F32),
                        pltpu.VMEM((S5_PRE_SLOTS, bsz, tt, d), F32), pltpu.SemaphoreType.DMA((S5_PRE_SLOTS,))],
        compiler_params=_cparams(("arbitrary",)),
        name="s5_pre",
    )(h, mod, g.reshape(1, d), perm, *ride_args)
    return outs[0], outs[1:]


def _s5_core_kernel(z_ref, win_ref, wt_ref, wx_ref, dec_ref, d_ref, o_ref, s_scr, x_scr, *, bsz):
    gc = GROUP_COLS
    npair = S5_PAIRS_PER_STEP
    nchunk = z_ref.shape[0] // bsz
    for q in range(npair):
        s_scr[:, q * gc:(q + 1) * gc] = jnp.dot(z_ref[:, q * PAIR_COLS:(q + 1) * PAIR_COLS],
                                                win_ref[2 * q:2 * q + 2].reshape(2 * gc, gc),
                                                preferred_element_type=F32)
    re_cols = [slice(q * gc, q * gc + LANES) for q in range(npair)]
    im_cols = [slice(q * gc + LANES, (q + 1) * gc) for q in range(npair)]
    ar = [dec_ref[:, cols] for cols in re_cols]
    ai = [dec_ref[:, cols] for cols in im_cols]

    def step(c, carry):
        rows = pl.ds(pl.multiple_of(c * bsz, bsz), bsz)
        new = []
        for q in range(npair):
            xr, xi = carry[2 * q], carry[2 * q + 1]
            x_scr[rows, re_cols[q]] = xr
            x_scr[rows, im_cols[q]] = xi
            new += [ar[q] * xr - ai[q] * xi + s_scr[rows, re_cols[q]],
                    ar[q] * xi + ai[q] * xr + s_scr[rows, im_cols[q]]]
        return tuple(new)

    zero = jnp.zeros((bsz, LANES), F32)
    lax.fori_loop(0, nchunk, step, (zero,) * (2 * npair))
    for g in range(2 * npair):
        z = z_ref[:, g * gc:(g + 1) * gc]
        x = x_scr[:, (g // 2) * gc:(g // 2 + 1) * gc].astype(BF16)
        y = jnp.dot(jnp.concatenate([z, x], axis=1), jnp.concatenate([wt_ref[g], wx_ref[g]], axis=0),
                    preferred_element_type=F32)
        y = y + d_ref[:, g * gc:(g + 1) * gc] * z.astype(F32)
        o_ref[:, g * gc:(g + 1) * gc] = jax.nn.gelu(y).astype(o_ref.dtype)


def _s5_core(z, tables, layer, bsz):
    nrow, width = z.shape
    w_in, w_toep, w_x, dec, d_perm = tables
    npair = S5_PAIRS_PER_STEP
    wspec = pl.BlockSpec((None, 2 * npair, GROUP_COLS, GROUP_COLS), lambda k: (layer, k, 0, 0))
    return pl.pallas_call(
        functools.partial(_s5_core_kernel, bsz=bsz),
        grid=(N_PAIRS // npair,),
        in_specs=[
            pl.BlockSpec((nrow, npair * PAIR_COLS), lambda k: (0, k)),
            wspec, wspec, wspec,
            pl.BlockSpec((None, 1, npair * GROUP_COLS), lambda k: (layer, 0, k)),
            pl.BlockSpec((None, 1, npair * PAIR_COLS), lambda k: (layer, 0, k)),
        ],
        out_specs=pl.BlockSpec((nrow, npair * PAIR_COLS), lambda k: (0, k)),
        out_shape=jax.ShapeDtypeStruct((nrow, width), BF16),
        scratch_shapes=[pltpu.VMEM((nrow, npair * GROUP_COLS), F32), pltpu.VMEM((nrow, npair * GROUP_COLS), F32)],
        compiler_params=_cparams(("parallel",)),
        name="s5_core",
    )(z, w_in, w_toep, w_x, dec, d_perm)


def _s5_weights(lam_re, lam_im, log_dt, b_re, b_im, c_re, c_im, d_skip):
    g, p, c16 = SSM_GROUPS, SSM_STATE, SSM_GROUP
    lam = lax.complex(lam_re.astype(F32), lam_im.astype(F32))
    dt = jnp.exp(log_dt.astype(F32))[:, None]
    steps = jnp.arange(CHUNK + 1, dtype=F32)
    apow = jnp.exp((lam * dt)[None] * steps[:, None, None])
    a = apow[1]
    bbar = ((a - 1.0) / lam)[..., None] * lax.complex(b_re.astype(F32), b_im.astype(F32))
    cmat = lax.complex(c_re.astype(F32), c_im.astype(F32))
    odd = (jnp.arange(g) % 2)[:, None, None]

    win = apow[CHUNK - 1::-1][..., None] * bbar[None]

    def lay_in(x):
        return jnp.transpose(x.astype(BF16), (1, 0, 3, 2)).reshape(g, GROUP_COLS, p)

    win_re, win_im = lay_in(win.real), lay_in(win.imag)
    zeros = jnp.zeros_like(win_re)
    w_in = jnp.concatenate([
        jnp.where(odd == 0, win_re, zeros), jnp.where(odd == 1, win_re, zeros),
        jnp.where(odd == 0, win_im, zeros), jnp.where(odd == 1, win_im, zeros)], axis=-1)

    kern = jnp.einsum('gop,kgp,gpi->kgoi', cmat, apow[:CHUNK], bbar, precision=HIGHEST).real
    idx = jnp.arange(CHUNK)
    onehot = (idx[None, None, :] - idx[None, :, None] == idx[:, None, None]).astype(BF16)
    w_toep = jnp.einsum('kst,kgoi->gsito', onehot, kern.astype(BF16),
                        preferred_element_type=BF16)
    w_toep = w_toep.reshape(g, GROUP_COLS, GROUP_COLS)

    cw = cmat[None] * apow[1:, :, None, :]

    def lay_x(x):
        return jnp.transpose(x.astype(BF16), (1, 3, 0, 2)).reshape(g, p, GROUP_COLS)

    cw_re, cw_im = lay_x(cw.real), lay_x(-cw.imag)
    zx = jnp.zeros_like(cw_re)
    w_x = jnp.concatenate([
        jnp.where(odd == 0, cw_re, zx), jnp.where(odd == 1, cw_re, zx),
        jnp.where(odd == 0, cw_im, zx), jnp.where(odd == 1, cw_im, zx)], axis=1)

    a16 = apow[CHUNK].reshape(N_PAIRS, 2 * p)
    dec = jnp.concatenate([a16.real, a16.imag], axis=-1).reshape(1, N_PAIRS * GROUP_COLS)
    d_perm = jnp.broadcast_to(d_skip.astype(F32).reshape(g, 1, c16), (g, CHUNK, c16)).reshape(1, g * GROUP_COLS)
    return w_in, w_toep, w_x, dec, d_perm


def _mlp(u, w1_ref, w2_ref):
    d = D_MODEL
    acc = None
    for k in range(D_FF // d):
        a = jnp.dot(u, w1_ref[:, k * d:(k + 1) * d], preferred_element_type=F32)
        a = jnp.square(jnp.maximum(a, 0.0)).astype(BF16)
        part = jnp.dot(a, w2_ref[k * d:(k + 1) * d, :], preferred_element_type=F32)
        acc = part if acc is None else acc + part
    return acc


def _final_norm(h, fg_ref):
    ms = jnp.mean(h * h, axis=-1, keepdims=True)
    return h * lax.rsqrt(ms + EPS) * fg_ref[...]


def _ride_specs(ride, nsteps):
    ins, outs, shapes = [], [], []
    for st, layer in ride:
        _, r, c = st.shape
        assert r % nsteps == 0
        ins.append(pl.BlockSpec((None, r // nsteps, c), lambda i, layer=layer: (layer, i, 0)))
        outs.append(pl.BlockSpec((r // nsteps, c), lambda i: (i, 0)))
        shapes.append(jax.ShapeDtypeStruct((r, c), BF16))
    return ins, outs, shapes, [st for st, _ in ride]


def _ride_cast(ride_in, ride_out):
    for src, dst in zip(ride_in, ride_out):
        dst[...] = src[...].astype(dst.dtype)


def _post_attn_kernel(*refs, final, n_ride):
    h_ref, y_ref, moda_ref, modm_ref, g_ref, wp_ref, w1_ref, w2_ref = refs[:8]
    pos = 8 + int(final)
    fg_ref = refs[8] if final else None
    o_ref = refs[pos + n_ride]
    _ride_cast(refs[pos:pos + n_ride], refs[pos + n_ride + 1:])
    d = D_MODEL
    ymix = jnp.dot(y_ref[...], wp_ref[...], preferred_element_type=F32)
    h1 = h_ref[...] + moda_ref[:, 2 * d:3 * d] * ymix
    u = _normmod(h1, g_ref[...], modm_ref[:, :d], modm_ref[:, d:2 * d]).astype(BF16)
    h2 = h1 + modm_ref[:, 2 * d:3 * d] * _mlp(u, w1_ref, w2_ref)
    o_ref[...] = _final_norm(h2, fg_ref) if final else h2


def _post_attn(h, o, moda, modm, g2, w_o, w1, w2, final_g, ride):
    bsz, seq, d = h.shape
    tm = MLP_ROWS
    per_b = seq // tm
    row_spec = pl.BlockSpec((None, tm, d), lambda i: (i // per_b, i % per_b, 0))
    final = final_g is not None
    ride_in, ride_out, ride_shapes, ride_args = _ride_specs(ride, bsz * per_b)
    in_specs = [
        row_spec, row_spec,
        pl.BlockSpec((None, 1, moda.shape[-1]), lambda i: (i // per_b, 0, 0)),
        pl.BlockSpec((None, 1, modm.shape[-1]), lambda i: (i // per_b, 0, 0)),
        pl.BlockSpec((1, d), lambda i: (0, 0)),
        _layer_spec(*w_o), _layer_spec(*w1), _layer_spec(*w2),
    ]
    args = [h, o, moda, modm, g2.reshape(1, d), w_o[0], w1[0], w2[0]]
    if final:
        in_specs.append(pl.BlockSpec((1, d), lambda i: (0, 0)))
        args.append(final_g.reshape(1, d))
    args += ride_args
    outs = pl.pallas_call(
        functools.partial(_post_attn_kernel, final=final, n_ride=len(ride_in)),
        grid=(bsz * per_b,),
        in_specs=in_specs + ride_in,
        out_specs=[row_spec] + ride_out,
        out_shape=[jax.ShapeDtypeStruct((bsz, seq, d), F32)] + ride_shapes,
        compiler_params=_cparams(("parallel",)),
        name="post_attn_mlp",
    )(*args)
    return outs[0], outs[1:]


def _post_s5_kernel(*refs, n_ride):
    h_ref, z_ref, moda_ref, modm_ref, g_ref, permt_ref, wp_ref, w1_ref, w2_ref = refs[:9]
    o_ref = refs[9 + n_ride]
    z_scr, h1_scr, u_scr = refs[10 + 2 * n_ride:]
    _ride_cast(refs[9:9 + n_ride], refs[10 + n_ride:10 + 2 * n_ride])
    d = D_MODEL
    bsz, tt, _ = h_ref.shape
    rows = z_ref.shape[0]
    for j in range(d // LANES):
        for half in range(CHUNK // 8):
            tiles = []
            for gl in range(8):
                col = (8 * j + gl) * GROUP_COLS + half * LANES
                tiles.append(z_ref[:, col:col + LANES].astype(F32))
            outs = _regroup8(tiles)
            for tl in range(8):
                t = 8 * half + tl
                z_scr[t * rows:(t + 1) * rows, j * LANES:(j + 1) * LANES] = outs[tl].astype(BF16)
    zn = jnp.dot(permt_ref[...], z_scr[...], preferred_element_type=F32).astype(BF16)
    zz = jnp.dot(zn, wp_ref[...], preferred_element_type=F32)
    ymix = zz[:, :d] * jax.nn.sigmoid(zz[:, d:])
    g = g_ref[...]
    for b in range(bsz):
        sl = slice(b * tt, (b + 1) * tt)
        h1 = h_ref[b] + moda_ref[b, :, 2 * d:3 * d] * ymix[sl, :]
        h1_scr[sl, :] = h1
        u_scr[sl, :] = _normmod(h1, g, modm_ref[b, :, :d], modm_ref[b, :, d:2 * d]).astype(BF16)
    acc = _mlp(u_scr[...], w1_ref, w2_ref)
    for b in range(bsz):
        sl = slice(b * tt, (b + 1) * tt)
        o_ref[b] = h1_scr[sl, :] + modm_ref[b, :, 2 * d:3 * d] * acc[sl, :]


def _post_s5(h, zo, moda, modm, g2, permt, w_glu, w1, w2, ride):
    bsz, seq, d = h.shape
    tt = S5_TILE_TOKENS
    rows = bsz * tt // CHUNK
    h_spec = pl.BlockSpec((bsz, tt, d), lambda i: (0, i, 0))
    ride_in, ride_out, ride_shapes, ride_args = _ride_specs(ride, seq // tt)
    outs = pl.pallas_call(
        functools.partial(_post_s5_kernel, n_ride=len(ride_in)),
        grid=(seq // tt,),
        in_specs=[
            h_spec,
            pl.BlockSpec((rows, CHUNK * d), lambda i: (i, 0)),
            pl.BlockSpec(moda.shape, lambda i: (0, 0, 0)),
            pl.BlockSpec(modm.shape, lambda i: (0, 0, 0)),
            pl.BlockSpec((1, d), lambda i: (0, 0)),
            pl.BlockSpec(permt.shape, lambda i: (0, 0)),
            _layer_spec(*w_glu), _layer_spec(*w1), _layer_spec(*w2),
        ] + ride_in,
        out_specs=[h_spec] + ride_out,
        out_shape=[jax.ShapeDtypeStruct((bsz, seq, d), F32)] + ride_shapes,
        scratch_shapes=[pltpu.VMEM((bsz * tt, d), BF16), pltpu.VMEM((bsz * tt, d), F32),
                        pltpu.VMEM((bsz * tt, d), BF16)],
        compiler_params=_cparams(("parallel",)),
        name="post_s5_mlp",
    )(h, zo, moda, modm, g2.reshape(1, d), permt, w_glu[0], w1[0], w2[0], *ride_args)
    return outs[0], outs[1:]


def kernel(x, c, ln_g, ada_w, ada_b, ssm_lam_re, ssm_lam_im, ssm_log_dt, ssm_b_re, ssm_b_im, ssm_c_re, ssm_c_im, ssm_d, ssm_w_glu, kv_g, kv_ada_w, kv_ada_b, w_kv, attn_w_q, attn_w_o, mlp_w1, mlp_w2, final_g):
    bsz, seq, d = x.shape
    depth = ln_g.shape[0]
    n_s5 = ssm_lam_re.shape[0]
    assert d == D_MODEL and seq % (DILATIONS[-1] * ATT_BLOCK) == 0 and 0 < n_s5 < depth

    mods = _ada_mods(c, ada_w.reshape(depth * 2, d, 3 * d), ada_b.reshape(depth * 2, 3 * d))
    mods = mods.reshape(depth, 2, bsz, 1, 3 * d)
    kv_mod = _ada_mods(c, kv_ada_w[None], kv_ada_b[None]).reshape(bsz, 1, 2 * d)

    perm_np = _s5_tile_perm(bsz)
    perm = jnp.asarray(perm_np, BF16)
    permt = jnp.asarray(perm_np.T, BF16)
    tables = jax.vmap(_s5_weights)(ssm_lam_re, ssm_lam_im, ssm_log_dt, ssm_b_re, ssm_b_im,
                                   ssm_c_re, ssm_c_im, ssm_d)

    def as_layer(cast):
        return (cast[None], 0)

    h = x
    for layer in range(n_s5):
        ride = [(mlp_w1, 0), (mlp_w2, 0), (ssm_w_glu, 0)] if layer == 0 else []
        z, cast = _s5_pre(h, mods[layer, 0], ln_g[layer, 0], perm, ride)
        if layer == 0:
            w1, w2, w_glu = map(as_layer, cast)
        zo = _s5_core(z, tables, layer, bsz)
        ride = [(mlp_w1, layer + 1), (mlp_w2, layer + 1)]
        if layer + 1 < n_s5:
            ride += [(ssm_w_glu, layer + 1)]
        else:
            ride += [(w_kv[None], 0), (attn_w_q, 0), (attn_w_o, 0)]
        h, cast = _post_s5(h, zo, mods[layer, 0], mods[layer, 1], ln_g[layer, 1], permt, w_glu, w1, w2, ride)
        w1, w2 = as_layer(cast[0]), as_layer(cast[1])
        if layer + 1 < n_s5:
            w_glu = as_layer(cast[2])
        else:
            w_kvb, w_q, w_o = map(as_layer, cast[2:])

    kv_cols = tuple((i * d, (N_BRANCHES + i) * d) for i in range(N_BRANCHES))
    q_cols = tuple((i * d,) for i in range(N_BRANCHES))
    kvs = _norm_proj(h, kv_mod, kv_g, w_kvb, kv_cols, 1.0, KV_PROJ_TILE)
    for layer in range(n_s5, depth):
        j = layer - n_s5
        last = layer == depth - 1
        qs = _norm_proj(h, mods[layer, 0], ln_g[layer, 0], w_q, q_cols, Q_SCALE, Q_PROJ_TILE)
        o = _attention(qs, kvs)
        ride = [] if last else [(mlp_w1, layer + 1), (mlp_w2, layer + 1), (attn_w_q, j + 1), (attn_w_o, j + 1)]
        h, cast = _post_attn(h, o, mods[layer, 0], mods[layer, 1], ln_g[layer, 1], w_o, w1, w2,
                             final_g if last else None, ride)
        if not last:
            w1, w2, w_q, w_o = map(as_layer, cast)
    return h
```

```python
import functools
import math

import numpy as np

import jax
import jax.numpy as jnp
from jax import lax
from jax.experimental import pallas as pl
from jax.experimental.pallas import tpu as pltpu

F32 = jnp.float32
BF16 = jnp.bfloat16

D_MODEL = 1024
SSM_GROUP = 16
SSM_GROUPS = D_MODEL // SSM_GROUP
SSM_STATE = 64
HEAD_DIM = 64
DILATIONS = (1, 4, 16)
N_BRANCHES = len(DILATIONS)
ATT_BLOCK = 128
ATT_GROUP = 16
ATT_PAIRS_PER_STEP = 2
N_STATS = 3
Q_SCALE = HEAD_DIM ** -0.5 * math.log2(math.e)
D_FF = 4 * D_MODEL
EPS = 1e-6
NEG = -1e30

LANES = 128
CHUNK = 16
GROUP_COLS = CHUNK * SSM_GROUP
PAIR_COLS = 2 * GROUP_COLS
N_PAIRS = SSM_GROUPS // 2
S5_PAIRS_PER_STEP = 2
S5_TILE_TOKENS = 2 * CHUNK
S5_PRE_SLOTS = 3
MLP_ROWS = 1024
KV_PROJ_TILE = 512
Q_PROJ_TILE = 1024
VMEM_LIMIT = 56 * 1024 * 1024

HIGHEST = lax.Precision.HIGHEST
NT_DIMS = (((1,), (1,)), ((), ()))


def _cparams(sem):
    return pltpu.CompilerParams(dimension_semantics=sem, vmem_limit_bytes=VMEM_LIMIT)


def _layer_spec(stack, layer):
    zeros = (0,) * (stack.ndim - 1)
    return pl.BlockSpec((None,) + stack.shape[1:], lambda i: (layer,) + zeros, pipeline_mode=pl.Buffered(1))


def _normmod(x, g, shift, scale):
    ms = jnp.mean(x * x, axis=-1, keepdims=True)
    return (x * lax.rsqrt(ms + EPS) * g) * (1.0 + scale) + shift


def _ada_kernel(c_ref, w_ref, b_ref, o_ref):
    c = c_ref[...]
    sc = (c * jax.nn.sigmoid(c)).astype(BF16)
    o_ref[...] = jnp.dot(sc, w_ref[...].astype(BF16), preferred_element_type=F32) + b_ref[...]


def _ada_mods(c, w, b):
    n, d, width = w.shape
    bsz = c.shape[0]
    return pl.pallas_call(
        _ada_kernel,
        grid=(n,),
        in_specs=[
            pl.BlockSpec((bsz, d), lambda i: (0, 0)),
            pl.BlockSpec((None, d, width), lambda i: (i, 0, 0)),
            pl.BlockSpec((None, 1, width), lambda i: (i, 0, 0)),
        ],
        out_specs=pl.BlockSpec((None, bsz, width), lambda i: (i, 0, 0)),
        out_shape=jax.ShapeDtypeStruct((n, bsz, width), F32),
        compiler_params=_cparams(("parallel",)),
        name="ada_mods",
    )(c, w, b.reshape(n, 1, width))


def _proj_kernel(h_ref, mod_ref, g_ref, w_ref, o1_ref, o2_ref, o3_ref, us, us4, up4, up16, *, wcols, out_scale):
    d = D_MODEL
    tm = h_ref.shape[0]
    d1 = DILATIONS[1]
    q4, q16 = tm // d1, tm // DILATIONS[2]
    cw = o1_ref.shape[1]
    u = _normmod(h_ref[...], g_ref[...], mod_ref[:, :d], mod_ref[:, d:2 * d])
    for l in range(d // LANES):
        us[l] = u[:, l * LANES:(l + 1) * LANES]
    ub = u.astype(BF16)
    for n in range(cw // d):
        cols = slice(n * d, (n + 1) * d)
        res = jnp.dot(ub, w_ref[:, wcols[0][n]:wcols[0][n] + d], preferred_element_type=F32)
        o1_ref[:, cols] = (res * out_scale).astype(o1_ref.dtype)
    for l in range(d // LANES):
        for a in range(d1):
            part = us[l, pl.ds(a, q4, stride=d1), :]
            us4[l, a * q4:(a + 1) * q4, :] = part
            up4[a * q4:(a + 1) * q4, l * LANES:(l + 1) * LANES] = part.astype(BF16)
    for n in range(cw // d):
        res = jnp.dot(up4[...], w_ref[:, wcols[1][n]:wcols[1][n] + d], preferred_element_type=F32)
        for a in range(d1):
            o2_ref[a, :, n * d:(n + 1) * d] = (res[a * q4:(a + 1) * q4, :] * out_scale).astype(o2_ref.dtype)
    for l in range(d // LANES):
        for a1 in range(d1):
            for a2 in range(d1):
                r = d1 * a2 + a1
                up16[r * q16:(r + 1) * q16, l * LANES:(l + 1) * LANES] = us4[
                    l, pl.ds(a1 * q4 + a2, q16, stride=d1), :].astype(BF16)
    for n in range(cw // d):
        res = jnp.dot(up16[...], w_ref[:, wcols[2][n]:wcols[2][n] + d], preferred_element_type=F32)
        for r in range(DILATIONS[2]):
            o3_ref[r, :, n * d:(n + 1) * d] = (res[r * q16:(r + 1) * q16, :] * out_scale).astype(o3_ref.dtype)


def _norm_proj(h, mod, g, w, wcols, out_scale, tm):
    bsz, seq, d = h.shape
    cw = len(wcols[0]) * d
    per_b = seq // tm
    d1, d2 = DILATIONS[1], DILATIONS[2]
    o1, o2, o3 = pl.pallas_call(
        functools.partial(_proj_kernel, wcols=wcols, out_scale=out_scale),
        grid=(bsz * per_b,),
        in_specs=[
            pl.BlockSpec((None, tm, d), lambda i: (i // per_b, i % per_b, 0)),
            pl.BlockSpec((None, 1, mod.shape[-1]), lambda i: (i // per_b, 0, 0)),
            pl.BlockSpec((1, d), lambda i: (0, 0)),
            _layer_spec(*w),
        ],
        out_specs=[
            pl.BlockSpec((None, tm, cw), lambda i: (i // per_b, i % per_b, 0)),
            pl.BlockSpec((None, d1, None, tm // d1, cw), lambda i: (i // per_b, 0, i % per_b, 0, 0)),
            pl.BlockSpec((None, d2, None, tm // d2, cw), lambda i: (i // per_b, 0, i % per_b, 0, 0)),
        ],
        out_shape=[
            jax.ShapeDtypeStruct((bsz, seq, cw), BF16),
            jax.ShapeDtypeStruct((bsz, d1, per_b, tm // d1, cw), BF16),
            jax.ShapeDtypeStruct((bsz, d2, per_b, tm // d2, cw), BF16),
        ],
        scratch_shapes=[pltpu.VMEM((d // LANES, tm, LANES), F32), pltpu.VMEM((d // LANES, tm, LANES), F32),
                        pltpu.VMEM((tm, d), BF16), pltpu.VMEM((tm, d), BF16)],
        compiler_params=_cparams(("parallel",)),
        name="norm_proj",
    )(h, mod, g.reshape(1, d), w[0])
    return o1, o2.reshape(bsz, seq, cw), o3.reshape(bsz, seq, cw)


def _attn_group(blocks):
    blk = ATT_BLOCK
    lane = lax.broadcasted_iota(jnp.int32, (blk, LANES), 1)
    lo = lane < HEAD_DIM
    scores = []
    for qb, kcat, _, mask, _ in blocks:
        zero = jnp.zeros_like(qb)
        q2 = jnp.concatenate([jnp.where(lo, qb, zero), jnp.where(lo, zero, qb)], axis=0)
        s = lax.dot_general(q2, kcat, NT_DIMS, preferred_element_type=F32)
        scores.append(s + mask[...])
    soft = []
    for s, (_, _, _, _, old) in zip(scores, blocks):
        ms, ps = [], []
        for hh in range(2):
            tiles = [s[hh * blk:(hh + 1) * blk, t * LANES:(t + 1) * LANES] for t in range(s.shape[1] // LANES)]
            mx = tiles[0]
            for t in tiles[1:]:
                mx = jnp.maximum(mx, t)
            m = jnp.broadcast_to(jnp.max(mx, axis=-1, keepdims=True), (blk, LANES))
            if old is not None:
                m = jnp.maximum(m, old[0][hh])
            ms.append(m)
            ps.append(jnp.concatenate([jnp.exp2(t - m).astype(BF16) for t in tiles], axis=1))
        alpha = None
        if old is not None:
            alpha = jnp.exp2(jnp.where(lo, old[0][0], old[0][1]) - jnp.where(lo, ms[0], ms[1]))
        soft.append((ms, jnp.concatenate(ps, axis=0), alpha))
    outs = []
    for (ms, p, alpha), (_, _, vcat, _, old) in zip(soft, blocks):
        pv = jnp.dot(p, vcat, preferred_element_type=F32)
        acc = jnp.where(lo, pv[:blk, :LANES], pv[blk:, :LANES])
        den = jnp.where(lo, pv[:blk, LANES:], pv[blk:, LANES:])
        if old is not None:
            acc = alpha * old[1] + acc
            den = alpha * old[0][2] + den
        outs.append((ms + [den], acc))
    return outs


def _attn_kernel(*refs):
    n_in = 3 * N_BRANCHES
    ins, o_ref, scratch = refs[:n_in], refs[n_in], refs[n_in + 1:]
    for half in range(ATT_PAIRS_PER_STEP):
        lanes = pl.ds(half * LANES, LANES)
        _attn_pair(*[r.at[:, lanes] for r in ins], o_ref.at[:, lanes], *scratch)


def _attn_pair(q1, qd2, qd3, k1, kd2, kd3, v1, v2, v3, o_ref, tmp_nat, vd1, vd2, vd3,
               acc_ref, st_ref, acc2_ref, st2_ref, mask_pc, mask_c):
    blk = ATT_BLOCK
    seq = q1.shape[0]
    nblk = seq // blk
    d1, d2 = DILATIONS[1], DILATIONS[2]
    quarter = seq // d1
    per_res = quarter // blk
    assert d2 == d1 * d1 and seq == d2 * blk and nblk % ATT_GROUP == 0 and ATT_GROUP % per_res == 0

    @pl.when(jnp.logical_and(pl.program_id(0) == 0, pl.program_id(1) == 0))
    def _():
        for vd in (vd1, vd2, vd3):
            vd[:, LANES:] = jnp.ones((seq, LANES), BF16)
        qq = lax.broadcasted_iota(jnp.int32, (2 * blk, 2 * blk), 0) & (blk - 1)
        kk = lax.broadcasted_iota(jnp.int32, (2 * blk, 2 * blk), 1)
        valid = jnp.logical_or(jnp.logical_and(kk < blk, kk >= qq), jnp.logical_and(kk >= blk, kk - blk <= qq))
        mask_pc[...] = jnp.where(valid, 0.0, NEG)
        valid_c = (lax.broadcasted_iota(jnp.int32, (2 * blk, blk), 1)
                   <= (lax.broadcasted_iota(jnp.int32, (2 * blk, blk), 0) & (blk - 1)))
        mask_c[...] = jnp.where(valid_c, 0.0, NEG)

    for v, vd in ((v1, vd1), (v2, vd2), (v3, vd3)):
        vd[:, :LANES] = v[...]

    def run_group(blocks, dests):
        for (st_dst, acc_dst, rows), (st, acc) in zip(dests, _attn_group(blocks)):
            for k in range(N_STATS):
                st_dst[k, rows, :] = st[k]
            acc_dst[rows, :] = acc

    def load_old(st_src, acc_src, rows):
        return [st_src[k, rows, :] for k in range(N_STATS)], acc_src[rows, :]

    def b0_group(ns, first):
        blocks, dests = [], []
        for n in ns:
            if first and n == 0:
                rows = pl.ds(0, blk)
                blocks.append((q1[rows, :], k1[rows, :], vd1[rows, :], mask_c, None))
            else:
                rows = pl.ds(pl.multiple_of(n * blk, blk), blk)
                krows = pl.ds(pl.multiple_of((n - 1) * blk, blk), 2 * blk)
                blocks.append((q1[rows, :], k1[krows, :], vd1[krows, :], mask_pc, None))
            dests.append((st_ref, acc_ref, rows))
        run_group(blocks, dests)

    b0_group(list(range(ATT_GROUP)), True)

    def b0_body(i, carry):
        b0_group([ATT_GROUP * i + j for j in range(ATT_GROUP)], False)
        return carry

    lax.fori_loop(1, nblk // ATT_GROUP, b0_body, 0)

    res_per_trip = ATT_GROUP // per_res

    def b1_body(i, carry):
        blocks, dests = [], []
        for jr in range(res_per_trip):
            a1 = res_per_trip * i + jr
            base = a1 * quarter
            for n in range(per_res):
                rows = pl.ds(a1 + d1 * blk * n, blk, stride=d1)
                qrows = pl.ds(pl.multiple_of(base + n * blk, blk), blk)
                if n == 0:
                    krows, mask = qrows, mask_c
                else:
                    krows, mask = pl.ds(pl.multiple_of(base + (n - 1) * blk, blk), 2 * blk), mask_pc
                blocks.append((qd2[qrows, :], kd2[krows, :], vd2[krows, :], mask,
                               load_old(st_ref, acc_ref, rows)))
                dests.append((st2_ref, acc2_ref, qrows))
        run_group(blocks, dests)
        return carry

    lax.fori_loop(0, d1 // res_per_trip, b1_body, 0)

    def b2_body(i, carry):
        blocks, dests = [], []
        for j in range(ATT_GROUP):
            a1, a2 = j % d1, (ATT_GROUP // d1) * i + j // d1
            qrows = pl.ds(pl.multiple_of((ATT_GROUP * i + j) * blk, blk), blk)
            rows = pl.ds(a1 * quarter + a2, blk, stride=d1)
            blocks.append((qd3[qrows, :], kd3[qrows, :], vd3[qrows, :], mask_c,
                           load_old(st2_ref, acc2_ref, rows)))
            dests.append((st2_ref, acc2_ref, rows))
        run_group(blocks, dests)
        return carry

    lax.fori_loop(0, d2 // ATT_GROUP, b2_body, 0)

    for a in range(d1):
        seg = slice(a * quarter, (a + 1) * quarter)
        tmp_nat[pl.ds(a, quarter, stride=d1), :] = acc2_ref[seg, :] / st2_ref[N_STATS - 1, seg, :]
    o_ref[...] = tmp_nat[...].astype(o_ref.dtype)


def _attention(qs, kvs):
    bsz, seq, _ = qs[0].shape
    steps = D_MODEL // (ATT_PAIRS_PER_STEP * LANES)

    def slab(col0):
        return pl.BlockSpec((None, seq, ATT_PAIRS_PER_STEP * LANES), lambda b, hp: (b, 0, col0 + hp))

    in_specs = [slab(0)] * N_BRANCHES + [slab(0)] * N_BRANCHES + [slab(steps)] * N_BRANCHES
    return pl.pallas_call(
        _attn_kernel,
        grid=(bsz, steps),
        in_specs=in_specs,
        out_specs=slab(0),
        out_shape=jax.ShapeDtypeStruct((bsz, seq, D_MODEL), BF16),
        scratch_shapes=([pltpu.VMEM((seq, LANES), F32)] + [pltpu.VMEM((seq, 2 * LANES), BF16)] * 3
                        + [pltpu.VMEM((seq, LANES), F32), pltpu.VMEM((N_STATS, seq, LANES), F32)] * 2
                        + [pltpu.VMEM((2 * ATT_BLOCK, 2 * ATT_BLOCK), F32), pltpu.VMEM((2 * ATT_BLOCK, ATT_BLOCK), F32)]),
        compiler_params=_cparams(("arbitrary", "arbitrary")),
        name="attention",
    )(*qs, *kvs, *kvs)


def _regroup8(tiles):
    t = list(tiles)
    lane = lax.broadcasted_iota(jnp.int32, t[0].shape, 1)
    piece = lane // SSM_GROUP
    for dist in (4, 2, 1):
        bit = (piece & dist) != 0
        shift = dist * SSM_GROUP
        for a in range(8):
            if a & dist:
                continue
            b = a | dist
            ta, tb = t[a], t[b]
            t[a] = jnp.where(bit, pltpu.roll(tb, shift, 1), ta)
            t[b] = jnp.where(bit, tb, pltpu.roll(ta, LANES - shift, 1))
    return t


def _s5_tile_perm(bsz):
    tt = S5_TILE_TOKENS
    n = bsz * tt
    perm = np.zeros((n, n), np.float32)
    for b in range(bsz):
        for c in range(tt // CHUNK):
            for t in range(CHUNK):
                perm[(t * (tt // CHUNK) + c) * bsz + b, b * tt + c * CHUNK + t] = 1.0
    return perm


def _s5_pre_kernel(*refs, n_ride):
    h_hbm, mod_ref, g_ref, perm_ref = refs[:4]
    z_ref = refs[4 + n_ride]
    u_scr, up_scr, h_buf, h_sem = refs[5 + 2 * n_ride:]
    _ride_cast(refs[4:4 + n_ride], refs[5 + n_ride:5 + 2 * n_ride])
    d = D_MODEL
    _, bsz, tt, _ = h_buf.shape
    step, nstep = pl.program_id(0), pl.num_programs(0)

    def tile_copy(s):
        slot = s % S5_PRE_SLOTS
        return pltpu.make_async_copy(h_hbm.at[:, pl.ds(pl.multiple_of(s * tt, tt), tt), :], h_buf.at[slot],
                                     h_sem.at[slot])

    @pl.when(step == 0)
    def _():
        for s in range(S5_PRE_SLOTS - 1):
            tile_copy(s).start()

    @pl.when(step + S5_PRE_SLOTS - 1 < nstep)
    def _():
        tile_copy(step + S5_PRE_SLOTS - 1).start()

    tile_copy(step).wait()
    h_ref = h_buf.at[step % S5_PRE_SLOTS]
    g = g_ref[...]
    for b in range(bsz):
        u_scr[b * tt:(b + 1) * tt, :] = _normmod(
            h_ref[b], g, mod_ref[b, :, :d], mod_ref[b, :, d:2 * d]).astype(BF16)
    up_scr[...] = jnp.dot(perm_ref[...], u_scr[...], preferred_element_type=F32)
    rows = z_ref.shape[0]
    for j in range(d // LANES):
        for half in range(CHUNK // 8):
            tiles = [up_scr[(8 * half + tl) * rows:(8 * half + tl + 1) * rows, j * LANES:(j + 1) * LANES]
                     for tl in range(8)]
            outs = _regroup8(tiles)
            for gl in range(8):
                col = (8 * j + gl) * GROUP_COLS + half * LANES
                z_ref[:, col:col + LANES] = outs[gl].astype(z_ref.dtype)


def _s5_pre(h, mod, g, perm, ride):
    bsz, seq, d = h.shape
    tt = S5_TILE_TOKENS
    rows = bsz * tt // CHUNK
    ride_in, ride_out, ride_shapes, ride_args = _ride_specs(ride, seq // tt)
    outs = pl.pallas_call(
        functools.partial(_s5_pre_kernel, n_ride=len(ride_in)),
        grid=(seq // tt,),
        in_specs=[
            pl.BlockSpec(memory_space=pl.ANY),
            pl.BlockSpec(mod.shape, lambda i: (0, 0, 0)),
            pl.BlockSpec((1, d), lambda i: (0, 0)),
            pl.BlockSpec(perm.shape, lambda i: (0, 0)),
        ] + ride_in,
        out_specs=[pl.BlockSpec((rows, CHUNK * d), lambda i: (i, 0))] + ride_out,
        out_shape=[jax.ShapeDtypeStruct((bsz * seq // CHUNK, CHUNK * d), BF16)] + ride_shapes,
        scratch_shapes=[pltpu.VMEM((bsz * tt, d), BF16), pltpu.VMEM((bsz * tt, d), F32),
                        pltpu.VMEM((S5_PRE_SLOTS, bsz, tt, d), F32), pltpu.SemaphoreType.DMA((S5_PRE_SLOTS,))],
        compiler_params=_cparams(("arbitrary",)),
        name="s5_pre",
    )(h, mod, g.reshape(1, d), perm, *ride_args)
    return outs[0], outs[1:]


def _s5_core_kernel(z_ref, win_ref, wt_ref, wx_ref, dec_ref, d_ref, o_ref, s_scr, x_scr, *, bsz):
    gc = GROUP_COLS
    npair = S5_PAIRS_PER_STEP
    nchunk = z_ref.shape[0] // bsz
    for q in range(npair):
        s_scr[:, q * gc:(q + 1) * gc] = jnp.dot(z_ref[:, q * PAIR_COLS:(q + 1) * PAIR_COLS],
                                                win_ref[2 * q:2 * q + 2].reshape(2 * gc, gc),
                                                preferred_element_type=F32)
    re_cols = [slice(q * gc, q * gc + LANES) for q in range(npair)]
    im_cols = [slice(q * gc + LANES, (q + 1) * gc) for q in range(npair)]
    ar = [dec_ref[:, cols] for cols in re_cols]
    ai = [dec_ref[:, cols] for cols in im_cols]

    def step(c, carry):
        rows = pl.ds(pl.multiple_of(c * bsz, bsz), bsz)
        new = []
        for q in range(npair):
            xr, xi = carry[2 * q], carry[2 * q + 1]
            x_scr[rows, re_cols[q]] = xr
            x_scr[rows, im_cols[q]] = xi
            new += [ar[q] * xr - ai[q] * xi + s_scr[rows, re_cols[q]],
                    ar[q] * xi + ai[q] * xr + s_scr[rows, im_cols[q]]]
        return tuple(new)

    zero = jnp.zeros((bsz, LANES), F32)
    lax.fori_loop(0, nchunk, step, (zero,) * (2 * npair))
    for g in range(2 * npair):
        z = z_ref[:, g * gc:(g + 1) * gc]
        x = x_scr[:, (g // 2) * gc:(g // 2 + 1) * gc].astype(BF16)
        y = jnp.dot(jnp.concatenate([z, x], axis=1), jnp.concatenate([wt_ref[g], wx_ref[g]], axis=0),
                    preferred_element_type=F32)
        y = y + d_ref[:, g * gc:(g + 1) * gc] * z.astype(F32)
        o_ref[:, g * gc:(g + 1) * gc] = jax.nn.gelu(y).astype(o_ref.dtype)


def _s5_core(z, tables, layer, bsz):
    nrow, width = z.shape
    w_in, w_toep, w_x, dec, d_perm = tables
    npair = S5_PAIRS_PER_STEP
    wspec = pl.BlockSpec((None, 2 * npair, GROUP_COLS, GROUP_COLS), lambda k: (layer, k, 0, 0))
    return pl.pallas_call(
        functools.partial(_s5_core_kernel, bsz=bsz),
        grid=(N_PAIRS // npair,),
        in_specs=[
            pl.BlockSpec((nrow, npair * PAIR_COLS), lambda k: (0, k)),
            wspec, wspec, wspec,
            pl.BlockSpec((None, 1, npair * GROUP_COLS), lambda k: (layer, 0, k)),
            pl.BlockSpec((None, 1, npair * PAIR_COLS), lambda k: (layer, 0, k)),
        ],
        out_specs=pl.BlockSpec((nrow, npair * PAIR_COLS), lambda k: (0, k)),
        out_shape=jax.ShapeDtypeStruct((nrow, width), BF16),
        scratch_shapes=[pltpu.VMEM((nrow, npair * GROUP_COLS), F32), pltpu.VMEM((nrow, npair * GROUP_COLS), F32)],
        compiler_params=_cparams(("parallel",)),
        name="s5_core",
    )(z, w_in, w_toep, w_x, dec, d_perm)


def _s5_weights(lam_re, lam_im, log_dt, b_re, b_im, c_re, c_im, d_skip):
    g, p, c16 = SSM_GROUPS, SSM_STATE, SSM_GROUP
    lam = lax.complex(lam_re.astype(F32), lam_im.astype(F32))
    dt = jnp.exp(log_dt.astype(F32))[:, None]
    steps = jnp.arange(CHUNK + 1, dtype=F32)
    apow = jnp.exp((lam * dt)[None] * steps[:, None, None])
    a = apow[1]
    bbar = ((a - 1.0) / lam)[..., None] * lax.complex(b_re.astype(F32), b_im.astype(F32))
    cmat = lax.complex(c_re.astype(F32), c_im.astype(F32))
    odd = (jnp.arange(g) % 2)[:, None, None]

    win = apow[CHUNK - 1::-1][..., None] * bbar[None]

    def lay_in(x):
        return jnp.transpose(x.astype(BF16), (1, 0, 3, 2)).reshape(g, GROUP_COLS, p)

    win_re, win_im = lay_in(win.real), lay_in(win.imag)
    zeros = jnp.zeros_like(win_re)
    w_in = jnp.concatenate([
        jnp.where(odd == 0, win_re, zeros), jnp.where(odd == 1, win_re, zeros),
        jnp.where(odd == 0, win_im, zeros), jnp.where(odd == 1, win_im, zeros)], axis=-1)

    kern = jnp.einsum('gop,kgp,gpi->kgoi', cmat, apow[:CHUNK], bbar, precision=HIGHEST).real
    idx = jnp.arange(CHUNK)
    onehot = (idx[None, None, :] - idx[None, :, None] == idx[:, None, None]).astype(BF16)
    w_toep = jnp.einsum('kst,kgoi->gsito', onehot, kern.astype(BF16),
                        preferred_element_type=BF16)
    w_toep = w_toep.reshape(g, GROUP_COLS, GROUP_COLS)

    cw = cmat[None] * apow[1:, :, None, :]

    def lay_x(x):
        return jnp.transpose(x.astype(BF16), (1, 3, 0, 2)).reshape(g, p, GROUP_COLS)

    cw_re, cw_im = lay_x(cw.real), lay_x(-cw.imag)
    zx = jnp.zeros_like(cw_re)
    w_x = jnp.concatenate([
        jnp.where(odd == 0, cw_re, zx), jnp.where(odd == 1, cw_re, zx),
        jnp.where(odd == 0, cw_im, zx), jnp.where(odd == 1, cw_im, zx)], axis=1)

    a16 = apow[CHUNK].reshape(N_PAIRS, 2 * p)
    dec = jnp.concatenate([a16.real, a16.imag], axis=-1).reshape(1, N_PAIRS * GROUP_COLS)
    d_perm = jnp.broadcast_to(d_skip.astype(F32).reshape(g, 1, c16), (g, CHUNK, c16)).reshape(1, g * GROUP_COLS)
    return w_in, w_toep, w_x, dec, d_perm


def _mlp(u, w1_ref, w2_ref):
    d = D_MODEL
    acc = None
    for k in range(D_FF // d):
        a = jnp.dot(u, w1_ref[:, k * d:(k + 1) * d], preferred_element_type=F32)
        a = jnp.square(jnp.maximum(a, 0.0)).astype(BF16)
        part = jnp.dot(a, w2_ref[k * d:(k + 1) * d, :], preferred_element_type=F32)
        acc = part if acc is None else acc + part
    return acc


def _final_norm(h, fg_ref):
    ms = jnp.mean(h * h, axis=-1, keepdims=True)
    return h * lax.rsqrt(ms + EPS) * fg_ref[...]


def _ride_specs(ride, nsteps):
    ins, outs, shapes = [], [], []
    for st, layer in ride:
        _, r, c = st.shape
        assert r % nsteps == 0
        ins.append(pl.BlockSpec((None, r // nsteps, c), lambda i, layer=layer: (layer, i, 0)))
        outs.append(pl.BlockSpec((r // nsteps, c), lambda i: (i, 0)))
        shapes.append(jax.ShapeDtypeStruct((r, c), BF16))
    return ins, outs, shapes, [st for st, _ in ride]


def _ride_cast(ride_in, ride_out):
    for src, dst in zip(ride_in, ride_out):
        dst[...] = src[...].astype(dst.dtype)


def _post_attn_kernel(*refs, final, n_ride):
    h_ref, y_ref, moda_ref, modm_ref, g_ref, wp_ref, w1_ref, w2_ref = refs[:8]
    pos = 8 + int(final)
    fg_ref = refs[8] if final else None
    o_ref = refs[pos + n_ride]
    _ride_cast(refs[pos:pos + n_ride], refs[pos + n_ride + 1:])
    d = D_MODEL
    ymix = jnp.dot(y_ref[...], wp_ref[...], preferred_element_type=F32)
    h1 = h_ref[...] + moda_ref[:, 2 * d:3 * d] * ymix
    u = _normmod(h1, g_ref[...], modm_ref[:, :d], modm_ref[:, d:2 * d]).astype(BF16)
    h2 = h1 + modm_ref[:, 2 * d:3 * d] * _mlp(u, w1_ref, w2_ref)
    o_ref[...] = _final_norm(h2, fg_ref) if final else h2


def _post_attn(h, o, moda, modm, g2, w_o, w1, w2, final_g, ride):
    bsz, seq, d = h.shape
    tm = MLP_ROWS
    per_b = seq // tm
    row_spec = pl.BlockSpec((None, tm, d), lambda i: (i // per_b, i % per_b, 0))
    final = final_g is not None
    ride_in, ride_out, ride_shapes, ride_args = _ride_specs(ride, bsz * per_b)
    in_specs = [
        row_spec, row_spec,
        pl.BlockSpec((None, 1, moda.shape[-1]), lambda i: (i // per_b, 0, 0)),
        pl.BlockSpec((None, 1, modm.shape[-1]), lambda i: (i // per_b, 0, 0)),
        pl.BlockSpec((1, d), lambda i: (0, 0)),
        _layer_spec(*w_o), _layer_spec(*w1), _layer_spec(*w2),
    ]
    args = [h, o, moda, modm, g2.reshape(1, d), w_o[0], w1[0], w2[0]]
    if final:
        in_specs.append(pl.BlockSpec((1, d), lambda i: (0, 0)))
        args.append(final_g.reshape(1, d))
    args += ride_args
    outs = pl.pallas_call(
        functools.partial(_post_attn_kernel, final=final, n_ride=len(ride_in)),
        grid=(bsz * per_b,),
        in_specs=in_specs + ride_in,
        out_specs=[row_spec] + ride_out,
        out_shape=[jax.ShapeDtypeStruct((bsz, seq, d), F32)] + ride_shapes,
        compiler_params=_cparams(("parallel",)),
        name="post_attn_mlp",
    )(*args)
    return outs[0], outs[1:]


def _post_s5_kernel(*refs, n_ride):
    h_hbm, z_ref, moda_ref, modm_ref, g_ref, permt_ref, wp_ref, w1_ref, w2_ref = refs[:9]
    o_ref = refs[9 + n_ride]
    z_scr, h1_scr, u_scr, h_buf, h_sem = refs[10 + 2 * n_ride:]
    _ride_cast(refs[9:9 + n_ride], refs[10 + n_ride:10 + 2 * n_ride])
    d = D_MODEL
    _, bsz, tt, _ = h_buf.shape
    rows = z_ref.shape[0]
    step, nstep = pl.program_id(0), pl.num_programs(0)

    def tile_copy(s):
        slot = s % S5_PRE_SLOTS
        return pltpu.make_async_copy(h_hbm.at[:, pl.ds(pl.multiple_of(s * tt, tt), tt), :], h_buf.at[slot],
                                     h_sem.at[slot])

    @pl.when(step == 0)
    def _():
        for s in range(S5_PRE_SLOTS - 1):
            tile_copy(s).start()

    @pl.when(step + S5_PRE_SLOTS - 1 < nstep)
    def _():
        tile_copy(step + S5_PRE_SLOTS - 1).start()

    tile_copy(step).wait()
    h_ref = h_buf.at[step % S5_PRE_SLOTS]
    for j in range(d // LANES):
        for half in range(CHUNK // 8):
            tiles = []
            for gl in range(8):
                col = (8 * j + gl) * GROUP_COLS + half * LANES
                tiles.append(z_ref[:, col:col + LANES].astype(F32))
            outs = _regroup8(tiles)
            for tl in range(8):
                t = 8 * half + tl
                z_scr[t * rows:(t + 1) * rows, j * LANES:(j + 1) * LANES] = outs[tl].astype(BF16)
    zn = jnp.dot(permt_ref[...], z_scr[...], preferred_element_type=F32).astype(BF16)
    zz = jnp.dot(zn, wp_ref[...], preferred_element_type=F32)
    ymix = zz[:, :d] * jax.nn.sigmoid(zz[:, d:])
    g = g_ref[...]
    for b in range(bsz):
        sl = slice(b * tt, (b + 1) * tt)
        h1 = h_ref[b] + moda_ref[b, :, 2 * d:3 * d] * ymix[sl, :]
        h1_scr[sl, :] = h1
        u_scr[sl, :] = _normmod(h1, g, modm_ref[b, :, :d], modm_ref[b, :, d:2 * d]).astype(BF16)
    acc = _mlp(u_scr[...], w1_ref, w2_ref)
    for b in range(bsz):
        sl = slice(b * tt, (b + 1) * tt)
        o_ref[b] = h1_scr[sl, :] + modm_ref[b, :, 2 * d:3 * d] * acc[sl, :]


def _post_s5(h, zo, moda, modm, g2, permt, w_glu, w1, w2, ride):
    bsz, seq, d = h.shape
    tt = S5_TILE_TOKENS
    rows = bsz * tt // CHUNK
    h_spec = pl.BlockSpec((bsz, tt, d), lambda i: (0, i, 0))
    ride_in, ride_out, ride_shapes, ride_args = _ride_specs(ride, seq // tt)
    outs = pl.pallas_call(
        functools.partial(_post_s5_kernel, n_ride=len(ride_in)),
        grid=(seq // tt,),
        in_specs=[
            pl.BlockSpec(memory_space=pl.ANY),
            pl.BlockSpec((rows, CHUNK * d), lambda i: (i, 0)),
            pl.BlockSpec(moda.shape, lambda i: (0, 0, 0)),
            pl.BlockSpec(modm.shape, lambda i: (0, 0, 0)),
            pl.BlockSpec((1, d), lambda i: (0, 0)),
            pl.BlockSpec(permt.shape, lambda i: (0, 0)),
            _layer_spec(*w_glu), _layer_spec(*w1), _layer_spec(*w2),
        ] + ride_in,
        out_specs=[h_spec] + ride_out,
        out_shape=[jax.ShapeDtypeStruct((bsz, seq, d), F32)] + ride_shapes,
        scratch_shapes=[pltpu.VMEM((bsz * tt, d), BF16), pltpu.VMEM((bsz * tt, d), F32),
                        pltpu.VMEM((bsz * tt, d), BF16),
                        pltpu.VMEM((S5_PRE_SLOTS, bsz, tt, d), F32), pltpu.SemaphoreType.DMA((S5_PRE_SLOTS,))],
        compiler_params=_cparams(("arbitrary",)),
        name="post_s5_mlp",
    )(h, zo, moda, modm, g2.reshape(1, d), permt, w_glu[0], w1[0], w2[0], *ride_args)
    return outs[0], outs[1:]


def kernel(x, c, ln_g, ada_w, ada_b, ssm_lam_re, ssm_lam_im, ssm_log_dt, ssm_b_re, ssm_b_im, ssm_c_re, ssm_c_im, ssm_d, ssm_w_glu, kv_g, kv_ada_w, kv_ada_b, w_kv, attn_w_q, attn_w_o, mlp_w1, mlp_w2, final_g):
    bsz, seq, d = x.shape
    depth = ln_g.shape[0]
    n_s5 = ssm_lam_re.shape[0]
    assert d == D_MODEL and seq % (DILATIONS[-1] * ATT_BLOCK) == 0 and 0 < n_s5 < depth

    mods = _ada_mods(c, ada_w.reshape(depth * 2, d, 3 * d), ada_b.reshape(depth * 2, 3 * d))
    mods = mods.reshape(depth, 2, bsz, 1, 3 * d)
    kv_mod = _ada_mods(c, kv_ada_w[None], kv_ada_b[None]).reshape(bsz, 1, 2 * d)

    perm_np = _s5_tile_perm(bsz)
    perm = jnp.asarray(perm_np, BF16)
    permt = jnp.asarray(perm_np.T, BF16)
    tables = jax.vmap(_s5_weights)(ssm_lam_re, ssm_lam_im, ssm_log_dt, ssm_b_re, ssm_b_im,
                                   ssm_c_re, ssm_c_im, ssm_d)

    def as_layer(cast):
        return (cast[None], 0)

    h = x
    for layer in range(n_s5):
        ride = [(mlp_w1, 0), (mlp_w2, 0), (ssm_w_glu, 0)] if layer == 0 else []
        z, cast = _s5_pre(h, mods[layer, 0], ln_g[layer, 0], perm, ride)
        if layer == 0:
            w1, w2, w_glu = map(as_layer, cast)
        zo = _s5_core(z, tables, layer, bsz)
        ride = [(mlp_w1, layer + 1), (mlp_w2, layer + 1)]
        if layer + 1 < n_s5:
            ride += [(ssm_w_glu, layer + 1)]
        else:
            ride += [(w_kv[None], 0), (attn_w_q, 0), (attn_w_o, 0)]
        h, cast = _post_s5(h, zo, mods[layer, 0], mods[layer, 1], ln_g[layer, 1], permt, w_glu, w1, w2, ride)
        w1, w2 = as_layer(cast[0]), as_layer(cast[1])
        if layer + 1 < n_s5:
            w_glu = as_layer(cast[2])
        else:
            w_kvb, w_q, w_o = map(as_layer, cast[2:])

    kv_cols = tuple((i * d, (N_BRANCHES + i) * d) for i in range(N_BRANCHES))
    q_cols = tuple((i * d,) for i in range(N_BRANCHES))
    kvs = _norm_proj(h, kv_mod, kv_g, w_kvb, kv_cols, 1.0, KV_PROJ_TILE)
    for layer in range(n_s5, depth):
        j = layer - n_s5
        last = layer == depth - 1
        qs = _norm_proj(h, mods[layer, 0], ln_g[layer, 0], w_q, q_cols, Q_SCALE, Q_PROJ_TILE)
        o = _attention(qs, kvs)
        ride = [] if last else [(mlp_w1, layer + 1), (mlp_w2, layer + 1), (attn_w_q, j + 1), (attn_w_o, j + 1)]
        h, cast = _post_attn(h, o, mods[layer, 0], mods[layer, 1], ln_g[layer, 1], w_o, w1, w2,
                             final_g if last else None, ride)
        if not last:
            w1, w2, w_q, w_o = map(as_layer, cast)
    return h
```
